```python
import jax, jax.numpy as jnp
from jax import lax
import numpy as np

D_MODEL = 2048
BATCH = 2
SEQ = 4096
DEPTH = 4
DEC_BATCH = 32
DEC_SEQ = 8
PAST_LEN = 16384
PAGE_SIZE = 128

A_HEADS = 16
A_KV_HEADS = 2
A_GROUP = A_HEADS // A_KV_HEADS
A_HEAD_DIM = 64
A_WIDTH = A_HEADS * A_HEAD_DIM
A_KV_WIDTH = A_KV_HEADS * A_HEAD_DIM
WINDOW = 128
CHUNK = 128
B_GROUPS = 4
B_WIDTH = D_MODEL - A_WIDTH
B_GROUP_DIM = B_WIDTH // B_GROUPS
EVEN_IN = 2 * A_WIDTH + 2 * A_KV_WIDTH + 3 * B_WIDTH
EVEN_OUT = A_WIDTH + B_WIDTH
R_HEADS = 8
R_KEY_DIM = D_MODEL // R_HEADS
R_VAL_DIM = 2 * R_KEY_DIM
R_K_WIDTH = R_HEADS * R_KEY_DIM
R_V_WIDTH = R_HEADS * R_VAL_DIM
R_CHUNK = 128
RET_IN = 2 * R_K_WIDTH + 2 * R_V_WIDTH
MEM_LEN = 256
X_HEADS = 4
X_HEAD_DIM = 128
X_WIDTH = X_HEADS * X_HEAD_DIM

N_ATTN_LAYERS = (DEPTH + 1) // 2
N_RET_LAYERS = DEPTH // 2
EPS = 1e-6
NEG_INF = -1e30
F32 = jnp.float32

kernel_name = 'hybrid_swa_gmlp_retention_step'


def rmsnorm(x, g):
    x32 = x.astype(F32)
    y = x32 * lax.rsqrt(jnp.mean(x32 * x32, axis=-1, keepdims=True) + EPS)
    return (y * g.astype(F32)).astype(x.dtype)


def standardize(x):
    x32 = x.astype(F32)
    xc = x32 - jnp.mean(x32, axis=-1, keepdims=True)
    return xc * lax.rsqrt(jnp.mean(xc * xc, axis=-1, keepdims=True) + EPS)


def alibi_slopes():
    return jnp.exp2(-8.0 * (jnp.arange(A_HEADS, dtype=F32) + 1.0) / A_HEADS)


def retention_log_decay():
    return jnp.log1p(-jnp.exp2(-5.0 - jnp.arange(R_HEADS, dtype=F32)))


def sink_window_attention(q, k, v, q_pos, k_pos, sinks):
    s = jnp.einsum('bnqkgd,bnskd->bnkgqs', q.astype(F32), k.astype(F32)) * (A_HEAD_DIM ** -0.5)
    dist = q_pos[:, :, None] - k_pos[:, None, :]
    valid = (dist >= 0) & (dist < WINDOW) & (k_pos[:, None, :] >= 0)
    slopes = alibi_slopes().reshape(A_KV_HEADS, A_GROUP)[None, None, :, :, None, None]
    s = jnp.where(valid[None, :, None, None], s - slopes * dist.astype(F32)[None, :, None, None], NEG_INF)
    sink = sinks.astype(F32).reshape(A_KV_HEADS, A_GROUP)[None, None, :, :, None, None]
    m = jnp.maximum(jnp.max(s, axis=-1, keepdims=True), sink)
    p = jnp.exp(s - m)
    denom = jnp.sum(p, axis=-1, keepdims=True) + jnp.exp(sink - m)
    return jnp.einsum('bnkgqs,bnskd->bnqkgd', p / denom, v.astype(F32))


def spatial_gate(u, v, w_s, b_s):
    tc = v.shape[2]
    causal = jnp.tril(jnp.ones((tc, tc), dtype=bool))
    ws = jnp.where(causal[None], w_s[:, :tc, :tc].astype(F32), 0.0)
    mixed = jnp.einsum('gij,bnjgc->bnigc', ws, v.astype(F32)) + b_s[:, :tc].astype(F32).T[:, :, None]
    return u.astype(F32) * mixed


def split_even(z):
    sizes = (A_WIDTH, A_KV_WIDTH, A_KV_WIDTH, A_WIDTH, B_WIDTH, B_WIDTH, B_WIDTH)
    idx = [int(i) for i in np.cumsum(sizes)[:-1]]
    return jnp.split(z, idx, axis=-1)


def even_output(x, a, ga, bo, gb, w_out):
    mixed = jnp.concatenate([jax.nn.silu(ga.astype(F32)) * a, jax.nn.silu(gb.astype(F32)) * bo], axis=-1)
    return x + mixed.astype(x.dtype) @ w_out


def shift_band(t):
    prev = jnp.concatenate([jnp.zeros_like(t[:, :1]), t[:, :-1]], axis=1)
    return jnp.concatenate([prev, t], axis=2)


def even_layer_prompt(x, g_mix, w_in, sinks, w_s, b_s, lnv_g, lnv_b, w_out):
    bsz, seq, _ = x.shape
    q, k, v, ga, u, vb, gb = split_even(rmsnorm(x, g_mix) @ w_in)
    nb = seq // WINDOW
    k = k.reshape(bsz, seq, A_KV_HEADS, A_HEAD_DIM)
    v = v.reshape(bsz, seq, A_KV_HEADS, A_HEAD_DIM)
    kb = shift_band(k.reshape(bsz, nb, WINDOW, A_KV_HEADS, A_HEAD_DIM))
    vbk = shift_band(v.reshape(bsz, nb, WINDOW, A_KV_HEADS, A_HEAD_DIM))
    pos = jnp.arange(seq, dtype=jnp.int32).reshape(nb, WINDOW)
    k_pos = jnp.concatenate([pos - WINDOW, pos], axis=1)
    qb = q.reshape(bsz, nb, WINDOW, A_KV_HEADS, A_GROUP, A_HEAD_DIM)
    a = sink_window_attention(qb, kb, vbk, pos, k_pos, sinks).reshape(bsz, seq, A_WIDTH)
    vn = standardize(vb) * lnv_g.astype(F32) + lnv_b.astype(F32)
    gshape = (bsz, seq // CHUNK, CHUNK, B_GROUPS, B_GROUP_DIM)
    bo = spatial_gate(u.reshape(gshape), vn.reshape(gshape), w_s, b_s).reshape(bsz, seq, B_WIDTH)
    return even_output(x, a, ga, bo, gb, w_out), k[:, -WINDOW:], v[:, -WINDOW:]


def even_layer_sample(x, win_k, win_v, g_mix, w_in, sinks, w_s, b_s, lnv_g, lnv_b, w_out):
    bsz, t, _ = x.shape
    q, k, v, ga, u, vb, gb = split_even(rmsnorm(x, g_mix) @ w_in)
    kk = jnp.concatenate([win_k.astype(x.dtype), k.reshape(bsz, t, A_KV_HEADS, A_HEAD_DIM)], axis=1)
    vv = jnp.concatenate([win_v.astype(x.dtype), v.reshape(bsz, t, A_KV_HEADS, A_HEAD_DIM)], axis=1)
    q_pos = (PAST_LEN + jnp.arange(t, dtype=jnp.int32))[None]
    k_pos = (PAST_LEN - WINDOW + jnp.arange(WINDOW + t, dtype=jnp.int32))[None]
    qb = q.reshape(bsz, 1, t, A_KV_HEADS, A_GROUP, A_HEAD_DIM)
    a = sink_window_attention(qb, kk[:, None], vv[:, None], q_pos, k_pos, sinks).reshape(bsz, t, A_WIDTH)
    vn = standardize(vb) * lnv_g.astype(F32) + lnv_b.astype(F32)
    gshape = (bsz, 1, t, B_GROUPS, B_GROUP_DIM)
    bo = spatial_gate(u.reshape(gshape), vn.reshape(gshape), w_s, b_s).reshape(bsz, t, B_WIDTH)
    y = even_output(x, a, ga, bo, gb, w_out)
    return y, kk[:, -WINDOW:], vv[:, -WINDOW:], vn.astype(x.dtype)


def retention_chunk(state, q, k, v, log_g):
    t = q.shape[1]
    i = jnp.arange(t, dtype=F32)
    rel = i[:, None] - i[None, :]
    decay = jnp.where(rel[None] >= 0, jnp.exp(jnp.maximum(rel, 0.0)[None] * log_g[:, None, None]), 0.0)
    scores = jnp.einsum('bthd,bshd->bhts', q, k) * decay[None]
    inner = jnp.einsum('bhts,bshe->bthe', scores, v)
    q_decay = jnp.exp((i[:, None] + 1.0) * log_g[None, :])
    cross = jnp.einsum('bthd,bhde->bthe', q * q_decay[None, :, :, None], state)
    k_decay = jnp.exp((t - 1.0 - i)[:, None] * log_g[None, :])
    new_state = jnp.exp(t * log_g)[None, :, None, None] * state + jnp.einsum('bshd,bshe->bhde', k * k_decay[None, :, :, None], v)
    return new_state, inner + cross


def split_ret(z, bsz, t):
    q, k, v, g = jnp.split(z, [R_K_WIDTH, 2 * R_K_WIDTH, 2 * R_K_WIDTH + R_V_WIDTH], axis=-1)
    q = q.reshape(bsz, t, R_HEADS, R_KEY_DIM).astype(F32)
    k = k.reshape(bsz, t, R_HEADS, R_KEY_DIM).astype(F32) * (R_KEY_DIM ** -0.5)
    v = v.reshape(bsz, t, R_HEADS, R_VAL_DIM).astype(F32)
    return q, k, v, g


def ret_output(x, o, g, w_out):
    bsz, t = x.shape[0], x.shape[1]
    on = standardize(o).reshape(bsz, t, R_V_WIDTH)
    return x + (jax.nn.silu(g.astype(F32)) * on).astype(x.dtype) @ w_out


def odd_layer_prompt(x, g_mix, w_in, w_out):
    bsz, seq, _ = x.shape
    q, k, v, g = split_ret(rmsnorm(x, g_mix) @ w_in, bsz, seq)
    log_g = retention_log_decay()
    nc = seq // R_CHUNK

    def to_chunks(a):
        return a.reshape(bsz, nc, R_CHUNK, a.shape[2], a.shape[3]).swapaxes(0, 1)

    def step(state, qkv):
        qc, kc, vc = qkv
        return retention_chunk(state, qc, kc, vc, log_g)

    state0 = jnp.zeros((bsz, R_HEADS, R_KEY_DIM, R_VAL_DIM), F32)
    state, o = lax.scan(step, state0, (to_chunks(q), to_chunks(k), to_chunks(v)))
    o = o.swapaxes(0, 1).reshape(bsz, seq, R_HEADS, R_VAL_DIM)
    return ret_output(x, o, g, w_out), state.astype(x.dtype)


def odd_layer_sample(x, state, g_mix, w_in, w_out):
    bsz, t, _ = x.shape
    q, k, v, g = split_ret(rmsnorm(x, g_mix) @ w_in, bsz, t)
    new_state, o = retention_chunk(state.astype(F32), q, k, v, retention_log_decay())
    return ret_output(x, o, g, w_out), new_state.astype(state.dtype)


def memory_kv(mem, g_mem, w_k, w_v):
    bsz, m, _ = mem.shape
    h = rmsnorm(mem, g_mem)
    return (h @ w_k).reshape(bsz, m, X_HEADS, X_HEAD_DIM), (h @ w_v).reshape(bsz, m, X_HEADS, X_HEAD_DIM)


def cross_attend(x, g, w_q, w_o, mem_k, mem_v):
    bsz, t, _ = x.shape
    q = (rmsnorm(x, g) @ w_q).reshape(bsz, t, X_HEADS, X_HEAD_DIM)
    s = jnp.einsum('bthd,bmhd->bhtm', q.astype(F32), mem_k.astype(F32)) * (X_HEAD_DIM ** -0.5)
    p = jax.nn.softmax(s, axis=-1)
    o = jnp.einsum('bhtm,bmhd->bthd', p, mem_v.astype(F32)).reshape(bsz, t, X_WIDTH)
    return x + o.astype(x.dtype) @ w_o


def setup_inputs(seed: int = 0) -> dict:
    key = jax.random.key(seed)
    ks = iter(jax.random.split(key, 32))

    def nrm(shape, scale):
        return scale * jax.random.normal(next(ks), shape, F32)

    na, nr = N_ATTN_LAYERS, N_RET_LAYERS
    return {
        'x_prompt': nrm((BATCH, SEQ, D_MODEL), 1.0),
        'x_sample': nrm((DEC_BATCH, DEC_SEQ, D_MODEL), 1.0),
        'cache_win_k': nrm((na, DEC_BATCH, WINDOW, A_KV_HEADS, A_HEAD_DIM), 1.0),
        'cache_win_v': nrm((na, DEC_BATCH, WINDOW, A_KV_HEADS, A_HEAD_DIM), 1.0),
        'state_ret': nrm((nr, DEC_BATCH, R_HEADS, R_KEY_DIM, R_VAL_DIM), 0.3),
        'cache_mem_k': nrm((DEPTH, DEC_BATCH, MEM_LEN, X_HEADS, X_HEAD_DIM), 1.0),
        'cache_mem_v': nrm((DEPTH, DEC_BATCH, MEM_LEN, X_HEADS, X_HEAD_DIM), 1.0),
        'mem_prompt': nrm((BATCH, MEM_LEN, D_MODEL), 1.0),
        'norm_mix': 1.0 + nrm((DEPTH, D_MODEL), 0.02),
        'w_in_even': nrm((na, D_MODEL, EVEN_IN), D_MODEL ** -0.5),
        'attn_sinks': nrm((na, A_HEADS), 1.0),
        'w_spatial': nrm((na, B_GROUPS, CHUNK, CHUNK), CHUNK ** -0.5),
        'b_spatial': 1.0 + nrm((na, B_GROUPS, CHUNK), 0.1),
        'norm_v_g': 1.0 + nrm((na, B_WIDTH), 0.02),
        'norm_v_b': nrm((na, B_WIDTH), 0.02),
        'w_out_even': nrm((na, EVEN_OUT, D_MODEL), EVEN_OUT ** -0.5),
        'w_in_ret': nrm((nr, D_MODEL, RET_IN), D_MODEL ** -0.5),
        'w_out_ret': nrm((nr, R_V_WIDTH, D_MODEL), R_V_WIDTH ** -0.5),
        'norm_cross': 1.0 + nrm((DEPTH, D_MODEL), 0.02),
        'norm_mem': 1.0 + nrm((DEPTH, D_MODEL), 0.02),
        'w_xq': nrm((DEPTH, D_MODEL, X_WIDTH), D_MODEL ** -0.5),
        'w_xk': nrm((DEPTH, D_MODEL, X_WIDTH), D_MODEL ** -0.5),
        'w_xv': nrm((DEPTH, D_MODEL, X_WIDTH), D_MODEL ** -0.5),
        'w_xo': nrm((DEPTH, X_WIDTH, D_MODEL), X_WIDTH ** -0.5),
        'norm_final': 1.0 + nrm((D_MODEL,), 0.02),
    }


def reference(x_prompt, x_sample, cache_win_k, cache_win_v, state_ret, cache_mem_k, cache_mem_v, mem_prompt,
              norm_mix, w_in_even, attn_sinks, w_spatial, b_spatial, norm_v_g, norm_v_b, w_out_even,
              w_in_ret, w_out_ret, norm_cross, norm_mem, w_xq, w_xk, w_xv, w_xo, norm_final):
    xp, xs = x_prompt, x_sample
    wkp, wvp, rsp, mkp, mvp = [], [], [], [], []
    wks, wvs, rss, cvs = [], [], [], []
    for layer in range(DEPTH):
        i = layer // 2
        if layer % 2 == 0:
            params = (norm_mix[layer], w_in_even[i], attn_sinks[i], w_spatial[i], b_spatial[i],
                      norm_v_g[i], norm_v_b[i], w_out_even[i])
            xp, kp, vp = even_layer_prompt(xp, *params)
            xs, ksn, vsn, cv = even_layer_sample(xs, cache_win_k[i], cache_win_v[i], *params)
            wkp.append(kp)
            wvp.append(vp)
            wks.append(ksn)
            wvs.append(vsn)
            cvs.append(cv)
        else:
            xp, sp = odd_layer_prompt(xp, norm_mix[layer], w_in_ret[i], w_out_ret[i])
            xs, ss = odd_layer_sample(xs, state_ret[i], norm_mix[layer], w_in_ret[i], w_out_ret[i])
            rsp.append(sp)
            rss.append(ss)
        mk, mv = memory_kv(mem_prompt, norm_mem[layer], w_xk[layer], w_xv[layer])
        mkp.append(mk)
        mvp.append(mv)
        xp = cross_attend(xp, norm_cross[layer], w_xq[layer], w_xo[layer], mk, mv)
        xs = cross_attend(xs, norm_cross[layer], w_xq[layer], w_xo[layer], cache_mem_k[layer], cache_mem_v[layer])
    y_prompt = rmsnorm(xp, norm_final)
    y_sample = rmsnorm(xs, norm_final)
    return (y_prompt, y_sample,
            jnp.stack(wkp), jnp.stack(wvp), jnp.stack(rsp), jnp.stack(mkp), jnp.stack(mvp),
            jnp.stack(wks), jnp.stack(wvs), jnp.stack(rss), jnp.stack(cvs))
```

```python
import functools

import jax
import jax.numpy as jnp
from jax import lax
from jax.experimental import pallas as pl
from jax.experimental.pallas import tpu as pltpu

F32 = jnp.float32
BF16 = jnp.bfloat16

D_MODEL = 2048
BATCH = 2
SEQ = 4096
DEPTH = 4
DEC_BATCH = 32
DEC_SEQ = 8
A_HEADS = 16
A_KV_HEADS = 2
A_HEAD_DIM = 64
A_WIDTH = 1024
A_KV_WIDTH = 128
WINDOW = 128
CHUNK = 128
B_GROUPS = 4
B_WIDTH = 1024
B_GROUP_DIM = 256
EVEN_IN = 5376
R_HEADS = 8
R_KEY_DIM = 256
R_VAL_DIM = 512
R_K_WIDTH = 2048
R_V_WIDTH = 4096
R_CHUNK = 128
RET_IN = 12288
MEM_LEN = 256
X_HEADS = 4
X_HEAD_DIM = 128
X_WIDTH = 512
EPS = 1e-6
NEG_INF = -1e30

N_PROMPT = BATCH * SEQ
N_SAMPLE = DEC_BATCH * DEC_SEQ
N_ROWS = N_PROMPT + N_SAMPLE

OFF_Q, OFF_K, OFF_V, OFF_GA, OFF_U, OFF_VB, OFF_GB = 0, 1024, 1152, 1280, 2304, 3328, 4352

VMEM_LIMIT = 52 * 1024 * 1024


def _cp(n_axes):
    return pltpu.CompilerParams(dimension_semantics=("arbitrary",) * n_axes,
                                vmem_limit_bytes=VMEM_LIMIT)


def _silu(x):
    return x * (1.0 / (1.0 + jnp.exp(-x)))


def _rms(x, g):
    ms = jnp.mean(x * x, axis=-1, keepdims=True)
    return x * lax.rsqrt(ms + EPS) * g


def _standardize(x):
    xc = x - jnp.mean(x, axis=-1, keepdims=True)
    return xc * lax.rsqrt(jnp.mean(xc * xc, axis=-1, keepdims=True) + EPS)


def _dot_nt(a, b):
    return lax.dot_general(a, b, (((1,), (1,)), ((), ())), preferred_element_type=F32)


def _rmsnorm_kernel(x_ref, g_ref, o_ref):
    o_ref[...] = _rms(x_ref[...], g_ref[...]).astype(o_ref.dtype)


def _rmsnorm(x, g, out_dtype, tm):
    m, d = x.shape
    return pl.pallas_call(
        _rmsnorm_kernel,
        grid=(m // tm,),
        in_specs=[pl.BlockSpec((tm, d), lambda i: (i, 0)),
                  pl.BlockSpec((1, d), lambda i: (0, 0))],
        out_specs=pl.BlockSpec((tm, d), lambda i: (i, 0)),
        out_shape=jax.ShapeDtypeStruct((m, d), out_dtype),
        compiler_params=_cp(1),
        name="rmsnorm",
    )(x, g.reshape(1, d))


def _mm_kernel(a_ref, w_ref, o_ref):
    o_ref[...] = jnp.dot(a_ref[...], w_ref[...], preferred_element_type=F32).astype(o_ref.dtype)


def _mm(a, w, out_dtype, tm, tn, row_blk0=0, n_row_blks=None, name="mm"):
    m, k = a.shape
    n = w.shape[1]
    if n_row_blks is None:
        n_row_blks = m // tm
    return pl.pallas_call(
        _mm_kernel,
        grid=(n_row_blks, n // tn),
        in_specs=[pl.BlockSpec((tm, k), lambda i, j: (i + row_blk0, 0)),
                  pl.BlockSpec((k, tn), lambda i, j: (0, j))],
        out_specs=pl.BlockSpec((tm, tn), lambda i, j: (i, j)),
        out_shape=jax.ShapeDtypeStruct((n_row_blks * tm, n), out_dtype),
        compiler_params=_cp(2),
        name=name,
    )(a, w)


def _mm_res_norm_kernel(nk, a_ref, w_ref, x_ref, g_ref, xo_ref, n_ref):
    k = pl.program_id(1)

    @pl.when(k == 0)
    def _():
        xo_ref[...] = x_ref[...]

    xo_ref[...] += jnp.dot(a_ref[...], w_ref[...], preferred_element_type=F32)

    @pl.when(k == nk - 1)
    def _():
        n_ref[...] = _rms(xo_ref[...], g_ref[...]).astype(n_ref.dtype)


def _mm_res_norm(a, w, x, g, norm_dtype, tm, tk):
    m, kdim = a.shape
    d = w.shape[1]
    nk = kdim // tk
    return pl.pallas_call(
        functools.partial(_mm_res_norm_kernel, nk),
        grid=(m // tm, nk),
        in_specs=[pl.BlockSpec((tm, tk), lambda i, k: (i, k)),
                  pl.BlockSpec((tk, d), lambda i, k: (k, 0)),
                  pl.BlockSpec((tm, d), lambda i, k: (i, 0)),
                  pl.BlockSpec((1, d), lambda i, k: (0, 0))],
        out_specs=[pl.BlockSpec((tm, d), lambda i, k: (i, 0)),
                   pl.BlockSpec((tm, d), lambda i, k: (i, 0))],
        out_shape=[jax.ShapeDtypeStruct((m, d), F32),
                   jax.ShapeDtypeStruct((m, d), norm_dtype)],
        compiler_params=_cp(2),
        name="mm_res_norm",
    )(a, w, x, g.reshape(1, d))


def _memkv_kernel(mem_ref, g_ref, wk_ref, wv_ref, mk_ref, mv_ref):
    h = _rms(mem_ref[...], g_ref[0]).astype(BF16)
    mk_ref[0] = jnp.dot(h, wk_ref[0].astype(BF16), preferred_element_type=F32)
    mv_ref[0] = jnp.dot(h, wv_ref[0].astype(BF16), preferred_element_type=F32)


def _memkv(mem, norm_mem, w_xk, w_xv):
    m = mem.shape[0]
    out = jax.ShapeDtypeStruct((DEPTH, m, X_WIDTH), F32)
    return pl.pallas_call(
        _memkv_kernel,
        grid=(DEPTH,),
        in_specs=[pl.BlockSpec((m, D_MODEL), lambda l: (0, 0)),
                  pl.BlockSpec((1, 1, D_MODEL), lambda l: (l, 0, 0)),
                  pl.BlockSpec((1, D_MODEL, X_WIDTH), lambda l: (l, 0, 0)),
                  pl.BlockSpec((1, D_MODEL, X_WIDTH), lambda l: (l, 0, 0))],
        out_specs=[pl.BlockSpec((1, m, X_WIDTH), lambda l: (l, 0, 0)),
                   pl.BlockSpec((1, m, X_WIDTH), lambda l: (l, 0, 0))],
        out_shape=[out, out],
        compiler_params=_cp(1),
        name="memkv",
    )(mem, norm_mem.reshape(DEPTH, 1, D_MODEL), w_xk, w_xv)


def _expand_band(band):
    x = band.astype(F32)
    xs = pltpu.roll(x, 64, 1)
    lo = lax.broadcasted_iota(jnp.int32, x.shape, 1) < 64
    z = jnp.zeros_like(x)
    e0 = jnp.concatenate([jnp.where(lo, x, z), jnp.where(lo, z, xs)], axis=0).astype(BF16)
    e1 = jnp.concatenate([jnp.where(lo, xs, z), jnp.where(lo, z, x)], axis=0).astype(BF16)
    return e0, e1


def _swa_pairs(q_pair_fn, kband, vband, valid, dist_f, sl_ref):
    kk = _expand_band(kband)
    vv = _expand_band(vband)
    t = dist_f.shape[0]
    lo = lax.broadcasted_iota(jnp.int32, (t, 128), 1) < 64
    for p in range(A_HEADS // 2):
        j = p // (A_HEADS // A_KV_HEADS // 2)
        s = _dot_nt(q_pair_fn(p), kk[j])
        probs, invs = [], []
        for half in range(2):
            h = 2 * p + half
            slope = sl_ref[0, h]
            sink = sl_ref[1, h]
            sh = s[:, half * 256:(half + 1) * 256] * (A_HEAD_DIM ** -0.5) - slope * dist_f
            sh = jnp.where(valid, sh, NEG_INF)
            m = jnp.maximum(jnp.max(sh, axis=-1, keepdims=True), sink)
            pe = jnp.exp(sh - m)
            den = jnp.sum(pe, axis=-1, keepdims=True) + jnp.exp(sink - m)
            probs.append(pe)
            invs.append(1.0 / den)
        pp = jnp.concatenate(probs, axis=1).astype(BF16)
        o = jnp.dot(pp, vv[j], preferred_element_type=F32)
        yield p, o * jnp.where(lo, invs[0], invs[1])


def _band_mask(t, first_key):
    row = lax.broadcasted_iota(jnp.int32, (t, 2 * WINDOW), 0)
    col = lax.broadcasted_iota(jnp.int32, (t, 2 * WINDOW), 1)
    dist = row + WINDOW - col
    valid = (dist >= 0) & (dist < WINDOW) & (col >= first_key)
    return valid, dist.astype(F32)


def _spatial_gate_group(ws_ref, bs_ref, g, vn_g, rows):
    r = lax.broadcasted_iota(jnp.int32, (CHUNK, CHUNK), 0)
    c = lax.broadcasted_iota(jnp.int32, (CHUNK, CHUNK), 1)
    w = jnp.where(r >= c, ws_ref[g], 0.0).astype(BF16)
    return jnp.dot(w[:rows], vn_g, preferred_element_type=F32) + bs_ref[g][:rows]


def _even_prompt_kernel(sl_ref, z_ref, kvprev_ref, ms_ref, ws_ref, bs_ref, lng_ref, lnb_ref, o_ref):
    s_id = pl.program_id(0)
    n_blk = N_PROMPT // WINDOW

    @pl.when(s_id >= n_blk)
    def _():
        o_ref[...] = ms_ref[...]

    @pl.when(s_id < n_blk)
    def _():
        c = s_id % (SEQ // WINDOW)
        kv_cur = z_ref[:, OFF_K:OFF_K + 2 * A_KV_WIDTH]
        band = jnp.concatenate([kvprev_ref[...], kv_cur], axis=0)
        valid, dist_f = _band_mask(WINDOW, jnp.maximum(WINDOW - c * WINDOW, 0))

        def q_pair(p):
            return z_ref[:, OFF_Q + p * 128:OFF_Q + (p + 1) * 128]

        for p, a in _swa_pairs(q_pair, band[:, :A_KV_WIDTH], band[:, A_KV_WIDTH:], valid, dist_f, sl_ref):
            ga = z_ref[:, OFF_GA + p * 128:OFF_GA + (p + 1) * 128].astype(F32)
            o_ref[:, p * 128:(p + 1) * 128] = (_silu(ga) * a).astype(o_ref.dtype)

        vn = _standardize(z_ref[:, OFF_VB:OFF_VB + B_WIDTH].astype(F32)) * lng_ref[...] + lnb_ref[...]
        for g in range(B_GROUPS):
            lo, hi = g * B_GROUP_DIM, (g + 1) * B_GROUP_DIM
            mixed = _spatial_gate_group(ws_ref, bs_ref, g, vn[:, lo:hi].astype(BF16), CHUNK)
            u = z_ref[:, OFF_U + lo:OFF_U + hi].astype(F32)
            gb = z_ref[:, OFF_GB + lo:OFF_GB + hi].astype(F32)
            o_ref[:, A_WIDTH + lo:A_WIDTH + hi] = (_silu(gb) * (u * mixed)).astype(o_ref.dtype)


def _even_prompt(z, mixed_sample, sl, w_s, b_s, lng, lnb):
    n_blk = N_PROMPT // WINDOW
    n_copy = N_SAMPLE // WINDOW
    kv_blk = OFF_K // (2 * A_KV_WIDTH)
    return pl.pallas_call(
        _even_prompt_kernel,
        grid=(n_blk + n_copy,),
        in_specs=[pl.BlockSpec(memory_space=pltpu.SMEM),
                  pl.BlockSpec((WINDOW, EVEN_IN), lambda s: (jnp.minimum(s, n_blk - 1), 0)),
                  pl.BlockSpec((WINDOW, 2 * A_KV_WIDTH),
                               lambda s: (jnp.maximum(jnp.minimum(s, n_blk - 1) - 1, 0), kv_blk)),
                  pl.BlockSpec((WINDOW, D_MODEL), lambda s: (jnp.maximum(s - n_blk, 0), 0)),
                  pl.BlockSpec((B_GROUPS, CHUNK, CHUNK), lambda s: (0, 0, 0)),
                  pl.BlockSpec((B_GROUPS, CHUNK, 1), lambda s: (0, 0, 0)),
                  pl.BlockSpec((1, B_WIDTH), lambda s: (0, 0)),
                  pl.BlockSpec((1, B_WIDTH), lambda s: (0, 0))],
        out_specs=pl.BlockSpec((WINDOW, D_MODEL), lambda s: (s, 0)),
        out_shape=jax.ShapeDtypeStruct((N_ROWS, D_MODEL), BF16),
        compiler_params=_cp(1),
        name="even_prompt",
    )(sl, z, z, mixed_sample, w_s, b_s, lng, lnb)


SEQ_PER_STEP = 2
ROWS_PER_STEP = SEQ_PER_STEP * DEC_SEQ
assert DEC_SEQ & (DEC_SEQ - 1) == 0


def _seq_of(row_idx):
    return lax.shift_right_logical(row_idx, DEC_SEQ.bit_length() - 1)


def _even_sample_kernel(sl_ref, z_ref, ck_ref, cv_ref, ws_ref, bs_ref, lng_ref, lnb_ref,
                        o_ref, wk_ref, wv_ref, cvo_ref):
    valid, dist_f = _band_mask(ROWS_PER_STEP, 0)
    pad_rows = jnp.zeros((ROWS_PER_STEP - DEC_SEQ, EVEN_IN), F32)
    attn = [[None] * SEQ_PER_STEP for _ in range(A_HEADS // 2)]
    gate = [[None] * SEQ_PER_STEP for _ in range(B_GROUPS)]
    for r in range(SEQ_PER_STEP):
        r0, r1 = r * DEC_SEQ, (r + 1) * DEC_SEQ
        zr = z_ref[r0:r1, :]
        k_new = zr[:, OFF_K:OFF_K + A_KV_WIDTH]
        v_new = zr[:, OFF_V:OFF_V + A_KV_WIDTH]
        ck = ck_ref[r]
        cv = cv_ref[r]
        wk_ref[r, 0:WINDOW - DEC_SEQ, :] = ck[DEC_SEQ:, :]
        wk_ref[r, WINDOW - DEC_SEQ:, :] = k_new
        wv_ref[r, 0:WINDOW - DEC_SEQ, :] = cv[DEC_SEQ:, :]
        wv_ref[r, WINDOW - DEC_SEQ:, :] = v_new
        tail = jnp.zeros((WINDOW - DEC_SEQ, A_KV_WIDTH), F32)
        kband = jnp.concatenate([ck, k_new, tail], axis=0)
        vband = jnp.concatenate([cv, v_new, tail], axis=0)
        zq = jnp.concatenate([zr, pad_rows], axis=0)

        def q_pair(p, zq=zq):
            return zq[:, OFF_Q + p * 128:OFF_Q + (p + 1) * 128].astype(BF16)

        for p, a in _swa_pairs(q_pair, kband, vband, valid, dist_f, sl_ref):
            ga = zr[:, OFF_GA + p * 128:OFF_GA + (p + 1) * 128]
            attn[p][r] = _silu(ga) * a[:DEC_SEQ]

        vn = _standardize(zr[:, OFF_VB:OFF_VB + B_WIDTH]) * lng_ref[...] + lnb_ref[...]
        cvo_ref[r0:r1, :] = vn
        vn_pad = jnp.concatenate([vn, jnp.zeros((CHUNK - DEC_SEQ, B_WIDTH), F32)], axis=0).astype(BF16)
        for g in range(B_GROUPS):
            lo, hi = g * B_GROUP_DIM, (g + 1) * B_GROUP_DIM
            mixed = _spatial_gate_group(ws_ref, bs_ref, g, vn_pad[:, lo:hi], ROWS_PER_STEP)[:DEC_SEQ]
            u = zr[:, OFF_U + lo:OFF_U + hi]
            gb = zr[:, OFF_GB + lo:OFF_GB + hi]
            gate[g][r] = _silu(gb) * (u * mixed)

    for p in range(A_HEADS // 2):
        o_ref[:, p * 128:(p + 1) * 128] = jnp.concatenate(attn[p], axis=0).astype(o_ref.dtype)
    for g in range(B_GROUPS):
        lo, hi = A_WIDTH + g * B_GROUP_DIM, A_WIDTH + (g + 1) * B_GROUP_DIM
        o_ref[:, lo:hi] = jnp.concatenate(gate[g], axis=0).astype(o_ref.dtype)


def _even_sample(zs, ck, cv, sl, w_s, b_s, lng, lnb):
    n_steps = DEC_BATCH // SEQ_PER_STEP
    win = jax.ShapeDtypeStruct((DEC_BATCH, WINDOW, A_KV_WIDTH), F32)
    return pl.pallas_call(
        _even_sample_kernel,
        grid=(n_steps,),
        in_specs=[pl.BlockSpec(memory_space=pltpu.SMEM),
                  pl.BlockSpec((ROWS_PER_STEP, EVEN_IN), lambda i: (i, 0)),
                  pl.BlockSpec((SEQ_PER_STEP, WINDOW, A_KV_WIDTH), lambda i: (i, 0, 0)),
                  pl.BlockSpec((SEQ_PER_STEP, WINDOW, A_KV_WIDTH), lambda i: (i, 0, 0)),
                  pl.BlockSpec((B_GROUPS, CHUNK, CHUNK), lambda i: (0, 0, 0)),
                  pl.BlockSpec((B_GROUPS, CHUNK, 1), lambda i: (0, 0, 0)),
                  pl.BlockSpec((1, B_WIDTH), lambda i: (0, 0)),
                  pl.BlockSpec((1, B_WIDTH), lambda i: (0, 0))],
        out_specs=[pl.BlockSpec((ROWS_PER_STEP, D_MODEL), lambda i: (i, 0)),
                   pl.BlockSpec((SEQ_PER_STEP, WINDOW, A_KV_WIDTH), lambda i: (i, 0, 0)),
                   pl.BlockSpec((SEQ_PER_STEP, WINDOW, A_KV_WIDTH), lambda i: (i, 0, 0)),
                   pl.BlockSpec((ROWS_PER_STEP, B_WIDTH), lambda i: (i, 0))],
        out_shape=[jax.ShapeDtypeStruct((N_SAMPLE, D_MODEL), BF16), win, win,
                   jax.ShapeDtypeStruct((N_SAMPLE, B_WIDTH), F32)],
        compiler_params=_cp(1),
        name="even_sample",
    )(sl, zs, ck, cv, w_s, b_s, lng, lnb)


RET_ROWS = 256
RET_BLKS_PER_SEQ = SEQ // RET_ROWS
RET_PROMPT_BLKS = N_PROMPT // RET_ROWS


def _gated_groupnorm(o, g):
    return (_silu(g.astype(F32)) * _standardize(o)).astype(BF16)


def _ret_prompt_kernel(lg_ref, q_ref, k_ref, v_ref, g_ref, os_ref, o_ref, so_ref, st_ref):
    h = pl.program_id(0)
    bc = pl.program_id(1)

    @pl.when(bc >= RET_PROMPT_BLKS)
    def _():
        o_ref[...] = os_ref[...]

    @pl.when(bc < RET_PROMPT_BLKS)
    def _():
        c = bc % RET_BLKS_PER_SEQ
        lg = lg_ref[h]

        @pl.when(c == 0)
        def _():
            st_ref[...] = jnp.zeros_like(st_ref)

        t = R_CHUNK
        row = lax.broadcasted_iota(jnp.int32, (t, t), 0)
        col = lax.broadcasted_iota(jnp.int32, (t, t), 1)
        rel = (row - col).astype(F32)
        k_scale = R_KEY_DIM ** -0.5
        dmat = jnp.where(rel >= 0, jnp.exp(jnp.maximum(rel, 0.0) * lg), 0.0) * k_scale
        ti = lax.broadcasted_iota(jnp.int32, (t, 1), 0).astype(F32)
        q_decay = jnp.exp((ti + 1.0) * lg)
        k_decay = jnp.exp((t - 1.0 - ti) * lg) * k_scale
        chunk_decay = jnp.exp(jnp.full((1, R_VAL_DIM), float(t), F32) * lg)

        for ch in range(RET_ROWS // t):
            rows = slice(ch * t, (ch + 1) * t)
            q = q_ref[rows, :]
            k = k_ref[rows, :]
            v = v_ref[rows, :]
            st = st_ref[...]
            scores = _dot_nt(q, k) * dmat
            inner = jnp.dot(scores.astype(BF16), v, preferred_element_type=F32)
            cross = jnp.dot(q, st.astype(BF16), preferred_element_type=F32) * q_decay
            kd_t = (k.astype(F32) * k_decay).T.astype(BF16)
            st_ref[...] = chunk_decay * st + jnp.dot(kd_t, v, preferred_element_type=F32)
            o_ref[rows, :] = _gated_groupnorm(inner + cross, g_ref[rows, :])

        @pl.when(c == RET_BLKS_PER_SEQ - 1)
        def _():
            so_ref[0, 0] = st_ref[...]


def _ret_prompt(z, o_sample, log_g):
    nb = RET_PROMPT_BLKS
    kcol = R_K_WIDTH // R_KEY_DIM
    vcol = 2 * R_K_WIDTH // R_VAL_DIM
    gcol = vcol + R_HEADS

    def rb(bc):
        return jnp.minimum(bc, nb - 1)

    return pl.pallas_call(
        _ret_prompt_kernel,
        grid=(R_HEADS, nb + 1),
        in_specs=[pl.BlockSpec(memory_space=pltpu.SMEM),
                  pl.BlockSpec((RET_ROWS, R_KEY_DIM), lambda h, bc: (rb(bc), h)),
                  pl.BlockSpec((RET_ROWS, R_KEY_DIM), lambda h, bc: (rb(bc), kcol + h)),
                  pl.BlockSpec((RET_ROWS, R_VAL_DIM), lambda h, bc: (rb(bc), vcol + h)),
                  pl.BlockSpec((RET_ROWS, R_VAL_DIM), lambda h, bc: (rb(bc), gcol + h)),
                  pl.BlockSpec((N_SAMPLE, R_VAL_DIM), lambda h, bc: (0, h))],
        out_specs=[pl.BlockSpec((RET_ROWS, R_VAL_DIM), lambda h, bc: (bc, h)),
                   pl.BlockSpec((1, 1, R_KEY_DIM, R_VAL_DIM),
                                lambda h, bc: (rb(bc) // RET_BLKS_PER_SEQ, h, 0, 0))],
        out_shape=[jax.ShapeDtypeStruct((N_ROWS, R_V_WIDTH), BF16),
                   jax.ShapeDtypeStruct((BATCH, R_HEADS, R_KEY_DIM, R_VAL_DIM), F32)],
        scratch_shapes=[pltpu.VMEM((R_KEY_DIM, R_VAL_DIM), F32)],
        compiler_params=_cp(2),
        name="ret_prompt",
    )(log_g, z, z, z, z, o_sample)


def _ret_sample_kernel(lg_ref, q_ref, k_ref, v_ref, g_ref, s_ref, o_ref, so_ref):
    h = pl.program_id(1)
    lg = lg_ref[h]
    n = ROWS_PER_STEP
    t = DEC_SEQ
    row = lax.broadcasted_iota(jnp.int32, (n, n), 0)
    col = lax.broadcasted_iota(jnp.int32, (n, n), 1)
    rel = (row - col).astype(F32)
    same = _seq_of(row) == _seq_of(col)
    k_scale = R_KEY_DIM ** -0.5
    dmat = jnp.where(same & (rel >= 0), jnp.exp(jnp.maximum(rel, 0.0) * lg), 0.0) * k_scale
    ri = lax.broadcasted_iota(jnp.int32, (n, 1), 0)
    ti = (ri & (t - 1)).astype(F32)
    q_decay = jnp.exp((ti + 1.0) * lg)
    k_decay = jnp.exp((t - 1.0 - ti) * lg) * k_scale
    step_decay = jnp.exp(jnp.full((1, R_VAL_DIM), float(t), F32) * lg)

    q = q_ref[...]
    k = k_ref[...]
    v = v_ref[...]
    scores = _dot_nt(q, k) * dmat
    o = jnp.dot(scores.astype(BF16), v, preferred_element_type=F32)
    kd = k.astype(F32) * k_decay
    k_pad = jnp.zeros((R_CHUNK - n, R_KEY_DIM), F32)
    v_pad = jnp.concatenate([v.astype(F32), jnp.zeros((R_CHUNK - n, R_VAL_DIM), F32)], axis=0).astype(BF16)
    for r in range(SEQ_PER_STEP):
        mine = _seq_of(ri) == r
        st = s_ref[r, 0]
        cross = jnp.dot(q, st.astype(BF16), preferred_element_type=F32) * q_decay
        o = o + jnp.where(mine, cross, 0.0)
        kd_t = jnp.concatenate([jnp.where(mine, kd, 0.0), k_pad], axis=0).T.astype(BF16)
        so_ref[r, 0] = step_decay * st + jnp.dot(kd_t, v_pad, preferred_element_type=F32)
    o_ref[...] = _gated_groupnorm(o, g_ref[...])


def _ret_sample(z, state, log_g):
    n_steps = DEC_BATCH // SEQ_PER_STEP
    rb0 = N_PROMPT // ROWS_PER_STEP
    kcol = R_K_WIDTH // R_KEY_DIM
    vcol = 2 * R_K_WIDTH // R_VAL_DIM
    gcol = vcol + R_HEADS
    st_spec = pl.BlockSpec((SEQ_PER_STEP, 1, R_KEY_DIM, R_VAL_DIM), lambda i, h: (i, h, 0, 0))
    return pl.pallas_call(
        _ret_sample_kernel,
        grid=(n_steps, R_HEADS),
        in_specs=[pl.BlockSpec(memory_space=pltpu.SMEM),
                  pl.BlockSpec((ROWS_PER_STEP, R_KEY_DIM), lambda i, h: (rb0 + i, h)),
                  pl.BlockSpec((ROWS_PER_STEP, R_KEY_DIM), lambda i, h: (rb0 + i, kcol + h)),
                  pl.BlockSpec((ROWS_PER_STEP, R_VAL_DIM), lambda i, h: (rb0 + i, vcol + h)),
                  pl.BlockSpec((ROWS_PER_STEP, R_VAL_DIM), lambda i, h: (rb0 + i, gcol + h)),
                  st_spec],
        out_specs=[pl.BlockSpec((ROWS_PER_STEP, R_VAL_DIM), lambda i, h: (i, h)), st_spec],
        out_shape=[jax.ShapeDtypeStruct((N_SAMPLE, R_V_WIDTH), BF16),
                   jax.ShapeDtypeStruct(state.shape, F32)],
        compiler_params=_cp(2),
        name="ret_sample",
    )(log_g, z, z, z, z, state)


XQ_ROWS = 256
XQ_PROMPT_BLKS = N_PROMPT // XQ_ROWS
XQ_BLKS_PER_SEQ = SEQ // XQ_ROWS


def _xattn_heads(q, mk, mv):
    outs = []
    for h in range(X_HEADS):
        sl = slice(h * X_HEAD_DIM, (h + 1) * X_HEAD_DIM)
        s = _dot_nt(q[:, sl], mk[:, sl]) * (X_HEAD_DIM ** -0.5)
        m = jnp.max(s, axis=-1, keepdims=True)
        p = jnp.exp(s - m)
        den = jnp.sum(p, axis=-1, keepdims=True)
        outs.append(jnp.dot(p.astype(BF16), mv[:, sl], preferred_element_type=F32) * (1.0 / den))
    return outs


def _xattn_prompt_kernel(q_ref, mk_ref, mv_ref, os_ref, o_ref):
    s_id = pl.program_id(0)

    @pl.when(s_id >= XQ_PROMPT_BLKS)
    def _():
        o_ref[...] = os_ref[...]

    @pl.when(s_id < XQ_PROMPT_BLKS)
    def _():
        outs = _xattn_heads(q_ref[...], mk_ref[...].astype(BF16), mv_ref[...].astype(BF16))
        for h, o in enumerate(outs):
            o_ref[:, h * X_HEAD_DIM:(h + 1) * X_HEAD_DIM] = o.astype(o_ref.dtype)


def _xattn_prompt(q, mk, mv, o_sample):
    nb = XQ_PROMPT_BLKS

    def rb(s):
        return jnp.minimum(s, nb - 1)

    return pl.pallas_call(
        _xattn_prompt_kernel,
        grid=(nb + 1,),
        in_specs=[pl.BlockSpec((XQ_ROWS, X_WIDTH), lambda s: (rb(s), 0)),
                  pl.BlockSpec((MEM_LEN, X_WIDTH), lambda s: (rb(s) // XQ_BLKS_PER_SEQ, 0)),
                  pl.BlockSpec((MEM_LEN, X_WIDTH), lambda s: (rb(s) // XQ_BLKS_PER_SEQ, 0)),
                  pl.BlockSpec((N_SAMPLE, X_WIDTH), lambda s: (0, 0))],
        out_specs=pl.BlockSpec((XQ_ROWS, X_WIDTH), lambda s: (s, 0)),
        out_shape=jax.ShapeDtypeStruct((N_ROWS, X_WIDTH), BF16),
        compiler_params=_cp(1),
        name="xattn_prompt",
    )(q, mk, mv, o_sample)


def _xattn_sample_kernel(q_ref, ck_ref, cv_ref, o_ref):
    q = q_ref[...]
    ri = lax.broadcasted_iota(jnp.int32, (ROWS_PER_STEP, 1), 0)
    acc = [jnp.zeros((ROWS_PER_STEP, X_HEAD_DIM), F32) for _ in range(X_HEADS)]
    for r in range(SEQ_PER_STEP):
        mine = _seq_of(ri) == r
        outs = _xattn_heads(q, ck_ref[r].astype(BF16), cv_ref[r].astype(BF16))
        acc = [a + jnp.where(mine, o, 0.0) for a, o in zip(acc, outs)]
    for h, a in enumerate(acc):
        o_ref[:, h * X_HEAD_DIM:(h + 1) * X_HEAD_DIM] = a.astype(o_ref.dtype)


def _xattn_sample(q, ck, cv):
    n_steps = DEC_BATCH // SEQ_PER_STEP
    rb0 = N_PROMPT // ROWS_PER_STEP
    return pl.pallas_call(
        _xattn_sample_kernel,
        grid=(n_steps,),
        in_specs=[pl.BlockSpec((ROWS_PER_STEP, X_WIDTH), lambda i: (rb0 + i, 0)),
                  pl.BlockSpec((SEQ_PER_STEP, MEM_LEN, X_WIDTH), lambda i: (i, 0, 0)),
                  pl.BlockSpec((SEQ_PER_STEP, MEM_LEN, X_WIDTH), lambda i: (i, 0, 0))],
        out_specs=pl.BlockSpec((ROWS_PER_STEP, X_WIDTH), lambda i: (i, 0)),
        out_shape=jax.ShapeDtypeStruct((N_SAMPLE, X_WIDTH), BF16),
        compiler_params=_cp(1),
        name="xattn_sample",
    )(q, ck, cv)


def kernel(x_prompt, x_sample, cache_win_k, cache_win_v, state_ret, cache_mem_k, cache_mem_v, mem_prompt,
           norm_mix, w_in_even, attn_sinks, w_spatial, b_spatial, norm_v_g, norm_v_b, w_out_even,
           w_in_ret, w_out_ret, norm_cross, norm_mem, w_xq, w_xk, w_xv, w_xo, norm_final):
    n_attn = w_in_even.shape[0]
    x = jnp.concatenate([x_prompt.reshape(N_PROMPT, D_MODEL), x_sample.reshape(N_SAMPLE, D_MODEL)], axis=0)

    w_in_even_b = w_in_even.astype(BF16)
    w_out_even_b = w_out_even.astype(BF16)
    w_in_ret_b = w_in_ret.astype(BF16)
    w_out_ret_b = w_out_ret.astype(BF16)
    w_xq_b = w_xq.astype(BF16)
    w_xo_b = w_xo.astype(BF16)

    slopes = jnp.exp2(-8.0 * (jnp.arange(A_HEADS, dtype=F32) + 1.0) / A_HEADS)
    log_g = jnp.log1p(-jnp.exp2(-5.0 - jnp.arange(R_HEADS, dtype=F32)))

    mk_all, mv_all = _memkv(mem_prompt.reshape(BATCH * MEM_LEN, D_MODEL), norm_mem, w_xk, w_xv)

    xn = _rmsnorm(x, norm_mix[0], BF16, 768)
    wkp, wvp, rsp, wks, wvs, rss, cvs = [], [], [], [], [], [], []
    for layer in range(DEPTH):
        i = layer // 2
        if layer % 2 == 0:
            sl = jnp.stack([slopes, attn_sinks[i].astype(F32)])
            b_s = b_spatial[i].reshape(B_GROUPS, CHUNK, 1)
            lng = norm_v_g[i].reshape(1, B_WIDTH)
            lnb = norm_v_b[i].reshape(1, B_WIDTH)
            z = _mm(xn, w_in_even_b[i], BF16, 1024, 768, n_row_blks=N_PROMPT // 1024, name="in_even")
            zs = _mm(xn, w_in_even_b[i], F32, N_SAMPLE, 768, row_blk0=N_PROMPT // N_SAMPLE, n_row_blks=1,
                     name="in_even_sample")
            xn_tail = jnp.concatenate([xn[SEQ - WINDOW:SEQ], xn[2 * SEQ - WINDOW:2 * SEQ]], axis=0)
            kv_tail = _mm(xn_tail, w_in_even_b[i][:, OFF_K:OFF_K + 2 * A_KV_WIDTH], F32, 2 * WINDOW,
                          2 * A_KV_WIDTH, name="in_even_tail")
            wkp.append(kv_tail[:, :A_KV_WIDTH].reshape(BATCH, WINDOW, A_KV_HEADS, A_HEAD_DIM))
            wvp.append(kv_tail[:, A_KV_WIDTH:].reshape(BATCH, WINDOW, A_KV_HEADS, A_HEAD_DIM))
            ck = cache_win_k[i].reshape(DEC_BATCH, WINDOW, A_KV_WIDTH)
            cv = cache_win_v[i].reshape(DEC_BATCH, WINDOW, A_KV_WIDTH)
            mixed_s, wk_new, wv_new, cv_new = _even_sample(zs, ck, cv, sl, w_spatial[i], b_s, lng, lnb)
            wks.append(wk_new.reshape(DEC_BATCH, WINDOW, A_KV_HEADS, A_HEAD_DIM))
            wvs.append(wv_new.reshape(DEC_BATCH, WINDOW, A_KV_HEADS, A_HEAD_DIM))
            cvs.append(cv_new.reshape(DEC_BATCH, DEC_SEQ, B_WIDTH))
            mixed = _even_prompt(z, mixed_s, sl, w_spatial[i], b_s, lng, lnb)
            x, xn = _mm_res_norm(mixed, w_out_even_b[i], x, norm_cross[layer], BF16, 768, 1024)
        else:
            z = _mm(xn, w_in_ret_b[i], BF16, 768, 1024, name="in_ret")
            o_s, st_s = _ret_sample(z, state_ret[i], log_g)
            rss.append(st_s)
            o_all, st_p = _ret_prompt(z, o_s, log_g)
            rsp.append(st_p)
            x, xn = _mm_res_norm(o_all, w_out_ret_b[i], x, norm_cross[layer], BF16, 768, 1024)
        q = _mm(xn, w_xq_b[layer], BF16, 768, X_WIDTH, name="xq")
        o_s = _xattn_sample(q, cache_mem_k[layer].reshape(DEC_BATCH, MEM_LEN, X_WIDTH),
                            cache_mem_v[layer].reshape(DEC_BATCH, MEM_LEN, X_WIDTH))
        o_all = _xattn_prompt(q, mk_all[layer], mv_all[layer], o_s)
        if layer + 1 < DEPTH:
            x, xn = _mm_res_norm(o_all, w_xo_b[layer], x, norm_mix[layer + 1], BF16, 768, X_WIDTH)
        else:
            x, y = _mm_res_norm(o_all, w_xo_b[layer], x, norm_final, F32, 768, X_WIDTH)

    y_prompt = y[:N_PROMPT].reshape(BATCH, SEQ, D_MODEL)
    y_sample = y[N_PROMPT:].reshape(DEC_BATCH, DEC_SEQ, D_MODEL)
    mem_shape = (DEPTH, BATCH, MEM_LEN, X_HEADS, X_HEAD_DIM)
    return (y_prompt, y_sample,
            jnp.stack(wkp), jnp.stack(wvp), jnp.stack(rsp),
            mk_all.reshape(mem_shape), mv_all.reshape(mem_shape),
            jnp.stack(wks), jnp.stack(wvs), jnp.stack(rss), jnp.stack(cvs))
```

```python
import functools

import jax
import jax.numpy as jnp
from jax import lax
from jax.experimental import pallas as pl
from jax.experimental.pallas import tpu as pltpu

F32 = jnp.float32
BF16 = jnp.bfloat16

D_MODEL = 2048
BATCH = 2
SEQ = 4096
DEPTH = 4
DEC_BATCH = 32
DEC_SEQ = 8
A_HEADS = 16
A_KV_HEADS = 2
A_HEAD_DIM = 64
A_WIDTH = 1024
A_KV_WIDTH = 128
WINDOW = 128
CHUNK = 128
B_GROUPS = 4
B_WIDTH = 1024
B_GROUP_DIM = 256
EVEN_IN = 5376
R_HEADS = 8
R_KEY_DIM = 256
R_VAL_DIM = 512
R_K_WIDTH = 2048
R_V_WIDTH = 4096
R_CHUNK = 128
RET_IN = 12288
MEM_LEN = 256
X_HEADS = 4
X_HEAD_DIM = 128
X_WIDTH = 512
EPS = 1e-6
NEG_INF = -1e30

N_PROMPT = BATCH * SEQ
N_SAMPLE = DEC_BATCH * DEC_SEQ
N_ROWS = N_PROMPT + N_SAMPLE

OFF_Q, OFF_K, OFF_V, OFF_GA, OFF_U, OFF_VB, OFF_GB = 0, 1024, 1152, 1280, 2304, 3328, 4352

VMEM_LIMIT = 52 * 1024 * 1024


def _cp(n_axes):
    return pltpu.CompilerParams(dimension_semantics=("arbitrary",) * n_axes,
                                vmem_limit_bytes=VMEM_LIMIT)


def _silu(x):
    return x * (1.0 / (1.0 + jnp.exp(-x)))


def _rms(x, g):
    ms = jnp.mean(x * x, axis=-1, keepdims=True)
    return x * lax.rsqrt(ms + EPS) * g


def _standardize(x):
    xc = x - jnp.mean(x, axis=-1, keepdims=True)
    return xc * lax.rsqrt(jnp.mean(xc * xc, axis=-1, keepdims=True) + EPS)


def _dot_nt(a, b):
    return lax.dot_general(a, b, (((1,), (1,)), ((), ())), preferred_element_type=F32)


def _rmsnorm_kernel(x_ref, g_ref, o_ref):
    o_ref[...] = _rms(x_ref[...], g_ref[...]).astype(o_ref.dtype)


def _rmsnorm(x, g, out_dtype, tm):
    m, d = x.shape
    return pl.pallas_call(
        _rmsnorm_kernel,
        grid=(m // tm,),
        in_specs=[pl.BlockSpec((tm, d), lambda i: (i, 0)),
                  pl.BlockSpec((1, d), lambda i: (0, 0))],
        out_specs=pl.BlockSpec((tm, d), lambda i: (i, 0)),
        out_shape=jax.ShapeDtypeStruct((m, d), out_dtype),
        compiler_params=_cp(1),
        name="rmsnorm",
    )(x, g.reshape(1, d))


def _mm_wcast_kernel(a_ref, w_ref, o_ref, wb_ref):
    @pl.when(pl.program_id(1) == 0)
    def _():
        wb_ref[...] = w_ref[0].astype(BF16)

    o_ref[...] = jnp.dot(a_ref[...], wb_ref[...], preferred_element_type=F32).astype(o_ref.dtype)


def _mm_wcast(a, w_stack, layer, out_dtype, tm, tn, name):
    m, k = a.shape
    n = w_stack.shape[2]
    return pl.pallas_call(
        _mm_wcast_kernel,
        grid=(n // tn, m // tm),
        in_specs=[pl.BlockSpec((tm, k), lambda j, i: (i, 0)),
                  pl.BlockSpec((1, k, tn), lambda j, i: (layer, 0, j))],
        out_specs=pl.BlockSpec((tm, tn), lambda j, i: (i, j)),
        out_shape=jax.ShapeDtypeStruct((m, n), out_dtype),
        scratch_shapes=[pltpu.VMEM((k, tn), BF16)],
        compiler_params=_cp(2),
        name=name,
    )(a, w_stack)


IN_EVEN_TM = 1024
IN_EVEN_TN = 768


def _in_even_kernel(n_prompt_blks, a_ref, as_ref, w_ref, z_ref, zs_ref, wb_ref):
    m = pl.program_id(1)

    @pl.when(m == 0)
    def _():
        wb_ref[...] = w_ref[0].astype(BF16)

    @pl.when(m < n_prompt_blks)
    def _():
        z_ref[...] = jnp.dot(a_ref[...], wb_ref[...], preferred_element_type=F32).astype(z_ref.dtype)

    @pl.when(m == n_prompt_blks)
    def _():
        zs_ref[...] = jnp.dot(as_ref[...], wb_ref[...], preferred_element_type=F32)


def _in_even(xn, w_stack, layer):
    k = xn.shape[1]
    n = w_stack.shape[2]
    tm, tn = IN_EVEN_TM, IN_EVEN_TN
    nb = N_PROMPT // tm
    return pl.pallas_call(
        functools.partial(_in_even_kernel, nb),
        grid=(n // tn, nb + 1),
        in_specs=[pl.BlockSpec((tm, k), lambda j, m: (jnp.minimum(m, nb - 1), 0)),
                  pl.BlockSpec((N_SAMPLE, k), lambda j, m: (N_PROMPT // N_SAMPLE, 0)),
                  pl.BlockSpec((1, k, tn), lambda j, m: (layer, 0, j))],
        out_specs=[pl.BlockSpec((tm, tn), lambda j, m: (jnp.minimum(m, nb - 1), j)),
                   pl.BlockSpec((N_SAMPLE, tn), lambda j, m: (0, j))],
        out_shape=[jax.ShapeDtypeStruct((N_PROMPT, n), BF16),
                   jax.ShapeDtypeStruct((N_SAMPLE, n), F32)],
        scratch_shapes=[pltpu.VMEM((k, tn), BF16)],
        compiler_params=_cp(2),
        name="in_even",
    )(xn, xn, w_stack)


def _kv_tail_kernel(a_ref, w_ref, o_ref):
    o_ref[...] = jnp.dot(a_ref[...], w_ref[0].astype(BF16), preferred_element_type=F32)


def _kv_tail(xn, w_stack, layer):
    k = xn.shape[1]
    blks_per_seq = SEQ // WINDOW
    width = 2 * A_KV_WIDTH
    return pl.pallas_call(
        _kv_tail_kernel,
        grid=(BATCH,),
        in_specs=[pl.BlockSpec((WINDOW, k), lambda b: (blks_per_seq * (b + 1) - 1, 0)),
                  pl.BlockSpec((1, k, width), lambda b: (layer, 0, OFF_K // width))],
        out_specs=pl.BlockSpec((WINDOW, width), lambda b: (b, 0)),
        out_shape=jax.ShapeDtypeStruct((BATCH * WINDOW, width), F32),
        compiler_params=_cp(1),
        name="kv_tail",
    )(xn, w_stack)


def _mm_res_norm_kernel(nk, a_ref, w_ref, x_ref, g_ref, xo_ref, n_ref):
    k = pl.program_id(1)

    @pl.when(k == 0)
    def _():
        xo_ref[...] = x_ref[...]

    xo_ref[...] += jnp.dot(a_ref[...], w_ref[0], preferred_element_type=F32)

    @pl.when(k == nk - 1)
    def _():
        n_ref[...] = _rms(xo_ref[...], g_ref[...]).astype(n_ref.dtype)


def _mm_res_norm(a, w_stack, layer, x, g, norm_dtype, tm, tk):
    m, kdim = a.shape
    d = w_stack.shape[2]
    nk = kdim // tk
    return pl.pallas_call(
        functools.partial(_mm_res_norm_kernel, nk),
        grid=(m // tm, nk),
        in_specs=[pl.BlockSpec((tm, tk), lambda i, k: (i, k)),
                  pl.BlockSpec((1, tk, d), lambda i, k: (layer, k, 0)),
                  pl.BlockSpec((tm, d), lambda i, k: (i, 0)),
                  pl.BlockSpec((1, d), lambda i, k: (0, 0))],
        out_specs=[pl.BlockSpec((tm, d), lambda i, k: (i, 0)),
                   pl.BlockSpec((tm, d), lambda i, k: (i, 0))],
        out_shape=[jax.ShapeDtypeStruct((m, d), F32),
                   jax.ShapeDtypeStruct((m, d), norm_dtype)],
        compiler_params=_cp(2),
        name="mm_res_norm",
    )(a, w_stack, x, g.reshape(1, d))


def _memkv_kernel(mem_ref, g_ref, wk_ref, wv_ref, mk_ref, mv_ref):
    h = _rms(mem_ref[...], g_ref[0]).astype(BF16)
    mk_ref[0] = jnp.dot(h, wk_ref[0].astype(BF16), preferred_element_type=F32)
    mv_ref[0] = jnp.dot(h, wv_ref[0].astype(BF16), preferred_element_type=F32)


def _memkv(mem, norm_mem, w_xk, w_xv):
    m = mem.shape[0]
    out = jax.ShapeDtypeStruct((DEPTH, m, X_WIDTH), F32)
    return pl.pallas_call(
        _memkv_kernel,
        grid=(DEPTH,),
        in_specs=[pl.BlockSpec((m, D_MODEL), lambda l: (0, 0)),
                  pl.BlockSpec((1, 1, D_MODEL), lambda l: (l, 0, 0)),
                  pl.BlockSpec((1, D_MODEL, X_WIDTH), lambda l: (l, 0, 0)),
                  pl.BlockSpec((1, D_MODEL, X_WIDTH), lambda l: (l, 0, 0))],
        out_specs=[pl.BlockSpec((1, m, X_WIDTH), lambda l: (l, 0, 0)),
                   pl.BlockSpec((1, m, X_WIDTH), lambda l: (l, 0, 0))],
        out_shape=[out, out],
        compiler_params=_cp(1),
        name="memkv",
    )(mem, norm_mem.reshape(DEPTH, 1, D_MODEL), w_xk, w_xv)


def _expand_band(band):
    x = band.astype(F32)
    xs = pltpu.roll(x, 64, 1)
    lo = lax.broadcasted_iota(jnp.int32, x.shape, 1) < 64
    z = jnp.zeros_like(x)
    e0 = jnp.concatenate([jnp.where(lo, x, z), jnp.where(lo, z, xs)], axis=0).astype(BF16)
    e1 = jnp.concatenate([jnp.where(lo, xs, z), jnp.where(lo, z, x)], axis=0).astype(BF16)
    return e0, e1


def _swa_pairs(q_pair_fn, kband, vband, valid, dist_f, sl_ref):
    kk = _expand_band(kband)
    vv = _expand_band(vband)
    t = dist_f.shape[0]
    lo = lax.broadcasted_iota(jnp.int32, (t, 128), 1) < 64
    for p in range(A_HEADS // 2):
        j = p // (A_HEADS // A_KV_HEADS // 2)
        s = _dot_nt(q_pair_fn(p), kk[j])
        probs, invs = [], []
        for half in range(2):
            h = 2 * p + half
            slope = sl_ref[0, h]
            sink = sl_ref[1, h]
            sh = s[:, half * 256:(half + 1) * 256] * (A_HEAD_DIM ** -0.5) - slope * dist_f
            sh = jnp.where(valid, sh, NEG_INF)
            m = jnp.maximum(jnp.max(sh, axis=-1, keepdims=True), sink)
            pe = jnp.exp(sh - m)
            den = jnp.sum(pe, axis=-1, keepdims=True) + jnp.exp(sink - m)
            probs.append(pe)
            invs.append(1.0 / den)
        pp = jnp.concatenate(probs, axis=1).astype(BF16)
        o = jnp.dot(pp, vv[j], preferred_element_type=F32)
        yield p, o * jnp.where(lo, invs[0], invs[1])


def _band_mask(t, first_key):
    row = lax.broadcasted_iota(jnp.int32, (t, 2 * WINDOW), 0)
    col = lax.broadcasted_iota(jnp.int32, (t, 2 * WINDOW), 1)
    dist = row + WINDOW - col
    valid = (dist >= 0) & (dist < WINDOW) & (col >= first_key)
    return valid, dist.astype(F32)


def _spatial_gate_group(ws_ref, bs_ref, g, vn_g, rows):
    r = lax.broadcasted_iota(jnp.int32, (CHUNK, CHUNK), 0)
    c = lax.broadcasted_iota(jnp.int32, (CHUNK, CHUNK), 1)
    w = jnp.where(r >= c, ws_ref[g], 0.0).astype(BF16)
    return jnp.dot(w[:rows], vn_g, preferred_element_type=F32) + bs_ref[g][:rows]


def _even_prompt_kernel(sl_ref, z_ref, kvprev_ref, ms_ref, ws_ref, bs_ref, lng_ref, lnb_ref, o_ref):
    s_id = pl.program_id(0)
    n_blk = N_PROMPT // WINDOW

    @pl.when(s_id >= n_blk)
    def _():
        o_ref[...] = ms_ref[...]

    @pl.when(s_id < n_blk)
    def _():
        c = s_id % (SEQ // WINDOW)
        kv_cur = z_ref[:, OFF_K:OFF_K + 2 * A_KV_WIDTH]
        band = jnp.concatenate([kvprev_ref[...], kv_cur], axis=0)
        valid, dist_f = _band_mask(WINDOW, jnp.maximum(WINDOW - c * WINDOW, 0))

        def q_pair(p):
            return z_ref[:, OFF_Q + p * 128:OFF_Q + (p + 1) * 128]

        for p, a in _swa_pairs(q_pair, band[:, :A_KV_WIDTH], band[:, A_KV_WIDTH:], valid, dist_f, sl_ref):
            ga = z_ref[:, OFF_GA + p * 128:OFF_GA + (p + 1) * 128].astype(F32)
            o_ref[:, p * 128:(p + 1) * 128] = (_silu(ga) * a).astype(o_ref.dtype)

        vn = _standardize(z_ref[:, OFF_VB:OFF_VB + B_WIDTH].astype(F32)) * lng_ref[...] + lnb_ref[...]
        for g in range(B_GROUPS):
            lo, hi = g * B_GROUP_DIM, (g + 1) * B_GROUP_DIM
            mixed = _spatial_gate_group(ws_ref, bs_ref, g, vn[:, lo:hi].astype(BF16), CHUNK)
            u = z_ref[:, OFF_U + lo:OFF_U + hi].astype(F32)
            gb = z_ref[:, OFF_GB + lo:OFF_GB + hi].astype(F32)
            o_ref[:, A_WIDTH + lo:A_WIDTH + hi] = (_silu(gb) * (u * mixed)).astype(o_ref.dtype)


def _even_prompt(z, mixed_sample, sl, w_s, b_s, lng, lnb):
    n_blk = N_PROMPT // WINDOW
    n_copy = N_SAMPLE // WINDOW
    kv_blk = OFF_K // (2 * A_KV_WIDTH)
    return pl.pallas_call(
        _even_prompt_kernel,
        grid=(n_blk + n_copy,),
        in_specs=[pl.BlockSpec(memory_space=pltpu.SMEM),
                  pl.BlockSpec((WINDOW, EVEN_IN), lambda s: (jnp.minimum(s, n_blk - 1), 0)),
                  pl.BlockSpec((WINDOW, 2 * A_KV_WIDTH),
                               lambda s: (jnp.maximum(jnp.minimum(s, n_blk - 1) - 1, 0), kv_blk)),
                  pl.BlockSpec((WINDOW, D_MODEL), lambda s: (jnp.maximum(s - n_blk, 0), 0)),
                  pl.BlockSpec((B_GROUPS, CHUNK, CHUNK), lambda s: (0, 0, 0)),
                  pl.BlockSpec((B_GROUPS, CHUNK, 1), lambda s: (0, 0, 0)),
                  pl.BlockSpec((1, B_WIDTH), lambda s: (0, 0)),
                  pl.BlockSpec((1, B_WIDTH), lambda s: (0, 0))],
        out_specs=pl.BlockSpec((WINDOW, D_MODEL), lambda s: (s, 0)),
        out_shape=jax.ShapeDtypeStruct((N_ROWS, D_MODEL), BF16),
        compiler_params=_cp(1),
        name="even_prompt",
    )(sl, z, z, mixed_sample, w_s, b_s, lng, lnb)


SEQ_PER_STEP = 2
ROWS_PER_STEP = SEQ_PER_STEP * DEC_SEQ
assert DEC_SEQ & (DEC_SEQ - 1) == 0


def _seq_of(row_idx):
    return lax.shift_right_logical(row_idx, DEC_SEQ.bit_length() - 1)


def _even_sample_kernel(sl_ref, z_ref, ck_ref, cv_ref, ws_ref, bs_ref, lng_ref, lnb_ref,
                        o_ref, wk_ref, wv_ref, cvo_ref):
    valid, dist_f = _band_mask(ROWS_PER_STEP, 0)
    pad_rows = jnp.zeros((ROWS_PER_STEP - DEC_SEQ, EVEN_IN), F32)
    attn = [[None] * SEQ_PER_STEP for _ in range(A_HEADS // 2)]
    gate = [[None] * SEQ_PER_STEP for _ in range(B_GROUPS)]
    for r in range(SEQ_PER_STEP):
        r0, r1 = r * DEC_SEQ, (r + 1) * DEC_SEQ
        zr = z_ref[r0:r1, :]
        k_new = zr[:, OFF_K:OFF_K + A_KV_WIDTH]
        v_new = zr[:, OFF_V:OFF_V + A_KV_WIDTH]
        ck = ck_ref[r]
        cv = cv_ref[r]
        wk_ref[r, 0:WINDOW - DEC_SEQ, :] = ck[DEC_SEQ:, :]
        wk_ref[r, WINDOW - DEC_SEQ:, :] = k_new
        wv_ref[r, 0:WINDOW - DEC_SEQ, :] = cv[DEC_SEQ:, :]
        wv_ref[r, WINDOW - DEC_SEQ:, :] = v_new
        tail = jnp.zeros((WINDOW - DEC_SEQ, A_KV_WIDTH), F32)
        kband = jnp.concatenate([ck, k_new, tail], axis=0)
        vband = jnp.concatenate([cv, v_new, tail], axis=0)
        zq = jnp.concatenate([zr, pad_rows], axis=0)

        def q_pair(p, zq=zq):
            return zq[:, OFF_Q + p * 128:OFF_Q + (p + 1) * 128].astype(BF16)

        for p, a in _swa_pairs(q_pair, kband, vband, valid, dist_f, sl_ref):
            ga = zr[:, OFF_GA + p * 128:OFF_GA + (p + 1) * 128]
            attn[p][r] = _silu(ga) * a[:DEC_SEQ]

        vn = _standardize(zr[:, OFF_VB:OFF_VB + B_WIDTH]) * lng_ref[...] + lnb_ref[...]
        cvo_ref[r0:r1, :] = vn
        vn_pad = jnp.concatenate([vn, jnp.zeros((CHUNK - DEC_SEQ, B_WIDTH), F32)], axis=0).astype(BF16)
        for g in range(B_GROUPS):
            lo, hi = g * B_GROUP_DIM, (g + 1) * B_GROUP_DIM
            mixed = _spatial_gate_group(ws_ref, bs_ref, g, vn_pad[:, lo:hi], ROWS_PER_STEP)[:DEC_SEQ]
            u = zr[:, OFF_U + lo:OFF_U + hi]
            gb = zr[:, OFF_GB + lo:OFF_GB + hi]
            gate[g][r] = _silu(gb) * (u * mixed)

    for p in range(A_HEADS // 2):
        o_ref[:, p * 128:(p + 1) * 128] = jnp.concatenate(attn[p], axis=0).astype(o_ref.dtype)
    for g in range(B_GROUPS):
        lo, hi = A_WIDTH + g * B_GROUP_DIM, A_WIDTH + (g + 1) * B_GROUP_DIM
        o_ref[:, lo:hi] = jnp.concatenate(gate[g], axis=0).astype(o_ref.dtype)


def _even_sample(zs, ck, cv, sl, w_s, b_s, lng, lnb):
    n_steps = DEC_BATCH // SEQ_PER_STEP
    win = jax.ShapeDtypeStruct((DEC_BATCH, WINDOW, A_KV_WIDTH), F32)
    return pl.pallas_call(
        _even_sample_kernel,
        grid=(n_steps,),
        in_specs=[pl.BlockSpec(memory_space=pltpu.SMEM),
                  pl.BlockSpec((ROWS_PER_STEP, EVEN_IN), lambda i: (i, 0)),
                  pl.BlockSpec((SEQ_PER_STEP, WINDOW, A_KV_WIDTH), lambda i: (i, 0, 0)),
                  pl.BlockSpec((SEQ_PER_STEP, WINDOW, A_KV_WIDTH), lambda i: (i, 0, 0)),
                  pl.BlockSpec((B_GROUPS, CHUNK, CHUNK), lambda i: (0, 0, 0)),
                  pl.BlockSpec((B_GROUPS, CHUNK, 1), lambda i: (0, 0, 0)),
                  pl.BlockSpec((1, B_WIDTH), lambda i: (0, 0)),
                  pl.BlockSpec((1, B_WIDTH), lambda i: (0, 0))],
        out_specs=[pl.BlockSpec((ROWS_PER_STEP, D_MODEL), lambda i: (i, 0)),
                   pl.BlockSpec((SEQ_PER_STEP, WINDOW, A_KV_WIDTH), lambda i: (i, 0, 0)),
                   pl.BlockSpec((SEQ_PER_STEP, WINDOW, A_KV_WIDTH), lambda i: (i, 0, 0)),
                   pl.BlockSpec((ROWS_PER_STEP, B_WIDTH), lambda i: (i, 0))],
        out_shape=[jax.ShapeDtypeStruct((N_SAMPLE, D_MODEL), BF16), win, win,
                   jax.ShapeDtypeStruct((N_SAMPLE, B_WIDTH), F32)],
        compiler_params=_cp(1),
        name="even_sample",
    )(sl, zs, ck, cv, w_s, b_s, lng, lnb)


RET_ROWS = 256
RET_BLKS_PER_SEQ = SEQ // RET_ROWS
RET_PROMPT_BLKS = N_PROMPT // RET_ROWS


def _gated_groupnorm(o, g):
    return (_silu(g.astype(F32)) * _standardize(o)).astype(BF16)


def _ret_prompt_kernel(lg_ref, q_ref, k_ref, v_ref, g_ref, os_ref, o_ref, so_ref, st_ref):
    h = pl.program_id(0)
    bc = pl.program_id(1)

    @pl.when(bc >= RET_PROMPT_BLKS)
    def _():
        o_ref[...] = os_ref[...]

    @pl.when(bc < RET_PROMPT_BLKS)
    def _():
        c = bc % RET_BLKS_PER_SEQ
        lg = lg_ref[h]

        @pl.when(c == 0)
        def _():
            st_ref[...] = jnp.zeros_like(st_ref)

        t = R_CHUNK
        row = lax.broadcasted_iota(jnp.int32, (t, t), 0)
        col = lax.broadcasted_iota(jnp.int32, (t, t), 1)
        rel = (row - col).astype(F32)
        k_scale = R_KEY_DIM ** -0.5
        dmat = jnp.where(rel >= 0, jnp.exp(jnp.maximum(rel, 0.0) * lg), 0.0) * k_scale
        ti = lax.broadcasted_iota(jnp.int32, (t, 1), 0).astype(F32)
        q_decay = jnp.exp((ti + 1.0) * lg)
        k_decay = jnp.exp((t - 1.0 - ti) * lg) * k_scale
        chunk_decay = jnp.exp(jnp.full((1, R_VAL_DIM), float(t), F32) * lg)

        for ch in range(RET_ROWS // t):
            rows = slice(ch * t, (ch + 1) * t)
            q = q_ref[rows, :]
            k = k_ref[rows, :]
            v = v_ref[rows, :]
            st = st_ref[...]
            scores = _dot_nt(q, k) * dmat
            inner = jnp.dot(scores.astype(BF16), v, preferred_element_type=F32)
            cross = jnp.dot(q, st.astype(BF16), preferred_element_type=F32) * q_decay
            kd_t = (k.astype(F32) * k_decay).T.astype(BF16)
            st_ref[...] = chunk_decay * st + jnp.dot(kd_t, v, preferred_element_type=F32)
            o_ref[rows, :] = _gated_groupnorm(inner + cross, g_ref[rows, :])

        @pl.when(c == RET_BLKS_PER_SEQ - 1)
        def _():
            so_ref[0, 0] = st_ref[...]


def _ret_prompt(z, o_sample, log_g):
    nb = RET_PROMPT_BLKS
    kcol = R_K_WIDTH // R_KEY_DIM
    vcol = 2 * R_K_WIDTH // R_VAL_DIM
    gcol = vcol + R_HEADS

    def rb(bc):
        return jnp.minimum(bc, nb - 1)

    return pl.pallas_call(
        _ret_prompt_kernel,
        grid=(R_HEADS, nb + 1),
        in_specs=[pl.BlockSpec(memory_space=pltpu.SMEM),
                  pl.BlockSpec((RET_ROWS, R_KEY_DIM), lambda h, bc: (rb(bc), h)),
                  pl.BlockSpec((RET_ROWS, R_KEY_DIM), lambda h, bc: (rb(bc), kcol + h)),
                  pl.BlockSpec((RET_ROWS, R_VAL_DIM), lambda h, bc: (rb(bc), vcol + h)),
                  pl.BlockSpec((RET_ROWS, R_VAL_DIM), lambda h, bc: (rb(bc), gcol + h)),
                  pl.BlockSpec((N_SAMPLE, R_VAL_DIM), lambda h, bc: (0, h))],
        out_specs=[pl.BlockSpec((RET_ROWS, R_VAL_DIM), lambda h, bc: (bc, h)),
                   pl.BlockSpec((1, 1, R_KEY_DIM, R_VAL_DIM),
                                lambda h, bc: (rb(bc) // RET_BLKS_PER_SEQ, h, 0, 0))],
        out_shape=[jax.ShapeDtypeStruct((N_ROWS, R_V_WIDTH), BF16),
                   jax.ShapeDtypeStruct((BATCH, R_HEADS, R_KEY_DIM, R_VAL_DIM), F32)],
        scratch_shapes=[pltpu.VMEM((R_KEY_DIM, R_VAL_DIM), F32)],
        compiler_params=_cp(2),
        name="ret_prompt",
    )(log_g, z, z, z, z, o_sample)


def _ret_sample_kernel(lg_ref, q_ref, k_ref, v_ref, g_ref, s_ref, *rest):
    o_ref, so_ref = rest[-2:]
    h = pl.program_id(1)
    lg = lg_ref[h]
    n = ROWS_PER_STEP
    t = DEC_SEQ
    row = lax.broadcasted_iota(jnp.int32, (n, n), 0)
    col = lax.broadcasted_iota(jnp.int32, (n, n), 1)
    rel = (row - col).astype(F32)
    same = _seq_of(row) == _seq_of(col)
    k_scale = R_KEY_DIM ** -0.5
    dmat = jnp.where(same & (rel >= 0), jnp.exp(jnp.maximum(rel, 0.0) * lg), 0.0) * k_scale
    ri = lax.broadcasted_iota(jnp.int32, (n, 1), 0)
    ti = (ri & (t - 1)).astype(F32)
    q_decay = jnp.exp((ti + 1.0) * lg)
    k_decay = jnp.exp((t - 1.0 - ti) * lg) * k_scale
    step_decay = jnp.exp(jnp.full((1, R_VAL_DIM), float(t), F32) * lg)

    q = q_ref[...]
    k = k_ref[...]
    v = v_ref[...]
    scores = _dot_nt(q, k) * dmat
    o = jnp.dot(scores.astype(BF16), v, preferred_element_type=F32)
    kd = k.astype(F32) * k_decay
    k_pad = jnp.zeros((R_CHUNK - n, R_KEY_DIM), F32)
    v_pad = jnp.concatenate([v.astype(F32), jnp.zeros((R_CHUNK - n, R_VAL_DIM), F32)], axis=0).astype(BF16)
    for r in range(SEQ_PER_STEP):
        mine = _seq_of(ri) == r
        st = s_ref[0, r, 0]
        cross = jnp.dot(q, st.astype(BF16), preferred_element_type=F32) * q_decay
        o = o + jnp.where(mine, cross, 0.0)
        kd_t = jnp.concatenate([jnp.where(mine, kd, 0.0), k_pad], axis=0).T.astype(BF16)
        so_ref[0, r, 0] = step_decay * st + jnp.dot(kd_t, v_pad, preferred_element_type=F32)
    o_ref[...] = _gated_groupnorm(o, g_ref[...])


def _ret_sample(z, state_all, idx, log_g, new_states):
    n_steps = DEC_BATCH // SEQ_PER_STEP
    rb0 = N_PROMPT // ROWS_PER_STEP
    kcol = R_K_WIDTH // R_KEY_DIM
    vcol = 2 * R_K_WIDTH // R_VAL_DIM
    gcol = vcol + R_HEADS
    st_spec = pl.BlockSpec((1, SEQ_PER_STEP, 1, R_KEY_DIM, R_VAL_DIM), lambda i, h: (idx, i, h, 0, 0))
    in_specs = [pl.BlockSpec(memory_space=pltpu.SMEM),
                pl.BlockSpec((ROWS_PER_STEP, R_KEY_DIM), lambda i, h: (rb0 + i, h)),
                pl.BlockSpec((ROWS_PER_STEP, R_KEY_DIM), lambda i, h: (rb0 + i, kcol + h)),
                pl.BlockSpec((ROWS_PER_STEP, R_VAL_DIM), lambda i, h: (rb0 + i, vcol + h)),
                pl.BlockSpec((ROWS_PER_STEP, R_VAL_DIM), lambda i, h: (rb0 + i, gcol + h)),
                st_spec]
    args = [log_g, z, z, z, z, state_all]
    aliases = {}
    if new_states is not None:
        in_specs.append(pl.BlockSpec(memory_space=pl.ANY))
        args.append(new_states)
        aliases = {len(args) - 1: 1}
    return pl.pallas_call(
        _ret_sample_kernel,
        grid=(n_steps, R_HEADS),
        in_specs=in_specs,
        out_specs=[pl.BlockSpec((ROWS_PER_STEP, R_VAL_DIM), lambda i, h: (i, h)), st_spec],
        out_shape=[jax.ShapeDtypeStruct((N_SAMPLE, R_V_WIDTH), BF16),
                   jax.ShapeDtypeStruct(state_all.shape, F32)],
        input_output_aliases=aliases,
        compiler_params=_cp(2),
        name="ret_sample",
    )(*args)


XQ_ROWS = 256
XQ_PROMPT_BLKS = N_PROMPT // XQ_ROWS
XQ_BLKS_PER_SEQ = SEQ // XQ_ROWS


def _xattn_heads(q, mk_head, mv_head):
    outs = []
    for h in range(X_HEADS):
        s = _dot_nt(q[:, h * X_HEAD_DIM:(h + 1) * X_HEAD_DIM], mk_head(h)) * (X_HEAD_DIM ** -0.5)
        m = jnp.max(s, axis=-1, keepdims=True)
        p = jnp.exp(s - m)
        den = jnp.sum(p, axis=-1, keepdims=True)
        outs.append(jnp.dot(p.astype(BF16), mv_head(h), preferred_element_type=F32) * (1.0 / den))
    return outs


def _xattn_prompt_kernel(q_ref, mk_ref, mv_ref, os_ref, o_ref):
    s_id = pl.program_id(0)

    @pl.when(s_id >= XQ_PROMPT_BLKS)
    def _():
        o_ref[...] = os_ref[...]

    @pl.when(s_id < XQ_PROMPT_BLKS)
    def _():
        def head_of(ref):
            return lambda h: ref[0, :, h * X_HEAD_DIM:(h + 1) * X_HEAD_DIM].astype(BF16)

        outs = _xattn_heads(q_ref[...], head_of(mk_ref), head_of(mv_ref))
        for h, o in enumerate(outs):
            o_ref[:, h * X_HEAD_DIM:(h + 1) * X_HEAD_DIM] = o.astype(o_ref.dtype)


def _xattn_prompt(q, mk_all, mv_all, layer, o_sample):
    nb = XQ_PROMPT_BLKS

    def rb(s):
        return jnp.minimum(s, nb - 1)

    mem_spec = pl.BlockSpec((1, MEM_LEN, X_WIDTH), lambda s: (layer, rb(s) // XQ_BLKS_PER_SEQ, 0))
    return pl.pallas_call(
        _xattn_prompt_kernel,
        grid=(nb + 1,),
        in_specs=[pl.BlockSpec((XQ_ROWS, X_WIDTH), lambda s: (rb(s), 0)),
                  mem_spec, mem_spec,
                  pl.BlockSpec((N_SAMPLE, X_WIDTH), lambda s: (0, 0))],
        out_specs=pl.BlockSpec((XQ_ROWS, X_WIDTH), lambda s: (s, 0)),
        out_shape=jax.ShapeDtypeStruct((N_ROWS, X_WIDTH), BF16),
        compiler_params=_cp(1),
        name="xattn_prompt",
    )(q, mk_all, mv_all, o_sample)


def _xattn_sample_kernel(q_ref, ck_ref, cv_ref, o_ref):
    q = q_ref[...]
    ri = lax.broadcasted_iota(jnp.int32, (ROWS_PER_STEP, 1), 0)
    acc = [jnp.zeros((ROWS_PER_STEP, X_HEAD_DIM), F32) for _ in range(X_HEADS)]
    for r in range(SEQ_PER_STEP):
        mine = _seq_of(ri) == r

        def head_of(ref, r=r):
            return lambda h: ref[0, r, :, h, :].astype(BF16)

        outs = _xattn_heads(q, head_of(ck_ref), head_of(cv_ref))
        acc = [a + jnp.where(mine, o, 0.0) for a, o in zip(acc, outs)]
    for h, a in enumerate(acc):
        o_ref[:, h * X_HEAD_DIM:(h + 1) * X_HEAD_DIM] = a.astype(o_ref.dtype)


def _xattn_sample(q, ck_all, cv_all, layer):
    n_steps = DEC_BATCH // SEQ_PER_STEP
    rb0 = N_PROMPT // ROWS_PER_STEP
    mem_spec = pl.BlockSpec((1, SEQ_PER_STEP, MEM_LEN, X_HEADS, X_HEAD_DIM), lambda i: (layer, i, 0, 0, 0))
    return pl.pallas_call(
        _xattn_sample_kernel,
        grid=(n_steps,),
        in_specs=[pl.BlockSpec((ROWS_PER_STEP, X_WIDTH), lambda i: (rb0 + i, 0)), mem_spec, mem_spec],
        out_specs=pl.BlockSpec((ROWS_PER_STEP, X_WIDTH), lambda i: (i, 0)),
        out_shape=jax.ShapeDtypeStruct((N_SAMPLE, X_WIDTH), BF16),
        compiler_params=_cp(1),
        name="xattn_sample",
    )(q, ck_all, cv_all)


def kernel(x_prompt, x_sample, cache_win_k, cache_win_v, state_ret, cache_mem_k, cache_mem_v, mem_prompt,
           norm_mix, w_in_even, attn_sinks, w_spatial, b_spatial, norm_v_g, norm_v_b, w_out_even,
           w_in_ret, w_out_ret, norm_cross, norm_mem, w_xq, w_xk, w_xv, w_xo, norm_final):
    x = jnp.concatenate([x_prompt.reshape(N_PROMPT, D_MODEL), x_sample.reshape(N_SAMPLE, D_MODEL)], axis=0)

    w_out_even_b = w_out_even.astype(BF16)
    w_out_ret_b = w_out_ret.astype(BF16)
    w_xo_b = w_xo.astype(BF16)

    slopes = jnp.exp2(-8.0 * (jnp.arange(A_HEADS, dtype=F32) + 1.0) / A_HEADS)
    log_g = jnp.log1p(-jnp.exp2(-5.0 - jnp.arange(R_HEADS, dtype=F32)))

    mk_all, mv_all = _memkv(mem_prompt.reshape(BATCH * MEM_LEN, D_MODEL), norm_mem, w_xk, w_xv)

    xn = _rmsnorm(x, norm_mix[0], BF16, 768)
    wkp, wvp, rsp, wks, wvs, cvs = [], [], [], [], [], []
    rss = None
    for layer in range(DEPTH):
        i = layer // 2
        if layer % 2 == 0:
            sl = jnp.stack([slopes, attn_sinks[i].astype(F32)])
            b_s = b_spatial[i].reshape(B_GROUPS, CHUNK, 1)
            lng = norm_v_g[i].reshape(1, B_WIDTH)
            lnb = norm_v_b[i].reshape(1, B_WIDTH)
            z, zs = _in_even(xn, w_in_even, i)
            kv_tail = _kv_tail(xn, w_in_even, i)
            wkp.append(kv_tail[:, :A_KV_WIDTH].reshape(BATCH, WINDOW, A_KV_HEADS, A_HEAD_DIM))
            wvp.append(kv_tail[:, A_KV_WIDTH:].reshape(BATCH, WINDOW, A_KV_HEADS, A_HEAD_DIM))
            ck = cache_win_k[i].reshape(DEC_BATCH, WINDOW, A_KV_WIDTH)
            cv = cache_win_v[i].reshape(DEC_BATCH, WINDOW, A_KV_WIDTH)
            mixed_s, wk_new, wv_new, cv_new = _even_sample(zs, ck, cv, sl, w_spatial[i], b_s, lng, lnb)
            wks.append(wk_new.reshape(DEC_BATCH, WINDOW, A_KV_HEADS, A_HEAD_DIM))
            wvs.append(wv_new.reshape(DEC_BATCH, WINDOW, A_KV_HEADS, A_HEAD_DIM))
            cvs.append(cv_new.reshape(DEC_BATCH, DEC_SEQ, B_WIDTH))
            mixed = _even_prompt(z, mixed_s, sl, w_spatial[i], b_s, lng, lnb)
            x, xn = _mm_res_norm(mixed, w_out_even_b, i, x, norm_cross[layer], BF16, 768, 1024)
        else:
            z = _mm_wcast(xn, w_in_ret, i, BF16, 768, 1024, "in_ret")
            o_s, rss = _ret_sample(z, state_ret, i, log_g, rss)
            o_all, st_p = _ret_prompt(z, o_s, log_g)
            rsp.append(st_p)
            x, xn = _mm_res_norm(o_all, w_out_ret_b, i, x, norm_cross[layer], BF16, 768, 1024)
        q = _mm_wcast(xn, w_xq, layer, BF16, 768, X_WIDTH, "xq")
        o_s = _xattn_sample(q, cache_mem_k, cache_mem_v, layer)
        o_all = _xattn_prompt(q, mk_all, mv_all, layer, o_s)
        if layer + 1 < DEPTH:
            x, xn = _mm_res_norm(o_all, w_xo_b, layer, x, norm_mix[layer + 1], BF16, 768, X_WIDTH)
        else:
            x, y = _mm_res_norm(o_all, w_xo_b, layer, x, norm_final, F32, 768, X_WIDTH)

    y_prompt = y[:N_PROMPT].reshape(BATCH, SEQ, D_MODEL)
    y_sample = y[N_PROMPT:].reshape(DEC_BATCH, DEC_SEQ, D_MODEL)
    mem_shape = (DEPTH, BATCH, MEM_LEN, X_HEADS, X_HEAD_DIM)
    return (y_prompt, y_sample,
            jnp.stack(wkp), jnp.stack(wvp), jnp.stack(rsp),
            mk_all.reshape(mem_shape), mv_all.reshape(mem_shape),
            jnp.stack(wks), jnp.stack(wvs), rss, jnp.stack(cvs))
```

```python
import functools

import jax
import jax.numpy as jnp
from jax import lax
from jax.experimental import pallas as pl
from jax.experimental.pallas import tpu as pltpu

F32 = jnp.float32
BF16 = jnp.bfloat16

D_MODEL = 2048
BATCH = 2
SEQ = 4096
DEPTH = 4
DEC_BATCH = 32
DEC_SEQ = 8
A_HEADS = 16
A_KV_HEADS = 2
A_HEAD_DIM = 64
A_WIDTH = 1024
A_KV_WIDTH = 128
WINDOW = 128
CHUNK = 128
B_GROUPS = 4
B_WIDTH = 1024
B_GROUP_DIM = 256
EVEN_IN = 5376
R_HEADS = 8
R_KEY_DIM = 256
R_VAL_DIM = 512
R_K_WIDTH = 2048
R_V_WIDTH = 4096
R_CHUNK = 128
RET_IN = 12288
MEM_LEN = 256
X_HEADS = 4
X_HEAD_DIM = 128
X_WIDTH = 512
EPS = 1e-6

N_PROMPT = BATCH * SEQ
N_SAMPLE = DEC_BATCH * DEC_SEQ
N_ROWS = N_PROMPT + N_SAMPLE

OFF_Q, OFF_K, OFF_V, OFF_GA, OFF_U, OFF_VB, OFF_GB = 0, 1024, 1152, 1280, 2304, 3328, 4352

VMEM_LIMIT = 52 * 1024 * 1024


def _cp(n_axes):
    return pltpu.CompilerParams(dimension_semantics=("arbitrary",) * n_axes,
                                vmem_limit_bytes=VMEM_LIMIT)


def _silu(x):
    return x * (1.0 / (1.0 + jnp.exp(-x)))


def _rms(x, g):
    ms = jnp.mean(x * x, axis=-1, keepdims=True)
    return x * lax.rsqrt(ms + EPS) * g


def _standardize(x):
    xc = x - jnp.mean(x, axis=-1, keepdims=True)
    return xc * lax.rsqrt(jnp.mean(xc * xc, axis=-1, keepdims=True) + EPS)


def _dot_nt(a, b):
    return lax.dot_general(a, b, (((1,), (1,)), ((), ())), preferred_element_type=F32)


def _mm_wcast_kernel(a_ref, w_ref, o_ref, wb_ref):
    @pl.when(pl.program_id(1) == 0)
    def _():
        wb_ref[...] = w_ref[0].astype(BF16)

    o_ref[...] = jnp.dot(a_ref[...], wb_ref[...], preferred_element_type=F32).astype(o_ref.dtype)


def _mm_wcast(a, w_stack, layer, out_dtype, tm, tn, name):
    m, k = a.shape
    n = w_stack.shape[2]
    return pl.pallas_call(
        _mm_wcast_kernel,
        grid=(n // tn, m // tm),
        in_specs=[pl.BlockSpec((tm, k), lambda j, i: (i, 0)),
                  pl.BlockSpec((1, k, tn), lambda j, i: (layer, 0, j))],
        out_specs=pl.BlockSpec((tm, tn), lambda j, i: (i, j)),
        out_shape=jax.ShapeDtypeStruct((m, n), out_dtype),
        scratch_shapes=[pltpu.VMEM((k, tn), BF16)],
        compiler_params=_cp(2),
        name=name,
    )(a, w_stack)


IN_EVEN_TM = 1024
IN_EVEN_TN = 768


def _in_even_kernel(n_prompt_blks, a_ref, as_ref, w_ref, z_ref, zs_ref, wb_ref):
    m = pl.program_id(1)

    @pl.when(m == 0)
    def _():
        wb_ref[...] = w_ref[0].astype(BF16)

    @pl.when(m < n_prompt_blks)
    def _():
        z_ref[...] = jnp.dot(a_ref[...], wb_ref[...], preferred_element_type=F32).astype(z_ref.dtype)

    @pl.when(m == n_prompt_blks)
    def _():
        zs_ref[...] = jnp.dot(as_ref[...], wb_ref[...], preferred_element_type=F32)


def _in_even(xn, w_stack, layer):
    k = xn.shape[1]
    n = w_stack.shape[2]
    tm, tn = IN_EVEN_TM, IN_EVEN_TN
    nb = N_PROMPT // tm
    return pl.pallas_call(
        functools.partial(_in_even_kernel, nb),
        grid=(n // tn, nb + 1),
        in_specs=[pl.BlockSpec((tm, k), lambda j, m: (jnp.minimum(m, nb - 1), 0)),
                  pl.BlockSpec((N_SAMPLE, k), lambda j, m: (N_PROMPT // N_SAMPLE, 0)),
                  pl.BlockSpec((1, k, tn), lambda j, m: (layer, 0, j))],
        out_specs=[pl.BlockSpec((tm, tn), lambda j, m: (jnp.minimum(m, nb - 1), j)),
                   pl.BlockSpec((N_SAMPLE, tn), lambda j, m: (0, j))],
        out_shape=[jax.ShapeDtypeStruct((N_PROMPT, n), BF16),
                   jax.ShapeDtypeStruct((N_SAMPLE, n), F32)],
        scratch_shapes=[pltpu.VMEM((k, tn), BF16)],
        compiler_params=_cp(2),
        name="in_even",
    )(xn, xn, w_stack)


def _kv_tail_kernel(a_ref, w_ref, o_ref):
    o_ref[...] = jnp.dot(a_ref[...], w_ref[0].astype(BF16), preferred_element_type=F32)


def _kv_tail(xn, w_stack, layer):
    k = xn.shape[1]
    blks_per_seq = SEQ // WINDOW
    width = 2 * A_KV_WIDTH
    return pl.pallas_call(
        _kv_tail_kernel,
        grid=(BATCH,),
        in_specs=[pl.BlockSpec((WINDOW, k), lambda b: (blks_per_seq * (b + 1) - 1, 0)),
                  pl.BlockSpec((1, k, width), lambda b: (layer, 0, OFF_K // width))],
        out_specs=pl.BlockSpec((WINDOW, width), lambda b: (b, 0)),
        out_shape=jax.ShapeDtypeStruct((BATCH * WINDOW, width), F32),
        compiler_params=_cp(1),
        name="kv_tail",
    )(xn, w_stack)


def _mm_res_norm_kernel(nk, with_q, a_ref, w_ref, x_ref, g_ref, *rest):
    if with_q:
        wq_ref, xo_ref, q_ref = rest
    else:
        xo_ref, n_ref = rest
    k = pl.program_id(1)

    @pl.when(k == 0)
    def _():
        xo_ref[...] = x_ref[...]

    xo_ref[...] += jnp.dot(a_ref[...], w_ref[0], preferred_element_type=F32)

    @pl.when(k == nk - 1)
    def _():
        xn = _rms(xo_ref[...], g_ref[...])
        if with_q:
            q_ref[...] = jnp.dot(xn.astype(BF16), wq_ref[0].astype(BF16),
                                 preferred_element_type=F32).astype(q_ref.dtype)
        else:
            n_ref[...] = xn.astype(n_ref.dtype)


def _mm_res_norm(a, w_stack, layer, x, g, tm, tk, wq_stack=None, wq_layer=None):
    m, kdim = a.shape
    d = w_stack.shape[2]
    nk = kdim // tk
    with_q = wq_stack is not None
    in_specs = [pl.BlockSpec((tm, tk), lambda i, k: (i, k)),
                pl.BlockSpec((1, tk, d), lambda i, k: (layer, k, 0)),
                pl.BlockSpec((tm, d), lambda i, k: (i, 0)),
                pl.BlockSpec((1, d), lambda i, k: (0, 0))]
    args = [a, w_stack, x, g.reshape(1, d)]
    n2 = d
    if with_q:
        n2 = wq_stack.shape[2]
        in_specs.append(pl.BlockSpec((1, d, n2), lambda i, k: (wq_layer, 0, 0)))
        args.append(wq_stack)
    return pl.pallas_call(
        functools.partial(_mm_res_norm_kernel, nk, with_q),
        grid=(m // tm, nk),
        in_specs=in_specs,
        out_specs=[pl.BlockSpec((tm, d), lambda i, k: (i, 0)),
                   pl.BlockSpec((tm, n2), lambda i, k: (i, 0))],
        out_shape=[jax.ShapeDtypeStruct((m, d), F32),
                   jax.ShapeDtypeStruct((m, n2), BF16)],
        compiler_params=_cp(2),
        name="mm_res_q" if with_q else "mm_res_norm",
    )(*args)


FINAL_ROWS = 256


def _final_kernel(n_prompt_blks, a_ref, w_ref, x_ref, g_ref, yp_ref, ys_ref):
    i = pl.program_id(0)
    y = _rms(x_ref[...] + jnp.dot(a_ref[...], w_ref[0], preferred_element_type=F32), g_ref[...])

    @pl.when(i < n_prompt_blks)
    def _():
        yp_ref[...] = y

    @pl.when(i >= n_prompt_blks)
    def _():
        ys_ref[...] = y


def _final(a, w_stack, layer, x, g):
    m, kdim = a.shape
    d = w_stack.shape[2]
    tm = FINAL_ROWS
    nb = N_PROMPT // tm
    return pl.pallas_call(
        functools.partial(_final_kernel, nb),
        grid=(m // tm,),
        in_specs=[pl.BlockSpec((tm, kdim), lambda i: (i, 0)),
                  pl.BlockSpec((1, kdim, d), lambda i: (layer, 0, 0)),
                  pl.BlockSpec((tm, d), lambda i: (i, 0)),
                  pl.BlockSpec((1, d), lambda i: (0, 0))],
        out_specs=[pl.BlockSpec((tm, d), lambda i: (jnp.minimum(i, nb - 1), 0)),
                   pl.BlockSpec((tm, d), lambda i: (jnp.maximum(i - nb, 0), 0))],
        out_shape=[jax.ShapeDtypeStruct((N_PROMPT, d), F32),
                   jax.ShapeDtypeStruct((N_SAMPLE, d), F32)],
        compiler_params=_cp(1),
        name="final",
    )(a, w_stack, x, g.reshape(1, d))


def _embed_kernel(n_prompt_blks, xp_ref, xs_ref, g_ref, x_ref, n_ref):
    i = pl.program_id(0)

    @pl.when(i < n_prompt_blks)
    def _():
        x_ref[...] = xp_ref[...]

    @pl.when(i >= n_prompt_blks)
    def _():
        x_ref[...] = xs_ref[...]

    n_ref[...] = _rms(x_ref[...], g_ref[...]).astype(n_ref.dtype)


def _embed(xp, xs, g):
    d = xp.shape[1]
    tm = N_SAMPLE
    nb = N_PROMPT // tm
    return pl.pallas_call(
        functools.partial(_embed_kernel, nb),
        grid=(nb + 1,),
        in_specs=[pl.BlockSpec((tm, d), lambda i: (jnp.minimum(i, nb - 1), 0)),
                  pl.BlockSpec((tm, d), lambda i: (0, 0)),
                  pl.BlockSpec((1, d), lambda i: (0, 0))],
        out_specs=[pl.BlockSpec((tm, d), lambda i: (i, 0)),
                   pl.BlockSpec((tm, d), lambda i: (i, 0))],
        out_shape=[jax.ShapeDtypeStruct((N_ROWS, d), F32),
                   jax.ShapeDtypeStruct((N_ROWS, d), BF16)],
        compiler_params=_cp(1),
        name="embed",
    )(xp, xs, g.reshape(1, d))


def _memkv_kernel(mem_ref, g_ref, wk_ref, wv_ref, mk_ref, mv_ref):
    h = _rms(mem_ref[...], g_ref[0]).astype(BF16)
    mk_ref[0] = jnp.dot(h, wk_ref[0].astype(BF16), preferred_element_type=F32)
    mv_ref[0] = jnp.dot(h, wv_ref[0].astype(BF16), preferred_element_type=F32)


def _memkv(mem, norm_mem, w_xk, w_xv):
    m = mem.shape[0]
    out = jax.ShapeDtypeStruct((DEPTH, m, X_WIDTH), F32)
    return pl.pallas_call(
        _memkv_kernel,
        grid=(DEPTH,),
        in_specs=[pl.BlockSpec((m, D_MODEL), lambda l: (0, 0)),
                  pl.BlockSpec((1, 1, D_MODEL), lambda l: (l, 0, 0)),
                  pl.BlockSpec((1, D_MODEL, X_WIDTH), lambda l: (l, 0, 0)),
                  pl.BlockSpec((1, D_MODEL, X_WIDTH), lambda l: (l, 0, 0))],
        out_specs=[pl.BlockSpec((1, m, X_WIDTH), lambda l: (l, 0, 0)),
                   pl.BlockSpec((1, m, X_WIDTH), lambda l: (l, 0, 0))],
        out_shape=[out, out],
        compiler_params=_cp(1),
        name="memkv",
    )(mem, norm_mem.reshape(DEPTH, 1, D_MODEL), w_xk, w_xv)


SLABS_PER_KV = A_HEADS // A_KV_HEADS // 2
MASK_DIST = 1e30 * 2.0 ** 8


def _expand_band(band, scale):
    x = band.astype(F32)
    if scale != 1.0:
        x = x * scale
    xs = pltpu.roll(x, 64, 1)
    lo = lax.broadcasted_iota(jnp.int32, x.shape, 1) < 64
    z = jnp.zeros_like(x)
    e0 = jnp.concatenate([jnp.where(lo, x, z), jnp.where(lo, z, xs)], axis=0).astype(BF16)
    e1 = jnp.concatenate([jnp.where(lo, xs, z), jnp.where(lo, z, x)], axis=0).astype(BF16)
    return e0, e1


def _band_bias(t, first_key):
    row = lax.broadcasted_iota(jnp.int32, (t, 2 * WINDOW), 0)
    col = lax.broadcasted_iota(jnp.int32, (t, 2 * WINDOW), 1)
    dist = row + WINDOW - col
    valid = (dist >= 0) & (dist < WINDOW) & (col >= first_key)
    return jnp.where(valid, -dist.astype(F32), -MASK_DIST)


def _swa(q_slab, kband, vband, nbias, sl_ref):
    t = nbias.shape[0]
    rows = SLABS_PER_KV * t
    kk = _expand_band(kband, A_HEAD_DIM ** -0.5)
    vv = _expand_band(vband, 1.0)
    nb = jnp.concatenate([nbias] * SLABS_PER_KV, axis=0)
    blk = lax.shift_right_logical(lax.broadcasted_iota(jnp.int32, (rows, 1), 0), t.bit_length() - 1)
    lo = lax.broadcasted_iota(jnp.int32, (rows, 128), 1) < 64
    outs = [None] * (A_HEADS // 2)
    for j in range(A_KV_HEADS):
        slab0 = SLABS_PER_KV * j

        def per_row(table_row, half, slab0=slab0):
            col = jnp.full((rows, 1), sl_ref[table_row, 2 * (slab0 + SLABS_PER_KV - 1) + half], F32)
            for i in reversed(range(SLABS_PER_KV - 1)):
                col = jnp.where(blk == i, sl_ref[table_row, 2 * (slab0 + i) + half], col)
            return col

        q = jnp.concatenate([q_slab(slab0 + i) for i in range(SLABS_PER_KV)], axis=0)
        s = _dot_nt(q, kk[j])
        probs, invs = [], []
        for half in range(2):
            slope = per_row(0, half)
            sink = per_row(1, half)
            sh = s[:, half * 256:(half + 1) * 256] + slope * nb
            m = jnp.maximum(jnp.max(sh, axis=-1, keepdims=True), sink)
            pe = jnp.exp(sh - m)
            den = jnp.sum(pe, axis=-1, keepdims=True) + jnp.exp(sink - m)
            probs.append(pe)
            invs.append(1.0 / den)
        pp = jnp.concatenate(probs, axis=1).astype(BF16)
        o = jnp.dot(pp, vv[j], preferred_element_type=F32) * jnp.where(lo, invs[0], invs[1])
        for i in range(SLABS_PER_KV):
            outs[slab0 + i] = o[i * t:(i + 1) * t]
    return outs


def _spatial_gate_group(ws_ref, bs_ref, g, vn_g, rows):
    r = lax.broadcasted_iota(jnp.int32, (CHUNK, CHUNK), 0)
    c = lax.broadcasted_iota(jnp.int32, (CHUNK, CHUNK), 1)
    w = jnp.where(r >= c, ws_ref[g], 0.0).astype(BF16)
    return jnp.dot(w[:rows], vn_g, preferred_element_type=F32) + bs_ref[g][:rows]


def _even_prompt_kernel(sl_ref, z_ref, kvprev_ref, ms_ref, ws_ref, bs_ref, lng_ref, lnb_ref, o_ref):
    s_id = pl.program_id(0)
    n_blk = N_PROMPT // WINDOW

    @pl.when(s_id >= n_blk)
    def _():
        o_ref[...] = ms_ref[...]

    @pl.when(s_id < n_blk)
    def _():
        c = s_id % (SEQ // WINDOW)
        kv_cur = z_ref[:, OFF_K:OFF_K + 2 * A_KV_WIDTH]
        band = jnp.concatenate([kvprev_ref[...], kv_cur], axis=0)
        nbias = _band_bias(WINDOW, jnp.maximum(WINDOW - c * WINDOW, 0))

        def q_slab(p):
            return z_ref[:, OFF_Q + p * 128:OFF_Q + (p + 1) * 128]

        for p, a in enumerate(_swa(q_slab, band[:, :A_KV_WIDTH], band[:, A_KV_WIDTH:], nbias, sl_ref)):
            ga = z_ref[:, OFF_GA + p * 128:OFF_GA + (p + 1) * 128].astype(F32)
            o_ref[:, p * 128:(p + 1) * 128] = (_silu(ga) * a).astype(o_ref.dtype)

        vn = _standardize(z_ref[:, OFF_VB:OFF_VB + B_WIDTH].astype(F32)) * lng_ref[...] + lnb_ref[...]
        for g in range(B_GROUPS):
            lo, hi = g * B_GROUP_DIM, (g + 1) * B_GROUP_DIM
            mixed = _spatial_gate_group(ws_ref, bs_ref, g, vn[:, lo:hi].astype(BF16), CHUNK)
            u = z_ref[:, OFF_U + lo:OFF_U + hi].astype(F32)
            gb = z_ref[:, OFF_GB + lo:OFF_GB + hi].astype(F32)
            o_ref[:, A_WIDTH + lo:A_WIDTH + hi] = (_silu(gb) * (u * mixed)).astype(o_ref.dtype)


def _even_prompt(z, mixed_sample, sl, w_s, b_s, lng, lnb):
    n_blk = N_PROMPT // WINDOW
    n_copy = N_SAMPLE // WINDOW
    kv_blk = OFF_K // (2 * A_KV_WIDTH)
    return pl.pallas_call(
        _even_prompt_kernel,
        grid=(n_blk + n_copy,),
        in_specs=[pl.BlockSpec(memory_space=pltpu.SMEM),
                  pl.BlockSpec((WINDOW, EVEN_IN), lambda s: (jnp.minimum(s, n_blk - 1), 0)),
                  pl.BlockSpec((WINDOW, 2 * A_KV_WIDTH),
                               lambda s: (jnp.maximum(jnp.minimum(s, n_blk - 1) - 1, 0), kv_blk)),
                  pl.BlockSpec((WINDOW, D_MODEL), lambda s: (jnp.maximum(s - n_blk, 0), 0)),
                  pl.BlockSpec((B_GROUPS, CHUNK, CHUNK), lambda s: (0, 0, 0)),
                  pl.BlockSpec((B_GROUPS, CHUNK, 1), lambda s: (0, 0, 0)),
                  pl.BlockSpec((1, B_WIDTH), lambda s: (0, 0)),
                  pl.BlockSpec((1, B_WIDTH), lambda s: (0, 0))],
        out_specs=pl.BlockSpec((WINDOW, D_MODEL), lambda s: (s, 0)),
        out_shape=jax.ShapeDtypeStruct((N_ROWS, D_MODEL), BF16),
        compiler_params=_cp(1),
        name="even_prompt",
    )(sl, z, z, mixed_sample, w_s, b_s, lng, lnb)


SEQ_PER_STEP = 2
ROWS_PER_STEP = SEQ_PER_STEP * DEC_SEQ
assert DEC_SEQ & (DEC_SEQ - 1) == 0


def _seq_of(row_idx):
    return lax.shift_right_logical(row_idx, DEC_SEQ.bit_length() - 1)


def _even_sample_kernel(sl_ref, z_ref, ck_ref, cv_ref, ws_ref, bs_ref, lng_ref, lnb_ref,
                        o_ref, wk_ref, wv_ref, cvo_ref):
    nbias = _band_bias(ROWS_PER_STEP, 0)
    pad_rows = jnp.zeros((ROWS_PER_STEP - DEC_SEQ, EVEN_IN), F32)
    attn = [[None] * SEQ_PER_STEP for _ in range(A_HEADS // 2)]
    gate = [[None] * SEQ_PER_STEP for _ in range(B_GROUPS)]
    for r in range(SEQ_PER_STEP):
        r0, r1 = r * DEC_SEQ, (r + 1) * DEC_SEQ
        zr = z_ref[r0:r1, :]
        k_new = zr[:, OFF_K:OFF_K + A_KV_WIDTH]
        v_new = zr[:, OFF_V:OFF_V + A_KV_WIDTH]
        ck = ck_ref[r]
        cv = cv_ref[r]
        wk_ref[r, 0:WINDOW - DEC_SEQ, :] = ck[DEC_SEQ:, :]
        wk_ref[r, WINDOW - DEC_SEQ:, :] = k_new
        wv_ref[r, 0:WINDOW - DEC_SEQ, :] = cv[DEC_SEQ:, :]
        wv_ref[r, WINDOW - DEC_SEQ:, :] = v_new
        tail = jnp.zeros((WINDOW - DEC_SEQ, A_KV_WIDTH), F32)
        kband = jnp.concatenate([ck, k_new, tail], axis=0)
        vband = jnp.concatenate([cv, v_new, tail], axis=0)
        zq = jnp.concatenate([zr, pad_rows], axis=0)

        def q_slab(p, zq=zq):
            return zq[:, OFF_Q + p * 128:OFF_Q + (p + 1) * 128].astype(BF16)

        for p, a in enumerate(_swa(q_slab, kband, vband, nbias, sl_ref)):
            ga = zr[:, OFF_GA + p * 128:OFF_GA + (p + 1) * 128]
            attn[p][r] = _silu(ga) * a[:DEC_SEQ]

        vn = _standardize(zr[:, OFF_VB:OFF_VB + B_WIDTH]) * lng_ref[...] + lnb_ref[...]
        cvo_ref[r0:r1, :] = vn
        vn_pad = jnp.concatenate([vn, jnp.zeros((CHUNK - DEC_SEQ, B_WIDTH), F32)], axis=0).astype(BF16)
        for g in range(B_GROUPS):
            lo, hi = g * B_GROUP_DIM, (g + 1) * B_GROUP_DIM
            mixed = _spatial_gate_group(ws_ref, bs_ref, g, vn_pad[:, lo:hi], ROWS_PER_STEP)[:DEC_SEQ]
            u = zr[:, OFF_U + lo:OFF_U + hi]
            gb = zr[:, OFF_GB + lo:OFF_GB + hi]
            gate[g][r] = _silu(gb) * (u * mixed)

    for p in range(A_HEADS // 2):
        o_ref[:, p * 128:(p + 1) * 128] = jnp.concatenate(attn[p], axis=0).astype(o_ref.dtype)
    for g in range(B_GROUPS):
        lo, hi = A_WIDTH + g * B_GROUP_DIM, A_WIDTH + (g + 1) * B_GROUP_DIM
        o_ref[:, lo:hi] = jnp.concatenate(gate[g], axis=0).astype(o_ref.dtype)


def _even_sample(zs, ck, cv, sl, w_s, b_s, lng, lnb):
    n_steps = DEC_BATCH // SEQ_PER_STEP
    win = jax.ShapeDtypeStruct((DEC_BATCH, WINDOW, A_KV_WIDTH), F32)
    return pl.pallas_call(
        _even_sample_kernel,
        grid=(n_steps,),
        in_specs=[pl.BlockSpec(memory_space=pltpu.SMEM),
                  pl.BlockSpec((ROWS_PER_STEP, EVEN_IN), lambda i: (i, 0)),
                  pl.BlockSpec((SEQ_PER_STEP, WINDOW, A_KV_WIDTH), lambda i: (i, 0, 0)),
                  pl.BlockSpec((SEQ_PER_STEP, WINDOW, A_KV_WIDTH), lambda i: (i, 0, 0)),
                  pl.BlockSpec((B_GROUPS, CHUNK, CHUNK), lambda i: (0, 0, 0)),
                  pl.BlockSpec((B_GROUPS, CHUNK, 1), lambda i: (0, 0, 0)),
                  pl.BlockSpec((1, B_WIDTH), lambda i: (0, 0)),
                  pl.BlockSpec((1, B_WIDTH), lambda i: (0, 0))],
        out_specs=[pl.BlockSpec((ROWS_PER_STEP, D_MODEL), lambda i: (i, 0)),
                   pl.BlockSpec((SEQ_PER_STEP, WINDOW, A_KV_WIDTH), lambda i: (i, 0, 0)),
                   pl.BlockSpec((SEQ_PER_STEP, WINDOW, A_KV_WIDTH), lambda i: (i, 0, 0)),
                   pl.BlockSpec((ROWS_PER_STEP, B_WIDTH), lambda i: (i, 0))],
        out_shape=[jax.ShapeDtypeStruct((N_SAMPLE, D_MODEL), BF16), win, win,
                   jax.ShapeDtypeStruct((N_SAMPLE, B_WIDTH), F32)],
        compiler_params=_cp(1),
        name="even_sample",
    )(sl, zs, ck, cv, w_s, b_s, lng, lnb)


RET_ROWS = 256
RET_BLKS_PER_SEQ = SEQ // RET_ROWS
RET_PROMPT_BLKS = N_PROMPT // RET_ROWS


def _gated_groupnorm(o, g):
    return (_silu(g.astype(F32)) * _standardize(o)).astype(BF16)


RET_HEADS_PER_STEP = 2


def _ret_prompt_kernel(lg_ref, q_ref, k_ref, v_ref, g_ref, os_ref, o_ref, so_ref, st_ref, dm_ref):
    hp = pl.program_id(0)
    bc = pl.program_id(1)
    t = RET_ROWS
    k_scale = R_KEY_DIM ** -0.5

    @pl.when(bc == 0)
    def _():
        row = lax.broadcasted_iota(jnp.int32, (t, t), 0)
        col = lax.broadcasted_iota(jnp.int32, (t, t), 1)
        rel = (row - col).astype(F32)
        for i in range(RET_HEADS_PER_STEP):
            lg = lg_ref[hp * RET_HEADS_PER_STEP + i]
            dm_ref[i] = jnp.where(rel >= 0, jnp.exp(jnp.maximum(rel, 0.0) * lg), 0.0) * k_scale

    @pl.when(bc >= RET_PROMPT_BLKS)
    def _():
        o_ref[...] = os_ref[...]

    @pl.when(bc < RET_PROMPT_BLKS)
    def _():
        c = bc % RET_BLKS_PER_SEQ

        @pl.when(c == 0)
        def _():
            st_ref[...] = jnp.zeros_like(st_ref)

        ti = lax.broadcasted_iota(jnp.int32, (t, 1), 0).astype(F32)
        for i in range(RET_HEADS_PER_STEP):
            lg = lg_ref[hp * RET_HEADS_PER_STEP + i]
            q_decay = jnp.exp((ti + 1.0) * lg)
            k_decay = jnp.exp((t - 1.0 - ti) * lg) * k_scale
            chunk_decay = jnp.exp(jnp.full((1, R_VAL_DIM), float(t), F32) * lg)
            kc = slice(i * R_KEY_DIM, (i + 1) * R_KEY_DIM)
            vc = slice(i * R_VAL_DIM, (i + 1) * R_VAL_DIM)
            q = q_ref[:, kc]
            k = k_ref[:, kc]
            v = v_ref[:, vc]
            st = st_ref[i]
            scores = _dot_nt(q, k) * dm_ref[i]
            inner = jnp.dot(scores.astype(BF16), v, preferred_element_type=F32)
            cross = jnp.dot(q, st.astype(BF16), preferred_element_type=F32) * q_decay
            kd_t = (k.astype(F32) * k_decay).T.astype(BF16)
            st_ref[i] = chunk_decay * st + jnp.dot(kd_t, v, preferred_element_type=F32)
            o_ref[:, vc] = _gated_groupnorm(inner + cross, g_ref[:, vc])

        @pl.when(c == RET_BLKS_PER_SEQ - 1)
        def _():
            so_ref[0] = st_ref[...]


def _ret_prompt(z, o_sample, log_g):
    nb = RET_PROMPT_BLKS
    hps = RET_HEADS_PER_STEP
    kw, vw = hps * R_KEY_DIM, hps * R_VAL_DIM
    kcol = R_K_WIDTH // kw
    vcol = 2 * R_K_WIDTH // vw
    gcol = vcol + R_HEADS // hps

    def rb(bc):
        return jnp.minimum(bc, nb - 1)

    return pl.pallas_call(
        _ret_prompt_kernel,
        grid=(R_HEADS // hps, nb + 1),
        in_specs=[pl.BlockSpec(memory_space=pltpu.SMEM),
                  pl.BlockSpec((RET_ROWS, kw), lambda h, bc: (rb(bc), h)),
                  pl.BlockSpec((RET_ROWS, kw), lambda h, bc: (rb(bc), kcol + h)),
                  pl.BlockSpec((RET_ROWS, vw), lambda h, bc: (rb(bc), vcol + h)),
                  pl.BlockSpec((RET_ROWS, vw), lambda h, bc: (rb(bc), gcol + h)),
                  pl.BlockSpec((N_SAMPLE, vw), lambda h, bc: (0, h))],
        out_specs=[pl.BlockSpec((RET_ROWS, vw), lambda h, bc: (bc, h)),
                   pl.BlockSpec((1, hps, R_KEY_DIM, R_VAL_DIM),
                                lambda h, bc: (rb(bc) // RET_BLKS_PER_SEQ, h, 0, 0))],
        out_shape=[jax.ShapeDtypeStruct((N_ROWS, R_V_WIDTH), BF16),
                   jax.ShapeDtypeStruct((BATCH, R_HEADS, R_KEY_DIM, R_VAL_DIM), F32)],
        scratch_shapes=[pltpu.VMEM((hps, R_KEY_DIM, R_VAL_DIM), F32), pltpu.VMEM((hps, RET_ROWS, RET_ROWS), F32)],
        compiler_params=_cp(2),
        name="ret_prompt",
    )(log_g, z, z, z, z, o_sample)


RET_SAMPLE_SEQS = 4
RET_SAMPLE_ROWS = RET_SAMPLE_SEQS * DEC_SEQ


def _ret_sample_kernel(lg_ref, q_ref, k_ref, v_ref, g_ref, s_ref, *rest):
    o_ref, so_ref = rest[-2:]
    h = pl.program_id(1)
    lg = lg_ref[h]
    n = RET_SAMPLE_ROWS
    t = DEC_SEQ
    row = lax.broadcasted_iota(jnp.int32, (n, n), 0)
    col = lax.broadcasted_iota(jnp.int32, (n, n), 1)
    rel = (row - col).astype(F32)
    same = _seq_of(row) == _seq_of(col)
    k_scale = R_KEY_DIM ** -0.5
    dmat = jnp.where(same & (rel >= 0), jnp.exp(jnp.maximum(rel, 0.0) * lg), 0.0) * k_scale
    ri = lax.broadcasted_iota(jnp.int32, (n, 1), 0)
    ti = (ri & (t - 1)).astype(F32)
    q_decay = jnp.exp((ti + 1.0) * lg)
    k_decay = jnp.exp((t - 1.0 - ti) * lg) * k_scale
    step_decay = jnp.exp(jnp.full((1, R_VAL_DIM), float(t), F32) * lg)

    q = q_ref[...]
    k = k_ref[...]
    v = v_ref[...]
    scores = _dot_nt(q, k) * dmat
    o = jnp.dot(scores.astype(BF16), v, preferred_element_type=F32)
    kd = k.astype(F32) * k_decay
    k_pad = jnp.zeros((R_CHUNK - n, R_KEY_DIM), F32)
    v_pad = jnp.concatenate([v.astype(F32), jnp.zeros((R_CHUNK - n, R_VAL_DIM), F32)], axis=0).astype(BF16)
    for r in range(RET_SAMPLE_SEQS):
        mine = _seq_of(ri) == r
        st = s_ref[0, r, 0]
        cross = jnp.dot(q, st.astype(BF16), preferred_element_type=F32) * q_decay
        o = o + jnp.where(mine, cross, 0.0)
        kd_t = jnp.concatenate([jnp.where(mine, kd, 0.0), k_pad], axis=0).T.astype(BF16)
        so_ref[0, r, 0] = step_decay * st + jnp.dot(kd_t, v_pad, preferred_element_type=F32)
    o_ref[...] = _gated_groupnorm(o, g_ref[...])


def _ret_sample(z, state_all, idx, log_g, new_states):
    n_steps = DEC_BATCH // RET_SAMPLE_SEQS
    rb0 = N_PROMPT // RET_SAMPLE_ROWS
    kcol = R_K_WIDTH // R_KEY_DIM
    vcol = 2 * R_K_WIDTH // R_VAL_DIM
    gcol = vcol + R_HEADS
    st_spec = pl.BlockSpec((1, RET_SAMPLE_SEQS, 1, R_KEY_DIM, R_VAL_DIM), lambda i, h: (idx, i, h, 0, 0))
    in_specs = [pl.BlockSpec(memory_space=pltpu.SMEM),
                pl.BlockSpec((RET_SAMPLE_ROWS, R_KEY_DIM), lambda i, h: (rb0 + i, h)),
                pl.BlockSpec((RET_SAMPLE_ROWS, R_KEY_DIM), lambda i, h: (rb0 + i, kcol + h)),
                pl.BlockSpec((RET_SAMPLE_ROWS, R_VAL_DIM), lambda i, h: (rb0 + i, vcol + h)),
                pl.BlockSpec((RET_SAMPLE_ROWS, R_VAL_DIM), lambda i, h: (rb0 + i, gcol + h)),
                st_spec]
    args = [log_g, z, z, z, z, state_all]
    aliases = {}
    if new_states is not None:
        in_specs.append(pl.BlockSpec(memory_space=pl.ANY))
        args.append(new_states)
        aliases = {len(args) - 1: 1}
    return pl.pallas_call(
        _ret_sample_kernel,
        grid=(n_steps, R_HEADS),
        in_specs=in_specs,
        out_specs=[pl.BlockSpec((RET_SAMPLE_ROWS, R_VAL_DIM), lambda i, h: (i, h)), st_spec],
        out_shape=[jax.ShapeDtypeStruct((N_SAMPLE, R_V_WIDTH), BF16),
                   jax.ShapeDtypeStruct(state_all.shape, F32)],
        input_output_aliases=aliases,
        compiler_params=_cp(2),
        name="ret_sample",
    )(*args)


XQ_ROWS = 256
XQ_PROMPT_BLKS = N_PROMPT // XQ_ROWS
XQ_BLKS_PER_SEQ = SEQ // XQ_ROWS


def _xattn_heads(q, mk_head, mv_head):
    outs = []
    for h in range(X_HEADS):
        s = _dot_nt(q[:, h * X_HEAD_DIM:(h + 1) * X_HEAD_DIM], mk_head(h)) * (X_HEAD_DIM ** -0.5)
        m = jnp.max(s, axis=-1, keepdims=True)
        p = jnp.exp(s - m)
        den = jnp.sum(p, axis=-1, keepdims=True)
        outs.append(jnp.dot(p.astype(BF16), mv_head(h), preferred_element_type=F32) * (1.0 / den))
    return outs


def _xattn_prompt_kernel(q_ref, mk_ref, mv_ref, os_ref, o_ref):
    s_id = pl.program_id(0)

    @pl.when(s_id >= XQ_PROMPT_BLKS)
    def _():
        o_ref[...] = os_ref[...]

    @pl.when(s_id < XQ_PROMPT_BLKS)
    def _():
        def head_of(ref):
            return lambda h: ref[0, :, h * X_HEAD_DIM:(h + 1) * X_HEAD_DIM].astype(BF16)

        outs = _xattn_heads(q_ref[...], head_of(mk_ref), head_of(mv_ref))
        for h, o in enumerate(outs):
            o_ref[:, h * X_HEAD_DIM:(h + 1) * X_HEAD_DIM] = o.astype(o_ref.dtype)


def _xattn_prompt(q, mk_all, mv_all, layer, o_sample):
    nb = XQ_PROMPT_BLKS

    def rb(s):
        return jnp.minimum(s, nb - 1)

    mem_spec = pl.BlockSpec((1, MEM_LEN, X_WIDTH), lambda s: (layer, rb(s) // XQ_BLKS_PER_SEQ, 0))
    return pl.pallas_call(
        _xattn_prompt_kernel,
        grid=(nb + 1,),
        in_specs=[pl.BlockSpec((XQ_ROWS, X_WIDTH), lambda s: (rb(s), 0)),
                  mem_spec, mem_spec,
                  pl.BlockSpec((N_SAMPLE, X_WIDTH), lambda s: (0, 0))],
        out_specs=pl.BlockSpec((XQ_ROWS, X_WIDTH), lambda s: (s, 0)),
        out_shape=jax.ShapeDtypeStruct((N_ROWS, X_WIDTH), BF16),
        compiler_params=_cp(1),
        name="xattn_prompt",
    )(q, mk_all, mv_all, o_sample)


def _xattn_sample_kernel(q_ref, ck_ref, cv_ref, o_ref):
    q = q_ref[...]
    ri = lax.broadcasted_iota(jnp.int32, (ROWS_PER_STEP, 1), 0)
    acc = [jnp.zeros((ROWS_PER_STEP, X_HEAD_DIM), F32) for _ in range(X_HEADS)]
    for r in range(SEQ_PER_STEP):
        mine = _seq_of(ri) == r

        def head_of(ref, r=r):
            return lambda h: ref[0, r, pl.ds(h, MEM_LEN, stride=X_HEADS), :].astype(BF16)

        outs = _xattn_heads(q, head_of(ck_ref), head_of(cv_ref))
        acc = [a + jnp.where(mine, o, 0.0) for a, o in zip(acc, outs)]
    for h, a in enumerate(acc):
        o_ref[:, h * X_HEAD_DIM:(h + 1) * X_HEAD_DIM] = a.astype(o_ref.dtype)


def _xattn_sample(q, ck_all, cv_all, layer):
    n_steps = DEC_BATCH // SEQ_PER_STEP
    rb0 = N_PROMPT // ROWS_PER_STEP
    mem_spec = pl.BlockSpec((1, SEQ_PER_STEP, MEM_LEN * X_HEADS, X_HEAD_DIM), lambda i: (layer, i, 0, 0))
    return pl.pallas_call(
        _xattn_sample_kernel,
        grid=(n_steps,),
        in_specs=[pl.BlockSpec((ROWS_PER_STEP, X_WIDTH), lambda i: (rb0 + i, 0)), mem_spec, mem_spec],
        out_specs=pl.BlockSpec((ROWS_PER_STEP, X_WIDTH), lambda i: (i, 0)),
        out_shape=jax.ShapeDtypeStruct((N_SAMPLE, X_WIDTH), BF16),
        compiler_params=_cp(1),
        name="xattn_sample",
    )(q, ck_all, cv_all)


def kernel(x_prompt, x_sample, cache_win_k, cache_win_v, state_ret, cache_mem_k, cache_mem_v, mem_prompt,
           norm_mix, w_in_even, attn_sinks, w_spatial, b_spatial, norm_v_g, norm_v_b, w_out_even,
           w_in_ret, w_out_ret, norm_cross, norm_mem, w_xq, w_xk, w_xv, w_xo, norm_final):
    w_out_even_b = w_out_even.astype(BF16)
    w_out_ret_b = w_out_ret.astype(BF16)
    w_xo_b = w_xo.astype(BF16)

    slopes = jnp.exp2(-8.0 * (jnp.arange(A_HEADS, dtype=F32) + 1.0) / A_HEADS)
    log_g = jnp.log1p(-jnp.exp2(-5.0 - jnp.arange(R_HEADS, dtype=F32)))

    def mem_rows(cache):
        return cache.reshape(DEPTH, DEC_BATCH, MEM_LEN * X_HEADS, X_HEAD_DIM)

    mk_all, mv_all = _memkv(mem_prompt.reshape(BATCH * MEM_LEN, D_MODEL), norm_mem, w_xk, w_xv)

    x, xn = _embed(x_prompt.reshape(N_PROMPT, D_MODEL), x_sample.reshape(N_SAMPLE, D_MODEL), norm_mix[0])
    wkp, wvp, rsp, wks, wvs, cvs = [], [], [], [], [], []
    rss = None
    for layer in range(DEPTH):
        i = layer // 2
        if layer % 2 == 0:
            sl = jnp.stack([slopes, attn_sinks[i].astype(F32)])
            b_s = b_spatial[i].reshape(B_GROUPS, CHUNK, 1)
            lng = norm_v_g[i].reshape(1, B_WIDTH)
            lnb = norm_v_b[i].reshape(1, B_WIDTH)
            z, zs = _in_even(xn, w_in_even, i)
            kv_tail = _kv_tail(xn, w_in_even, i)
            wkp.append(kv_tail[:, :A_KV_WIDTH].reshape(BATCH, WINDOW, A_KV_HEADS, A_HEAD_DIM))
            wvp.append(kv_tail[:, A_KV_WIDTH:].reshape(BATCH, WINDOW, A_KV_HEADS, A_HEAD_DIM))
            ck = cache_win_k[i].reshape(DEC_BATCH, WINDOW, A_KV_WIDTH)
            cv = cache_win_v[i].reshape(DEC_BATCH, WINDOW, A_KV_WIDTH)
            mixed_s, wk_new, wv_new, cv_new = _even_sample(zs, ck, cv, sl, w_spatial[i], b_s, lng, lnb)
            wks.append(wk_new.reshape(DEC_BATCH, WINDOW, A_KV_HEADS, A_HEAD_DIM))
            wvs.append(wv_new.reshape(DEC_BATCH, WINDOW, A_KV_HEADS, A_HEAD_DIM))
            cvs.append(cv_new.reshape(DEC_BATCH, DEC_SEQ, B_WIDTH))
            mixed = _even_prompt(z, mixed_s, sl, w_spatial[i], b_s, lng, lnb)
            x, q = _mm_res_norm(mixed, w_out_even_b, i, x, norm_cross[layer], 768, 1024, w_xq, layer)
        else:
            z = _mm_wcast(xn, w_in_ret, i, BF16, 768, 1024, "in_ret")
            o_s, rss = _ret_sample(z, state_ret, i, log_g, rss)
            o_all, st_p = _ret_prompt(z, o_s, log_g)
            rsp.append(st_p)
            x, q = _mm_res_norm(o_all, w_out_ret_b, i, x, norm_cross[layer], 768, 1024, w_xq, layer)
        o_s = _xattn_sample(q, mem_rows(cache_mem_k), mem_rows(cache_mem_v), layer)
        o_all = _xattn_prompt(q, mk_all, mv_all, layer, o_s)
        if layer + 1 < DEPTH:
            x, xn = _mm_res_norm(o_all, w_xo_b, layer, x, norm_mix[layer + 1], 768, X_WIDTH)
        else:
            y_prompt, y_sample = _final(o_all, w_xo_b, layer, x, norm_final)

    y_prompt = y_prompt.reshape(BATCH, SEQ, D_MODEL)
    y_sample = y_sample.reshape(DEC_BATCH, DEC_SEQ, D_MODEL)
    mem_shape = (DEPTH, BATCH, MEM_LEN, X_HEADS, X_HEAD_DIM)
    return (y_prompt, y_sample,
            jnp.stack(wkp), jnp.stack(wvp), jnp.stack(rsp),
            mk_all.reshape(mem_shape), mv_all.reshape(mem_shape),
            jnp.stack(wks), jnp.stack(wvs), rss, jnp.stack(cvs))
```

```python
import functools

import jax
import jax.numpy as jnp
from jax import lax
from jax.experimental import pallas as pl
from jax.experimental.pallas import tpu as pltpu

F32 = jnp.float32
BF16 = jnp.bfloat16

D_MODEL = 2048
BATCH = 2
SEQ = 4096
DEPTH = 4
DEC_BATCH = 32
DEC_SEQ = 8
A_HEADS = 16
A_KV_HEADS = 2
A_HEAD_DIM = 64
A_WIDTH = 1024
A_KV_WIDTH = 128
WINDOW = 128
CHUNK = 128
B_GROUPS = 4
B_WIDTH = 1024
B_GROUP_DIM = 256
EVEN_IN = 5376
R_HEADS = 8
R_KEY_DIM = 256
R_VAL_DIM = 512
R_K_WIDTH = 2048
R_V_WIDTH = 4096
R_CHUNK = 128
RET_IN = 12288
MEM_LEN = 256
X_HEADS = 4
X_HEAD_DIM = 128
X_WIDTH = 512
EPS = 1e-6

N_PROMPT = BATCH * SEQ
N_SAMPLE = DEC_BATCH * DEC_SEQ
N_ROWS = N_PROMPT + N_SAMPLE

OFF_Q, OFF_K, OFF_V, OFF_GA, OFF_U, OFF_VB, OFF_GB = 0, 1024, 1152, 1280, 2304, 3328, 4352

VMEM_LIMIT = 52 * 1024 * 1024


def _cp(n_axes):
    return pltpu.CompilerParams(dimension_semantics=("arbitrary",) * n_axes,
                                vmem_limit_bytes=VMEM_LIMIT)


def _silu(x):
    return x * (1.0 / (1.0 + jnp.exp(-x)))


def _rms(x, g):
    ms = jnp.mean(x * x, axis=-1, keepdims=True)
    return x * lax.rsqrt(ms + EPS) * g


def _standardize(x):
    xc = x - jnp.mean(x, axis=-1, keepdims=True)
    return xc * lax.rsqrt(jnp.mean(xc * xc, axis=-1, keepdims=True) + EPS)


def _dot_nt(a, b):
    return lax.dot_general(a, b, (((1,), (1,)), ((), ())), preferred_element_type=F32)


def _mm_wcast_kernel(a_ref, w_ref, o_ref, wb_ref):
    @pl.when(pl.program_id(1) == 0)
    def _():
        wb_ref[...] = w_ref[0].astype(BF16)

    o_ref[...] = jnp.dot(a_ref[...], wb_ref[...], preferred_element_type=F32).astype(o_ref.dtype)


def _mm_wcast(a, w_stack, layer, out_dtype, tm, tn, name):
    m, k = a.shape
    n = w_stack.shape[2]
    return pl.pallas_call(
        _mm_wcast_kernel,
        grid=(n // tn, m // tm),
        in_specs=[pl.BlockSpec((tm, k), lambda j, i: (i, 0)),
                  pl.BlockSpec((1, k, tn), lambda j, i: (layer, 0, j))],
        out_specs=pl.BlockSpec((tm, tn), lambda j, i: (i, j)),
        out_shape=jax.ShapeDtypeStruct((m, n), out_dtype),
        scratch_shapes=[pltpu.VMEM((k, tn), BF16)],
        compiler_params=_cp(2),
        name=name,
    )(a, w_stack)


IN_EVEN_TM = 2048
IN_EVEN_TN = 768
IN_RET_TM = 1408
IN_RET_TN = 1024


def _in_even_kernel(n_prompt_blks, a_ref, as_ref, w_ref, z_ref, zs_ref, wb_ref):
    m = pl.program_id(1)

    @pl.when(m == 0)
    def _():
        wb_ref[...] = w_ref[0].astype(BF16)

    @pl.when(m < n_prompt_blks)
    def _():
        z_ref[...] = jnp.dot(a_ref[...], wb_ref[...], preferred_element_type=F32).astype(z_ref.dtype)

    @pl.when(m == n_prompt_blks)
    def _():
        zs_ref[...] = jnp.dot(as_ref[...], wb_ref[...], preferred_element_type=F32)


def _in_even(xn, w_stack, layer):
    k = xn.shape[1]
    n = w_stack.shape[2]
    tm, tn = IN_EVEN_TM, IN_EVEN_TN
    nb = N_PROMPT // tm
    return pl.pallas_call(
        functools.partial(_in_even_kernel, nb),
        grid=(n // tn, nb + 1),
        in_specs=[pl.BlockSpec((tm, k), lambda j, m: (jnp.minimum(m, nb - 1), 0)),
                  pl.BlockSpec((N_SAMPLE, k), lambda j, m: (N_PROMPT // N_SAMPLE, 0)),
                  pl.BlockSpec((1, k, tn), lambda j, m: (layer, 0, j))],
        out_specs=[pl.BlockSpec((tm, tn), lambda j, m: (jnp.minimum(m, nb - 1), j)),
                   pl.BlockSpec((N_SAMPLE, tn), lambda j, m: (0, j))],
        out_shape=[jax.ShapeDtypeStruct((N_PROMPT, n), BF16),
                   jax.ShapeDtypeStruct((N_SAMPLE, n), F32)],
        scratch_shapes=[pltpu.VMEM((k, tn), BF16)],
        compiler_params=_cp(2),
        name="in_even",
    )(xn, xn, w_stack)


def _kv_tail_kernel(a_ref, w_ref, o_ref):
    o_ref[...] = jnp.dot(a_ref[...], w_ref[0].astype(BF16), preferred_element_type=F32)


def _kv_tail(xn, w_stack, layer):
    k = xn.shape[1]
    blks_per_seq = SEQ // WINDOW
    width = 2 * A_KV_WIDTH
    return pl.pallas_call(
        _kv_tail_kernel,
        grid=(BATCH,),
        in_specs=[pl.BlockSpec((WINDOW, k), lambda b: (blks_per_seq * (b + 1) - 1, 0)),
                  pl.BlockSpec((1, k, width), lambda b: (layer, 0, OFF_K // width))],
        out_specs=pl.BlockSpec((WINDOW, width), lambda b: (b, 0)),
        out_shape=jax.ShapeDtypeStruct((BATCH * WINDOW, width), F32),
        compiler_params=_cp(1),
        name="kv_tail",
    )(xn, w_stack)


def _mm_res_q_kernel(nk, a_ref, w_ref, x_ref, g_ref, wq_ref, xo_ref, q_ref):
    k = pl.program_id(1)

    @pl.when(k == 0)
    def _():
        xo_ref[...] = x_ref[...]

    xo_ref[...] += jnp.dot(a_ref[...], w_ref[0], preferred_element_type=F32)

    @pl.when(k == nk - 1)
    def _():
        xn = _rms(xo_ref[...], g_ref[...]).astype(BF16)
        q_ref[...] = jnp.dot(xn, wq_ref[0].astype(BF16), preferred_element_type=F32).astype(q_ref.dtype)


def _mm_res_q(a, w_stack, layer, x, g, wq_stack, wq_layer, tm, tk):
    m, kdim = a.shape
    d = w_stack.shape[2]
    nq = wq_stack.shape[2]
    nk = kdim // tk
    return pl.pallas_call(
        functools.partial(_mm_res_q_kernel, nk),
        grid=(m // tm, nk),
        in_specs=[pl.BlockSpec((tm, tk), lambda i, k: (i, k)),
                  pl.BlockSpec((1, tk, d), lambda i, k: (layer, k, 0)),
                  pl.BlockSpec((tm, d), lambda i, k: (i, 0)),
                  pl.BlockSpec((1, d), lambda i, k: (0, 0)),
                  pl.BlockSpec((1, d, nq), lambda i, k: (wq_layer, 0, 0))],
        out_specs=[pl.BlockSpec((tm, d), lambda i, k: (i, 0)),
                   pl.BlockSpec((tm, nq), lambda i, k: (i, 0))],
        out_shape=[jax.ShapeDtypeStruct((m, d), F32),
                   jax.ShapeDtypeStruct((m, nq), BF16)],
        compiler_params=_cp(2),
        name="mm_res_q",
    )(a, w_stack, x, g.reshape(1, d), wq_stack)


def _embed_kernel(n_prompt_blks, xp_ref, xs_ref, g_ref, x_ref, n_ref):
    i = pl.program_id(0)

    @pl.when(i < n_prompt_blks)
    def _():
        x_ref[...] = xp_ref[...]

    @pl.when(i >= n_prompt_blks)
    def _():
        x_ref[...] = xs_ref[...]

    n_ref[...] = _rms(x_ref[...], g_ref[...]).astype(n_ref.dtype)


def _embed(xp, xs, g):
    d = xp.shape[1]
    tm = N_SAMPLE
    nb = N_PROMPT // tm
    return pl.pallas_call(
        functools.partial(_embed_kernel, nb),
        grid=(nb + 1,),
        in_specs=[pl.BlockSpec((tm, d), lambda i: (jnp.minimum(i, nb - 1), 0)),
                  pl.BlockSpec((tm, d), lambda i: (0, 0)),
                  pl.BlockSpec((1, d), lambda i: (0, 0))],
        out_specs=[pl.BlockSpec((tm, d), lambda i: (i, 0)),
                   pl.BlockSpec((tm, d), lambda i: (i, 0))],
        out_shape=[jax.ShapeDtypeStruct((N_ROWS, d), F32),
                   jax.ShapeDtypeStruct((N_ROWS, d), BF16)],
        compiler_params=_cp(1),
        name="embed",
    )(xp, xs, g.reshape(1, d))


def _memkv_kernel(mem_ref, g_ref, wk_ref, wv_ref, mk_ref, mv_ref):
    h = _rms(mem_ref[...], g_ref[0]).astype(BF16)
    mk_ref[0] = jnp.dot(h, wk_ref[0].astype(BF16), preferred_element_type=F32)
    mv_ref[0] = jnp.dot(h, wv_ref[0].astype(BF16), preferred_element_type=F32)


def _memkv(mem, norm_mem, w_xk, w_xv):
    m = mem.shape[0]
    out = jax.ShapeDtypeStruct((DEPTH, m, X_WIDTH), F32)
    return pl.pallas_call(
        _memkv_kernel,
        grid=(DEPTH,),
        in_specs=[pl.BlockSpec((m, D_MODEL), lambda l: (0, 0)),
                  pl.BlockSpec((1, 1, D_MODEL), lambda l: (l, 0, 0)),
                  pl.BlockSpec((1, D_MODEL, X_WIDTH), lambda l: (l, 0, 0)),
                  pl.BlockSpec((1, D_MODEL, X_WIDTH), lambda l: (l, 0, 0))],
        out_specs=[pl.BlockSpec((1, m, X_WIDTH), lambda l: (l, 0, 0)),
                   pl.BlockSpec((1, m, X_WIDTH), lambda l: (l, 0, 0))],
        out_shape=[out, out],
        compiler_params=_cp(1),
        name="memkv",
    )(mem, norm_mem.reshape(DEPTH, 1, D_MODEL), w_xk, w_xv)


SLABS_PER_KV = A_HEADS // A_KV_HEADS // 2
MASK_DIST = 1e30 * 2.0 ** 8


def _expand_band(band, scale):
    x = band.astype(F32)
    if scale != 1.0:
        x = x * scale
    xs = pltpu.roll(x, 64, 1)
    lo = lax.broadcasted_iota(jnp.int32, x.shape, 1) < 64
    z = jnp.zeros_like(x)
    e0 = jnp.concatenate([jnp.where(lo, x, z), jnp.where(lo, z, xs)], axis=0).astype(BF16)
    e1 = jnp.concatenate([jnp.where(lo, xs, z), jnp.where(lo, z, x)], axis=0).astype(BF16)
    return e0, e1


def _band_bias(t, first_key):
    row = lax.broadcasted_iota(jnp.int32, (t, 2 * WINDOW), 0)
    col = lax.broadcasted_iota(jnp.int32, (t, 2 * WINDOW), 1)
    dist = row + WINDOW - col
    valid = (dist >= 0) & (dist < WINDOW) & (col >= first_key)
    return jnp.where(valid, -dist.astype(F32), -MASK_DIST)


def _swa(q_slab, kband, vband, nbias, sl_ref):
    t = nbias.shape[0]
    rows = SLABS_PER_KV * t
    kk = _expand_band(kband, A_HEAD_DIM ** -0.5)
    vv = _expand_band(vband, 1.0)
    nb = jnp.concatenate([nbias] * SLABS_PER_KV, axis=0)
    blk = lax.shift_right_logical(lax.broadcasted_iota(jnp.int32, (rows, 1), 0), t.bit_length() - 1)
    lo = lax.broadcasted_iota(jnp.int32, (rows, 128), 1) < 64
    outs = [None] * (A_HEADS // 2)
    for j in range(A_KV_HEADS):
        slab0 = SLABS_PER_KV * j

        def per_row(table_row, half, slab0=slab0):
            col = jnp.full((rows, 1), sl_ref[table_row, 2 * (slab0 + SLABS_PER_KV - 1) + half], F32)
            for i in reversed(range(SLABS_PER_KV - 1)):
                col = jnp.where(blk == i, sl_ref[table_row, 2 * (slab0 + i) + half], col)
            return col

        q = jnp.concatenate([q_slab(slab0 + i) for i in range(SLABS_PER_KV)], axis=0)
        s = _dot_nt(q, kk[j])
        probs, invs = [], []
        for half in range(2):
            slope = per_row(0, half)
            sink = per_row(1, half)
            sh = s[:, half * 256:(half + 1) * 256] + slope * nb
            m = jnp.maximum(jnp.max(sh, axis=-1, keepdims=True), sink)
            pe = jnp.exp(sh - m)
            den = jnp.sum(pe, axis=-1, keepdims=True) + jnp.exp(sink - m)
            probs.append(pe)
            invs.append(1.0 / den)
        pp = jnp.concatenate(probs, axis=1).astype(BF16)
        o = jnp.dot(pp, vv[j], preferred_element_type=F32) * jnp.where(lo, invs[0], invs[1])
        for i in range(SLABS_PER_KV):
            outs[slab0 + i] = o[i * t:(i + 1) * t]
    return outs


def _spatial_gate_group(ws_ref, bs_ref, g, vn_g, rows):
    r = lax.broadcasted_iota(jnp.int32, (CHUNK, CHUNK), 0)
    c = lax.broadcasted_iota(jnp.int32, (CHUNK, CHUNK), 1)
    w = jnp.where(r >= c, ws_ref[g], 0.0).astype(BF16)
    return jnp.dot(w[:rows], vn_g, preferred_element_type=F32) + bs_ref[g][:rows]


def _even_prompt_kernel(sl_ref, z_ref, kvprev_ref, ms_ref, ws_ref, bs_ref, lng_ref, lnb_ref, o_ref):
    s_id = pl.program_id(0)
    n_blk = N_PROMPT // WINDOW

    @pl.when(s_id >= n_blk)
    def _():
        o_ref[...] = ms_ref[...]

    @pl.when(s_id < n_blk)
    def _():
        c = s_id % (SEQ // WINDOW)
        kv_cur = z_ref[:, OFF_K:OFF_K + 2 * A_KV_WIDTH]
        band = jnp.concatenate([kvprev_ref[...], kv_cur], axis=0)
        nbias = _band_bias(WINDOW, jnp.maximum(WINDOW - c * WINDOW, 0))

        def q_slab(p):
            return z_ref[:, OFF_Q + p * 128:OFF_Q + (p + 1) * 128]

        for p, a in enumerate(_swa(q_slab, band[:, :A_KV_WIDTH], band[:, A_KV_WIDTH:], nbias, sl_ref)):
            ga = z_ref[:, OFF_GA + p * 128:OFF_GA + (p + 1) * 128].astype(F32)
            o_ref[:, p * 128:(p + 1) * 128] = (_silu(ga) * a).astype(o_ref.dtype)

        vn = _standardize(z_ref[:, OFF_VB:OFF_VB + B_WIDTH].astype(F32)) * lng_ref[...] + lnb_ref[...]
        for g in range(B_GROUPS):
            lo, hi = g * B_GROUP_DIM, (g + 1) * B_GROUP_DIM
            mixed = _spatial_gate_group(ws_ref, bs_ref, g, vn[:, lo:hi].astype(BF16), CHUNK)
            u = z_ref[:, OFF_U + lo:OFF_U + hi].astype(F32)
            gb = z_ref[:, OFF_GB + lo:OFF_GB + hi].astype(F32)
            o_ref[:, A_WIDTH + lo:A_WIDTH + hi] = (_silu(gb) * (u * mixed)).astype(o_ref.dtype)


def _even_prompt(z, mixed_sample, sl, w_s, b_s, lng, lnb):
    n_blk = N_PROMPT // WINDOW
    n_copy = N_SAMPLE // WINDOW
    kv_blk = OFF_K // (2 * A_KV_WIDTH)
    return pl.pallas_call(
        _even_prompt_kernel,
        grid=(n_blk + n_copy,),
        in_specs=[pl.BlockSpec(memory_space=pltpu.SMEM),
                  pl.BlockSpec((WINDOW, EVEN_IN), lambda s: (jnp.minimum(s, n_blk - 1), 0)),
                  pl.BlockSpec((WINDOW, 2 * A_KV_WIDTH),
                               lambda s: (jnp.maximum(jnp.minimum(s, n_blk - 1) - 1, 0), kv_blk)),
                  pl.BlockSpec((WINDOW, D_MODEL), lambda s: (jnp.maximum(s - n_blk, 0), 0)),
                  pl.BlockSpec((B_GROUPS, CHUNK, CHUNK), lambda s: (0, 0, 0)),
                  pl.BlockSpec((B_GROUPS, CHUNK, 1), lambda s: (0, 0, 0)),
                  pl.BlockSpec((1, B_WIDTH), lambda s: (0, 0)),
                  pl.BlockSpec((1, B_WIDTH), lambda s: (0, 0))],
        out_specs=pl.BlockSpec((WINDOW, D_MODEL), lambda s: (s, 0)),
        out_shape=jax.ShapeDtypeStruct((N_ROWS, D_MODEL), BF16),
        compiler_params=_cp(1),
        name="even_prompt",
    )(sl, z, z, mixed_sample, w_s, b_s, lng, lnb)


SEQ_PER_STEP = 2
ROWS_PER_STEP = SEQ_PER_STEP * DEC_SEQ
assert DEC_SEQ & (DEC_SEQ - 1) == 0


def _seq_of(row_idx):
    return lax.shift_right_logical(row_idx, DEC_SEQ.bit_length() - 1)


def _even_sample_kernel(sl_ref, z_ref, ck_ref, cv_ref, ws_ref, bs_ref, lng_ref, lnb_ref,
                        o_ref, wk_ref, wv_ref, cvo_ref):
    nbias = _band_bias(ROWS_PER_STEP, 0)
    pad_rows = jnp.zeros((ROWS_PER_STEP - DEC_SEQ, EVEN_IN), F32)
    attn = [[None] * SEQ_PER_STEP for _ in range(A_HEADS // 2)]
    gate = [[None] * SEQ_PER_STEP for _ in range(B_GROUPS)]
    for r in range(SEQ_PER_STEP):
        r0, r1 = r * DEC_SEQ, (r + 1) * DEC_SEQ
        zr = z_ref[r0:r1, :]
        k_new = zr[:, OFF_K:OFF_K + A_KV_WIDTH]
        v_new = zr[:, OFF_V:OFF_V + A_KV_WIDTH]
        ck = ck_ref[r]
        cv = cv_ref[r]
        wk_ref[r, 0:WINDOW - DEC_SEQ, :] = ck[DEC_SEQ:, :]
        wk_ref[r, WINDOW - DEC_SEQ:, :] = k_new
        wv_ref[r, 0:WINDOW - DEC_SEQ, :] = cv[DEC_SEQ:, :]
        wv_ref[r, WINDOW - DEC_SEQ:, :] = v_new
        tail = jnp.zeros((WINDOW - DEC_SEQ, A_KV_WIDTH), F32)
        kband = jnp.concatenate([ck, k_new, tail], axis=0)
        vband = jnp.concatenate([cv, v_new, tail], axis=0)
        zq = jnp.concatenate([zr, pad_rows], axis=0)

        def q_slab(p, zq=zq):
            return zq[:, OFF_Q + p * 128:OFF_Q + (p + 1) * 128].astype(BF16)

        for p, a in enumerate(_swa(q_slab, kband, vband, nbias, sl_ref)):
            ga = zr[:, OFF_GA + p * 128:OFF_GA + (p + 1) * 128]
            attn[p][r] = _silu(ga) * a[:DEC_SEQ]

        vn = _standardize(zr[:, OFF_VB:OFF_VB + B_WIDTH]) * lng_ref[...] + lnb_ref[...]
        cvo_ref[r0:r1, :] = vn
        vn_pad = jnp.concatenate([vn, jnp.zeros((CHUNK - DEC_SEQ, B_WIDTH), F32)], axis=0).astype(BF16)
        for g in range(B_GROUPS):
            lo, hi = g * B_GROUP_DIM, (g + 1) * B_GROUP_DIM
            mixed = _spatial_gate_group(ws_ref, bs_ref, g, vn_pad[:, lo:hi], ROWS_PER_STEP)[:DEC_SEQ]
            u = zr[:, OFF_U + lo:OFF_U + hi]
            gb = zr[:, OFF_GB + lo:OFF_GB + hi]
            gate[g][r] = _silu(gb) * (u * mixed)

    for p in range(A_HEADS // 2):
        o_ref[:, p * 128:(p + 1) * 128] = jnp.concatenate(attn[p], axis=0).astype(o_ref.dtype)
    for g in range(B_GROUPS):
        lo, hi = A_WIDTH + g * B_GROUP_DIM, A_WIDTH + (g + 1) * B_GROUP_DIM
        o_ref[:, lo:hi] = jnp.concatenate(gate[g], axis=0).astype(o_ref.dtype)


def _even_sample(zs, ck, cv, sl, w_s, b_s, lng, lnb):
    n_steps = DEC_BATCH // SEQ_PER_STEP
    win = jax.ShapeDtypeStruct((DEC_BATCH, WINDOW, A_KV_WIDTH), F32)
    return pl.pallas_call(
        _even_sample_kernel,
        grid=(n_steps,),
        in_specs=[pl.BlockSpec(memory_space=pltpu.SMEM),
                  pl.BlockSpec((ROWS_PER_STEP, EVEN_IN), lambda i: (i, 0)),
                  pl.BlockSpec((SEQ_PER_STEP, WINDOW, A_KV_WIDTH), lambda i: (i, 0, 0)),
                  pl.BlockSpec((SEQ_PER_STEP, WINDOW, A_KV_WIDTH), lambda i: (i, 0, 0)),
                  pl.BlockSpec((B_GROUPS, CHUNK, CHUNK), lambda i: (0, 0, 0)),
                  pl.BlockSpec((B_GROUPS, CHUNK, 1), lambda i: (0, 0, 0)),
                  pl.BlockSpec((1, B_WIDTH), lambda i: (0, 0)),
                  pl.BlockSpec((1, B_WIDTH), lambda i: (0, 0))],
        out_specs=[pl.BlockSpec((ROWS_PER_STEP, D_MODEL), lambda i: (i, 0)),
                   pl.BlockSpec((SEQ_PER_STEP, WINDOW, A_KV_WIDTH), lambda i: (i, 0, 0)),
                   pl.BlockSpec((SEQ_PER_STEP, WINDOW, A_KV_WIDTH), lambda i: (i, 0, 0)),
                   pl.BlockSpec((ROWS_PER_STEP, B_WIDTH), lambda i: (i, 0))],
        out_shape=[jax.ShapeDtypeStruct((N_SAMPLE, D_MODEL), BF16), win, win,
                   jax.ShapeDtypeStruct((N_SAMPLE, B_WIDTH), F32)],
        compiler_params=_cp(1),
        name="even_sample",
    )(sl, zs, ck, cv, w_s, b_s, lng, lnb)


RET_ROWS = 256
RET_BLKS_PER_SEQ = SEQ // RET_ROWS
RET_PROMPT_BLKS = N_PROMPT // RET_ROWS


def _gated_groupnorm(o, g):
    return (_silu(g.astype(F32)) * _standardize(o)).astype(BF16)


RET_HEADS_PER_STEP = 2


def _ret_prompt_kernel(lg_ref, q_ref, k_ref, v_ref, g_ref, os_ref, o_ref, so_ref, st_ref, dm_ref, dc_ref):
    hp = pl.program_id(0)
    bc = pl.program_id(1)
    t = RET_ROWS
    k_scale = R_KEY_DIM ** -0.5

    @pl.when(bc == 0)
    def _():
        row = lax.broadcasted_iota(jnp.int32, (t, t), 0)
        col = lax.broadcasted_iota(jnp.int32, (t, t), 1)
        rel = (row - col).astype(F32)
        ti = lax.broadcasted_iota(jnp.int32, (t, 1), 0).astype(F32)
        for i in range(RET_HEADS_PER_STEP):
            lg = lg_ref[hp * RET_HEADS_PER_STEP + i]
            dm_ref[i] = jnp.where(rel >= 0, jnp.exp(jnp.maximum(rel, 0.0) * lg), 0.0) * k_scale
            dc_ref[i, 0] = jnp.exp((ti + 1.0) * lg)
            dc_ref[i, 1] = jnp.exp((t - 1.0 - ti) * lg) * k_scale

    @pl.when(bc >= RET_PROMPT_BLKS)
    def _():
        o_ref[...] = os_ref[...]

    @pl.when(bc < RET_PROMPT_BLKS)
    def _():
        c = bc % RET_BLKS_PER_SEQ

        @pl.when(c == 0)
        def _():
            st_ref[...] = jnp.zeros_like(st_ref)

        for i in range(RET_HEADS_PER_STEP):
            lg = lg_ref[hp * RET_HEADS_PER_STEP + i]
            q_decay = dc_ref[i, 0]
            k_decay = dc_ref[i, 1]
            chunk_decay = jnp.exp(jnp.full((1, R_VAL_DIM), float(t), F32) * lg)
            kc = slice(i * R_KEY_DIM, (i + 1) * R_KEY_DIM)
            vc = slice(i * R_VAL_DIM, (i + 1) * R_VAL_DIM)
            q = q_ref[:, kc]
            k = k_ref[:, kc]
            v = v_ref[:, vc]
            st = st_ref[i]
            scores = _dot_nt(q, k) * dm_ref[i]
            inner = jnp.dot(scores.astype(BF16), v, preferred_element_type=F32)
            cross = jnp.dot(q, st.astype(BF16), preferred_element_type=F32) * q_decay
            kd_t = (k.astype(F32) * k_decay).T.astype(BF16)
            st_ref[i] = chunk_decay * st + jnp.dot(kd_t, v, preferred_element_type=F32)
            o_ref[:, vc] = _gated_groupnorm(inner + cross, g_ref[:, vc])

        @pl.when(c == RET_BLKS_PER_SEQ - 1)
        def _():
            so_ref[0] = st_ref[...]


def _ret_prompt(z, o_sample, log_g):
    nb = RET_PROMPT_BLKS
    hps = RET_HEADS_PER_STEP
    kw, vw = hps * R_KEY_DIM, hps * R_VAL_DIM
    kcol = R_K_WIDTH // kw
    vcol = 2 * R_K_WIDTH // vw
    gcol = vcol + R_HEADS // hps

    def rb(bc):
        return jnp.minimum(bc, nb - 1)

    return pl.pallas_call(
        _ret_prompt_kernel,
        grid=(R_HEADS // hps, nb + 1),
        in_specs=[pl.BlockSpec(memory_space=pltpu.SMEM),
                  pl.BlockSpec((RET_ROWS, kw), lambda h, bc: (rb(bc), h)),
                  pl.BlockSpec((RET_ROWS, kw), lambda h, bc: (rb(bc), kcol + h)),
                  pl.BlockSpec((RET_ROWS, vw), lambda h, bc: (rb(bc), vcol + h)),
                  pl.BlockSpec((RET_ROWS, vw), lambda h, bc: (rb(bc), gcol + h)),
                  pl.BlockSpec((N_SAMPLE, vw), lambda h, bc: (0, h))],
        out_specs=[pl.BlockSpec((RET_ROWS, vw), lambda h, bc: (bc, h)),
                   pl.BlockSpec((1, hps, R_KEY_DIM, R_VAL_DIM),
                                lambda h, bc: (rb(bc) // RET_BLKS_PER_SEQ, h, 0, 0))],
        out_shape=[jax.ShapeDtypeStruct((N_ROWS, R_V_WIDTH), BF16),
                   jax.ShapeDtypeStruct((BATCH, R_HEADS, R_KEY_DIM, R_VAL_DIM), F32)],
        scratch_shapes=[pltpu.VMEM((hps, R_KEY_DIM, R_VAL_DIM), F32), pltpu.VMEM((hps, RET_ROWS, RET_ROWS), F32),
                        pltpu.VMEM((hps, 2, RET_ROWS, 1), F32)],
        compiler_params=_cp(2),
        name="ret_prompt",
    )(log_g, z, z, z, z, o_sample)


RET_SAMPLE_SEQS = 4
RET_SAMPLE_ROWS = RET_SAMPLE_SEQS * DEC_SEQ


def _ret_sample_kernel(lg_ref, q_ref, k_ref, v_ref, g_ref, s_ref, *rest):
    o_ref, so_ref = rest[-2:]
    h = pl.program_id(1)
    lg = lg_ref[h]
    n = RET_SAMPLE_ROWS
    t = DEC_SEQ
    row = lax.broadcasted_iota(jnp.int32, (n, n), 0)
    col = lax.broadcasted_iota(jnp.int32, (n, n), 1)
    rel = (row - col).astype(F32)
    same = _seq_of(row) == _seq_of(col)
    k_scale = R_KEY_DIM ** -0.5
    dmat = jnp.where(same & (rel >= 0), jnp.exp(jnp.maximum(rel, 0.0) * lg), 0.0) * k_scale
    ri = lax.broadcasted_iota(jnp.int32, (n, 1), 0)
    ti = (ri & (t - 1)).astype(F32)
    q_decay = jnp.exp((ti + 1.0) * lg)
    k_decay = jnp.exp((t - 1.0 - ti) * lg) * k_scale
    step_decay = jnp.exp(jnp.full((1, R_VAL_DIM), float(t), F32) * lg)

    q = q_ref[...]
    k = k_ref[...]
    v = v_ref[...]
    scores = _dot_nt(q, k) * dmat
    o = jnp.dot(scores.astype(BF16), v, preferred_element_type=F32)
    kd = k.astype(F32) * k_decay
    k_pad = jnp.zeros((R_CHUNK - n, R_KEY_DIM), F32)
    v_pad = jnp.concatenate([v.astype(F32), jnp.zeros((R_CHUNK - n, R_VAL_DIM), F32)], axis=0).astype(BF16)
    for r in range(RET_SAMPLE_SEQS):
        mine = _seq_of(ri) == r
        st = s_ref[0, r, 0]
        cross = jnp.dot(q, st.astype(BF16), preferred_element_type=F32) * q_decay
        o = o + jnp.where(mine, cross, 0.0)
        kd_t = jnp.concatenate([jnp.where(mine, kd, 0.0), k_pad], axis=0).T.astype(BF16)
        so_ref[0, r, 0] = step_decay * st + jnp.dot(kd_t, v_pad, preferred_element_type=F32)
    o_ref[...] = _gated_groupnorm(o, g_ref[...])


def _ret_sample(z, state_all, idx, log_g, new_states):
    n_steps = DEC_BATCH // RET_SAMPLE_SEQS
    rb0 = N_PROMPT // RET_SAMPLE_ROWS
    kcol = R_K_WIDTH // R_KEY_DIM
    vcol = 2 * R_K_WIDTH // R_VAL_DIM
    gcol = vcol + R_HEADS
    st_spec = pl.BlockSpec((1, RET_SAMPLE_SEQS, 1, R_KEY_DIM, R_VAL_DIM), lambda i, h: (idx, i, h, 0, 0))
    in_specs = [pl.BlockSpec(memory_space=pltpu.SMEM),
                pl.BlockSpec((RET_SAMPLE_ROWS, R_KEY_DIM), lambda i, h: (rb0 + i, h)),
                pl.BlockSpec((RET_SAMPLE_ROWS, R_KEY_DIM), lambda i, h: (rb0 + i, kcol + h)),
                pl.BlockSpec((RET_SAMPLE_ROWS, R_VAL_DIM), lambda i, h: (rb0 + i, vcol + h)),
                pl.BlockSpec((RET_SAMPLE_ROWS, R_VAL_DIM), lambda i, h: (rb0 + i, gcol + h)),
                st_spec]
    args = [log_g, z, z, z, z, state_all]
    aliases = {}
    if new_states is not None:
        in_specs.append(pl.BlockSpec(memory_space=pl.ANY))
        args.append(new_states)
        aliases = {len(args) - 1: 1}
    return pl.pallas_call(
        _ret_sample_kernel,
        grid=(n_steps, R_HEADS),
        in_specs=in_specs,
        out_specs=[pl.BlockSpec((RET_SAMPLE_ROWS, R_VAL_DIM), lambda i, h: (i, h)), st_spec],
        out_shape=[jax.ShapeDtypeStruct((N_SAMPLE, R_V_WIDTH), BF16),
                   jax.ShapeDtypeStruct(state_all.shape, F32)],
        input_output_aliases=aliases,
        compiler_params=_cp(2),
        name="ret_sample",
    )(*args)


XQ_ROWS = 256
XQ_PROMPT_BLKS = N_PROMPT // XQ_ROWS
XQ_BLKS_PER_SEQ = SEQ // XQ_ROWS


def _xattn_heads(q, mk_head, mv_head):
    outs = []
    for h in range(X_HEADS):
        s = _dot_nt(q[:, h * X_HEAD_DIM:(h + 1) * X_HEAD_DIM], mk_head(h)) * (X_HEAD_DIM ** -0.5)
        m = jnp.max(s, axis=-1, keepdims=True)
        p = jnp.exp(s - m)
        den = jnp.sum(p, axis=-1, keepdims=True)
        outs.append(jnp.dot(p.astype(BF16), mv_head(h), preferred_element_type=F32) * (1.0 / den))
    return outs


def _xattn_proj_kernel(final, q_ref, mk_ref, mv_ref, os_ref, w_ref, x_ref, g_ref, out_a, out_b):
    s_id = pl.program_id(0)

    def project(o):
        return x_ref[...] + jnp.dot(o, w_ref[0], preferred_element_type=F32)

    @pl.when(s_id < XQ_PROMPT_BLKS)
    def _():
        def head_of(ref):
            return lambda h: ref[0, :, h * X_HEAD_DIM:(h + 1) * X_HEAD_DIM].astype(BF16)

        outs = _xattn_heads(q_ref[...], head_of(mk_ref), head_of(mv_ref))
        x2 = project(jnp.concatenate(outs, axis=1).astype(BF16))
        if final:
            out_a[...] = _rms(x2, g_ref[...])
        else:
            out_a[...] = x2
            out_b[...] = _rms(x2, g_ref[...]).astype(out_b.dtype)

    @pl.when(s_id >= XQ_PROMPT_BLKS)
    def _():
        x2 = project(os_ref[...])
        if final:
            out_b[...] = _rms(x2, g_ref[...])
        else:
            out_a[...] = x2
            out_b[...] = _rms(x2, g_ref[...]).astype(out_b.dtype)


def _xattn_proj(q, mk_all, mv_all, layer, o_sample, w_stack, x, g, final):
    nb = XQ_PROMPT_BLKS
    d = w_stack.shape[2]

    def rb(s):
        return jnp.minimum(s, nb - 1)

    mem_spec = pl.BlockSpec((1, MEM_LEN, X_WIDTH), lambda s: (layer, rb(s) // XQ_BLKS_PER_SEQ, 0))
    if final:
        out_specs = [pl.BlockSpec((XQ_ROWS, d), lambda s: (rb(s), 0)),
                     pl.BlockSpec((N_SAMPLE, d), lambda s: (0, 0))]
        out_shape = [jax.ShapeDtypeStruct((N_PROMPT, d), F32), jax.ShapeDtypeStruct((N_SAMPLE, d), F32)]
    else:
        out_specs = [pl.BlockSpec((XQ_ROWS, d), lambda s: (s, 0)),
                     pl.BlockSpec((XQ_ROWS, d), lambda s: (s, 0))]
        out_shape = [jax.ShapeDtypeStruct((N_ROWS, d), F32), jax.ShapeDtypeStruct((N_ROWS, d), BF16)]
    return pl.pallas_call(
        functools.partial(_xattn_proj_kernel, final),
        grid=(nb + 1,),
        in_specs=[pl.BlockSpec((XQ_ROWS, X_WIDTH), lambda s: (rb(s), 0)),
                  mem_spec, mem_spec,
                  pl.BlockSpec((N_SAMPLE, X_WIDTH), lambda s: (0, 0)),
                  pl.BlockSpec((1, X_WIDTH, d), lambda s: (layer, 0, 0)),
                  pl.BlockSpec((XQ_ROWS, d), lambda s: (s, 0)),
                  pl.BlockSpec((1, d), lambda s: (0, 0))],
        out_specs=out_specs,
        out_shape=out_shape,
        compiler_params=_cp(1),
        name="xattn_proj",
    )(q, mk_all, mv_all, o_sample, w_stack, x, g.reshape(1, d))


XS_SEQS = 4
XS_ROWS = XS_SEQS * DEC_SEQ


def _xattn_sample_kernel(q_ref, ck_ref, cv_ref, o_ref):
    q = q_ref[...]
    ri = lax.broadcasted_iota(jnp.int32, (XS_ROWS, 1), 0)
    acc = [jnp.zeros((XS_ROWS, X_HEAD_DIM), F32) for _ in range(X_HEADS)]
    for r in range(XS_SEQS):
        mine = _seq_of(ri) == r

        def head_of(ref, r=r):
            return lambda h: ref[0, r, pl.ds(h, MEM_LEN, stride=X_HEADS), :].astype(BF16)

        outs = _xattn_heads(q, head_of(ck_ref), head_of(cv_ref))
        acc = [a + jnp.where(mine, o, 0.0) for a, o in zip(acc, outs)]
    for h, a in enumerate(acc):
        o_ref[:, h * X_HEAD_DIM:(h + 1) * X_HEAD_DIM] = a.astype(o_ref.dtype)


def _xattn_sample(q, ck_all, cv_all, layer):
    n_steps = DEC_BATCH // XS_SEQS
    rb0 = N_PROMPT // XS_ROWS
    mem_spec = pl.BlockSpec((1, XS_SEQS, MEM_LEN * X_HEADS, X_HEAD_DIM), lambda i: (layer, i, 0, 0))
    return pl.pallas_call(
        _xattn_sample_kernel,
        grid=(n_steps,),
        in_specs=[pl.BlockSpec((XS_ROWS, X_WIDTH), lambda i: (rb0 + i, 0)), mem_spec, mem_spec],
        out_specs=pl.BlockSpec((XS_ROWS, X_WIDTH), lambda i: (i, 0)),
        out_shape=jax.ShapeDtypeStruct((N_SAMPLE, X_WIDTH), BF16),
        compiler_params=_cp(1),
        name="xattn_sample",
    )(q, ck_all, cv_all)


def kernel(x_prompt, x_sample, cache_win_k, cache_win_v, state_ret, cache_mem_k, cache_mem_v, mem_prompt,
           norm_mix, w_in_even, attn_sinks, w_spatial, b_spatial, norm_v_g, norm_v_b, w_out_even,
           w_in_ret, w_out_ret, norm_cross, norm_mem, w_xq, w_xk, w_xv, w_xo, norm_final):
    w_out_even_b = w_out_even.astype(BF16)
    w_out_ret_b = w_out_ret.astype(BF16)
    w_xo_b = w_xo.astype(BF16)

    slopes = jnp.exp2(-8.0 * (jnp.arange(A_HEADS, dtype=F32) + 1.0) / A_HEADS)
    log_g = jnp.log1p(-jnp.exp2(-5.0 - jnp.arange(R_HEADS, dtype=F32)))

    def mem_rows(cache):
        return cache.reshape(DEPTH, DEC_BATCH, MEM_LEN * X_HEADS, X_HEAD_DIM)

    mk_all, mv_all = _memkv(mem_prompt.reshape(BATCH * MEM_LEN, D_MODEL), norm_mem, w_xk, w_xv)

    x, xn = _embed(x_prompt.reshape(N_PROMPT, D_MODEL), x_sample.reshape(N_SAMPLE, D_MODEL), norm_mix[0])
    wkp, wvp, rsp, wks, wvs, cvs = [], [], [], [], [], []
    rss = None
    for layer in range(DEPTH):
        i = layer // 2
        if layer % 2 == 0:
            sl = jnp.stack([slopes, attn_sinks[i].astype(F32)])
            b_s = b_spatial[i].reshape(B_GROUPS, CHUNK, 1)
            lng = norm_v_g[i].reshape(1, B_WIDTH)
            lnb = norm_v_b[i].reshape(1, B_WIDTH)
            z, zs = _in_even(xn, w_in_even, i)
            kv_tail = _kv_tail(xn, w_in_even, i)
            wkp.append(kv_tail[:, :A_KV_WIDTH].reshape(BATCH, WINDOW, A_KV_HEADS, A_HEAD_DIM))
            wvp.append(kv_tail[:, A_KV_WIDTH:].reshape(BATCH, WINDOW, A_KV_HEADS, A_HEAD_DIM))
            ck = cache_win_k[i].reshape(DEC_BATCH, WINDOW, A_KV_WIDTH)
            cv = cache_win_v[i].reshape(DEC_BATCH, WINDOW, A_KV_WIDTH)
            mixed_s, wk_new, wv_new, cv_new = _even_sample(zs, ck, cv, sl, w_spatial[i], b_s, lng, lnb)
            wks.append(wk_new.reshape(DEC_BATCH, WINDOW, A_KV_HEADS, A_HEAD_DIM))
            wvs.append(wv_new.reshape(DEC_BATCH, WINDOW, A_KV_HEADS, A_HEAD_DIM))
            cvs.append(cv_new.reshape(DEC_BATCH, DEC_SEQ, B_WIDTH))
            mixed = _even_prompt(z, mixed_s, sl, w_spatial[i], b_s, lng, lnb)
            x, q = _mm_res_q(mixed, w_out_even_b, i, x, norm_cross[layer], w_xq, layer, 768, 1024)
        else:
            z = _mm_wcast(xn, w_in_ret, i, BF16, IN_RET_TM, IN_RET_TN, "in_ret")
            o_s, rss = _ret_sample(z, state_ret, i, log_g, rss)
            o_all, st_p = _ret_prompt(z, o_s, log_g)
            rsp.append(st_p)
            x, q = _mm_res_q(o_all, w_out_ret_b, i, x, norm_cross[layer], w_xq, layer, 768, 1024)
        o_s = _xattn_sample(q, mem_rows(cache_mem_k), mem_rows(cache_mem_v), layer)
        if layer + 1 < DEPTH:
            x, xn = _xattn_proj(q, mk_all, mv_all, layer, o_s, w_xo_b, x, norm_mix[layer + 1], False)
        else:
            y_prompt, y_sample = _xattn_proj(q, mk_all, mv_all, layer, o_s, w_xo_b, x, norm_final, True)

    y_prompt = y_prompt.reshape(BATCH, SEQ, D_MODEL)
    y_sample = y_sample.reshape(DEC_BATCH, DEC_SEQ, D_MODEL)
    mem_shape = (DEPTH, BATCH, MEM_LEN, X_HEADS, X_HEAD_DIM)
    return (y_prompt, y_sample,
            jnp.stack(wkp), jnp.stack(wvp), jnp.stack(rsp),
            mk_all.reshape(mem_shape), mv_all.reshape(mem_shape),
            jnp.stack(wks), jnp.stack(wvs), rss, jnp.stack(cvs))
```

```python
import functools

import jax
import jax.numpy as jnp
from jax import lax
from jax.experimental import pallas as pl
from jax.experimental.pallas import tpu as pltpu

F32 = jnp.float32
BF16 = jnp.bfloat16

D_MODEL = 2048
BATCH = 2
SEQ = 4096
DEPTH = 4
DEC_BATCH = 32
DEC_SEQ = 8
A_HEADS = 16
A_KV_HEADS = 2
A_HEAD_DIM = 64
A_WIDTH = 1024
A_KV_WIDTH = 128
WINDOW = 128
CHUNK = 128
B_GROUPS = 4
B_WIDTH = 1024
B_GROUP_DIM = 256
EVEN_IN = 5376
R_HEADS = 8
R_KEY_DIM = 256
R_VAL_DIM = 512
R_K_WIDTH = 2048
R_V_WIDTH = 4096
R_CHUNK = 128
RET_IN = 12288
MEM_LEN = 256
X_HEADS = 4
X_HEAD_DIM = 128
X_WIDTH = 512
EPS = 1e-6

N_PROMPT = BATCH * SEQ
N_SAMPLE = DEC_BATCH * DEC_SEQ
N_ROWS = N_PROMPT + N_SAMPLE

OFF_Q, OFF_K, OFF_V, OFF_GA, OFF_U, OFF_VB, OFF_GB = 0, 1024, 1152, 1280, 2304, 3328, 4352

VMEM_LIMIT = 52 * 1024 * 1024


def _cp(n_axes):
    return pltpu.CompilerParams(dimension_semantics=("arbitrary",) * n_axes,
                                vmem_limit_bytes=VMEM_LIMIT)


def _silu(x):
    return x * (1.0 / (1.0 + jnp.exp(-x)))


def _rms(x, g):
    ms = jnp.mean(x * x, axis=-1, keepdims=True)
    return x * lax.rsqrt(ms + EPS) * g


def _standardize(x):
    xc = x - jnp.mean(x, axis=-1, keepdims=True)
    return xc * lax.rsqrt(jnp.mean(xc * xc, axis=-1, keepdims=True) + EPS)


def _dot_nt(a, b):
    return lax.dot_general(a, b, (((1,), (1,)), ((), ())), preferred_element_type=F32)


def _mm_wcast_kernel(a_ref, w_ref, o_ref, wb_ref):
    @pl.when(pl.program_id(1) == 0)
    def _():
        wb_ref[...] = w_ref[0].astype(BF16)

    o_ref[...] = jnp.dot(a_ref[...], wb_ref[...], preferred_element_type=F32).astype(o_ref.dtype)


def _mm_wcast(a, w_stack, layer, out_dtype, tm, tn, name):
    m, k = a.shape
    n = w_stack.shape[2]
    return pl.pallas_call(
        _mm_wcast_kernel,
        grid=(n // tn, m // tm),
        in_specs=[pl.BlockSpec((tm, k), lambda j, i: (i, 0)),
                  pl.BlockSpec((1, k, tn), lambda j, i: (layer, 0, j))],
        out_specs=pl.BlockSpec((tm, tn), lambda j, i: (i, j)),
        out_shape=jax.ShapeDtypeStruct((m, n), out_dtype),
        scratch_shapes=[pltpu.VMEM((k, tn), BF16)],
        compiler_params=_cp(2),
        name=name,
    )(a, w_stack)


IN_EVEN_TM = 2048
IN_EVEN_TN = 768
IN_RET_TM = 1408
IN_RET_TN = 1024


def _in_even_kernel(n_prompt_blks, a_ref, as_ref, w_ref, z_ref, zs_ref, wb_ref):
    m = pl.program_id(1)

    @pl.when(m == 0)
    def _():
        wb_ref[...] = w_ref[0].astype(BF16)

    @pl.when(m < n_prompt_blks)
    def _():
        z_ref[...] = jnp.dot(a_ref[...], wb_ref[...], preferred_element_type=F32).astype(z_ref.dtype)

    @pl.when(m == n_prompt_blks)
    def _():
        zs_ref[...] = jnp.dot(as_ref[...], wb_ref[...], preferred_element_type=F32)


def _in_even(xn, w_stack, layer):
    k = xn.shape[1]
    n = w_stack.shape[2]
    tm, tn = IN_EVEN_TM, IN_EVEN_TN
    nb = N_PROMPT // tm
    return pl.pallas_call(
        functools.partial(_in_even_kernel, nb),
        grid=(n // tn, nb + 1),
        in_specs=[pl.BlockSpec((tm, k), lambda j, m: (jnp.minimum(m, nb - 1), 0)),
                  pl.BlockSpec((N_SAMPLE, k), lambda j, m: (N_PROMPT // N_SAMPLE, 0)),
                  pl.BlockSpec((1, k, tn), lambda j, m: (layer, 0, j))],
        out_specs=[pl.BlockSpec((tm, tn), lambda j, m: (jnp.minimum(m, nb - 1), j)),
                   pl.BlockSpec((N_SAMPLE, tn), lambda j, m: (0, j))],
        out_shape=[jax.ShapeDtypeStruct((N_PROMPT, n), BF16),
                   jax.ShapeDtypeStruct((N_SAMPLE, n), F32)],
        scratch_shapes=[pltpu.VMEM((k, tn), BF16)],
        compiler_params=_cp(2),
        name="in_even",
    )(xn, xn, w_stack)


def _kv_tail_kernel(a_ref, w_ref, o_ref):
    o_ref[...] = jnp.dot(a_ref[...], w_ref[0].astype(BF16), preferred_element_type=F32)


def _kv_tail(xn, w_stack, layer):
    k = xn.shape[1]
    blks_per_seq = SEQ // WINDOW
    width = 2 * A_KV_WIDTH
    return pl.pallas_call(
        _kv_tail_kernel,
        grid=(BATCH,),
        in_specs=[pl.BlockSpec((WINDOW, k), lambda b: (blks_per_seq * (b + 1) - 1, 0)),
                  pl.BlockSpec((1, k, width), lambda b: (layer, 0, OFF_K // width))],
        out_specs=pl.BlockSpec((WINDOW, width), lambda b: (b, 0)),
        out_shape=jax.ShapeDtypeStruct((BATCH * WINDOW, width), F32),
        compiler_params=_cp(1),
        name="kv_tail",
    )(xn, w_stack)


def _mm_res_q_kernel(nk, a_ref, w_ref, x_ref, g_ref, wq_ref, xo_ref, q_ref):
    k = pl.program_id(1)

    @pl.when(k == 0)
    def _():
        xo_ref[...] = x_ref[...]

    xo_ref[...] += jnp.dot(a_ref[...], w_ref[0], preferred_element_type=F32)

    @pl.when(k == nk - 1)
    def _():
        xn = _rms(xo_ref[...], g_ref[...]).astype(BF16)
        q_ref[...] = jnp.dot(xn, wq_ref[0].astype(BF16), preferred_element_type=F32).astype(q_ref.dtype)


def _mm_res_q(a, w_stack, layer, x, g, wq_stack, wq_layer, tm, tk, x_blk0=0):
    m, kdim = a.shape
    d = w_stack.shape[2]
    nq = wq_stack.shape[2]
    nk = kdim // tk
    return pl.pallas_call(
        functools.partial(_mm_res_q_kernel, nk),
        grid=(m // tm, nk),
        in_specs=[pl.BlockSpec((tm, tk), lambda i, k: (i, k)),
                  pl.BlockSpec((1, tk, d), lambda i, k: (layer, k, 0)),
                  pl.BlockSpec((tm, d), lambda i, k: (x_blk0 + i, 0)),
                  pl.BlockSpec((1, d), lambda i, k: (0, 0)),
                  pl.BlockSpec((1, d, nq), lambda i, k: (wq_layer, 0, 0))],
        out_specs=[pl.BlockSpec((tm, d), lambda i, k: (i, 0)),
                   pl.BlockSpec((tm, nq), lambda i, k: (i, 0))],
        out_shape=[jax.ShapeDtypeStruct((m, d), F32),
                   jax.ShapeDtypeStruct((m, nq), BF16)],
        compiler_params=_cp(2),
        name="mm_res_q",
    )(a, w_stack, x, g.reshape(1, d), wq_stack)


def _embed_kernel(n_prompt_blks, xp_ref, xs_ref, g_ref, x_ref, n_ref):
    i = pl.program_id(0)

    @pl.when(i < n_prompt_blks)
    def _():
        x_ref[...] = xp_ref[...]

    @pl.when(i >= n_prompt_blks)
    def _():
        x_ref[...] = xs_ref[...]

    n_ref[...] = _rms(x_ref[...], g_ref[...]).astype(n_ref.dtype)


def _embed(xp, xs, g):
    d = xp.shape[1]
    tm = N_SAMPLE
    nb = N_PROMPT // tm
    return pl.pallas_call(
        functools.partial(_embed_kernel, nb),
        grid=(nb + 1,),
        in_specs=[pl.BlockSpec((tm, d), lambda i: (jnp.minimum(i, nb - 1), 0)),
                  pl.BlockSpec((tm, d), lambda i: (0, 0)),
                  pl.BlockSpec((1, d), lambda i: (0, 0))],
        out_specs=[pl.BlockSpec((tm, d), lambda i: (i, 0)),
                   pl.BlockSpec((tm, d), lambda i: (i, 0))],
        out_shape=[jax.ShapeDtypeStruct((N_ROWS, d), F32),
                   jax.ShapeDtypeStruct((N_ROWS, d), BF16)],
        compiler_params=_cp(1),
        name="embed",
    )(xp, xs, g.reshape(1, d))


def _memkv_kernel(mem_ref, g_ref, wk_ref, wv_ref, mk_ref, mv_ref):
    h = _rms(mem_ref[...], g_ref[0]).astype(BF16)
    mk_ref[0] = jnp.dot(h, wk_ref[0].astype(BF16), preferred_element_type=F32)
    mv_ref[0] = jnp.dot(h, wv_ref[0].astype(BF16), preferred_element_type=F32)


def _memkv(mem, norm_mem, w_xk, w_xv):
    m = mem.shape[0]
    out = jax.ShapeDtypeStruct((DEPTH, m, X_WIDTH), F32)
    return pl.pallas_call(
        _memkv_kernel,
        grid=(DEPTH,),
        in_specs=[pl.BlockSpec((m, D_MODEL), lambda l: (0, 0)),
                  pl.BlockSpec((1, 1, D_MODEL), lambda l: (l, 0, 0)),
                  pl.BlockSpec((1, D_MODEL, X_WIDTH), lambda l: (l, 0, 0)),
                  pl.BlockSpec((1, D_MODEL, X_WIDTH), lambda l: (l, 0, 0))],
        out_specs=[pl.BlockSpec((1, m, X_WIDTH), lambda l: (l, 0, 0)),
                   pl.BlockSpec((1, m, X_WIDTH), lambda l: (l, 0, 0))],
        out_shape=[out, out],
        compiler_params=_cp(1),
        name="memkv",
    )(mem, norm_mem.reshape(DEPTH, 1, D_MODEL), w_xk, w_xv)


SLABS_PER_KV = A_HEADS // A_KV_HEADS // 2
MASK_DIST = 1e30 * 2.0 ** 8


def _expand_band(band, scale):
    x = band.astype(F32)
    if scale != 1.0:
        x = x * scale
    xs = pltpu.roll(x, 64, 1)
    lo = lax.broadcasted_iota(jnp.int32, x.shape, 1) < 64
    z = jnp.zeros_like(x)
    e0 = jnp.concatenate([jnp.where(lo, x, z), jnp.where(lo, z, xs)], axis=0).astype(BF16)
    e1 = jnp.concatenate([jnp.where(lo, xs, z), jnp.where(lo, z, x)], axis=0).astype(BF16)
    return e0, e1


def _band_bias(t, first_key):
    row = lax.broadcasted_iota(jnp.int32, (t, 2 * WINDOW), 0)
    col = lax.broadcasted_iota(jnp.int32, (t, 2 * WINDOW), 1)
    dist = row + WINDOW - col
    valid = (dist >= 0) & (dist < WINDOW) & (col >= first_key)
    return jnp.where(valid, -dist.astype(F32), -MASK_DIST)


def _swa(q_slab, kband, vband, nbias, sl_ref):
    t = nbias.shape[0]
    rows = SLABS_PER_KV * t
    kk = _expand_band(kband, A_HEAD_DIM ** -0.5)
    vv = _expand_band(vband, 1.0)
    nb = jnp.concatenate([nbias] * SLABS_PER_KV, axis=0)
    blk = lax.shift_right_logical(lax.broadcasted_iota(jnp.int32, (rows, 1), 0), t.bit_length() - 1)
    lo = lax.broadcasted_iota(jnp.int32, (rows, 128), 1) < 64
    outs = [None] * (A_HEADS // 2)
    for j in range(A_KV_HEADS):
        slab0 = SLABS_PER_KV * j

        def per_row(table_row, half, slab0=slab0):
            col = jnp.full((rows, 1), sl_ref[table_row, 2 * (slab0 + SLABS_PER_KV - 1) + half], F32)
            for i in reversed(range(SLABS_PER_KV - 1)):
                col = jnp.where(blk == i, sl_ref[table_row, 2 * (slab0 + i) + half], col)
            return col

        q = jnp.concatenate([q_slab(slab0 + i) for i in range(SLABS_PER_KV)], axis=0)
        s = _dot_nt(q, kk[j])
        probs, invs = [], []
        for half in range(2):
            slope = per_row(0, half)
            sink = per_row(1, half)
            sh = s[:, half * 256:(half + 1) * 256] + slope * nb
            m = jnp.maximum(jnp.max(sh, axis=-1, keepdims=True), sink)
            pe = jnp.exp(sh - m)
            den = jnp.sum(pe, axis=-1, keepdims=True) + jnp.exp(sink - m)
            probs.append(pe)
            invs.append(1.0 / den)
        pp = jnp.concatenate(probs, axis=1).astype(BF16)
        o = jnp.dot(pp, vv[j], preferred_element_type=F32) * jnp.where(lo, invs[0], invs[1])
        for i in range(SLABS_PER_KV):
            outs[slab0 + i] = o[i * t:(i + 1) * t]
    return outs


def _spatial_gate_group(ws_ref, bs_ref, g, vn_g, rows):
    r = lax.broadcasted_iota(jnp.int32, (CHUNK, CHUNK), 0)
    c = lax.broadcasted_iota(jnp.int32, (CHUNK, CHUNK), 1)
    w = jnp.where(r >= c, ws_ref[g], 0.0).astype(BF16)
    return jnp.dot(w[:rows], vn_g, preferred_element_type=F32) + bs_ref[g][:rows]


EVEN_STEP_ROWS = 512
EVEN_SUBBLKS = EVEN_STEP_ROWS // WINDOW
EVEN_STEPS = N_PROMPT // EVEN_STEP_ROWS


def _even_fused_kernel(sl_ref, z_ref, kvprev_ref, ws_ref, bs_ref, lng_ref, lnb_ref,
                       w_ref, x_ref, g_ref, wq_ref, xo_ref, q_ref, mixed_ref):
    s_id = pl.program_id(0)
    slot = s_id % 2

    @pl.when(s_id == 0)
    def _():
        mixed_ref[1] = jnp.zeros(mixed_ref.shape[1:], mixed_ref.dtype)

    a = mixed_ref[1 - slot]
    x1 = x_ref[...] + jnp.dot(a, w_ref[0], preferred_element_type=F32)
    xo_ref[...] = x1
    q_ref[...] = jnp.dot(_rms(x1, g_ref[...]).astype(BF16), wq_ref[0],
                         preferred_element_type=F32).astype(q_ref.dtype)

    blk0 = jnp.minimum(s_id, EVEN_STEPS - 1) * EVEN_SUBBLKS
    for b in range(EVEN_SUBBLKS):
        rows = slice(b * WINDOW, (b + 1) * WINDOW)
        c = (blk0 + b) % (SEQ // WINDOW)
        kv_cur = z_ref[rows, OFF_K:OFF_K + 2 * A_KV_WIDTH]
        kv_prev = kvprev_ref[...] if b == 0 else z_ref[(b - 1) * WINDOW:b * WINDOW, OFF_K:OFF_K + 2 * A_KV_WIDTH]
        band = jnp.concatenate([kv_prev, kv_cur], axis=0)
        nbias = _band_bias(WINDOW, jnp.maximum(WINDOW - c * WINDOW, 0))

        def q_slab(p, rows=rows):
            return z_ref[rows, OFF_Q + p * 128:OFF_Q + (p + 1) * 128]

        for p, att in enumerate(_swa(q_slab, band[:, :A_KV_WIDTH], band[:, A_KV_WIDTH:], nbias, sl_ref)):
            ga = z_ref[rows, OFF_GA + p * 128:OFF_GA + (p + 1) * 128].astype(F32)
            mixed_ref[slot, rows, p * 128:(p + 1) * 128] = (_silu(ga) * att).astype(mixed_ref.dtype)

        vn = _standardize(z_ref[rows, OFF_VB:OFF_VB + B_WIDTH].astype(F32)) * lng_ref[...] + lnb_ref[...]
        for g in range(B_GROUPS):
            lo, hi = g * B_GROUP_DIM, (g + 1) * B_GROUP_DIM
            sg = _spatial_gate_group(ws_ref, bs_ref, g, vn[:, lo:hi].astype(BF16), CHUNK)
            u = z_ref[rows, OFF_U + lo:OFF_U + hi].astype(F32)
            gb = z_ref[rows, OFF_GB + lo:OFF_GB + hi].astype(F32)
            mixed_ref[slot, rows, A_WIDTH + lo:A_WIDTH + hi] = (_silu(gb) * (u * sg)).astype(mixed_ref.dtype)


def _even_fused(z, sl, w_s, b_s, lng, lnb, w_out_b, layer_i, x, g, wq_b, layer):
    ns = EVEN_STEPS
    r = EVEN_STEP_ROWS
    kv_blk = OFF_K // (2 * A_KV_WIDTH)

    def cur(s):
        return jnp.minimum(s, ns - 1)

    def prev(s):
        return jnp.maximum(s - 1, 0)

    once = pl.Buffered(1)
    return pl.pallas_call(
        _even_fused_kernel,
        grid=(ns + 1,),
        in_specs=[pl.BlockSpec(memory_space=pltpu.SMEM),
                  pl.BlockSpec((r, EVEN_IN), lambda s: (cur(s), 0)),
                  pl.BlockSpec((WINDOW, 2 * A_KV_WIDTH),
                               lambda s: (jnp.maximum(cur(s) * EVEN_SUBBLKS - 1, 0), kv_blk)),
                  pl.BlockSpec((B_GROUPS, CHUNK, CHUNK), lambda s: (0, 0, 0)),
                  pl.BlockSpec((B_GROUPS, CHUNK, 1), lambda s: (0, 0, 0)),
                  pl.BlockSpec((1, B_WIDTH), lambda s: (0, 0)),
                  pl.BlockSpec((1, B_WIDTH), lambda s: (0, 0)),
                  pl.BlockSpec((1, D_MODEL, D_MODEL), lambda s: (layer_i, 0, 0), pipeline_mode=once),
                  pl.BlockSpec((r, D_MODEL), lambda s: (prev(s), 0)),
                  pl.BlockSpec((1, D_MODEL), lambda s: (0, 0)),
                  pl.BlockSpec((1, D_MODEL, X_WIDTH), lambda s: (layer, 0, 0), pipeline_mode=once)],
        out_specs=[pl.BlockSpec((r, D_MODEL), lambda s: (prev(s), 0)),
                   pl.BlockSpec((r, X_WIDTH), lambda s: (prev(s), 0))],
        out_shape=[jax.ShapeDtypeStruct((N_PROMPT, D_MODEL), F32),
                   jax.ShapeDtypeStruct((N_PROMPT, X_WIDTH), BF16)],
        scratch_shapes=[pltpu.VMEM((2, r, D_MODEL), BF16)],
        compiler_params=_cp(1),
        name="even_fused",
    )(sl, z, z, w_s, b_s, lng, lnb, w_out_b, x, g.reshape(1, D_MODEL), wq_b)


SEQ_PER_STEP = 2
ROWS_PER_STEP = SEQ_PER_STEP * DEC_SEQ
assert DEC_SEQ & (DEC_SEQ - 1) == 0


def _seq_of(row_idx):
    return lax.shift_right_logical(row_idx, DEC_SEQ.bit_length() - 1)


def _even_sample_kernel(sl_ref, z_ref, ck_ref, cv_ref, ws_ref, bs_ref, lng_ref, lnb_ref,
                        o_ref, wk_ref, wv_ref, cvo_ref):
    nbias = _band_bias(ROWS_PER_STEP, 0)
    pad_rows = jnp.zeros((ROWS_PER_STEP - DEC_SEQ, EVEN_IN), F32)
    attn = [[None] * SEQ_PER_STEP for _ in range(A_HEADS // 2)]
    gate = [[None] * SEQ_PER_STEP for _ in range(B_GROUPS)]
    for r in range(SEQ_PER_STEP):
        r0, r1 = r * DEC_SEQ, (r + 1) * DEC_SEQ
        zr = z_ref[r0:r1, :]
        k_new = zr[:, OFF_K:OFF_K + A_KV_WIDTH]
        v_new = zr[:, OFF_V:OFF_V + A_KV_WIDTH]
        ck = ck_ref[r]
        cv = cv_ref[r]
        wk_ref[r, 0:WINDOW - DEC_SEQ, :] = ck[DEC_SEQ:, :]
        wk_ref[r, WINDOW - DEC_SEQ:, :] = k_new
        wv_ref[r, 0:WINDOW - DEC_SEQ, :] = cv[DEC_SEQ:, :]
        wv_ref[r, WINDOW - DEC_SEQ:, :] = v_new
        tail = jnp.zeros((WINDOW - DEC_SEQ, A_KV_WIDTH), F32)
        kband = jnp.concatenate([ck, k_new, tail], axis=0)
        vband = jnp.concatenate([cv, v_new, tail], axis=0)
        zq = jnp.concatenate([zr, pad_rows], axis=0)

        def q_slab(p, zq=zq):
            return zq[:, OFF_Q + p * 128:OFF_Q + (p + 1) * 128].astype(BF16)

        for p, a in enumerate(_swa(q_slab, kband, vband, nbias, sl_ref)):
            ga = zr[:, OFF_GA + p * 128:OFF_GA + (p + 1) * 128]
            attn[p][r] = _silu(ga) * a[:DEC_SEQ]

        vn = _standardize(zr[:, OFF_VB:OFF_VB + B_WIDTH]) * lng_ref[...] + lnb_ref[...]
        cvo_ref[r0:r1, :] = vn
        vn_pad = jnp.concatenate([vn, jnp.zeros((CHUNK - DEC_SEQ, B_WIDTH), F32)], axis=0).astype(BF16)
        for g in range(B_GROUPS):
            lo, hi = g * B_GROUP_DIM, (g + 1) * B_GROUP_DIM
            mixed = _spatial_gate_group(ws_ref, bs_ref, g, vn_pad[:, lo:hi], ROWS_PER_STEP)[:DEC_SEQ]
            u = zr[:, OFF_U + lo:OFF_U + hi]
            gb = zr[:, OFF_GB + lo:OFF_GB + hi]
            gate[g][r] = _silu(gb) * (u * mixed)

    for p in range(A_HEADS // 2):
        o_ref[:, p * 128:(p + 1) * 128] = jnp.concatenate(attn[p], axis=0).astype(o_ref.dtype)
    for g in range(B_GROUPS):
        lo, hi = A_WIDTH + g * B_GROUP_DIM, A_WIDTH + (g + 1) * B_GROUP_DIM
        o_ref[:, lo:hi] = jnp.concatenate(gate[g], axis=0).astype(o_ref.dtype)


def _even_sample(zs, ck, cv, sl, w_s, b_s, lng, lnb):
    n_steps = DEC_BATCH // SEQ_PER_STEP
    win = jax.ShapeDtypeStruct((DEC_BATCH, WINDOW, A_KV_WIDTH), F32)
    return pl.pallas_call(
        _even_sample_kernel,
        grid=(n_steps,),
        in_specs=[pl.BlockSpec(memory_space=pltpu.SMEM),
                  pl.BlockSpec((ROWS_PER_STEP, EVEN_IN), lambda i: (i, 0)),
                  pl.BlockSpec((SEQ_PER_STEP, WINDOW, A_KV_WIDTH), lambda i: (i, 0, 0)),
                  pl.BlockSpec((SEQ_PER_STEP, WINDOW, A_KV_WIDTH), lambda i: (i, 0, 0)),
                  pl.BlockSpec((B_GROUPS, CHUNK, CHUNK), lambda i: (0, 0, 0)),
                  pl.BlockSpec((B_GROUPS, CHUNK, 1), lambda i: (0, 0, 0)),
                  pl.BlockSpec((1, B_WIDTH), lambda i: (0, 0)),
                  pl.BlockSpec((1, B_WIDTH), lambda i: (0, 0))],
        out_specs=[pl.BlockSpec((ROWS_PER_STEP, D_MODEL), lambda i: (i, 0)),
                   pl.BlockSpec((SEQ_PER_STEP, WINDOW, A_KV_WIDTH), lambda i: (i, 0, 0)),
                   pl.BlockSpec((SEQ_PER_STEP, WINDOW, A_KV_WIDTH), lambda i: (i, 0, 0)),
                   pl.BlockSpec((ROWS_PER_STEP, B_WIDTH), lambda i: (i, 0))],
        out_shape=[jax.ShapeDtypeStruct((N_SAMPLE, D_MODEL), BF16), win, win,
                   jax.ShapeDtypeStruct((N_SAMPLE, B_WIDTH), F32)],
        compiler_params=_cp(1),
        name="even_sample",
    )(sl, zs, ck, cv, w_s, b_s, lng, lnb)


RET_ROWS = 256
RET_BLKS_PER_SEQ = SEQ // RET_ROWS
RET_PROMPT_BLKS = N_PROMPT // RET_ROWS


def _gated_groupnorm(o, g):
    return (_silu(g.astype(F32)) * _standardize(o)).astype(BF16)


RET_HEADS_PER_STEP = 2


def _ret_prompt_kernel(lg_ref, q_ref, k_ref, v_ref, g_ref, os_ref, o_ref, so_ref, st_ref, dm_ref, dc_ref):
    hp = pl.program_id(0)
    bc = pl.program_id(1)
    t = RET_ROWS
    k_scale = R_KEY_DIM ** -0.5

    @pl.when(bc == 0)
    def _():
        row = lax.broadcasted_iota(jnp.int32, (t, t), 0)
        col = lax.broadcasted_iota(jnp.int32, (t, t), 1)
        rel = (row - col).astype(F32)
        ti = lax.broadcasted_iota(jnp.int32, (t, 1), 0).astype(F32)
        for i in range(RET_HEADS_PER_STEP):
            lg = lg_ref[hp * RET_HEADS_PER_STEP + i]
            dm_ref[i] = jnp.where(rel >= 0, jnp.exp(jnp.maximum(rel, 0.0) * lg), 0.0) * k_scale
            dc_ref[i, 0] = jnp.exp((ti + 1.0) * lg)
            dc_ref[i, 1] = jnp.exp((t - 1.0 - ti) * lg) * k_scale

    @pl.when(bc >= RET_PROMPT_BLKS)
    def _():
        o_ref[...] = os_ref[...]

    @pl.when(bc < RET_PROMPT_BLKS)
    def _():
        c = bc % RET_BLKS_PER_SEQ

        @pl.when(c == 0)
        def _():
            st_ref[...] = jnp.zeros_like(st_ref)

        for i in range(RET_HEADS_PER_STEP):
            lg = lg_ref[hp * RET_HEADS_PER_STEP + i]
            q_decay = dc_ref[i, 0]
            k_decay = dc_ref[i, 1]
            chunk_decay = jnp.exp(jnp.full((1, R_VAL_DIM), float(t), F32) * lg)
            kc = slice(i * R_KEY_DIM, (i + 1) * R_KEY_DIM)
            vc = slice(i * R_VAL_DIM, (i + 1) * R_VAL_DIM)
            q = q_ref[:, kc]
            k = k_ref[:, kc]
            v = v_ref[:, vc]
            st = st_ref[i]
            scores = _dot_nt(q, k) * dm_ref[i]
            inner = jnp.dot(scores.astype(BF16), v, preferred_element_type=F32)
            cross = jnp.dot(q, st.astype(BF16), preferred_element_type=F32) * q_decay
            kd_t = (k.astype(F32) * k_decay).T.astype(BF16)
            st_ref[i] = chunk_decay * st + jnp.dot(kd_t, v, preferred_element_type=F32)
            o_ref[:, vc] = _gated_groupnorm(inner + cross, g_ref[:, vc])

        @pl.when(c == RET_BLKS_PER_SEQ - 1)
        def _():
            so_ref[0] = st_ref[...]


def _ret_prompt(z, o_sample, log_g):
    nb = RET_PROMPT_BLKS
    hps = RET_HEADS_PER_STEP
    kw, vw = hps * R_KEY_DIM, hps * R_VAL_DIM
    kcol = R_K_WIDTH // kw
    vcol = 2 * R_K_WIDTH // vw
    gcol = vcol + R_HEADS // hps

    def rb(bc):
        return jnp.minimum(bc, nb - 1)

    return pl.pallas_call(
        _ret_prompt_kernel,
        grid=(R_HEADS // hps, nb + 1),
        in_specs=[pl.BlockSpec(memory_space=pltpu.SMEM),
                  pl.BlockSpec((RET_ROWS, kw), lambda h, bc: (rb(bc), h)),
                  pl.BlockSpec((RET_ROWS, kw), lambda h, bc: (rb(bc), kcol + h)),
                  pl.BlockSpec((RET_ROWS, vw), lambda h, bc: (rb(bc), vcol + h)),
                  pl.BlockSpec((RET_ROWS, vw), lambda h, bc: (rb(bc), gcol + h)),
                  pl.BlockSpec((N_SAMPLE, vw), lambda h, bc: (0, h))],
        out_specs=[pl.BlockSpec((RET_ROWS, vw), lambda h, bc: (bc, h)),
                   pl.BlockSpec((1, hps, R_KEY_DIM, R_VAL_DIM),
                                lambda h, bc: (rb(bc) // RET_BLKS_PER_SEQ, h, 0, 0))],
        out_shape=[jax.ShapeDtypeStruct((N_ROWS, R_V_WIDTH), BF16),
                   jax.ShapeDtypeStruct((BATCH, R_HEADS, R_KEY_DIM, R_VAL_DIM), F32)],
        scratch_shapes=[pltpu.VMEM((hps, R_KEY_DIM, R_VAL_DIM), F32), pltpu.VMEM((hps, RET_ROWS, RET_ROWS), F32),
                        pltpu.VMEM((hps, 2, RET_ROWS, 1), F32)],
        compiler_params=_cp(2),
        name="ret_prompt",
    )(log_g, z, z, z, z, o_sample)


RET_SAMPLE_SEQS = 4
RET_SAMPLE_ROWS = RET_SAMPLE_SEQS * DEC_SEQ


def _ret_sample_kernel(lg_ref, q_ref, k_ref, v_ref, g_ref, s_ref, *rest):
    o_ref, so_ref = rest[-2:]
    h = pl.program_id(1)
    lg = lg_ref[h]
    n = RET_SAMPLE_ROWS
    t = DEC_SEQ
    row = lax.broadcasted_iota(jnp.int32, (n, n), 0)
    col = lax.broadcasted_iota(jnp.int32, (n, n), 1)
    rel = (row - col).astype(F32)
    same = _seq_of(row) == _seq_of(col)
    k_scale = R_KEY_DIM ** -0.5
    dmat = jnp.where(same & (rel >= 0), jnp.exp(jnp.maximum(rel, 0.0) * lg), 0.0) * k_scale
    ri = lax.broadcasted_iota(jnp.int32, (n, 1), 0)
    ti = (ri & (t - 1)).astype(F32)
    q_decay = jnp.exp((ti + 1.0) * lg)
    k_decay = jnp.exp((t - 1.0 - ti) * lg) * k_scale
    step_decay = jnp.exp(jnp.full((1, R_VAL_DIM), float(t), F32) * lg)

    q = q_ref[...]
    k = k_ref[...]
    v = v_ref[...]
    scores = _dot_nt(q, k) * dmat
    o = jnp.dot(scores.astype(BF16), v, preferred_element_type=F32)
    kd = k.astype(F32) * k_decay
    k_pad = jnp.zeros((R_CHUNK - n, R_KEY_DIM), F32)
    v_pad = jnp.concatenate([v.astype(F32), jnp.zeros((R_CHUNK - n, R_VAL_DIM), F32)], axis=0).astype(BF16)
    for r in range(RET_SAMPLE_SEQS):
        mine = _seq_of(ri) == r
        st = s_ref[0, r, 0]
        cross = jnp.dot(q, st.astype(BF16), preferred_element_type=F32) * q_decay
        o = o + jnp.where(mine, cross, 0.0)
        kd_t = jnp.concatenate([jnp.where(mine, kd, 0.0), k_pad], axis=0).T.astype(BF16)
        so_ref[0, r, 0] = step_decay * st + jnp.dot(kd_t, v_pad, preferred_element_type=F32)
    o_ref[...] = _gated_groupnorm(o, g_ref[...])


def _ret_sample(z, state_all, idx, log_g, new_states):
    n_steps = DEC_BATCH // RET_SAMPLE_SEQS
    rb0 = N_PROMPT // RET_SAMPLE_ROWS
    kcol = R_K_WIDTH // R_KEY_DIM
    vcol = 2 * R_K_WIDTH // R_VAL_DIM
    gcol = vcol + R_HEADS
    st_spec = pl.BlockSpec((1, RET_SAMPLE_SEQS, 1, R_KEY_DIM, R_VAL_DIM), lambda i, h: (idx, i, h, 0, 0))
    in_specs = [pl.BlockSpec(memory_space=pltpu.SMEM),
                pl.BlockSpec((RET_SAMPLE_ROWS, R_KEY_DIM), lambda i, h: (rb0 + i, h)),
                pl.BlockSpec((RET_SAMPLE_ROWS, R_KEY_DIM), lambda i, h: (rb0 + i, kcol + h)),
                pl.BlockSpec((RET_SAMPLE_ROWS, R_VAL_DIM), lambda i, h: (rb0 + i, vcol + h)),
                pl.BlockSpec((RET_SAMPLE_ROWS, R_VAL_DIM), lambda i, h: (rb0 + i, gcol + h)),
                st_spec]
    args = [log_g, z, z, z, z, state_all]
    aliases = {}
    if new_states is not None:
        in_specs.append(pl.BlockSpec(memory_space=pl.ANY))
        args.append(new_states)
        aliases = {len(args) - 1: 1}
    return pl.pallas_call(
        _ret_sample_kernel,
        grid=(n_steps, R_HEADS),
        in_specs=in_specs,
        out_specs=[pl.BlockSpec((RET_SAMPLE_ROWS, R_VAL_DIM), lambda i, h: (i, h)), st_spec],
        out_shape=[jax.ShapeDtypeStruct((N_SAMPLE, R_V_WIDTH), BF16),
                   jax.ShapeDtypeStruct(state_all.shape, F32)],
        input_output_aliases=aliases,
        compiler_params=_cp(2),
        name="ret_sample",
    )(*args)


XQ_ROWS = 256
XQ_PROMPT_BLKS = N_PROMPT // XQ_ROWS
XQ_BLKS_PER_SEQ = SEQ // XQ_ROWS


def _xattn_heads(q, mk_head, mv_head):
    outs = []
    for h in range(X_HEADS):
        s = _dot_nt(q[:, h * X_HEAD_DIM:(h + 1) * X_HEAD_DIM], mk_head(h)) * (X_HEAD_DIM ** -0.5)
        m = jnp.max(s, axis=-1, keepdims=True)
        p = jnp.exp(s - m)
        den = jnp.sum(p, axis=-1, keepdims=True)
        outs.append(jnp.dot(p.astype(BF16), mv_head(h), preferred_element_type=F32) * (1.0 / den))
    return outs


def _xattn_proj_kernel(final, q_ref, mk_ref, mv_ref, os_ref, w_ref, xp_ref, xs_ref, g_ref, out_a, out_b):
    s_id = pl.program_id(0)

    def project(o, x_ref):
        return x_ref[...] + jnp.dot(o, w_ref[0], preferred_element_type=F32)

    @pl.when(s_id < XQ_PROMPT_BLKS)
    def _():
        def head_of(ref):
            return lambda h: ref[0, :, h * X_HEAD_DIM:(h + 1) * X_HEAD_DIM].astype(BF16)

        outs = _xattn_heads(q_ref[...], head_of(mk_ref), head_of(mv_ref))
        x2 = project(jnp.concatenate(outs, axis=1).astype(BF16), xp_ref)
        if final:
            out_a[...] = _rms(x2, g_ref[...])
        else:
            out_a[...] = x2
            out_b[...] = _rms(x2, g_ref[...]).astype(out_b.dtype)

    @pl.when(s_id >= XQ_PROMPT_BLKS)
    def _():
        x2 = project(os_ref[...], xs_ref)
        if final:
            out_b[...] = _rms(x2, g_ref[...])
        else:
            out_a[...] = x2
            out_b[...] = _rms(x2, g_ref[...]).astype(out_b.dtype)


def _xattn_proj(q, mk_all, mv_all, layer, o_sample, w_stack, x_prompt, x_sample, g, final):
    nb = XQ_PROMPT_BLKS
    d = w_stack.shape[2]
    xs_blk = (x_sample.shape[0] - N_SAMPLE) // N_SAMPLE

    def rb(s):
        return jnp.minimum(s, nb - 1)

    mem_spec = pl.BlockSpec((1, MEM_LEN, X_WIDTH), lambda s: (layer, rb(s) // XQ_BLKS_PER_SEQ, 0))
    if final:
        out_specs = [pl.BlockSpec((XQ_ROWS, d), lambda s: (rb(s), 0)),
                     pl.BlockSpec((N_SAMPLE, d), lambda s: (0, 0))]
        out_shape = [jax.ShapeDtypeStruct((N_PROMPT, d), F32), jax.ShapeDtypeStruct((N_SAMPLE, d), F32)]
    else:
        out_specs = [pl.BlockSpec((XQ_ROWS, d), lambda s: (s, 0)),
                     pl.BlockSpec((XQ_ROWS, d), lambda s: (s, 0))]
        out_shape = [jax.ShapeDtypeStruct((N_ROWS, d), F32), jax.ShapeDtypeStruct((N_ROWS, d), BF16)]
    return pl.pallas_call(
        functools.partial(_xattn_proj_kernel, final),
        grid=(nb + 1,),
        in_specs=[pl.BlockSpec((XQ_ROWS, X_WIDTH), lambda s: (rb(s), 0)),
                  mem_spec, mem_spec,
                  pl.BlockSpec((N_SAMPLE, X_WIDTH), lambda s: (0, 0)),
                  pl.BlockSpec((1, X_WIDTH, d), lambda s: (layer, 0, 0)),
                  pl.BlockSpec((XQ_ROWS, d), lambda s: (rb(s), 0)),
                  pl.BlockSpec((N_SAMPLE, d), lambda s: (xs_blk, 0)),
                  pl.BlockSpec((1, d), lambda s: (0, 0))],
        out_specs=out_specs,
        out_shape=out_shape,
        compiler_params=_cp(1),
        name="xattn_proj",
    )(q, mk_all, mv_all, o_sample, w_stack, x_prompt, x_sample, g.reshape(1, d))


XS_SEQS = 4
XS_ROWS = XS_SEQS * DEC_SEQ


def _xattn_sample_kernel(q_ref, ck_ref, cv_ref, o_ref):
    q = q_ref[...]
    ri = lax.broadcasted_iota(jnp.int32, (XS_ROWS, 1), 0)
    acc = [jnp.zeros((XS_ROWS, X_HEAD_DIM), F32) for _ in range(X_HEADS)]
    for r in range(XS_SEQS):
        mine = _seq_of(ri) == r

        def head_of(ref, r=r):
            return lambda h: ref[0, r, pl.ds(h, MEM_LEN, stride=X_HEADS), :].astype(BF16)

        outs = _xattn_heads(q, head_of(ck_ref), head_of(cv_ref))
        acc = [a + jnp.where(mine, o, 0.0) for a, o in zip(acc, outs)]
    for h, a in enumerate(acc):
        o_ref[:, h * X_HEAD_DIM:(h + 1) * X_HEAD_DIM] = a.astype(o_ref.dtype)


def _xattn_sample(q, ck_all, cv_all, layer):
    n_steps = DEC_BATCH // XS_SEQS
    rb0 = (q.shape[0] - N_SAMPLE) // XS_ROWS
    mem_spec = pl.BlockSpec((1, XS_SEQS, MEM_LEN * X_HEADS, X_HEAD_DIM), lambda i: (layer, i, 0, 0))
    return pl.pallas_call(
        _xattn_sample_kernel,
        grid=(n_steps,),
        in_specs=[pl.BlockSpec((XS_ROWS, X_WIDTH), lambda i: (rb0 + i, 0)), mem_spec, mem_spec],
        out_specs=pl.BlockSpec((XS_ROWS, X_WIDTH), lambda i: (i, 0)),
        out_shape=jax.ShapeDtypeStruct((N_SAMPLE, X_WIDTH), BF16),
        compiler_params=_cp(1),
        name="xattn_sample",
    )(q, ck_all, cv_all)


def kernel(x_prompt, x_sample, cache_win_k, cache_win_v, state_ret, cache_mem_k, cache_mem_v, mem_prompt,
           norm_mix, w_in_even, attn_sinks, w_spatial, b_spatial, norm_v_g, norm_v_b, w_out_even,
           w_in_ret, w_out_ret, norm_cross, norm_mem, w_xq, w_xk, w_xv, w_xo, norm_final):
    w_out_even_b = w_out_even.astype(BF16)
    w_out_ret_b = w_out_ret.astype(BF16)
    w_xo_b = w_xo.astype(BF16)
    w_xq_b = w_xq.astype(BF16)

    slopes = jnp.exp2(-8.0 * (jnp.arange(A_HEADS, dtype=F32) + 1.0) / A_HEADS)
    log_g = jnp.log1p(-jnp.exp2(-5.0 - jnp.arange(R_HEADS, dtype=F32)))

    def mem_rows(cache):
        return cache.reshape(DEPTH, DEC_BATCH, MEM_LEN * X_HEADS, X_HEAD_DIM)

    mk_all, mv_all = _memkv(mem_prompt.reshape(BATCH * MEM_LEN, D_MODEL), norm_mem, w_xk, w_xv)

    x, xn = _embed(x_prompt.reshape(N_PROMPT, D_MODEL), x_sample.reshape(N_SAMPLE, D_MODEL), norm_mix[0])
    wkp, wvp, rsp, wks, wvs, cvs = [], [], [], [], [], []
    rss = None
    for layer in range(DEPTH):
        i = layer // 2
        if layer % 2 == 0:
            sl = jnp.stack([slopes, attn_sinks[i].astype(F32)])
            b_s = b_spatial[i].reshape(B_GROUPS, CHUNK, 1)
            lng = norm_v_g[i].reshape(1, B_WIDTH)
            lnb = norm_v_b[i].reshape(1, B_WIDTH)
            z, zs = _in_even(xn, w_in_even, i)
            kv_tail = _kv_tail(xn, w_in_even, i)
            wkp.append(kv_tail[:, :A_KV_WIDTH].reshape(BATCH, WINDOW, A_KV_HEADS, A_HEAD_DIM))
            wvp.append(kv_tail[:, A_KV_WIDTH:].reshape(BATCH, WINDOW, A_KV_HEADS, A_HEAD_DIM))
            ck = cache_win_k[i].reshape(DEC_BATCH, WINDOW, A_KV_WIDTH)
            cv = cache_win_v[i].reshape(DEC_BATCH, WINDOW, A_KV_WIDTH)
            mixed_s, wk_new, wv_new, cv_new = _even_sample(zs, ck, cv, sl, w_spatial[i], b_s, lng, lnb)
            wks.append(wk_new.reshape(DEC_BATCH, WINDOW, A_KV_HEADS, A_HEAD_DIM))
            wvs.append(wv_new.reshape(DEC_BATCH, WINDOW, A_KV_HEADS, A_HEAD_DIM))
            cvs.append(cv_new.reshape(DEC_BATCH, DEC_SEQ, B_WIDTH))
            x_p, q = _even_fused(z, sl, w_spatial[i], b_s, lng, lnb, w_out_even_b, i, x, norm_cross[layer],
                                 w_xq_b, layer)
            x_s, q_s = _mm_res_q(mixed_s, w_out_even_b, i, x, norm_cross[layer], w_xq_b, layer,
                                 N_SAMPLE, 1024, x_blk0=N_PROMPT // N_SAMPLE)
        else:
            z = _mm_wcast(xn, w_in_ret, i, BF16, IN_RET_TM, IN_RET_TN, "in_ret")
            o_s, rss = _ret_sample(z, state_ret, i, log_g, rss)
            o_all, st_p = _ret_prompt(z, o_s, log_g)
            rsp.append(st_p)
            x_p, q = _mm_res_q(o_all, w_out_ret_b, i, x, norm_cross[layer], w_xq_b, layer, 768, 1024)
            x_s, q_s = x_p, q
        o_s = _xattn_sample(q_s, mem_rows(cache_mem_k), mem_rows(cache_mem_v), layer)
        if layer + 1 < DEPTH:
            x, xn = _xattn_proj(q, mk_all, mv_all, layer, o_s, w_xo_b, x_p, x_s, norm_mix[layer + 1], False)
        else:
            y_prompt, y_sample = _xattn_proj(q, mk_all, mv_all, layer, o_s, w_xo_b, x_p, x_s, norm_final, True)

    y_prompt = y_prompt.reshape(BATCH, SEQ, D_MODEL)
    y_sample = y_sample.reshape(DEC_BATCH, DEC_SEQ, D_MODEL)
    mem_shape = (DEPTH, BATCH, MEM_LEN, X_HEADS, X_HEAD_DIM)
    return (y_prompt, y_sample,
            jnp.stack(wkp), jnp.stack(wvp), jnp.stack(rsp),
            mk_all.reshape(mem_shape), mv_all.reshape(mem_shape),
            jnp.stack(wks), jnp.stack(wvs), rss, jnp.stack(cvs))
```

```python
import functools

import jax
import jax.numpy as jnp
from jax import lax
from jax.experimental import pallas as pl
from jax.experimental.pallas import tpu as pltpu

F32 = jnp.float32
BF16 = jnp.bfloat16

D_MODEL = 2048
BATCH = 2
SEQ = 4096
DEPTH = 4
DEC_BATCH = 32
DEC_SEQ = 8
A_HEADS = 16
A_KV_HEADS = 2
A_HEAD_DIM = 64
A_WIDTH = 1024
A_KV_WIDTH = 128
WINDOW = 128
CHUNK = 128
B_GROUPS = 4
B_WIDTH = 1024
B_GROUP_DIM = 256
EVEN_IN = 5376
R_HEADS = 8
R_KEY_DIM = 256
R_VAL_DIM = 512
R_K_WIDTH = 2048
R_V_WIDTH = 4096
R_CHUNK = 128
RET_IN = 12288
MEM_LEN = 256
X_HEADS = 4
X_HEAD_DIM = 128
X_WIDTH = 512
EPS = 1e-6

N_PROMPT = BATCH * SEQ
N_SAMPLE = DEC_BATCH * DEC_SEQ
N_ROWS = N_PROMPT + N_SAMPLE

OFF_Q, OFF_K, OFF_V, OFF_GA, OFF_U, OFF_VB, OFF_GB = 0, 1024, 1152, 1280, 2304, 3328, 4352

VMEM_LIMIT = 52 * 1024 * 1024


def _cp(n_axes):
    return pltpu.CompilerParams(dimension_semantics=("arbitrary",) * n_axes,
                                vmem_limit_bytes=VMEM_LIMIT)


def _silu(x):
    return x * (1.0 / (1.0 + jnp.exp(-x)))


def _rms(x, g):
    ms = jnp.mean(x * x, axis=-1, keepdims=True)
    return x * lax.rsqrt(ms + EPS) * g


def _standardize(x):
    xc = x - jnp.mean(x, axis=-1, keepdims=True)
    return xc * lax.rsqrt(jnp.mean(xc * xc, axis=-1, keepdims=True) + EPS)


def _dot_nt(a, b):
    return lax.dot_general(a, b, (((1,), (1,)), ((), ())), preferred_element_type=F32)


def _mm_wcast_kernel(a_ref, w_ref, o_ref, wb_ref):
    @pl.when(pl.program_id(1) == 0)
    def _():
        wb_ref[...] = w_ref[0].astype(BF16)

    o_ref[...] = jnp.dot(a_ref[...], wb_ref[...], preferred_element_type=F32).astype(o_ref.dtype)


def _mm_wcast(a, w_stack, layer, out_dtype, tm, tn, name):
    m, k = a.shape
    n = w_stack.shape[2]
    return pl.pallas_call(
        _mm_wcast_kernel,
        grid=(n // tn, m // tm),
        in_specs=[pl.BlockSpec((tm, k), lambda j, i: (i, 0)),
                  pl.BlockSpec((1, k, tn), lambda j, i: (layer, 0, j))],
        out_specs=pl.BlockSpec((tm, tn), lambda j, i: (i, j)),
        out_shape=jax.ShapeDtypeStruct((m, n), out_dtype),
        scratch_shapes=[pltpu.VMEM((k, tn), BF16)],
        compiler_params=_cp(2),
        name=name,
    )(a, w_stack)


IN_EVEN_TM = 2048
IN_EVEN_TN = 768
IN_RET_TM = 1408
IN_RET_TN = 1024


def _in_even_kernel(n_prompt_blks, a_ref, as_ref, w_ref, z_ref, zs_ref, wb_ref):
    m = pl.program_id(1)

    @pl.when(m == 0)
    def _():
        wb_ref[...] = w_ref[0].astype(BF16)

    @pl.when(m < n_prompt_blks)
    def _():
        z_ref[...] = jnp.dot(a_ref[...], wb_ref[...], preferred_element_type=F32).astype(z_ref.dtype)

    @pl.when(m == n_prompt_blks)
    def _():
        zs_ref[...] = jnp.dot(as_ref[...], wb_ref[...], preferred_element_type=F32)


def _in_even(xn, w_stack, layer):
    k = xn.shape[1]
    n = w_stack.shape[2]
    tm, tn = IN_EVEN_TM, IN_EVEN_TN
    nb = N_PROMPT // tm
    return pl.pallas_call(
        functools.partial(_in_even_kernel, nb),
        grid=(n // tn, nb + 1),
        in_specs=[pl.BlockSpec((tm, k), lambda j, m: (jnp.minimum(m, nb - 1), 0)),
                  pl.BlockSpec((N_SAMPLE, k), lambda j, m: (N_PROMPT // N_SAMPLE, 0)),
                  pl.BlockSpec((1, k, tn), lambda j, m: (layer, 0, j))],
        out_specs=[pl.BlockSpec((tm, tn), lambda j, m: (jnp.minimum(m, nb - 1), j)),
                   pl.BlockSpec((N_SAMPLE, tn), lambda j, m: (0, j))],
        out_shape=[jax.ShapeDtypeStruct((N_PROMPT, n), BF16),
                   jax.ShapeDtypeStruct((N_SAMPLE, n), F32)],
        scratch_shapes=[pltpu.VMEM((k, tn), BF16)],
        compiler_params=_cp(2),
        name="in_even",
    )(xn, xn, w_stack)


def _kv_tail_kernel(a_ref, w_ref, o_ref):
    o_ref[...] = jnp.dot(a_ref[...], w_ref[0].astype(BF16), preferred_element_type=F32)


def _kv_tail(xn, w_stack, layer):
    k = xn.shape[1]
    blks_per_seq = SEQ // WINDOW
    width = 2 * A_KV_WIDTH
    return pl.pallas_call(
        _kv_tail_kernel,
        grid=(BATCH,),
        in_specs=[pl.BlockSpec((WINDOW, k), lambda b: (blks_per_seq * (b + 1) - 1, 0)),
                  pl.BlockSpec((1, k, width), lambda b: (layer, 0, OFF_K // width))],
        out_specs=pl.BlockSpec((WINDOW, width), lambda b: (b, 0)),
        out_shape=jax.ShapeDtypeStruct((BATCH * WINDOW, width), F32),
        compiler_params=_cp(1),
        name="kv_tail",
    )(xn, w_stack)


def _mm_res_q_kernel(nk, a_ref, w_ref, x_ref, g_ref, wq_ref, xo_ref, q_ref):
    k = pl.program_id(1)

    @pl.when(k == 0)
    def _():
        xo_ref[...] = x_ref[...]

    xo_ref[...] += jnp.dot(a_ref[...], w_ref[0], preferred_element_type=F32)

    @pl.when(k == nk - 1)
    def _():
        xn = _rms(xo_ref[...], g_ref[...]).astype(BF16)
        q_ref[...] = jnp.dot(xn, wq_ref[0].astype(BF16), preferred_element_type=F32).astype(q_ref.dtype)


def _mm_res_q(a, w_stack, layer, x, g, wq_stack, wq_layer, tm, tk, x_blk0=0):
    m, kdim = a.shape
    d = w_stack.shape[2]
    nq = wq_stack.shape[2]
    nk = kdim // tk
    return pl.pallas_call(
        functools.partial(_mm_res_q_kernel, nk),
        grid=(m // tm, nk),
        in_specs=[pl.BlockSpec((tm, tk), lambda i, k: (i, k)),
                  pl.BlockSpec((1, tk, d), lambda i, k: (layer, k, 0)),
                  pl.BlockSpec((tm, d), lambda i, k: (x_blk0 + i, 0)),
                  pl.BlockSpec((1, d), lambda i, k: (0, 0)),
                  pl.BlockSpec((1, d, nq), lambda i, k: (wq_layer, 0, 0))],
        out_specs=[pl.BlockSpec((tm, d), lambda i, k: (i, 0)),
                   pl.BlockSpec((tm, nq), lambda i, k: (i, 0))],
        out_shape=[jax.ShapeDtypeStruct((m, d), F32),
                   jax.ShapeDtypeStruct((m, nq), BF16)],
        compiler_params=_cp(2),
        name="mm_res_q",
    )(a, w_stack, x, g.reshape(1, d), wq_stack)


def _embed_kernel(n_prompt_blks, xp_ref, xs_ref, g_ref, x_ref, n_ref):
    i = pl.program_id(0)

    @pl.when(i < n_prompt_blks)
    def _():
        x_ref[...] = xp_ref[...]

    @pl.when(i >= n_prompt_blks)
    def _():
        x_ref[...] = xs_ref[...]

    n_ref[...] = _rms(x_ref[...], g_ref[...]).astype(n_ref.dtype)


def _embed(xp, xs, g):
    d = xp.shape[1]
    tm = N_SAMPLE
    nb = N_PROMPT // tm
    return pl.pallas_call(
        functools.partial(_embed_kernel, nb),
        grid=(nb + 1,),
        in_specs=[pl.BlockSpec((tm, d), lambda i: (jnp.minimum(i, nb - 1), 0)),
                  pl.BlockSpec((tm, d), lambda i: (0, 0)),
                  pl.BlockSpec((1, d), lambda i: (0, 0))],
        out_specs=[pl.BlockSpec((tm, d), lambda i: (i, 0)),
                   pl.BlockSpec((tm, d), lambda i: (i, 0))],
        out_shape=[jax.ShapeDtypeStruct((N_ROWS, d), F32),
                   jax.ShapeDtypeStruct((N_ROWS, d), BF16)],
        compiler_params=_cp(1),
        name="embed",
    )(xp, xs, g.reshape(1, d))


def _memkv_kernel(mem_ref, g_ref, wk_ref, wv_ref, mk_ref, mv_ref):
    h = _rms(mem_ref[...], g_ref[0]).astype(BF16)
    mk_ref[0] = jnp.dot(h, wk_ref[0].astype(BF16), preferred_element_type=F32)
    mv_ref[0] = jnp.dot(h, wv_ref[0].astype(BF16), preferred_element_type=F32)


def _memkv(mem, norm_mem, w_xk, w_xv):
    m = mem.shape[0]
    out = jax.ShapeDtypeStruct((DEPTH, m, X_WIDTH), F32)
    return pl.pallas_call(
        _memkv_kernel,
        grid=(DEPTH,),
        in_specs=[pl.BlockSpec((m, D_MODEL), lambda l: (0, 0)),
                  pl.BlockSpec((1, 1, D_MODEL), lambda l: (l, 0, 0)),
                  pl.BlockSpec((1, D_MODEL, X_WIDTH), lambda l: (l, 0, 0)),
                  pl.BlockSpec((1, D_MODEL, X_WIDTH), lambda l: (l, 0, 0))],
        out_specs=[pl.BlockSpec((1, m, X_WIDTH), lambda l: (l, 0, 0)),
                   pl.BlockSpec((1, m, X_WIDTH), lambda l: (l, 0, 0))],
        out_shape=[out, out],
        compiler_params=_cp(1),
        name="memkv",
    )(mem, norm_mem.reshape(DEPTH, 1, D_MODEL), w_xk, w_xv)


SLABS_PER_KV = A_HEADS // A_KV_HEADS // 2
MASK_DIST = 1e30 * 2.0 ** 8


def _expand_band(band, scale):
    x = band.astype(F32)
    if scale != 1.0:
        x = x * scale
    xs = pltpu.roll(x, 64, 1)
    lo = lax.broadcasted_iota(jnp.int32, x.shape, 1) < 64
    z = jnp.zeros_like(x)
    e0 = jnp.concatenate([jnp.where(lo, x, z), jnp.where(lo, z, xs)], axis=0).astype(BF16)
    e1 = jnp.concatenate([jnp.where(lo, xs, z), jnp.where(lo, z, x)], axis=0).astype(BF16)
    return e0, e1


def _band_bias(t, first_key):
    row = lax.broadcasted_iota(jnp.int32, (t, 2 * WINDOW), 0)
    col = lax.broadcasted_iota(jnp.int32, (t, 2 * WINDOW), 1)
    dist = row + WINDOW - col
    valid = (dist >= 0) & (dist < WINDOW) & (col >= first_key)
    return jnp.where(valid, -dist.astype(F32), -MASK_DIST)


def _swa(q_slab, kband, vband, nbias, sl_ref):
    t = nbias.shape[0]
    rows = SLABS_PER_KV * t
    kk = _expand_band(kband, A_HEAD_DIM ** -0.5)
    vv = _expand_band(vband, 1.0)
    nb = jnp.concatenate([nbias] * SLABS_PER_KV, axis=0)
    blk = lax.shift_right_logical(lax.broadcasted_iota(jnp.int32, (rows, 1), 0), t.bit_length() - 1)
    lo = lax.broadcasted_iota(jnp.int32, (rows, 128), 1) < 64
    outs = [None] * (A_HEADS // 2)
    for j in range(A_KV_HEADS):
        slab0 = SLABS_PER_KV * j

        def per_row(table_row, half, slab0=slab0):
            col = jnp.full((rows, 1), sl_ref[table_row, 2 * (slab0 + SLABS_PER_KV - 1) + half], F32)
            for i in reversed(range(SLABS_PER_KV - 1)):
                col = jnp.where(blk == i, sl_ref[table_row, 2 * (slab0 + i) + half], col)
            return col

        q = jnp.concatenate([q_slab(slab0 + i) for i in range(SLABS_PER_KV)], axis=0)
        s = _dot_nt(q, kk[j])
        probs, invs = [], []
        for half in range(2):
            slope = per_row(0, half)
            sink = per_row(1, half)
            sh = s[:, half * 256:(half + 1) * 256] + slope * nb
            m = jnp.maximum(jnp.max(sh, axis=-1, keepdims=True), sink)
            pe = jnp.exp(sh - m)
            den = jnp.sum(pe, axis=-1, keepdims=True) + jnp.exp(sink - m)
            probs.append(pe)
            invs.append(1.0 / den)
        pp = jnp.concatenate(probs, axis=1).astype(BF16)
        o = jnp.dot(pp, vv[j], preferred_element_type=F32) * jnp.where(lo, invs[0], invs[1])
        for i in range(SLABS_PER_KV):
            outs[slab0 + i] = o[i * t:(i + 1) * t]
    return outs


def _spatial_gate_group(ws_ref, bs_ref, g, vn_g, rows):
    r = lax.broadcasted_iota(jnp.int32, (CHUNK, CHUNK), 0)
    c = lax.broadcasted_iota(jnp.int32, (CHUNK, CHUNK), 1)
    w = jnp.where(r >= c, ws_ref[g], 0.0).astype(BF16)
    return jnp.dot(w[:rows], vn_g, preferred_element_type=F32) + bs_ref[g][:rows]


EVEN_STEP_ROWS = 512
EVEN_SUBBLKS = EVEN_STEP_ROWS // WINDOW
EVEN_STEPS = N_PROMPT // EVEN_STEP_ROWS


def _even_fused_kernel(sl_ref, z_ref, kvprev_ref, ws_ref, bs_ref, lng_ref, lnb_ref,
                       w_ref, x_ref, g_ref, wq_ref, xo_ref, q_ref, mixed_ref):
    s_id = pl.program_id(0)
    slot = s_id % 2

    @pl.when(s_id == 0)
    def _():
        mixed_ref[1] = jnp.zeros(mixed_ref.shape[1:], mixed_ref.dtype)

    a = mixed_ref[1 - slot]
    x1 = x_ref[...] + jnp.dot(a, w_ref[0], preferred_element_type=F32)
    xo_ref[...] = x1
    q_ref[...] = jnp.dot(_rms(x1, g_ref[...]).astype(BF16), wq_ref[0],
                         preferred_element_type=F32).astype(q_ref.dtype)

    blk0 = jnp.minimum(s_id, EVEN_STEPS - 1) * EVEN_SUBBLKS
    for b in range(EVEN_SUBBLKS):
        rows = slice(b * WINDOW, (b + 1) * WINDOW)
        c = (blk0 + b) % (SEQ // WINDOW)
        kv_cur = z_ref[rows, OFF_K:OFF_K + 2 * A_KV_WIDTH]
        kv_prev = kvprev_ref[...] if b == 0 else z_ref[(b - 1) * WINDOW:b * WINDOW, OFF_K:OFF_K + 2 * A_KV_WIDTH]
        band = jnp.concatenate([kv_prev, kv_cur], axis=0)
        nbias = _band_bias(WINDOW, jnp.maximum(WINDOW - c * WINDOW, 0))

        def q_slab(p, rows=rows):
            return z_ref[rows, OFF_Q + p * 128:OFF_Q + (p + 1) * 128]

        for p, att in enumerate(_swa(q_slab, band[:, :A_KV_WIDTH], band[:, A_KV_WIDTH:], nbias, sl_ref)):
            ga = z_ref[rows, OFF_GA + p * 128:OFF_GA + (p + 1) * 128].astype(F32)
            mixed_ref[slot, rows, p * 128:(p + 1) * 128] = (_silu(ga) * att).astype(mixed_ref.dtype)

        vn = _standardize(z_ref[rows, OFF_VB:OFF_VB + B_WIDTH].astype(F32)) * lng_ref[...] + lnb_ref[...]
        for g in range(B_GROUPS):
            lo, hi = g * B_GROUP_DIM, (g + 1) * B_GROUP_DIM
            sg = _spatial_gate_group(ws_ref, bs_ref, g, vn[:, lo:hi].astype(BF16), CHUNK)
            u = z_ref[rows, OFF_U + lo:OFF_U + hi].astype(F32)
            gb = z_ref[rows, OFF_GB + lo:OFF_GB + hi].astype(F32)
            mixed_ref[slot, rows, A_WIDTH + lo:A_WIDTH + hi] = (_silu(gb) * (u * sg)).astype(mixed_ref.dtype)


def _even_fused(z, sl, w_s, b_s, lng, lnb, w_out_b, layer_i, x, g, wq_b, layer):
    ns = EVEN_STEPS
    r = EVEN_STEP_ROWS
    kv_blk = OFF_K // (2 * A_KV_WIDTH)

    def cur(s):
        return jnp.minimum(s, ns - 1)

    def prev(s):
        return jnp.maximum(s - 1, 0)

    once = pl.Buffered(1)
    return pl.pallas_call(
        _even_fused_kernel,
        grid=(ns + 1,),
        in_specs=[pl.BlockSpec(memory_space=pltpu.SMEM),
                  pl.BlockSpec((r, EVEN_IN), lambda s: (cur(s), 0)),
                  pl.BlockSpec((WINDOW, 2 * A_KV_WIDTH),
                               lambda s: (jnp.maximum(cur(s) * EVEN_SUBBLKS - 1, 0), kv_blk)),
                  pl.BlockSpec((B_GROUPS, CHUNK, CHUNK), lambda s: (0, 0, 0)),
                  pl.BlockSpec((B_GROUPS, CHUNK, 1), lambda s: (0, 0, 0)),
                  pl.BlockSpec((1, B_WIDTH), lambda s: (0, 0)),
                  pl.BlockSpec((1, B_WIDTH), lambda s: (0, 0)),
                  pl.BlockSpec((1, D_MODEL, D_MODEL), lambda s: (layer_i, 0, 0), pipeline_mode=once),
                  pl.BlockSpec((r, D_MODEL), lambda s: (prev(s), 0)),
                  pl.BlockSpec((1, D_MODEL), lambda s: (0, 0)),
                  pl.BlockSpec((1, D_MODEL, X_WIDTH), lambda s: (layer, 0, 0), pipeline_mode=once)],
        out_specs=[pl.BlockSpec((r, D_MODEL), lambda s: (prev(s), 0)),
                   pl.BlockSpec((r, X_WIDTH), lambda s: (prev(s), 0))],
        out_shape=[jax.ShapeDtypeStruct((N_PROMPT, D_MODEL), F32),
                   jax.ShapeDtypeStruct((N_PROMPT, X_WIDTH), BF16)],
        scratch_shapes=[pltpu.VMEM((2, r, D_MODEL), BF16)],
        compiler_params=_cp(1),
        name="even_fused",
    )(sl, z, z, w_s, b_s, lng, lnb, w_out_b, x, g.reshape(1, D_MODEL), wq_b)


SEQ_PER_STEP = 2
ROWS_PER_STEP = SEQ_PER_STEP * DEC_SEQ
assert DEC_SEQ & (DEC_SEQ - 1) == 0


def _seq_of(row_idx):
    return lax.shift_right_logical(row_idx, DEC_SEQ.bit_length() - 1)


def _even_sample_kernel(sl_ref, z_ref, ck_ref, cv_ref, ws_ref, bs_ref, lng_ref, lnb_ref,
                        o_ref, wk_ref, wv_ref, cvo_ref):
    nbias = _band_bias(ROWS_PER_STEP, 0)
    pad_rows = jnp.zeros((ROWS_PER_STEP - DEC_SEQ, EVEN_IN), F32)
    attn = [[None] * SEQ_PER_STEP for _ in range(A_HEADS // 2)]
    gate = [[None] * SEQ_PER_STEP for _ in range(B_GROUPS)]
    for r in range(SEQ_PER_STEP):
        r0, r1 = r * DEC_SEQ, (r + 1) * DEC_SEQ
        zr = z_ref[r0:r1, :]
        k_new = zr[:, OFF_K:OFF_K + A_KV_WIDTH]
        v_new = zr[:, OFF_V:OFF_V + A_KV_WIDTH]
        ck = ck_ref[r]
        cv = cv_ref[r]
        wk_ref[r, 0:WINDOW - DEC_SEQ, :] = ck[DEC_SEQ:, :]
        wk_ref[r, WINDOW - DEC_SEQ:, :] = k_new
        wv_ref[r, 0:WINDOW - DEC_SEQ, :] = cv[DEC_SEQ:, :]
        wv_ref[r, WINDOW - DEC_SEQ:, :] = v_new
        tail = jnp.zeros((WINDOW - DEC_SEQ, A_KV_WIDTH), F32)
        kband = jnp.concatenate([ck, k_new, tail], axis=0)
        vband = jnp.concatenate([cv, v_new, tail], axis=0)
        zq = jnp.concatenate([zr, pad_rows], axis=0)

        def q_slab(p, zq=zq):
            return zq[:, OFF_Q + p * 128:OFF_Q + (p + 1) * 128].astype(BF16)

        for p, a in enumerate(_swa(q_slab, kband, vband, nbias, sl_ref)):
            ga = zr[:, OFF_GA + p * 128:OFF_GA + (p + 1) * 128]
            attn[p][r] = _silu(ga) * a[:DEC_SEQ]

        vn = _standardize(zr[:, OFF_VB:OFF_VB + B_WIDTH]) * lng_ref[...] + lnb_ref[...]
        cvo_ref[r0:r1, :] = vn
        vn_pad = jnp.concatenate([vn, jnp.zeros((CHUNK - DEC_SEQ, B_WIDTH), F32)], axis=0).astype(BF16)
        for g in range(B_GROUPS):
            lo, hi = g * B_GROUP_DIM, (g + 1) * B_GROUP_DIM
            mixed = _spatial_gate_group(ws_ref, bs_ref, g, vn_pad[:, lo:hi], ROWS_PER_STEP)[:DEC_SEQ]
            u = zr[:, OFF_U + lo:OFF_U + hi]
            gb = zr[:, OFF_GB + lo:OFF_GB + hi]
            gate[g][r] = _silu(gb) * (u * mixed)

    for p in range(A_HEADS // 2):
        o_ref[:, p * 128:(p + 1) * 128] = jnp.concatenate(attn[p], axis=0).astype(o_ref.dtype)
    for g in range(B_GROUPS):
        lo, hi = A_WIDTH + g * B_GROUP_DIM, A_WIDTH + (g + 1) * B_GROUP_DIM
        o_ref[:, lo:hi] = jnp.concatenate(gate[g], axis=0).astype(o_ref.dtype)


def _even_sample(zs, ck, cv, sl, w_s, b_s, lng, lnb):
    n_steps = DEC_BATCH // SEQ_PER_STEP
    win = jax.ShapeDtypeStruct((DEC_BATCH, WINDOW, A_KV_WIDTH), F32)
    return pl.pallas_call(
        _even_sample_kernel,
        grid=(n_steps,),
        in_specs=[pl.BlockSpec(memory_space=pltpu.SMEM),
                  pl.BlockSpec((ROWS_PER_STEP, EVEN_IN), lambda i: (i, 0)),
                  pl.BlockSpec((SEQ_PER_STEP, WINDOW, A_KV_WIDTH), lambda i: (i, 0, 0)),
                  pl.BlockSpec((SEQ_PER_STEP, WINDOW, A_KV_WIDTH), lambda i: (i, 0, 0)),
                  pl.BlockSpec((B_GROUPS, CHUNK, CHUNK), lambda i: (0, 0, 0)),
                  pl.BlockSpec((B_GROUPS, CHUNK, 1), lambda i: (0, 0, 0)),
                  pl.BlockSpec((1, B_WIDTH), lambda i: (0, 0)),
                  pl.BlockSpec((1, B_WIDTH), lambda i: (0, 0))],
        out_specs=[pl.BlockSpec((ROWS_PER_STEP, D_MODEL), lambda i: (i, 0)),
                   pl.BlockSpec((SEQ_PER_STEP, WINDOW, A_KV_WIDTH), lambda i: (i, 0, 0)),
                   pl.BlockSpec((SEQ_PER_STEP, WINDOW, A_KV_WIDTH), lambda i: (i, 0, 0)),
                   pl.BlockSpec((ROWS_PER_STEP, B_WIDTH), lambda i: (i, 0))],
        out_shape=[jax.ShapeDtypeStruct((N_SAMPLE, D_MODEL), BF16), win, win,
                   jax.ShapeDtypeStruct((N_SAMPLE, B_WIDTH), F32)],
        compiler_params=_cp(1),
        name="even_sample",
    )(sl, zs, ck, cv, w_s, b_s, lng, lnb)


RET_ROWS = 256
RET_BLKS_PER_SEQ = SEQ // RET_ROWS
RET_PROMPT_BLKS = N_PROMPT // RET_ROWS


def _gated_groupnorm(o, g):
    return (_silu(g.astype(F32)) * _standardize(o)).astype(BF16)


def _ret_fused_kernel(lg_ref, z_ref, w_ref, x_ref, g_ref, wq_ref, xo_ref, q_ref, so_ref, og_ref, dm_ref, dc_ref):
    s_id = pl.program_id(0)
    slot = s_id % 2
    t = RET_ROWS
    k_scale = R_KEY_DIM ** -0.5
    live = s_id < RET_PROMPT_BLKS
    c = jnp.minimum(s_id, RET_PROMPT_BLKS - 1) % RET_BLKS_PER_SEQ

    @pl.when(s_id == 0)
    def _():
        og_ref[1] = jnp.zeros(og_ref.shape[1:], og_ref.dtype)
        row = lax.broadcasted_iota(jnp.int32, (t, t), 0)
        col = lax.broadcasted_iota(jnp.int32, (t, t), 1)
        rel = (row - col).astype(F32)
        ti = lax.broadcasted_iota(jnp.int32, (t, 1), 0).astype(F32)
        for h in range(R_HEADS):
            lg = lg_ref[h]
            dm_ref[h] = jnp.where(rel >= 0, jnp.exp(jnp.maximum(rel, 0.0) * lg), 0.0) * k_scale
            dc_ref[h, 0] = jnp.exp((ti + 1.0) * lg)
            dc_ref[h, 1] = jnp.exp((t - 1.0 - ti) * lg) * k_scale

    @pl.when(c == 0)
    def _():
        so_ref[...] = jnp.zeros_like(so_ref)

    x1 = x_ref[...] + jnp.dot(og_ref[1 - slot], w_ref[0], preferred_element_type=F32)
    xo_ref[...] = x1
    q_ref[...] = jnp.dot(_rms(x1, g_ref[...]).astype(BF16), wq_ref[0],
                         preferred_element_type=F32).astype(q_ref.dtype)

    for h in range(R_HEADS):
        lg = lg_ref[h]
        q_decay = dc_ref[h, 0]
        k_decay = dc_ref[h, 1]
        chunk_decay = jnp.exp(jnp.full((1, R_VAL_DIM), float(t), F32) * lg)
        q = z_ref[:, h * R_KEY_DIM:(h + 1) * R_KEY_DIM]
        k = z_ref[:, R_K_WIDTH + h * R_KEY_DIM:R_K_WIDTH + (h + 1) * R_KEY_DIM]
        v = z_ref[:, 2 * R_K_WIDTH + h * R_VAL_DIM:2 * R_K_WIDTH + (h + 1) * R_VAL_DIM]
        g = z_ref[:, 2 * R_K_WIDTH + R_V_WIDTH + h * R_VAL_DIM:2 * R_K_WIDTH + R_V_WIDTH + (h + 1) * R_VAL_DIM]
        st = so_ref[0, h]
        scores = _dot_nt(q, k) * dm_ref[h]
        inner = jnp.dot(scores.astype(BF16), v, preferred_element_type=F32)
        cross = jnp.dot(q, st.astype(BF16), preferred_element_type=F32) * q_decay
        kd_t = (k.astype(F32) * k_decay).T.astype(BF16)
        st_new = chunk_decay * st + jnp.dot(kd_t, v, preferred_element_type=F32)
        so_ref[0, h] = jnp.where(live, st_new, st)
        og_ref[slot, :, h * R_VAL_DIM:(h + 1) * R_VAL_DIM] = _gated_groupnorm(inner + cross, g)


RET_FUSED_VMEM = 60 * 1024 * 1024


def _ret_fused(z, log_g, w_out_b, layer_i, x, g, wq_b, layer):
    nb = RET_PROMPT_BLKS
    r = RET_ROWS

    def cur(s):
        return jnp.minimum(s, nb - 1)

    def prev(s):
        return jnp.maximum(s - 1, 0)

    once = pl.Buffered(1)
    return pl.pallas_call(
        _ret_fused_kernel,
        grid=(nb + 1,),
        in_specs=[pl.BlockSpec(memory_space=pltpu.SMEM),
                  pl.BlockSpec((r, RET_IN), lambda s: (cur(s), 0)),
                  pl.BlockSpec((1, R_V_WIDTH, D_MODEL), lambda s: (layer_i, 0, 0), pipeline_mode=once),
                  pl.BlockSpec((r, D_MODEL), lambda s: (prev(s), 0)),
                  pl.BlockSpec((1, D_MODEL), lambda s: (0, 0)),
                  pl.BlockSpec((1, D_MODEL, X_WIDTH), lambda s: (layer, 0, 0), pipeline_mode=once)],
        out_specs=[pl.BlockSpec((r, D_MODEL), lambda s: (prev(s), 0)),
                   pl.BlockSpec((r, X_WIDTH), lambda s: (prev(s), 0)),
                   pl.BlockSpec((1, R_HEADS, R_KEY_DIM, R_VAL_DIM),
                                lambda s: (cur(s) // RET_BLKS_PER_SEQ, 0, 0, 0))],
        out_shape=[jax.ShapeDtypeStruct((N_PROMPT, D_MODEL), F32),
                   jax.ShapeDtypeStruct((N_PROMPT, X_WIDTH), BF16),
                   jax.ShapeDtypeStruct((BATCH, R_HEADS, R_KEY_DIM, R_VAL_DIM), F32)],
        scratch_shapes=[pltpu.VMEM((2, r, R_V_WIDTH), BF16),
                        pltpu.VMEM((R_HEADS, r, r), F32),
                        pltpu.VMEM((R_HEADS, 2, r, 1), F32)],
        compiler_params=pltpu.CompilerParams(dimension_semantics=("arbitrary",),
                                             vmem_limit_bytes=RET_FUSED_VMEM),
        name="ret_fused",
    )(log_g, z, w_out_b, x, g.reshape(1, D_MODEL), wq_b)


RET_SAMPLE_SEQS = 8
RET_SAMPLE_ROWS = RET_SAMPLE_SEQS * DEC_SEQ


def _ret_sample_kernel(lg_ref, q_ref, k_ref, v_ref, g_ref, s_ref, *rest):
    o_ref, so_ref = rest[-2:]
    h = pl.program_id(1)
    lg = lg_ref[h]
    n = RET_SAMPLE_ROWS
    t = DEC_SEQ
    row = lax.broadcasted_iota(jnp.int32, (n, n), 0)
    col = lax.broadcasted_iota(jnp.int32, (n, n), 1)
    rel = (row - col).astype(F32)
    same = _seq_of(row) == _seq_of(col)
    k_scale = R_KEY_DIM ** -0.5
    dmat = jnp.where(same & (rel >= 0), jnp.exp(jnp.maximum(rel, 0.0) * lg), 0.0) * k_scale
    ri = lax.broadcasted_iota(jnp.int32, (n, 1), 0)
    ti = (ri & (t - 1)).astype(F32)
    q_decay = jnp.exp((ti + 1.0) * lg)
    k_decay = jnp.exp((t - 1.0 - ti) * lg) * k_scale
    step_decay = jnp.exp(jnp.full((1, R_VAL_DIM), float(t), F32) * lg)

    q = q_ref[...]
    k = k_ref[...]
    v = v_ref[...]
    scores = _dot_nt(q, k) * dmat
    o = jnp.dot(scores.astype(BF16), v, preferred_element_type=F32)
    kd = k.astype(F32) * k_decay
    k_pad = jnp.zeros((R_CHUNK - n, R_KEY_DIM), F32)
    v_pad = jnp.concatenate([v.astype(F32), jnp.zeros((R_CHUNK - n, R_VAL_DIM), F32)], axis=0).astype(BF16)
    for r in range(RET_SAMPLE_SEQS):
        mine = _seq_of(ri) == r
        st = s_ref[0, r, 0]
        cross = jnp.dot(q, st.astype(BF16), preferred_element_type=F32) * q_decay
        o = o + jnp.where(mine, cross, 0.0)
        kd_t = jnp.concatenate([jnp.where(mine, kd, 0.0), k_pad], axis=0).T.astype(BF16)
        so_ref[0, r, 0] = step_decay * st + jnp.dot(kd_t, v_pad, preferred_element_type=F32)
    o_ref[...] = _gated_groupnorm(o, g_ref[...])


def _ret_sample(z, state_all, idx, log_g, new_states):
    n_steps = DEC_BATCH // RET_SAMPLE_SEQS
    rb0 = N_PROMPT // RET_SAMPLE_ROWS
    kcol = R_K_WIDTH // R_KEY_DIM
    vcol = 2 * R_K_WIDTH // R_VAL_DIM
    gcol = vcol + R_HEADS
    st_spec = pl.BlockSpec((1, RET_SAMPLE_SEQS, 1, R_KEY_DIM, R_VAL_DIM), lambda i, h: (idx, i, h, 0, 0))
    in_specs = [pl.BlockSpec(memory_space=pltpu.SMEM),
                pl.BlockSpec((RET_SAMPLE_ROWS, R_KEY_DIM), lambda i, h: (rb0 + i, h)),
                pl.BlockSpec((RET_SAMPLE_ROWS, R_KEY_DIM), lambda i, h: (rb0 + i, kcol + h)),
                pl.BlockSpec((RET_SAMPLE_ROWS, R_VAL_DIM), lambda i, h: (rb0 + i, vcol + h)),
                pl.BlockSpec((RET_SAMPLE_ROWS, R_VAL_DIM), lambda i, h: (rb0 + i, gcol + h)),
                st_spec]
    args = [log_g, z, z, z, z, state_all]
    aliases = {}
    if new_states is not None:
        in_specs.append(pl.BlockSpec(memory_space=pl.ANY))
        args.append(new_states)
        aliases = {len(args) - 1: 1}
    return pl.pallas_call(
        _ret_sample_kernel,
        grid=(n_steps, R_HEADS),
        in_specs=in_specs,
        out_specs=[pl.BlockSpec((RET_SAMPLE_ROWS, R_VAL_DIM), lambda i, h: (i, h)), st_spec],
        out_shape=[jax.ShapeDtypeStruct((N_SAMPLE, R_V_WIDTH), BF16),
                   jax.ShapeDtypeStruct(state_all.shape, F32)],
        input_output_aliases=aliases,
        compiler_params=_cp(2),
        name="ret_sample",
    )(*args)


XQ_ROWS = 256
XQ_PROMPT_BLKS = N_PROMPT // XQ_ROWS
XQ_BLKS_PER_SEQ = SEQ // XQ_ROWS


def _xattn_heads(q, mk_head, mv_head):
    outs = []
    for h in range(X_HEADS):
        s = _dot_nt(q[:, h * X_HEAD_DIM:(h + 1) * X_HEAD_DIM], mk_head(h)) * (X_HEAD_DIM ** -0.5)
        m = jnp.max(s, axis=-1, keepdims=True)
        p = jnp.exp(s - m)
        den = jnp.sum(p, axis=-1, keepdims=True)
        outs.append(jnp.dot(p.astype(BF16), mv_head(h), preferred_element_type=F32) * (1.0 / den))
    return outs


def _xattn_proj_kernel(final, q_ref, mk_ref, mv_ref, os_ref, w_ref, xp_ref, xs_ref, g_ref, out_a, out_b):
    s_id = pl.program_id(0)

    def project(o, x_ref):
        return x_ref[...] + jnp.dot(o, w_ref[0], preferred_element_type=F32)

    @pl.when(s_id < XQ_PROMPT_BLKS)
    def _():
        def head_of(ref):
            return lambda h: ref[0, :, h * X_HEAD_DIM:(h + 1) * X_HEAD_DIM].astype(BF16)

        outs = _xattn_heads(q_ref[...], head_of(mk_ref), head_of(mv_ref))
        x2 = project(jnp.concatenate(outs, axis=1).astype(BF16), xp_ref)
        if final:
            out_a[...] = _rms(x2, g_ref[...])
        else:
            out_a[...] = x2
            out_b[...] = _rms(x2, g_ref[...]).astype(out_b.dtype)

    @pl.when(s_id >= XQ_PROMPT_BLKS)
    def _():
        x2 = project(os_ref[...], xs_ref)
        if final:
            out_b[...] = _rms(x2, g_ref[...])
        else:
            out_a[...] = x2
            out_b[...] = _rms(x2, g_ref[...]).astype(out_b.dtype)


def _xattn_proj(q, mk_all, mv_all, layer, o_sample, w_stack, x_prompt, x_sample, g, final):
    nb = XQ_PROMPT_BLKS
    d = w_stack.shape[2]
    xs_blk = (x_sample.shape[0] - N_SAMPLE) // N_SAMPLE

    def rb(s):
        return jnp.minimum(s, nb - 1)

    mem_spec = pl.BlockSpec((1, MEM_LEN, X_WIDTH), lambda s: (layer, rb(s) // XQ_BLKS_PER_SEQ, 0))
    if final:
        out_specs = [pl.BlockSpec((XQ_ROWS, d), lambda s: (rb(s), 0)),
                     pl.BlockSpec((N_SAMPLE, d), lambda s: (0, 0))]
        out_shape = [jax.ShapeDtypeStruct((N_PROMPT, d), F32), jax.ShapeDtypeStruct((N_SAMPLE, d), F32)]
    else:
        out_specs = [pl.BlockSpec((XQ_ROWS, d), lambda s: (s, 0)),
                     pl.BlockSpec((XQ_ROWS, d), lambda s: (s, 0))]
        out_shape = [jax.ShapeDtypeStruct((N_ROWS, d), F32), jax.ShapeDtypeStruct((N_ROWS, d), BF16)]
    return pl.pallas_call(
        functools.partial(_xattn_proj_kernel, final),
        grid=(nb + 1,),
        in_specs=[pl.BlockSpec((XQ_ROWS, X_WIDTH), lambda s: (rb(s), 0)),
                  mem_spec, mem_spec,
                  pl.BlockSpec((N_SAMPLE, X_WIDTH), lambda s: (0, 0)),
                  pl.BlockSpec((1, X_WIDTH, d), lambda s: (layer, 0, 0)),
                  pl.BlockSpec((XQ_ROWS, d), lambda s: (rb(s), 0)),
                  pl.BlockSpec((N_SAMPLE, d), lambda s: (xs_blk, 0)),
                  pl.BlockSpec((1, d), lambda s: (0, 0))],
        out_specs=out_specs,
        out_shape=out_shape,
        compiler_params=_cp(1),
        name="xattn_proj",
    )(q, mk_all, mv_all, o_sample, w_stack, x_prompt, x_sample, g.reshape(1, d))


XS_SEQS = 4
XS_ROWS = XS_SEQS * DEC_SEQ


def _xattn_sample_kernel(q_ref, ck_ref, cv_ref, o_ref):
    q = q_ref[...]
    ri = lax.broadcasted_iota(jnp.int32, (XS_ROWS, 1), 0)
    acc = [jnp.zeros((XS_ROWS, X_HEAD_DIM), F32) for _ in range(X_HEADS)]
    for r in range(XS_SEQS):
        mine = _seq_of(ri) == r

        def head_of(ref, r=r):
            return lambda h: ref[0, r, pl.ds(h, MEM_LEN, stride=X_HEADS), :].astype(BF16)

        outs = _xattn_heads(q, head_of(ck_ref), head_of(cv_ref))
        acc = [a + jnp.where(mine, o, 0.0) for a, o in zip(acc, outs)]
    for h, a in enumerate(acc):
        o_ref[:, h * X_HEAD_DIM:(h + 1) * X_HEAD_DIM] = a.astype(o_ref.dtype)


def _xattn_sample(q, ck_all, cv_all, layer):
    n_steps = DEC_BATCH // XS_SEQS
    rb0 = (q.shape[0] - N_SAMPLE) // XS_ROWS
    mem_spec = pl.BlockSpec((1, XS_SEQS, MEM_LEN * X_HEADS, X_HEAD_DIM), lambda i: (layer, i, 0, 0))
    return pl.pallas_call(
        _xattn_sample_kernel,
        grid=(n_steps,),
        in_specs=[pl.BlockSpec((XS_ROWS, X_WIDTH), lambda i: (rb0 + i, 0)), mem_spec, mem_spec],
        out_specs=pl.BlockSpec((XS_ROWS, X_WIDTH), lambda i: (i, 0)),
        out_shape=jax.ShapeDtypeStruct((N_SAMPLE, X_WIDTH), BF16),
        compiler_params=_cp(1),
        name="xattn_sample",
    )(q, ck_all, cv_all)


def kernel(x_prompt, x_sample, cache_win_k, cache_win_v, state_ret, cache_mem_k, cache_mem_v, mem_prompt,
           norm_mix, w_in_even, attn_sinks, w_spatial, b_spatial, norm_v_g, norm_v_b, w_out_even,
           w_in_ret, w_out_ret, norm_cross, norm_mem, w_xq, w_xk, w_xv, w_xo, norm_final):
    w_out_even_b = w_out_even.astype(BF16)
    w_out_ret_b = w_out_ret.astype(BF16)
    w_xo_b = w_xo.astype(BF16)
    w_xq_b = w_xq.astype(BF16)

    slopes = jnp.exp2(-8.0 * (jnp.arange(A_HEADS, dtype=F32) + 1.0) / A_HEADS)
    log_g = jnp.log1p(-jnp.exp2(-5.0 - jnp.arange(R_HEADS, dtype=F32)))

    def mem_rows(cache):
        return cache.reshape(DEPTH, DEC_BATCH, MEM_LEN * X_HEADS, X_HEAD_DIM)

    mk_all, mv_all = _memkv(mem_prompt.reshape(BATCH * MEM_LEN, D_MODEL), norm_mem, w_xk, w_xv)

    x, xn = _embed(x_prompt.reshape(N_PROMPT, D_MODEL), x_sample.reshape(N_SAMPLE, D_MODEL), norm_mix[0])
    wkp, wvp, rsp, wks, wvs, cvs = [], [], [], [], [], []
    rss = None
    for layer in range(DEPTH):
        i = layer // 2
        if layer % 2 == 0:
            sl = jnp.stack([slopes, attn_sinks[i].astype(F32)])
            b_s = b_spatial[i].reshape(B_GROUPS, CHUNK, 1)
            lng = norm_v_g[i].reshape(1, B_WIDTH)
            lnb = norm_v_b[i].reshape(1, B_WIDTH)
            z, zs = _in_even(xn, w_in_even, i)
            kv_tail = _kv_tail(xn, w_in_even, i)
            wkp.append(kv_tail[:, :A_KV_WIDTH].reshape(BATCH, WINDOW, A_KV_HEADS, A_HEAD_DIM))
            wvp.append(kv_tail[:, A_KV_WIDTH:].reshape(BATCH, WINDOW, A_KV_HEADS, A_HEAD_DIM))
            ck = cache_win_k[i].reshape(DEC_BATCH, WINDOW, A_KV_WIDTH)
            cv = cache_win_v[i].reshape(DEC_BATCH, WINDOW, A_KV_WIDTH)
            mixed_s, wk_new, wv_new, cv_new = _even_sample(zs, ck, cv, sl, w_spatial[i], b_s, lng, lnb)
            wks.append(wk_new.reshape(DEC_BATCH, WINDOW, A_KV_HEADS, A_HEAD_DIM))
            wvs.append(wv_new.reshape(DEC_BATCH, WINDOW, A_KV_HEADS, A_HEAD_DIM))
            cvs.append(cv_new.reshape(DEC_BATCH, DEC_SEQ, B_WIDTH))
            x_p, q = _even_fused(z, sl, w_spatial[i], b_s, lng, lnb, w_out_even_b, i, x, norm_cross[layer],
                                 w_xq_b, layer)
            x_s, q_s = _mm_res_q(mixed_s, w_out_even_b, i, x, norm_cross[layer], w_xq_b, layer,
                                 N_SAMPLE, 1024, x_blk0=N_PROMPT // N_SAMPLE)
        else:
            z = _mm_wcast(xn, w_in_ret, i, BF16, IN_RET_TM, IN_RET_TN, "in_ret")
            o_s, rss = _ret_sample(z, state_ret, i, log_g, rss)
            x_p, q, st_p = _ret_fused(z, log_g, w_out_ret_b, i, x, norm_cross[layer], w_xq_b, layer)
            rsp.append(st_p)
            x_s, q_s = _mm_res_q(o_s, w_out_ret_b, i, x, norm_cross[layer], w_xq_b, layer,
                                 N_SAMPLE, 1024, x_blk0=N_PROMPT // N_SAMPLE)
        o_s = _xattn_sample(q_s, mem_rows(cache_mem_k), mem_rows(cache_mem_v), layer)
        if layer + 1 < DEPTH:
            x, xn = _xattn_proj(q, mk_all, mv_all, layer, o_s, w_xo_b, x_p, x_s, norm_mix[layer + 1], False)
        else:
            y_prompt, y_sample = _xattn_proj(q, mk_all, mv_all, layer, o_s, w_xo_b, x_p, x_s, norm_final, True)

    y_prompt = y_prompt.reshape(BATCH, SEQ, D_MODEL)
    y_sample = y_sample.reshape(DEC_BATCH, DEC_SEQ, D_MODEL)
    mem_shape = (DEPTH, BATCH, MEM_LEN, X_HEADS, X_HEAD_DIM)
    return (y_prompt, y_sample,
            jnp.stack(wkp), jnp.stack(wvp), jnp.stack(rsp),
            mk_all.reshape(mem_shape), mv_all.reshape(mem_shape),
            jnp.stack(wks), jnp.stack(wvs), rss, jnp.stack(cvs))
```

```python
import functools

import jax
import jax.numpy as jnp
from jax import lax
from jax.experimental import pallas as pl
from jax.experimental.pallas import tpu as pltpu

F32 = jnp.float32
BF16 = jnp.bfloat16

D_MODEL = 2048
BATCH = 2
SEQ = 4096
DEPTH = 4
DEC_BATCH = 32
DEC_SEQ = 8
A_HEADS = 16
A_KV_HEADS = 2
A_HEAD_DIM = 64
A_WIDTH = 1024
A_KV_WIDTH = 128
WINDOW = 128
CHUNK = 128
B_GROUPS = 4
B_WIDTH = 1024
B_GROUP_DIM = 256
EVEN_IN = 5376
R_HEADS = 8
R_KEY_DIM = 256
R_VAL_DIM = 512
R_K_WIDTH = 2048
R_V_WIDTH = 4096
R_CHUNK = 128
RET_IN = 12288
MEM_LEN = 256
X_HEADS = 4
X_HEAD_DIM = 128
X_WIDTH = 512
EPS = 1e-6

N_PROMPT = BATCH * SEQ
N_SAMPLE = DEC_BATCH * DEC_SEQ
N_ROWS = N_PROMPT + N_SAMPLE

OFF_Q, OFF_K, OFF_V, OFF_GA, OFF_U, OFF_VB, OFF_GB = 0, 1024, 1152, 1280, 2304, 3328, 4352

VMEM_LIMIT = 52 * 1024 * 1024
FUSED_VMEM_LIMIT = 56 * 1024 * 1024


def _cp(n_axes, vmem_limit=VMEM_LIMIT):
    return pltpu.CompilerParams(dimension_semantics=("arbitrary",) * n_axes,
                                vmem_limit_bytes=vmem_limit)


def _silu(x):
    return x * (1.0 / (1.0 + jnp.exp(-x)))


def _rms(x, g):
    ms = jnp.mean(x * x, axis=-1, keepdims=True)
    return x * lax.rsqrt(ms + EPS) * g


def _standardize(x):
    xc = x - jnp.mean(x, axis=-1, keepdims=True)
    return xc * lax.rsqrt(jnp.mean(xc * xc, axis=-1, keepdims=True) + EPS)


def _dot_nt(a, b):
    return lax.dot_general(a, b, (((1,), (1,)), ((), ())), preferred_element_type=F32)


def _mm_wcast_kernel(a_ref, w_ref, o_ref, wb_ref):
    @pl.when(pl.program_id(1) == 0)
    def _():
        wb_ref[...] = w_ref[0].astype(BF16)

    o_ref[...] = jnp.dot(a_ref[...], wb_ref[...], preferred_element_type=F32).astype(o_ref.dtype)


def _mm_wcast(a, w_stack, layer, out_dtype, tm, tn, name):
    m, k = a.shape
    n = w_stack.shape[2]
    return pl.pallas_call(
        _mm_wcast_kernel,
        grid=(n // tn, m // tm),
        in_specs=[pl.BlockSpec((tm, k), lambda j, i: (i, 0)),
                  pl.BlockSpec((1, k, tn), lambda j, i: (layer, 0, j))],
        out_specs=pl.BlockSpec((tm, tn), lambda j, i: (i, j)),
        out_shape=jax.ShapeDtypeStruct((m, n), out_dtype),
        scratch_shapes=[pltpu.VMEM((k, tn), BF16)],
        compiler_params=_cp(2),
        name=name,
    )(a, w_stack)


IN_EVEN_TM = 1408
IN_EVEN_TN = 768
IN_RET_TM = 1408
IN_RET_TN = 1024


def _in_even_kernel(a_ref, as_ref, w_ref, z_ref, zs_ref, wb_ref):
    @pl.when(pl.program_id(1) == 0)
    def _():
        wb_ref[...] = w_ref[0].astype(BF16)
        zs_ref[...] = jnp.dot(as_ref[...], wb_ref[...], preferred_element_type=F32)

    z_ref[...] = jnp.dot(a_ref[...], wb_ref[...], preferred_element_type=F32).astype(z_ref.dtype)


def _in_even(xn, w_stack, layer):
    m, k = xn.shape
    n = w_stack.shape[2]
    tm, tn = IN_EVEN_TM, IN_EVEN_TN
    return pl.pallas_call(
        _in_even_kernel,
        grid=(n // tn, m // tm),
        in_specs=[pl.BlockSpec((tm, k), lambda j, i: (i, 0)),
                  pl.BlockSpec((N_SAMPLE, k), lambda j, i: (N_PROMPT // N_SAMPLE, 0)),
                  pl.BlockSpec((1, k, tn), lambda j, i: (layer, 0, j))],
        out_specs=[pl.BlockSpec((tm, tn), lambda j, i: (i, j)),
                   pl.BlockSpec((N_SAMPLE, tn), lambda j, i: (0, j))],
        out_shape=[jax.ShapeDtypeStruct((m, n), BF16),
                   jax.ShapeDtypeStruct((N_SAMPLE, n), F32)],
        scratch_shapes=[pltpu.VMEM((k, tn), BF16)],
        compiler_params=_cp(2),
        name="in_even",
    )(xn, xn, w_stack)


def _kv_tail_kernel(a_ref, w_ref, o_ref):
    o_ref[...] = jnp.dot(a_ref[...], w_ref[0].astype(BF16), preferred_element_type=F32)


def _kv_tail(xn, w_stack, layer):
    k = xn.shape[1]
    blks_per_seq = SEQ // WINDOW
    width = 2 * A_KV_WIDTH
    return pl.pallas_call(
        _kv_tail_kernel,
        grid=(BATCH,),
        in_specs=[pl.BlockSpec((WINDOW, k), lambda b: (blks_per_seq * (b + 1) - 1, 0)),
                  pl.BlockSpec((1, k, width), lambda b: (layer, 0, OFF_K // width))],
        out_specs=pl.BlockSpec((WINDOW, width), lambda b: (b, 0)),
        out_shape=jax.ShapeDtypeStruct((BATCH * WINDOW, width), F32),
        compiler_params=_cp(1),
        name="kv_tail",
    )(xn, w_stack)


def _mm_res_q_kernel(nk, a_ref, w_ref, x_ref, g_ref, wq_ref, xo_ref, q_ref):
    k = pl.program_id(1)

    @pl.when(k == 0)
    def _():
        xo_ref[...] = x_ref[...]

    xo_ref[...] += jnp.dot(a_ref[...], w_ref[0], preferred_element_type=F32)

    @pl.when(k == nk - 1)
    def _():
        xn = _rms(xo_ref[...], g_ref[...]).astype(BF16)
        q_ref[...] = jnp.dot(xn, wq_ref[0].astype(BF16), preferred_element_type=F32).astype(q_ref.dtype)


def _mm_res_q(a, w_stack, layer, x, g, wq_stack, wq_layer, tm, tk, x_blk0=0):
    m, kdim = a.shape
    d = w_stack.shape[2]
    nq = wq_stack.shape[2]
    nk = kdim // tk
    return pl.pallas_call(
        functools.partial(_mm_res_q_kernel, nk),
        grid=(m // tm, nk),
        in_specs=[pl.BlockSpec((tm, tk), lambda i, k: (i, k)),
                  pl.BlockSpec((1, tk, d), lambda i, k: (layer, k, 0)),
                  pl.BlockSpec((tm, d), lambda i, k: (x_blk0 + i, 0)),
                  pl.BlockSpec((1, d), lambda i, k: (0, 0)),
                  pl.BlockSpec((1, d, nq), lambda i, k: (wq_layer, 0, 0))],
        out_specs=[pl.BlockSpec((tm, d), lambda i, k: (i, 0)),
                   pl.BlockSpec((tm, nq), lambda i, k: (i, 0))],
        out_shape=[jax.ShapeDtypeStruct((m, d), F32),
                   jax.ShapeDtypeStruct((m, nq), BF16)],
        compiler_params=_cp(2),
        name="mm_res_q",
    )(a, w_stack, x, g.reshape(1, d), wq_stack)


def _embed_kernel(n_prompt_blks, xp_ref, xs_ref, g_ref, n_ref):
    i = pl.program_id(0)

    @pl.when(i < n_prompt_blks)
    def _():
        n_ref[...] = _rms(xp_ref[...], g_ref[...]).astype(n_ref.dtype)

    @pl.when(i >= n_prompt_blks)
    def _():
        n_ref[...] = _rms(xs_ref[...], g_ref[...]).astype(n_ref.dtype)


def _embed(xp, xs, g):
    d = xp.shape[1]
    tm = N_SAMPLE
    nb = N_PROMPT // tm
    return pl.pallas_call(
        functools.partial(_embed_kernel, nb),
        grid=(nb + 1,),
        in_specs=[pl.BlockSpec((tm, d), lambda i: (jnp.minimum(i, nb - 1), 0)),
                  pl.BlockSpec((tm, d), lambda i: (0, 0)),
                  pl.BlockSpec((1, d), lambda i: (0, 0))],
        out_specs=pl.BlockSpec((tm, d), lambda i: (i, 0)),
        out_shape=jax.ShapeDtypeStruct((N_ROWS, d), BF16),
        compiler_params=_cp(1),
        name="embed",
    )(xp, xs, g.reshape(1, d))


def _memkv_kernel(mem_ref, g_ref, wk_ref, wv_ref, mk_ref, mv_ref):
    h = _rms(mem_ref[...], g_ref[0]).astype(BF16)
    mk_ref[0] = jnp.dot(h, wk_ref[0].astype(BF16), preferred_element_type=F32)
    mv_ref[0] = jnp.dot(h, wv_ref[0].astype(BF16), preferred_element_type=F32)


def _memkv(mem, norm_mem, w_xk, w_xv):
    m = mem.shape[0]
    out = jax.ShapeDtypeStruct((DEPTH, m, X_WIDTH), F32)
    return pl.pallas_call(
        _memkv_kernel,
        grid=(DEPTH,),
        in_specs=[pl.BlockSpec((m, D_MODEL), lambda l: (0, 0)),
                  pl.BlockSpec((1, 1, D_MODEL), lambda l: (l, 0, 0)),
                  pl.BlockSpec((1, D_MODEL, X_WIDTH), lambda l: (l, 0, 0)),
                  pl.BlockSpec((1, D_MODEL, X_WIDTH), lambda l: (l, 0, 0))],
        out_specs=[pl.BlockSpec((1, m, X_WIDTH), lambda l: (l, 0, 0)),
                   pl.BlockSpec((1, m, X_WIDTH), lambda l: (l, 0, 0))],
        out_shape=[out, out],
        compiler_params=_cp(1),
        name="memkv",
    )(mem, norm_mem.reshape(DEPTH, 1, D_MODEL), w_xk, w_xv)


SLABS_PER_KV = A_HEADS // A_KV_HEADS // 2
MASK_DIST = 1e30 * 2.0 ** 8


def _expand_band(band, scale):
    x = band.astype(F32)
    if scale != 1.0:
        x = x * scale
    xs = pltpu.roll(x, 64, 1)
    lo = lax.broadcasted_iota(jnp.int32, x.shape, 1) < 64
    z = jnp.zeros_like(x)
    e0 = jnp.concatenate([jnp.where(lo, x, z), jnp.where(lo, z, xs)], axis=0).astype(BF16)
    e1 = jnp.concatenate([jnp.where(lo, xs, z), jnp.where(lo, z, x)], axis=0).astype(BF16)
    return e0, e1


def _band_bias(t, first_key):
    row = lax.broadcasted_iota(jnp.int32, (t, 2 * WINDOW), 0)
    col = lax.broadcasted_iota(jnp.int32, (t, 2 * WINDOW), 1)
    dist = row + WINDOW - col
    valid = (dist >= 0) & (dist < WINDOW) & (col >= first_key)
    return jnp.where(valid, -dist.astype(F32), -MASK_DIST)


def _swa(q_slab, kband, vband, nbias, sl_ref):
    t = nbias.shape[0]
    rows = SLABS_PER_KV * t
    kk = _expand_band(kband, A_HEAD_DIM ** -0.5)
    vv = _expand_band(vband, 1.0)
    nb = jnp.concatenate([nbias] * SLABS_PER_KV, axis=0)
    blk = lax.shift_right_logical(lax.broadcasted_iota(jnp.int32, (rows, 1), 0), t.bit_length() - 1)
    lo = lax.broadcasted_iota(jnp.int32, (rows, 128), 1) < 64
    outs = [None] * (A_HEADS // 2)
    for j in range(A_KV_HEADS):
        slab0 = SLABS_PER_KV * j

        def per_row(table_row, half, slab0=slab0):
            col = jnp.full((rows, 1), sl_ref[table_row, 2 * (slab0 + SLABS_PER_KV - 1) + half], F32)
            for i in reversed(range(SLABS_PER_KV - 1)):
                col = jnp.where(blk == i, sl_ref[table_row, 2 * (slab0 + i) + half], col)
            return col

        q = jnp.concatenate([q_slab(slab0 + i) for i in range(SLABS_PER_KV)], axis=0)
        s = _dot_nt(q, kk[j])
        probs, invs = [], []
        for half in range(2):
            slope = per_row(0, half)
            sink = per_row(1, half)
            sh = s[:, half * 256:(half + 1) * 256] + slope * nb
            m = jnp.maximum(jnp.max(sh, axis=-1, keepdims=True), sink)
            pe = jnp.exp(sh - m)
            den = jnp.sum(pe, axis=-1, keepdims=True) + jnp.exp(sink - m)
            probs.append(pe)
            invs.append(1.0 / den)
        pp = jnp.concatenate(probs, axis=1).astype(BF16)
        o = jnp.dot(pp, vv[j], preferred_element_type=F32) * jnp.where(lo, invs[0], invs[1])
        for i in range(SLABS_PER_KV):
            outs[slab0 + i] = o[i * t:(i + 1) * t]
    return outs


def _spatial_gate_group(ws_ref, bs_ref, g, vn_g, rows):
    r = lax.broadcasted_iota(jnp.int32, (CHUNK, CHUNK), 0)
    c = lax.broadcasted_iota(jnp.int32, (CHUNK, CHUNK), 1)
    w = jnp.where(r >= c, ws_ref[g], 0.0).astype(BF16)
    return jnp.dot(w[:rows], vn_g, preferred_element_type=F32) + bs_ref[g][:rows]


EVEN_STEP_ROWS = 512
EVEN_SUBBLKS = EVEN_STEP_ROWS // WINDOW
EVEN_STEPS = N_PROMPT // EVEN_STEP_ROWS


def _even_fused_kernel(sl_ref, z_ref, kvprev_ref, ws_ref, bs_ref, lng_ref, lnb_ref,
                       w_ref, x_ref, g_ref, wq_ref, xo_ref, q_ref, mixed_ref):
    s_id = pl.program_id(0)
    slot = s_id % 2

    @pl.when(s_id == 0)
    def _():
        mixed_ref[1] = jnp.zeros(mixed_ref.shape[1:], mixed_ref.dtype)

    kc = D_MODEL // EVEN_SUBBLKS
    xo_ref[...] = x_ref[...]
    blk0 = jnp.minimum(s_id, EVEN_STEPS - 1) * EVEN_SUBBLKS
    for b in range(EVEN_SUBBLKS):
        xo_ref[...] += jnp.dot(mixed_ref[1 - slot, :, b * kc:(b + 1) * kc], w_ref[0, b * kc:(b + 1) * kc, :],
                               preferred_element_type=F32)
        rows = slice(b * WINDOW, (b + 1) * WINDOW)
        c = (blk0 + b) % (SEQ // WINDOW)
        kv_cur = z_ref[rows, OFF_K:OFF_K + 2 * A_KV_WIDTH]
        kv_prev = kvprev_ref[...] if b == 0 else z_ref[(b - 1) * WINDOW:b * WINDOW, OFF_K:OFF_K + 2 * A_KV_WIDTH]
        band = jnp.concatenate([kv_prev, kv_cur], axis=0)
        nbias = _band_bias(WINDOW, jnp.maximum(WINDOW - c * WINDOW, 0))

        def q_slab(p, rows=rows):
            return z_ref[rows, OFF_Q + p * 128:OFF_Q + (p + 1) * 128]

        for p, att in enumerate(_swa(q_slab, band[:, :A_KV_WIDTH], band[:, A_KV_WIDTH:], nbias, sl_ref)):
            ga = z_ref[rows, OFF_GA + p * 128:OFF_GA + (p + 1) * 128].astype(F32)
            mixed_ref[slot, rows, p * 128:(p + 1) * 128] = (_silu(ga) * att).astype(mixed_ref.dtype)

        vn = _standardize(z_ref[rows, OFF_VB:OFF_VB + B_WIDTH].astype(F32)) * lng_ref[...] + lnb_ref[...]
        for g in range(B_GROUPS):
            lo, hi = g * B_GROUP_DIM, (g + 1) * B_GROUP_DIM
            sg = _spatial_gate_group(ws_ref, bs_ref, g, vn[:, lo:hi].astype(BF16), CHUNK)
            u = z_ref[rows, OFF_U + lo:OFF_U + hi].astype(F32)
            gb = z_ref[rows, OFF_GB + lo:OFF_GB + hi].astype(F32)
            mixed_ref[slot, rows, A_WIDTH + lo:A_WIDTH + hi] = (_silu(gb) * (u * sg)).astype(mixed_ref.dtype)

    q_ref[...] = jnp.dot(_rms(xo_ref[...], g_ref[...]).astype(BF16), wq_ref[0],
                         preferred_element_type=F32).astype(q_ref.dtype)


def _even_fused(z, sl, w_s, b_s, lng, lnb, w_out_b, layer_i, x, g, wq_b, layer):
    ns = EVEN_STEPS
    r = EVEN_STEP_ROWS
    kv_blk = OFF_K // (2 * A_KV_WIDTH)

    def cur(s):
        return jnp.minimum(s, ns - 1)

    def prev(s):
        return jnp.maximum(s - 1, 0)

    once = pl.Buffered(1)
    return pl.pallas_call(
        _even_fused_kernel,
        grid=(ns + 1,),
        in_specs=[pl.BlockSpec(memory_space=pltpu.SMEM),
                  pl.BlockSpec((r, EVEN_IN), lambda s: (cur(s), 0)),
                  pl.BlockSpec((WINDOW, 2 * A_KV_WIDTH),
                               lambda s: (jnp.maximum(cur(s) * EVEN_SUBBLKS - 1, 0), kv_blk)),
                  pl.BlockSpec((B_GROUPS, CHUNK, CHUNK), lambda s: (0, 0, 0)),
                  pl.BlockSpec((B_GROUPS, CHUNK, 1), lambda s: (0, 0, 0)),
                  pl.BlockSpec((1, B_WIDTH), lambda s: (0, 0)),
                  pl.BlockSpec((1, B_WIDTH), lambda s: (0, 0)),
                  pl.BlockSpec((1, D_MODEL, D_MODEL), lambda s: (layer_i, 0, 0), pipeline_mode=once),
                  pl.BlockSpec((r, D_MODEL), lambda s: (prev(s), 0)),
                  pl.BlockSpec((1, D_MODEL), lambda s: (0, 0)),
                  pl.BlockSpec((1, D_MODEL, X_WIDTH), lambda s: (layer, 0, 0), pipeline_mode=once)],
        out_specs=[pl.BlockSpec((r, D_MODEL), lambda s: (prev(s), 0)),
                   pl.BlockSpec((r, X_WIDTH), lambda s: (prev(s), 0))],
        out_shape=[jax.ShapeDtypeStruct((N_PROMPT, D_MODEL), F32),
                   jax.ShapeDtypeStruct((N_PROMPT, X_WIDTH), BF16)],
        scratch_shapes=[pltpu.VMEM((2, r, D_MODEL), BF16)],
        compiler_params=_cp(1, FUSED_VMEM_LIMIT),
        name="even_fused",
    )(sl, z, z, w_s, b_s, lng, lnb, w_out_b, x, g.reshape(1, D_MODEL), wq_b)


SEQ_PER_STEP = 2
ROWS_PER_STEP = SEQ_PER_STEP * DEC_SEQ
assert DEC_SEQ & (DEC_SEQ - 1) == 0


def _seq_of(row_idx):
    return lax.shift_right_logical(row_idx, DEC_SEQ.bit_length() - 1)


def _even_sample_kernel(sl_ref, z_ref, ck_ref, cv_ref, ws_ref, bs_ref, lng_ref, lnb_ref,
                        o_ref, wk_ref, wv_ref, cvo_ref):
    nbias = _band_bias(ROWS_PER_STEP, 0)
    pad_rows = jnp.zeros((ROWS_PER_STEP - DEC_SEQ, EVEN_IN), F32)
    attn = [[None] * SEQ_PER_STEP for _ in range(A_HEADS // 2)]
    gate = [[None] * SEQ_PER_STEP for _ in range(B_GROUPS)]
    for r in range(SEQ_PER_STEP):
        r0, r1 = r * DEC_SEQ, (r + 1) * DEC_SEQ
        zr = z_ref[r0:r1, :]
        k_new = zr[:, OFF_K:OFF_K + A_KV_WIDTH]
        v_new = zr[:, OFF_V:OFF_V + A_KV_WIDTH]
        ck = ck_ref[r]
        cv = cv_ref[r]
        wk_ref[r, 0:WINDOW - DEC_SEQ, :] = ck[DEC_SEQ:, :]
        wk_ref[r, WINDOW - DEC_SEQ:, :] = k_new
        wv_ref[r, 0:WINDOW - DEC_SEQ, :] = cv[DEC_SEQ:, :]
        wv_ref[r, WINDOW - DEC_SEQ:, :] = v_new
        tail = jnp.zeros((WINDOW - DEC_SEQ, A_KV_WIDTH), F32)
        kband = jnp.concatenate([ck, k_new, tail], axis=0)
        vband = jnp.concatenate([cv, v_new, tail], axis=0)
        zq = jnp.concatenate([zr, pad_rows], axis=0)

        def q_slab(p, zq=zq):
            return zq[:, OFF_Q + p * 128:OFF_Q + (p + 1) * 128].astype(BF16)

        for p, a in enumerate(_swa(q_slab, kband, vband, nbias, sl_ref)):
            ga = zr[:, OFF_GA + p * 128:OFF_GA + (p + 1) * 128]
            attn[p][r] = _silu(ga) * a[:DEC_SEQ]

        vn = _standardize(zr[:, OFF_VB:OFF_VB + B_WIDTH]) * lng_ref[...] + lnb_ref[...]
        cvo_ref[r0:r1, :] = vn
        vn_pad = jnp.concatenate([vn, jnp.zeros((CHUNK - DEC_SEQ, B_WIDTH), F32)], axis=0).astype(BF16)
        for g in range(B_GROUPS):
            lo, hi = g * B_GROUP_DIM, (g + 1) * B_GROUP_DIM
            mixed = _spatial_gate_group(ws_ref, bs_ref, g, vn_pad[:, lo:hi], ROWS_PER_STEP)[:DEC_SEQ]
            u = zr[:, OFF_U + lo:OFF_U + hi]
            gb = zr[:, OFF_GB + lo:OFF_GB + hi]
            gate[g][r] = _silu(gb) * (u * mixed)

    for p in range(A_HEADS // 2):
        o_ref[:, p * 128:(p + 1) * 128] = jnp.concatenate(attn[p], axis=0).astype(o_ref.dtype)
    for g in range(B_GROUPS):
        lo, hi = A_WIDTH + g * B_GROUP_DIM, A_WIDTH + (g + 1) * B_GROUP_DIM
        o_ref[:, lo:hi] = jnp.concatenate(gate[g], axis=0).astype(o_ref.dtype)


def _even_sample(zs, ck, cv, sl, w_s, b_s, lng, lnb):
    n_steps = DEC_BATCH // SEQ_PER_STEP
    win = jax.ShapeDtypeStruct((DEC_BATCH, WINDOW, A_KV_WIDTH), F32)
    return pl.pallas_call(
        _even_sample_kernel,
        grid=(n_steps,),
        in_specs=[pl.BlockSpec(memory_space=pltpu.SMEM),
                  pl.BlockSpec((ROWS_PER_STEP, EVEN_IN), lambda i: (i, 0)),
                  pl.BlockSpec((SEQ_PER_STEP, WINDOW, A_KV_WIDTH), lambda i: (i, 0, 0)),
                  pl.BlockSpec((SEQ_PER_STEP, WINDOW, A_KV_WIDTH), lambda i: (i, 0, 0)),
                  pl.BlockSpec((B_GROUPS, CHUNK, CHUNK), lambda i: (0, 0, 0)),
                  pl.BlockSpec((B_GROUPS, CHUNK, 1), lambda i: (0, 0, 0)),
                  pl.BlockSpec((1, B_WIDTH), lambda i: (0, 0)),
                  pl.BlockSpec((1, B_WIDTH), lambda i: (0, 0))],
        out_specs=[pl.BlockSpec((ROWS_PER_STEP, D_MODEL), lambda i: (i, 0)),
                   pl.BlockSpec((SEQ_PER_STEP, WINDOW, A_KV_WIDTH), lambda i: (i, 0, 0)),
                   pl.BlockSpec((SEQ_PER_STEP, WINDOW, A_KV_WIDTH), lambda i: (i, 0, 0)),
                   pl.BlockSpec((ROWS_PER_STEP, B_WIDTH), lambda i: (i, 0))],
        out_shape=[jax.ShapeDtypeStruct((N_SAMPLE, D_MODEL), BF16), win, win,
                   jax.ShapeDtypeStruct((N_SAMPLE, B_WIDTH), F32)],
        compiler_params=_cp(1),
        name="even_sample",
    )(sl, zs, ck, cv, w_s, b_s, lng, lnb)


RET_ROWS = 256
RET_BLKS_PER_SEQ = SEQ // RET_ROWS
RET_PROMPT_BLKS = N_PROMPT // RET_ROWS


def _gated_groupnorm(o, g):
    return (_silu(g.astype(F32)) * _standardize(o)).astype(BF16)


RET_LOOKAHEAD = 1


def _ret_fused_kernel(lg_ref, z_ref, w_ref, x_ref, g_ref, wq_ref, xo_ref, q_ref, so_ref, dm_ref, dc_ref):
    s_id = pl.program_id(0)
    t = RET_ROWS
    k_scale = R_KEY_DIM ** -0.5
    c = s_id % RET_BLKS_PER_SEQ

    @pl.when(s_id == 0)
    def _():
        row = lax.broadcasted_iota(jnp.int32, (t, t), 0)
        col = lax.broadcasted_iota(jnp.int32, (t, t), 1)
        rel = (row - col).astype(F32)
        ti = lax.broadcasted_iota(jnp.int32, (t, 1), 0).astype(F32)
        for h in range(R_HEADS):
            lg = lg_ref[h]
            dm_ref[h] = jnp.where(rel >= 0, jnp.exp(jnp.maximum(rel, 0.0) * lg), 0.0) * k_scale
            dc_ref[h, 0] = jnp.exp((ti + 1.0) * lg)
            dc_ref[h, 1] = jnp.exp((t - 1.0 - ti) * lg) * k_scale

    @pl.when(c == 0)
    def _():
        so_ref[...] = jnp.zeros_like(so_ref)

    def head_matmuls(h):
        lg = lg_ref[h]
        q_decay = dc_ref[h, 0]
        k_decay = dc_ref[h, 1]
        chunk_decay = jnp.exp(jnp.full((1, R_VAL_DIM), float(t), F32) * lg)
        q = z_ref[:, h * R_KEY_DIM:(h + 1) * R_KEY_DIM]
        k = z_ref[:, R_K_WIDTH + h * R_KEY_DIM:R_K_WIDTH + (h + 1) * R_KEY_DIM]
        v = z_ref[:, 2 * R_K_WIDTH + h * R_VAL_DIM:2 * R_K_WIDTH + (h + 1) * R_VAL_DIM]
        st = so_ref[0, h]
        scores = _dot_nt(q, k) * dm_ref[h]
        inner = jnp.dot(scores.astype(BF16), v, preferred_element_type=F32)
        cross = jnp.dot(q, st.astype(BF16), preferred_element_type=F32) * q_decay
        kd_t = (k.astype(F32) * k_decay).T.astype(BF16)
        so_ref[0, h] = chunk_decay * st + jnp.dot(kd_t, v, preferred_element_type=F32)
        return inner + cross

    x1 = x_ref[...]
    ahead = [head_matmuls(h) for h in range(RET_LOOKAHEAD)]
    for h in range(R_HEADS):
        if h + RET_LOOKAHEAD < R_HEADS:
            ahead.append(head_matmuls(h + RET_LOOKAHEAD))
        raw = ahead.pop(0)
        g = z_ref[:, 2 * R_K_WIDTH + R_V_WIDTH + h * R_VAL_DIM:2 * R_K_WIDTH + R_V_WIDTH + (h + 1) * R_VAL_DIM]
        x1 = x1 + jnp.dot(_gated_groupnorm(raw, g), w_ref[0, h * R_VAL_DIM:(h + 1) * R_VAL_DIM, :],
                          preferred_element_type=F32)
    xo_ref[...] = x1
    q_ref[...] = jnp.dot(_rms(x1, g_ref[...]).astype(BF16), wq_ref[0],
                         preferred_element_type=F32).astype(q_ref.dtype)


def _ret_fused(z, log_g, w_out_b, layer_i, x, g, wq_b, layer):
    nb = RET_PROMPT_BLKS
    r = RET_ROWS
    once = pl.Buffered(1)
    return pl.pallas_call(
        _ret_fused_kernel,
        grid=(nb,),
        in_specs=[pl.BlockSpec(memory_space=pltpu.SMEM),
                  pl.BlockSpec((r, RET_IN), lambda s: (s, 0)),
                  pl.BlockSpec((1, R_V_WIDTH, D_MODEL), lambda s: (layer_i, 0, 0), pipeline_mode=once),
                  pl.BlockSpec((r, D_MODEL), lambda s: (s, 0)),
                  pl.BlockSpec((1, D_MODEL), lambda s: (0, 0)),
                  pl.BlockSpec((1, D_MODEL, X_WIDTH), lambda s: (layer, 0, 0), pipeline_mode=once)],
        out_specs=[pl.BlockSpec((r, D_MODEL), lambda s: (s, 0)),
                   pl.BlockSpec((r, X_WIDTH), lambda s: (s, 0)),
                   pl.BlockSpec((1, R_HEADS, R_KEY_DIM, R_VAL_DIM),
                                lambda s: (s // RET_BLKS_PER_SEQ, 0, 0, 0))],
        out_shape=[jax.ShapeDtypeStruct((N_PROMPT, D_MODEL), F32),
                   jax.ShapeDtypeStruct((N_PROMPT, X_WIDTH), BF16),
                   jax.ShapeDtypeStruct((BATCH, R_HEADS, R_KEY_DIM, R_VAL_DIM), F32)],
        scratch_shapes=[pltpu.VMEM((R_HEADS, r, r), F32),
                        pltpu.VMEM((R_HEADS, 2, r, 1), F32)],
        compiler_params=_cp(1, FUSED_VMEM_LIMIT),
        name="ret_fused",
    )(log_g, z, w_out_b, x, g.reshape(1, D_MODEL), wq_b)


RET_SAMPLE_SEQS = 8
RET_SAMPLE_ROWS = RET_SAMPLE_SEQS * DEC_SEQ


def _ret_sample_kernel(lg_ref, q_ref, k_ref, v_ref, g_ref, s_ref, *rest):
    o_ref, so_ref = rest[-2:]
    h = pl.program_id(1)
    lg = lg_ref[h]
    n = RET_SAMPLE_ROWS
    t = DEC_SEQ
    row = lax.broadcasted_iota(jnp.int32, (n, n), 0)
    col = lax.broadcasted_iota(jnp.int32, (n, n), 1)
    rel = (row - col).astype(F32)
    same = _seq_of(row) == _seq_of(col)
    k_scale = R_KEY_DIM ** -0.5
    dmat = jnp.where(same & (rel >= 0), jnp.exp(jnp.maximum(rel, 0.0) * lg), 0.0) * k_scale
    ri = lax.broadcasted_iota(jnp.int32, (n, 1), 0)
    ti = (ri & (t - 1)).astype(F32)
    q_decay = jnp.exp((ti + 1.0) * lg)
    k_decay = jnp.exp((t - 1.0 - ti) * lg) * k_scale
    step_decay = jnp.exp(jnp.full((1, R_VAL_DIM), float(t), F32) * lg)

    q = q_ref[...]
    k = k_ref[...]
    v = v_ref[...]
    scores = _dot_nt(q, k) * dmat
    o = jnp.dot(scores.astype(BF16), v, preferred_element_type=F32)
    kd = k.astype(F32) * k_decay
    k_pad = jnp.zeros((R_CHUNK - n, R_KEY_DIM), F32)
    v_pad = jnp.concatenate([v.astype(F32), jnp.zeros((R_CHUNK - n, R_VAL_DIM), F32)], axis=0).astype(BF16)
    for r in range(RET_SAMPLE_SEQS):
        mine = _seq_of(ri) == r
        st = s_ref[0, r, 0]
        cross = jnp.dot(q, st.astype(BF16), preferred_element_type=F32) * q_decay
        o = o + jnp.where(mine, cross, 0.0)
        kd_t = jnp.concatenate([jnp.where(mine, kd, 0.0), k_pad], axis=0).T.astype(BF16)
        so_ref[0, r, 0] = step_decay * st + jnp.dot(kd_t, v_pad, preferred_element_type=F32)
    o_ref[...] = _gated_groupnorm(o, g_ref[...])


def _ret_sample(z, state_all, idx, log_g, new_states):
    n_steps = DEC_BATCH // RET_SAMPLE_SEQS
    rb0 = N_PROMPT // RET_SAMPLE_ROWS
    kcol = R_K_WIDTH // R_KEY_DIM
    vcol = 2 * R_K_WIDTH // R_VAL_DIM
    gcol = vcol + R_HEADS
    st_spec = pl.BlockSpec((1, RET_SAMPLE_SEQS, 1, R_KEY_DIM, R_VAL_DIM), lambda i, h: (idx, i, h, 0, 0))
    in_specs = [pl.BlockSpec(memory_space=pltpu.SMEM),
                pl.BlockSpec((RET_SAMPLE_ROWS, R_KEY_DIM), lambda i, h: (rb0 + i, h)),
                pl.BlockSpec((RET_SAMPLE_ROWS, R_KEY_DIM), lambda i, h: (rb0 + i, kcol + h)),
                pl.BlockSpec((RET_SAMPLE_ROWS, R_VAL_DIM), lambda i, h: (rb0 + i, vcol + h)),
                pl.BlockSpec((RET_SAMPLE_ROWS, R_VAL_DIM), lambda i, h: (rb0 + i, gcol + h)),
                st_spec]
    args = [log_g, z, z, z, z, state_all]
    aliases = {}
    if new_states is not None:
        in_specs.append(pl.BlockSpec(memory_space=pl.ANY))
        args.append(new_states)
        aliases = {len(args) - 1: 1}
    return pl.pallas_call(
        _ret_sample_kernel,
        grid=(n_steps, R_HEADS),
        in_specs=in_specs,
        out_specs=[pl.BlockSpec((RET_SAMPLE_ROWS, R_VAL_DIM), lambda i, h: (i, h)), st_spec],
        out_shape=[jax.ShapeDtypeStruct((N_SAMPLE, R_V_WIDTH), BF16),
                   jax.ShapeDtypeStruct(state_all.shape, F32)],
        input_output_aliases=aliases,
        compiler_params=_cp(2),
        name="ret_sample",
    )(*args)


XQ_ROWS = 256
XQ_PROMPT_BLKS = N_PROMPT // XQ_ROWS
XQ_BLKS_PER_SEQ = SEQ // XQ_ROWS


def _xattn_heads(q, mk_head, mv_head):
    outs = []
    for h in range(X_HEADS):
        s = _dot_nt(q[:, h * X_HEAD_DIM:(h + 1) * X_HEAD_DIM], mk_head(h)) * (X_HEAD_DIM ** -0.5)
        m = jnp.max(s, axis=-1, keepdims=True)
        p = jnp.exp(s - m)
        den = jnp.sum(p, axis=-1, keepdims=True)
        outs.append(jnp.dot(p.astype(BF16), mv_head(h), preferred_element_type=F32) * (1.0 / den))
    return outs


def _xattn_proj_kernel(final, q_ref, mk_ref, mv_ref, os_ref, w_ref, xp_ref, xs_ref, g_ref, out_a, out_b):
    s_id = pl.program_id(0)

    def project(o, x_ref):
        return x_ref[...] + jnp.dot(o, w_ref[0], preferred_element_type=F32)

    @pl.when(s_id < XQ_PROMPT_BLKS)
    def _():
        def head_of(ref):
            return lambda h: ref[0, :, h * X_HEAD_DIM:(h + 1) * X_HEAD_DIM].astype(BF16)

        outs = _xattn_heads(q_ref[...], head_of(mk_ref), head_of(mv_ref))
        x2 = project(jnp.concatenate(outs, axis=1).astype(BF16), xp_ref)
        if final:
            out_a[...] = _rms(x2, g_ref[...])
        else:
            out_a[...] = x2
            out_b[...] = _rms(x2, g_ref[...]).astype(out_b.dtype)

    @pl.when(s_id >= XQ_PROMPT_BLKS)
    def _():
        x2 = project(os_ref[...], xs_ref)
        if final:
            out_b[...] = _rms(x2, g_ref[...])
        else:
            out_a[...] = x2
            out_b[...] = _rms(x2, g_ref[...]).astype(out_b.dtype)


def _xattn_proj(q, mk_all, mv_all, layer, o_sample, w_stack, x_prompt, x_sample, g, final):
    nb = XQ_PROMPT_BLKS
    d = w_stack.shape[2]
    xs_blk = (x_sample.shape[0] - N_SAMPLE) // N_SAMPLE

    def rb(s):
        return jnp.minimum(s, nb - 1)

    mem_spec = pl.BlockSpec((1, MEM_LEN, X_WIDTH), lambda s: (layer, rb(s) // XQ_BLKS_PER_SEQ, 0))
    if final:
        out_specs = [pl.BlockSpec((XQ_ROWS, d), lambda s: (rb(s), 0)),
                     pl.BlockSpec((N_SAMPLE, d), lambda s: (0, 0))]
        out_shape = [jax.ShapeDtypeStruct((N_PROMPT, d), F32), jax.ShapeDtypeStruct((N_SAMPLE, d), F32)]
    else:
        out_specs = [pl.BlockSpec((XQ_ROWS, d), lambda s: (s, 0)),
                     pl.BlockSpec((XQ_ROWS, d), lambda s: (s, 0))]
        out_shape = [jax.ShapeDtypeStruct((N_ROWS, d), F32), jax.ShapeDtypeStruct((N_ROWS, d), BF16)]
    return pl.pallas_call(
        functools.partial(_xattn_proj_kernel, final),
        grid=(nb + 1,),
        in_specs=[pl.BlockSpec((XQ_ROWS, X_WIDTH), lambda s: (rb(s), 0)),
                  mem_spec, mem_spec,
                  pl.BlockSpec((N_SAMPLE, X_WIDTH), lambda s: (0, 0)),
                  pl.BlockSpec((1, X_WIDTH, d), lambda s: (layer, 0, 0)),
                  pl.BlockSpec((XQ_ROWS, d), lambda s: (rb(s), 0)),
                  pl.BlockSpec((N_SAMPLE, d), lambda s: (xs_blk, 0)),
                  pl.BlockSpec((1, d), lambda s: (0, 0))],
        out_specs=out_specs,
        out_shape=out_shape,
        compiler_params=_cp(1),
        name="xattn_proj",
    )(q, mk_all, mv_all, o_sample, w_stack, x_prompt, x_sample, g.reshape(1, d))


XS_SEQS = 4
XS_ROWS = XS_SEQS * DEC_SEQ


def _xattn_sample_kernel(q_ref, ck_ref, cv_ref, o_ref):
    q = q_ref[...]
    ri = lax.broadcasted_iota(jnp.int32, (XS_ROWS, 1), 0)
    acc = [jnp.zeros((XS_ROWS, X_HEAD_DIM), F32) for _ in range(X_HEADS)]
    for r in range(XS_SEQS):
        mine = _seq_of(ri) == r

        def head_of(ref, r=r):
            return lambda h: ref[0, r, pl.ds(h, MEM_LEN, stride=X_HEADS), :].astype(BF16)

        outs = _xattn_heads(q, head_of(ck_ref), head_of(cv_ref))
        acc = [a + jnp.where(mine, o, 0.0) for a, o in zip(acc, outs)]
    for h, a in enumerate(acc):
        o_ref[:, h * X_HEAD_DIM:(h + 1) * X_HEAD_DIM] = a.astype(o_ref.dtype)


def _xattn_sample(q, ck_all, cv_all, layer):
    n_steps = DEC_BATCH // XS_SEQS
    rb0 = (q.shape[0] - N_SAMPLE) // XS_ROWS
    mem_spec = pl.BlockSpec((1, XS_SEQS, MEM_LEN * X_HEADS, X_HEAD_DIM), lambda i: (layer, i, 0, 0))
    return pl.pallas_call(
        _xattn_sample_kernel,
        grid=(n_steps,),
        in_specs=[pl.BlockSpec((XS_ROWS, X_WIDTH), lambda i: (rb0 + i, 0)), mem_spec, mem_spec],
        out_specs=pl.BlockSpec((XS_ROWS, X_WIDTH), lambda i: (i, 0)),
        out_shape=jax.ShapeDtypeStruct((N_SAMPLE, X_WIDTH), BF16),
        compiler_params=_cp(1),
        name="xattn_sample",
    )(q, ck_all, cv_all)


def kernel(x_prompt, x_sample, cache_win_k, cache_win_v, state_ret, cache_mem_k, cache_mem_v, mem_prompt,
           norm_mix, w_in_even, attn_sinks, w_spatial, b_spatial, norm_v_g, norm_v_b, w_out_even,
           w_in_ret, w_out_ret, norm_cross, norm_mem, w_xq, w_xk, w_xv, w_xo, norm_final):
    w_out_even_b = w_out_even.astype(BF16)
    w_out_ret_b = w_out_ret.astype(BF16)
    w_xo_b = w_xo.astype(BF16)
    w_xq_b = w_xq.astype(BF16)

    slopes = jnp.exp2(-8.0 * (jnp.arange(A_HEADS, dtype=F32) + 1.0) / A_HEADS)
    log_g = jnp.log1p(-jnp.exp2(-5.0 - jnp.arange(R_HEADS, dtype=F32)))

    def mem_rows(cache):
        return cache.reshape(DEPTH, DEC_BATCH, MEM_LEN * X_HEADS, X_HEAD_DIM)

    mk_all, mv_all = _memkv(mem_prompt.reshape(BATCH * MEM_LEN, D_MODEL), norm_mem, w_xk, w_xv)

    xp_src = x_prompt.reshape(N_PROMPT, D_MODEL)
    xs_src = x_sample.reshape(N_SAMPLE, D_MODEL)
    xn = _embed(xp_src, xs_src, norm_mix[0])
    wkp, wvp, rsp, wks, wvs, cvs = [], [], [], [], [], []
    rss = None
    for layer in range(DEPTH):
        i = layer // 2
        xs_blk0 = (xs_src.shape[0] - N_SAMPLE) // N_SAMPLE
        if layer % 2 == 0:
            sl = jnp.stack([slopes, attn_sinks[i].astype(F32)])
            b_s = b_spatial[i].reshape(B_GROUPS, CHUNK, 1)
            lng = norm_v_g[i].reshape(1, B_WIDTH)
            lnb = norm_v_b[i].reshape(1, B_WIDTH)
            z, zs = _in_even(xn, w_in_even, i)
            kv_tail = _kv_tail(xn, w_in_even, i)
            wkp.append(kv_tail[:, :A_KV_WIDTH].reshape(BATCH, WINDOW, A_KV_HEADS, A_HEAD_DIM))
            wvp.append(kv_tail[:, A_KV_WIDTH:].reshape(BATCH, WINDOW, A_KV_HEADS, A_HEAD_DIM))
            ck = cache_win_k[i].reshape(DEC_BATCH, WINDOW, A_KV_WIDTH)
            cv = cache_win_v[i].reshape(DEC_BATCH, WINDOW, A_KV_WIDTH)
            mixed_s, wk_new, wv_new, cv_new = _even_sample(zs, ck, cv, sl, w_spatial[i], b_s, lng, lnb)
            wks.append(wk_new.reshape(DEC_BATCH, WINDOW, A_KV_HEADS, A_HEAD_DIM))
            wvs.append(wv_new.reshape(DEC_BATCH, WINDOW, A_KV_HEADS, A_HEAD_DIM))
            cvs.append(cv_new.reshape(DEC_BATCH, DEC_SEQ, B_WIDTH))
            x_p, q = _even_fused(z, sl, w_spatial[i], b_s, lng, lnb, w_out_even_b, i, xp_src, norm_cross[layer],
                                 w_xq_b, layer)
            x_s, q_s = _mm_res_q(mixed_s, w_out_even_b, i, xs_src, norm_cross[layer], w_xq_b, layer,
                                 N_SAMPLE, 1024, x_blk0=xs_blk0)
        else:
            z = _mm_wcast(xn, w_in_ret, i, BF16, IN_RET_TM, IN_RET_TN, "in_ret")
            o_s, rss = _ret_sample(z, state_ret, i, log_g, rss)
            x_p, q, st_p = _ret_fused(z, log_g, w_out_ret_b, i, xp_src, norm_cross[layer], w_xq_b, layer)
            rsp.append(st_p)
            x_s, q_s = _mm_res_q(o_s, w_out_ret_b, i, xs_src, norm_cross[layer], w_xq_b, layer,
                                 N_SAMPLE, 1024, x_blk0=xs_blk0)
        o_s = _xattn_sample(q_s, mem_rows(cache_mem_k), mem_rows(cache_mem_v), layer)
        if layer + 1 < DEPTH:
            x, xn = _xattn_proj(q, mk_all, mv_all, layer, o_s, w_xo_b, x_p, x_s, norm_mix[layer + 1], False)
            xp_src = xs_src = x
        else:
            y_prompt, y_sample = _xattn_proj(q, mk_all, mv_all, layer, o_s, w_xo_b, x_p, x_s, norm_final, True)

    y_prompt = y_prompt.reshape(BATCH, SEQ, D_MODEL)
    y_sample = y_sample.reshape(DEC_BATCH, DEC_SEQ, D_MODEL)
    mem_shape = (DEPTH, BATCH, MEM_LEN, X_HEADS, X_HEAD_DIM)
    return (y_prompt, y_sample,
            jnp.stack(wkp), jnp.stack(wvp), jnp.stack(rsp),
            mk_all.reshape(mem_shape), mv_all.reshape(mem_shape),
            jnp.stack(wks), jnp.stack(wvs), rss, jnp.stack(cvs))
```

```python
import functools

import jax
import jax.numpy as jnp
from jax import lax
from jax.experimental import pallas as pl
from jax.experimental.pallas import tpu as pltpu

F32 = jnp.float32
BF16 = jnp.bfloat16

D_MODEL = 2048
BATCH = 2
SEQ = 4096
DEPTH = 4
DEC_BATCH = 32
DEC_SEQ = 8
A_HEADS = 16
A_KV_HEADS = 2
A_HEAD_DIM = 64
A_WIDTH = 1024
A_KV_WIDTH = 128
WINDOW = 128
CHUNK = 128
B_GROUPS = 4
B_WIDTH = 1024
B_GROUP_DIM = 256
EVEN_IN = 5376
R_HEADS = 8
R_KEY_DIM = 256
R_VAL_DIM = 512
R_K_WIDTH = 2048
R_V_WIDTH = 4096
R_CHUNK = 128
RET_IN = 12288
MEM_LEN = 256
X_HEADS = 4
X_HEAD_DIM = 128
X_WIDTH = 512
EPS = 1e-6

N_PROMPT = BATCH * SEQ
N_SAMPLE = DEC_BATCH * DEC_SEQ
N_ROWS = N_PROMPT + N_SAMPLE

OFF_Q, OFF_K, OFF_V, OFF_GA, OFF_U, OFF_VB, OFF_GB = 0, 1024, 1152, 1280, 2304, 3328, 4352

VMEM_LIMIT = 52 * 1024 * 1024
FUSED_VMEM_LIMIT = 56 * 1024 * 1024


def _cp(n_axes, vmem_limit=VMEM_LIMIT):
    return pltpu.CompilerParams(dimension_semantics=("arbitrary",) * n_axes,
                                vmem_limit_bytes=vmem_limit)


def _silu(x):
    return x * (1.0 / (1.0 + jnp.exp(-x)))


def _rms(x, g):
    ms = jnp.mean(x * x, axis=-1, keepdims=True)
    return x * lax.rsqrt(ms + EPS) * g


def _standardize(x):
    xc = x - jnp.mean(x, axis=-1, keepdims=True)
    return xc * lax.rsqrt(jnp.mean(xc * xc, axis=-1, keepdims=True) + EPS)


def _dot_nt(a, b):
    return lax.dot_general(a, b, (((1,), (1,)), ((), ())), preferred_element_type=F32)


def _mm_wcast_kernel(a_ref, w_ref, o_ref, wb_ref):
    @pl.when(pl.program_id(1) == 0)
    def _():
        wb_ref[...] = w_ref[0].astype(BF16)

    o_ref[...] = jnp.dot(a_ref[...], wb_ref[...], preferred_element_type=F32).astype(o_ref.dtype)


def _mm_wcast(a, w_stack, layer, out_dtype, tm, tn, name):
    m, k = a.shape
    n = w_stack.shape[2]
    return pl.pallas_call(
        _mm_wcast_kernel,
        grid=(n // tn, m // tm),
        in_specs=[pl.BlockSpec((tm, k), lambda j, i: (i, 0)),
                  pl.BlockSpec((1, k, tn), lambda j, i: (layer, 0, j))],
        out_specs=pl.BlockSpec((tm, tn), lambda j, i: (i, j)),
        out_shape=jax.ShapeDtypeStruct((m, n), out_dtype),
        scratch_shapes=[pltpu.VMEM((k, tn), BF16)],
        compiler_params=_cp(2),
        name=name,
    )(a, w_stack)


IN_EVEN_TM = 1408
IN_EVEN_TN = 768
IN_RET_TM = 1408
IN_RET_TN = 1024


def _in_even_kernel(a_ref, as_ref, w_ref, z_ref, zs_ref, wb_ref):
    @pl.when(pl.program_id(1) == 0)
    def _():
        wb_ref[...] = w_ref[0].astype(BF16)
        zs_ref[...] = jnp.dot(as_ref[...], wb_ref[...], preferred_element_type=F32)

    z_ref[...] = jnp.dot(a_ref[...], wb_ref[...], preferred_element_type=F32).astype(z_ref.dtype)


def _in_even(xn, w_stack, layer):
    m, k = xn.shape
    n = w_stack.shape[2]
    tm, tn = IN_EVEN_TM, IN_EVEN_TN
    return pl.pallas_call(
        _in_even_kernel,
        grid=(n // tn, m // tm),
        in_specs=[pl.BlockSpec((tm, k), lambda j, i: (i, 0)),
                  pl.BlockSpec((N_SAMPLE, k), lambda j, i: (N_PROMPT // N_SAMPLE, 0)),
                  pl.BlockSpec((1, k, tn), lambda j, i: (layer, 0, j))],
        out_specs=[pl.BlockSpec((tm, tn), lambda j, i: (i, j)),
                   pl.BlockSpec((N_SAMPLE, tn), lambda j, i: (0, j))],
        out_shape=[jax.ShapeDtypeStruct((m, n), BF16),
                   jax.ShapeDtypeStruct((N_SAMPLE, n), F32)],
        scratch_shapes=[pltpu.VMEM((k, tn), BF16)],
        compiler_params=_cp(2),
        name="in_even",
    )(xn, xn, w_stack)


def _kv_tail_kernel(a_ref, w_ref, o_ref):
    o_ref[...] = jnp.dot(a_ref[...], w_ref[0].astype(BF16), preferred_element_type=F32)


def _kv_tail(xn, w_stack, layer):
    k = xn.shape[1]
    blks_per_seq = SEQ // WINDOW
    width = 2 * A_KV_WIDTH
    return pl.pallas_call(
        _kv_tail_kernel,
        grid=(BATCH,),
        in_specs=[pl.BlockSpec((WINDOW, k), lambda b: (blks_per_seq * (b + 1) - 1, 0)),
                  pl.BlockSpec((1, k, width), lambda b: (layer, 0, OFF_K // width))],
        out_specs=pl.BlockSpec((WINDOW, width), lambda b: (b, 0)),
        out_shape=jax.ShapeDtypeStruct((BATCH * WINDOW, width), F32),
        compiler_params=_cp(1),
        name="kv_tail",
    )(xn, w_stack)


def _mm_res_q_kernel(nk, a_ref, w_ref, x_ref, g_ref, wq_ref, xo_ref, q_ref):
    k = pl.program_id(1)

    @pl.when(k == 0)
    def _():
        xo_ref[...] = x_ref[...]

    xo_ref[...] += jnp.dot(a_ref[...], w_ref[0], preferred_element_type=F32)

    @pl.when(k == nk - 1)
    def _():
        xn = _rms(xo_ref[...], g_ref[...]).astype(BF16)
        q_ref[...] = jnp.dot(xn, wq_ref[0].astype(BF16), preferred_element_type=F32).astype(q_ref.dtype)


def _mm_res_q(a, w_stack, layer, x, g, wq_stack, wq_layer, tm, tk, x_blk0=0):
    m, kdim = a.shape
    d = w_stack.shape[2]
    nq = wq_stack.shape[2]
    nk = kdim // tk
    return pl.pallas_call(
        functools.partial(_mm_res_q_kernel, nk),
        grid=(m // tm, nk),
        in_specs=[pl.BlockSpec((tm, tk), lambda i, k: (i, k)),
                  pl.BlockSpec((1, tk, d), lambda i, k: (layer, k, 0)),
                  pl.BlockSpec((tm, d), lambda i, k: (x_blk0 + i, 0)),
                  pl.BlockSpec((1, d), lambda i, k: (0, 0)),
                  pl.BlockSpec((1, d, nq), lambda i, k: (wq_layer, 0, 0))],
        out_specs=[pl.BlockSpec((tm, d), lambda i, k: (i, 0)),
                   pl.BlockSpec((tm, nq), lambda i, k: (i, 0))],
        out_shape=[jax.ShapeDtypeStruct((m, d), F32),
                   jax.ShapeDtypeStruct((m, nq), BF16)],
        compiler_params=_cp(2),
        name="mm_res_q",
    )(a, w_stack, x, g.reshape(1, d), wq_stack)


def _embed_kernel(n_prompt_blks, xp_ref, xs_ref, g_ref, n_ref):
    i = pl.program_id(0)

    @pl.when(i < n_prompt_blks)
    def _():
        n_ref[...] = _rms(xp_ref[...], g_ref[...]).astype(n_ref.dtype)

    @pl.when(i >= n_prompt_blks)
    def _():
        n_ref[...] = _rms(xs_ref[...], g_ref[...]).astype(n_ref.dtype)


def _embed(xp, xs, g):
    d = xp.shape[1]
    tm = N_SAMPLE
    nb = N_PROMPT // tm
    return pl.pallas_call(
        functools.partial(_embed_kernel, nb),
        grid=(nb + 1,),
        in_specs=[pl.BlockSpec((tm, d), lambda i: (jnp.minimum(i, nb - 1), 0)),
                  pl.BlockSpec((tm, d), lambda i: (0, 0)),
                  pl.BlockSpec((1, d), lambda i: (0, 0))],
        out_specs=pl.BlockSpec((tm, d), lambda i: (i, 0)),
        out_shape=jax.ShapeDtypeStruct((N_ROWS, d), BF16),
        compiler_params=_cp(1),
        name="embed",
    )(xp, xs, g.reshape(1, d))


def _memkv_kernel(mem_ref, g_ref, wk_ref, wv_ref, mk_ref, mv_ref):
    h = _rms(mem_ref[...], g_ref[0]).astype(BF16)
    mk_ref[0] = jnp.dot(h, wk_ref[0].astype(BF16), preferred_element_type=F32)
    mv_ref[0] = jnp.dot(h, wv_ref[0].astype(BF16), preferred_element_type=F32)


def _memkv(mem, norm_mem, w_xk, w_xv):
    m = mem.shape[0]
    out = jax.ShapeDtypeStruct((DEPTH, m, X_WIDTH), F32)
    return pl.pallas_call(
        _memkv_kernel,
        grid=(DEPTH,),
        in_specs=[pl.BlockSpec((m, D_MODEL), lambda l: (0, 0)),
                  pl.BlockSpec((1, 1, D_MODEL), lambda l: (l, 0, 0)),
                  pl.BlockSpec((1, D_MODEL, X_WIDTH), lambda l: (l, 0, 0)),
                  pl.BlockSpec((1, D_MODEL, X_WIDTH), lambda l: (l, 0, 0))],
        out_specs=[pl.BlockSpec((1, m, X_WIDTH), lambda l: (l, 0, 0)),
                   pl.BlockSpec((1, m, X_WIDTH), lambda l: (l, 0, 0))],
        out_shape=[out, out],
        compiler_params=_cp(1),
        name="memkv",
    )(mem, norm_mem.reshape(DEPTH, 1, D_MODEL), w_xk, w_xv)


SLABS_PER_KV = A_HEADS // A_KV_HEADS // 2
MASK_DIST = 1e30 * 2.0 ** 8


def _expand_band(band, scale):
    x = band.astype(F32)
    if scale != 1.0:
        x = x * scale
    xs = pltpu.roll(x, 64, 1)
    lo = lax.broadcasted_iota(jnp.int32, x.shape, 1) < 64
    z = jnp.zeros_like(x)
    e0 = jnp.concatenate([jnp.where(lo, x, z), jnp.where(lo, z, xs)], axis=0).astype(BF16)
    e1 = jnp.concatenate([jnp.where(lo, xs, z), jnp.where(lo, z, x)], axis=0).astype(BF16)
    return e0, e1


def _band_bias(t, first_key):
    row = lax.broadcasted_iota(jnp.int32, (t, 2 * WINDOW), 0)
    col = lax.broadcasted_iota(jnp.int32, (t, 2 * WINDOW), 1)
    dist = row + WINDOW - col
    valid = (dist >= 0) & (dist < WINDOW) & (col >= first_key)
    return jnp.where(valid, -dist.astype(F32), -MASK_DIST)


def _swa(q_slab, kband, vband, nbias, sl_ref):
    t = nbias.shape[0]
    rows = SLABS_PER_KV * t
    kk = _expand_band(kband, A_HEAD_DIM ** -0.5)
    vv = _expand_band(vband, 1.0)
    nb = jnp.concatenate([nbias] * SLABS_PER_KV, axis=0)
    blk = lax.shift_right_logical(lax.broadcasted_iota(jnp.int32, (rows, 1), 0), t.bit_length() - 1)
    lo = lax.broadcasted_iota(jnp.int32, (rows, 128), 1) < 64
    def per_row(table_row, j, half):
        slab0 = SLABS_PER_KV * j
        col = jnp.full((rows, 1), sl_ref[table_row, 2 * (slab0 + SLABS_PER_KV - 1) + half], F32)
        for i in reversed(range(SLABS_PER_KV - 1)):
            col = jnp.where(blk == i, sl_ref[table_row, 2 * (slab0 + i) + half], col)
        return col

    scores = [_dot_nt(jnp.concatenate([q_slab(SLABS_PER_KV * j + i) for i in range(SLABS_PER_KV)], axis=0), kk[j])
              for j in range(A_KV_HEADS)]
    pps, scales = [], []
    for j, s in enumerate(scores):
        probs, invs = [], []
        for half in range(2):
            slope = per_row(0, j, half)
            sink = per_row(1, j, half)
            sh = s[:, half * 256:(half + 1) * 256] + slope * nb
            m = jnp.maximum(jnp.max(sh, axis=-1, keepdims=True), sink)
            pe = jnp.exp(sh - m)
            den = jnp.sum(pe, axis=-1, keepdims=True) + jnp.exp(sink - m)
            probs.append(pe)
            invs.append(1.0 / den)
        pps.append(jnp.concatenate(probs, axis=1).astype(BF16))
        scales.append(jnp.where(lo, invs[0], invs[1]))
    outs = []
    for j in range(A_KV_HEADS):
        o = jnp.dot(pps[j], vv[j], preferred_element_type=F32) * scales[j]
        outs += [o[i * t:(i + 1) * t] for i in range(SLABS_PER_KV)]
    return outs


def _spatial_gate_group(ws_ref, bs_ref, g, vn_g, rows):
    r = lax.broadcasted_iota(jnp.int32, (CHUNK, CHUNK), 0)
    c = lax.broadcasted_iota(jnp.int32, (CHUNK, CHUNK), 1)
    w = jnp.where(r >= c, ws_ref[g], 0.0).astype(BF16)
    return jnp.dot(w[:rows], vn_g, preferred_element_type=F32) + bs_ref[g][:rows]


EVEN_STEP_ROWS = 512
EVEN_SUBBLKS = EVEN_STEP_ROWS // WINDOW
EVEN_STEPS = N_PROMPT // EVEN_STEP_ROWS


def _even_fused_kernel(sl_ref, z_ref, kvprev_ref, ws_ref, bs_ref, lng_ref, lnb_ref,
                       w_ref, x_ref, g_ref, wq_ref, xo_ref, q_ref, mixed_ref):
    s_id = pl.program_id(0)
    slot = s_id % 2

    @pl.when(s_id == 0)
    def _():
        mixed_ref[1] = jnp.zeros(mixed_ref.shape[1:], mixed_ref.dtype)

    kc = D_MODEL // EVEN_SUBBLKS
    xo_ref[...] = x_ref[...]
    blk0 = jnp.minimum(s_id, EVEN_STEPS - 1) * EVEN_SUBBLKS
    for b in range(EVEN_SUBBLKS):
        xo_ref[...] += jnp.dot(mixed_ref[1 - slot, :, b * kc:(b + 1) * kc], w_ref[0, b * kc:(b + 1) * kc, :],
                               preferred_element_type=F32)
        rows = slice(b * WINDOW, (b + 1) * WINDOW)
        c = (blk0 + b) % (SEQ // WINDOW)
        kv_cur = z_ref[rows, OFF_K:OFF_K + 2 * A_KV_WIDTH]
        kv_prev = kvprev_ref[...] if b == 0 else z_ref[(b - 1) * WINDOW:b * WINDOW, OFF_K:OFF_K + 2 * A_KV_WIDTH]
        band = jnp.concatenate([kv_prev, kv_cur], axis=0)
        nbias = _band_bias(WINDOW, jnp.maximum(WINDOW - c * WINDOW, 0))

        def q_slab(p, rows=rows):
            return z_ref[rows, OFF_Q + p * 128:OFF_Q + (p + 1) * 128]

        for p, att in enumerate(_swa(q_slab, band[:, :A_KV_WIDTH], band[:, A_KV_WIDTH:], nbias, sl_ref)):
            ga = z_ref[rows, OFF_GA + p * 128:OFF_GA + (p + 1) * 128].astype(F32)
            mixed_ref[slot, rows, p * 128:(p + 1) * 128] = (_silu(ga) * att).astype(mixed_ref.dtype)

        vn = _standardize(z_ref[rows, OFF_VB:OFF_VB + B_WIDTH].astype(F32)) * lng_ref[...] + lnb_ref[...]
        for g in range(B_GROUPS):
            lo, hi = g * B_GROUP_DIM, (g + 1) * B_GROUP_DIM
            sg = _spatial_gate_group(ws_ref, bs_ref, g, vn[:, lo:hi].astype(BF16), CHUNK)
            u = z_ref[rows, OFF_U + lo:OFF_U + hi].astype(F32)
            gb = z_ref[rows, OFF_GB + lo:OFF_GB + hi].astype(F32)
            mixed_ref[slot, rows, A_WIDTH + lo:A_WIDTH + hi] = (_silu(gb) * (u * sg)).astype(mixed_ref.dtype)

    q_ref[...] = jnp.dot(_rms(xo_ref[...], g_ref[...]).astype(BF16), wq_ref[0],
                         preferred_element_type=F32).astype(q_ref.dtype)


def _even_fused(z, sl, w_s, b_s, lng, lnb, w_out_b, layer_i, x, g, wq_b, layer):
    ns = EVEN_STEPS
    r = EVEN_STEP_ROWS
    kv_blk = OFF_K // (2 * A_KV_WIDTH)

    def cur(s):
        return jnp.minimum(s, ns - 1)

    def prev(s):
        return jnp.maximum(s - 1, 0)

    once = pl.Buffered(1)
    return pl.pallas_call(
        _even_fused_kernel,
        grid=(ns + 1,),
        in_specs=[pl.BlockSpec(memory_space=pltpu.SMEM),
                  pl.BlockSpec((r, EVEN_IN), lambda s: (cur(s), 0)),
                  pl.BlockSpec((WINDOW, 2 * A_KV_WIDTH),
                               lambda s: (jnp.maximum(cur(s) * EVEN_SUBBLKS - 1, 0), kv_blk)),
                  pl.BlockSpec((B_GROUPS, CHUNK, CHUNK), lambda s: (0, 0, 0)),
                  pl.BlockSpec((B_GROUPS, CHUNK, 1), lambda s: (0, 0, 0)),
                  pl.BlockSpec((1, B_WIDTH), lambda s: (0, 0)),
                  pl.BlockSpec((1, B_WIDTH), lambda s: (0, 0)),
                  pl.BlockSpec((1, D_MODEL, D_MODEL), lambda s: (layer_i, 0, 0), pipeline_mode=once),
                  pl.BlockSpec((r, D_MODEL), lambda s: (prev(s), 0)),
                  pl.BlockSpec((1, D_MODEL), lambda s: (0, 0)),
                  pl.BlockSpec((1, D_MODEL, X_WIDTH), lambda s: (layer, 0, 0), pipeline_mode=once)],
        out_specs=[pl.BlockSpec((r, D_MODEL), lambda s: (prev(s), 0)),
                   pl.BlockSpec((r, X_WIDTH), lambda s: (prev(s), 0))],
        out_shape=[jax.ShapeDtypeStruct((N_PROMPT, D_MODEL), F32),
                   jax.ShapeDtypeStruct((N_PROMPT, X_WIDTH), BF16)],
        scratch_shapes=[pltpu.VMEM((2, r, D_MODEL), BF16)],
        compiler_params=_cp(1, FUSED_VMEM_LIMIT),
        name="even_fused",
    )(sl, z, z, w_s, b_s, lng, lnb, w_out_b, x, g.reshape(1, D_MODEL), wq_b)


SEQ_PER_STEP = 2
ROWS_PER_STEP = SEQ_PER_STEP * DEC_SEQ
assert DEC_SEQ & (DEC_SEQ - 1) == 0


def _seq_of(row_idx):
    return lax.shift_right_logical(row_idx, DEC_SEQ.bit_length() - 1)


def _even_sample_kernel(sl_ref, z_ref, ck_ref, cv_ref, ws_ref, bs_ref, lng_ref, lnb_ref,
                        o_ref, wk_ref, wv_ref, cvo_ref):
    nbias = _band_bias(ROWS_PER_STEP, 0)
    pad_rows = jnp.zeros((ROWS_PER_STEP - DEC_SEQ, EVEN_IN), F32)
    attn = [[None] * SEQ_PER_STEP for _ in range(A_HEADS // 2)]
    gate = [[None] * SEQ_PER_STEP for _ in range(B_GROUPS)]
    for r in range(SEQ_PER_STEP):
        r0, r1 = r * DEC_SEQ, (r + 1) * DEC_SEQ
        zr = z_ref[r0:r1, :]
        k_new = zr[:, OFF_K:OFF_K + A_KV_WIDTH]
        v_new = zr[:, OFF_V:OFF_V + A_KV_WIDTH]
        ck = ck_ref[r]
        cv = cv_ref[r]
        wk_ref[r, 0:WINDOW - DEC_SEQ, :] = ck[DEC_SEQ:, :]
        wk_ref[r, WINDOW - DEC_SEQ:, :] = k_new
        wv_ref[r, 0:WINDOW - DEC_SEQ, :] = cv[DEC_SEQ:, :]
        wv_ref[r, WINDOW - DEC_SEQ:, :] = v_new
        tail = jnp.zeros((WINDOW - DEC_SEQ, A_KV_WIDTH), F32)
        kband = jnp.concatenate([ck, k_new, tail], axis=0)
        vband = jnp.concatenate([cv, v_new, tail], axis=0)
        zq = jnp.concatenate([zr, pad_rows], axis=0)

        def q_slab(p, zq=zq):
            return zq[:, OFF_Q + p * 128:OFF_Q + (p + 1) * 128].astype(BF16)

        for p, a in enumerate(_swa(q_slab, kband, vband, nbias, sl_ref)):
            ga = zr[:, OFF_GA + p * 128:OFF_GA + (p + 1) * 128]
            attn[p][r] = _silu(ga) * a[:DEC_SEQ]

        vn = _standardize(zr[:, OFF_VB:OFF_VB + B_WIDTH]) * lng_ref[...] + lnb_ref[...]
        cvo_ref[r0:r1, :] = vn
        vn_pad = jnp.concatenate([vn, jnp.zeros((CHUNK - DEC_SEQ, B_WIDTH), F32)], axis=0).astype(BF16)
        for g in range(B_GROUPS):
            lo, hi = g * B_GROUP_DIM, (g + 1) * B_GROUP_DIM
            mixed = _spatial_gate_group(ws_ref, bs_ref, g, vn_pad[:, lo:hi], ROWS_PER_STEP)[:DEC_SEQ]
            u = zr[:, OFF_U + lo:OFF_U + hi]
            gb = zr[:, OFF_GB + lo:OFF_GB + hi]
            gate[g][r] = _silu(gb) * (u * mixed)

    for p in range(A_HEADS // 2):
        o_ref[:, p * 128:(p + 1) * 128] = jnp.concatenate(attn[p], axis=0).astype(o_ref.dtype)
    for g in range(B_GROUPS):
        lo, hi = A_WIDTH + g * B_GROUP_DIM, A_WIDTH + (g + 1) * B_GROUP_DIM
        o_ref[:, lo:hi] = jnp.concatenate(gate[g], axis=0).astype(o_ref.dtype)


def _even_sample(zs, ck, cv, sl, w_s, b_s, lng, lnb):
    n_steps = DEC_BATCH // SEQ_PER_STEP
    win = jax.ShapeDtypeStruct((DEC_BATCH, WINDOW, A_KV_WIDTH), F32)
    return pl.pallas_call(
        _even_sample_kernel,
        grid=(n_steps,),
        in_specs=[pl.BlockSpec(memory_space=pltpu.SMEM),
                  pl.BlockSpec((ROWS_PER_STEP, EVEN_IN), lambda i: (i, 0)),
                  pl.BlockSpec((SEQ_PER_STEP, WINDOW, A_KV_WIDTH), lambda i: (i, 0, 0)),
                  pl.BlockSpec((SEQ_PER_STEP, WINDOW, A_KV_WIDTH), lambda i: (i, 0, 0)),
                  pl.BlockSpec((B_GROUPS, CHUNK, CHUNK), lambda i: (0, 0, 0)),
                  pl.BlockSpec((B_GROUPS, CHUNK, 1), lambda i: (0, 0, 0)),
                  pl.BlockSpec((1, B_WIDTH), lambda i: (0, 0)),
                  pl.BlockSpec((1, B_WIDTH), lambda i: (0, 0))],
        out_specs=[pl.BlockSpec((ROWS_PER_STEP, D_MODEL), lambda i: (i, 0)),
                   pl.BlockSpec((SEQ_PER_STEP, WINDOW, A_KV_WIDTH), lambda i: (i, 0, 0)),
                   pl.BlockSpec((SEQ_PER_STEP, WINDOW, A_KV_WIDTH), lambda i: (i, 0, 0)),
                   pl.BlockSpec((ROWS_PER_STEP, B_WIDTH), lambda i: (i, 0))],
        out_shape=[jax.ShapeDtypeStruct((N_SAMPLE, D_MODEL), BF16), win, win,
                   jax.ShapeDtypeStruct((N_SAMPLE, B_WIDTH), F32)],
        compiler_params=_cp(1),
        name="even_sample",
    )(sl, zs, ck, cv, w_s, b_s, lng, lnb)


RET_ROWS = 256
RET_BLKS_PER_SEQ = SEQ // RET_ROWS
RET_PROMPT_BLKS = N_PROMPT // RET_ROWS


def _gated_groupnorm(o, g):
    return (_silu(g.astype(F32)) * _standardize(o)).astype(BF16)


RET_LOOKAHEAD = 1


def _ret_fused_kernel(lg_ref, z_ref, w_ref, x_ref, g_ref, wq_ref, xo_ref, q_ref, so_ref, dm_ref, dc_ref):
    s_id = pl.program_id(0)
    t = RET_ROWS
    k_scale = R_KEY_DIM ** -0.5
    c = s_id % RET_BLKS_PER_SEQ

    @pl.when(s_id == 0)
    def _():
        row = lax.broadcasted_iota(jnp.int32, (t, t), 0)
        col = lax.broadcasted_iota(jnp.int32, (t, t), 1)
        rel = (row - col).astype(F32)
        ti = lax.broadcasted_iota(jnp.int32, (t, 1), 0).astype(F32)
        for h in range(R_HEADS):
            lg = lg_ref[h]
            dm_ref[h] = jnp.where(rel >= 0, jnp.exp(jnp.maximum(rel, 0.0) * lg), 0.0) * k_scale
            dc_ref[h, 0] = jnp.exp((ti + 1.0) * lg)
            dc_ref[h, 1] = jnp.exp((t - 1.0 - ti) * lg) * k_scale

    @pl.when(c == 0)
    def _():
        so_ref[...] = jnp.zeros_like(so_ref)

    def head_matmuls(h):
        lg = lg_ref[h]
        q_decay = dc_ref[h, 0]
        k_decay = dc_ref[h, 1]
        chunk_decay = jnp.exp(jnp.full((1, R_VAL_DIM), float(t), F32) * lg)
        q = z_ref[:, h * R_KEY_DIM:(h + 1) * R_KEY_DIM]
        k = z_ref[:, R_K_WIDTH + h * R_KEY_DIM:R_K_WIDTH + (h + 1) * R_KEY_DIM]
        v = z_ref[:, 2 * R_K_WIDTH + h * R_VAL_DIM:2 * R_K_WIDTH + (h + 1) * R_VAL_DIM]
        st = so_ref[0, h]
        scores = _dot_nt(q, k) * dm_ref[h]
        inner = jnp.dot(scores.astype(BF16), v, preferred_element_type=F32)
        cross = jnp.dot(q, st.astype(BF16), preferred_element_type=F32) * q_decay
        kd_t = (k.astype(F32) * k_decay).T.astype(BF16)
        so_ref[0, h] = chunk_decay * st + jnp.dot(kd_t, v, preferred_element_type=F32)
        return inner + cross

    x1 = x_ref[...]
    ahead = [head_matmuls(h) for h in range(RET_LOOKAHEAD)]
    for h in range(R_HEADS):
        if h + RET_LOOKAHEAD < R_HEADS:
            ahead.append(head_matmuls(h + RET_LOOKAHEAD))
        raw = ahead.pop(0)
        g = z_ref[:, 2 * R_K_WIDTH + R_V_WIDTH + h * R_VAL_DIM:2 * R_K_WIDTH + R_V_WIDTH + (h + 1) * R_VAL_DIM]
        x1 = x1 + jnp.dot(_gated_groupnorm(raw, g), w_ref[0, h * R_VAL_DIM:(h + 1) * R_VAL_DIM, :],
                          preferred_element_type=F32)
    xo_ref[...] = x1
    q_ref[...] = jnp.dot(_rms(x1, g_ref[...]).astype(BF16), wq_ref[0],
                         preferred_element_type=F32).astype(q_ref.dtype)


def _ret_fused(z, log_g, w_out_b, layer_i, x, g, wq_b, layer):
    nb = RET_PROMPT_BLKS
    r = RET_ROWS
    once = pl.Buffered(1)
    return pl.pallas_call(
        _ret_fused_kernel,
        grid=(nb,),
        in_specs=[pl.BlockSpec(memory_space=pltpu.SMEM),
                  pl.BlockSpec((r, RET_IN), lambda s: (s, 0)),
                  pl.BlockSpec((1, R_V_WIDTH, D_MODEL), lambda s: (layer_i, 0, 0), pipeline_mode=once),
                  pl.BlockSpec((r, D_MODEL), lambda s: (s, 0)),
                  pl.BlockSpec((1, D_MODEL), lambda s: (0, 0)),
                  pl.BlockSpec((1, D_MODEL, X_WIDTH), lambda s: (layer, 0, 0), pipeline_mode=once)],
        out_specs=[pl.BlockSpec((r, D_MODEL), lambda s: (s, 0)),
                   pl.BlockSpec((r, X_WIDTH), lambda s: (s, 0)),
                   pl.BlockSpec((1, R_HEADS, R_KEY_DIM, R_VAL_DIM),
                                lambda s: (s // RET_BLKS_PER_SEQ, 0, 0, 0))],
        out_shape=[jax.ShapeDtypeStruct((N_PROMPT, D_MODEL), F32),
                   jax.ShapeDtypeStruct((N_PROMPT, X_WIDTH), BF16),
                   jax.ShapeDtypeStruct((BATCH, R_HEADS, R_KEY_DIM, R_VAL_DIM), F32)],
        scratch_shapes=[pltpu.VMEM((R_HEADS, r, r), F32),
                        pltpu.VMEM((R_HEADS, 2, r, 1), F32)],
        compiler_params=_cp(1, FUSED_VMEM_LIMIT),
        name="ret_fused",
    )(log_g, z, w_out_b, x, g.reshape(1, D_MODEL), wq_b)


RET_SAMPLE_SEQS = 8
RET_SAMPLE_ROWS = RET_SAMPLE_SEQS * DEC_SEQ


def _ret_sample_kernel(lg_ref, q_ref, k_ref, v_ref, g_ref, s_ref, *rest):
    o_ref, so_ref = rest[-2:]
    h = pl.program_id(1)
    lg = lg_ref[h]
    n = RET_SAMPLE_ROWS
    t = DEC_SEQ
    row = lax.broadcasted_iota(jnp.int32, (n, n), 0)
    col = lax.broadcasted_iota(jnp.int32, (n, n), 1)
    rel = (row - col).astype(F32)
    same = _seq_of(row) == _seq_of(col)
    k_scale = R_KEY_DIM ** -0.5
    dmat = jnp.where(same & (rel >= 0), jnp.exp(jnp.maximum(rel, 0.0) * lg), 0.0) * k_scale
    ri = lax.broadcasted_iota(jnp.int32, (n, 1), 0)
    ti = (ri & (t - 1)).astype(F32)
    q_decay = jnp.exp((ti + 1.0) * lg)
    k_decay = jnp.exp((t - 1.0 - ti) * lg) * k_scale
    step_decay = jnp.exp(jnp.full((1, R_VAL_DIM), float(t), F32) * lg)

    q = q_ref[...]
    k = k_ref[...]
    v = v_ref[...]
    scores = _dot_nt(q, k) * dmat
    o = jnp.dot(scores.astype(BF16), v, preferred_element_type=F32)
    kd = k.astype(F32) * k_decay
    k_pad = jnp.zeros((R_CHUNK - n, R_KEY_DIM), F32)
    v_pad = jnp.concatenate([v.astype(F32), jnp.zeros((R_CHUNK - n, R_VAL_DIM), F32)], axis=0).astype(BF16)
    for r in range(RET_SAMPLE_SEQS):
        mine = _seq_of(ri) == r
        st = s_ref[0, r, 0]
        cross = jnp.dot(q, st.astype(BF16), preferred_element_type=F32) * q_decay
        o = o + jnp.where(mine, cross, 0.0)
        kd_t = jnp.concatenate([jnp.where(mine, kd, 0.0), k_pad], axis=0).T.astype(BF16)
        so_ref[0, r, 0] = step_decay * st + jnp.dot(kd_t, v_pad, preferred_element_type=F32)
    o_ref[...] = _gated_groupnorm(o, g_ref[...])


def _ret_sample(z, state_all, idx, log_g, new_states):
    n_steps = DEC_BATCH // RET_SAMPLE_SEQS
    rb0 = N_PROMPT // RET_SAMPLE_ROWS
    kcol = R_K_WIDTH // R_KEY_DIM
    vcol = 2 * R_K_WIDTH // R_VAL_DIM
    gcol = vcol + R_HEADS
    st_spec = pl.BlockSpec((1, RET_SAMPLE_SEQS, 1, R_KEY_DIM, R_VAL_DIM), lambda i, h: (idx, i, h, 0, 0))
    in_specs = [pl.BlockSpec(memory_space=pltpu.SMEM),
                pl.BlockSpec((RET_SAMPLE_ROWS, R_KEY_DIM), lambda i, h: (rb0 + i, h)),
                pl.BlockSpec((RET_SAMPLE_ROWS, R_KEY_DIM), lambda i, h: (rb0 + i, kcol + h)),
                pl.BlockSpec((RET_SAMPLE_ROWS, R_VAL_DIM), lambda i, h: (rb0 + i, vcol + h)),
                pl.BlockSpec((RET_SAMPLE_ROWS, R_VAL_DIM), lambda i, h: (rb0 + i, gcol + h)),
                st_spec]
    args = [log_g, z, z, z, z, state_all]
    aliases = {}
    if new_states is not None:
        in_specs.append(pl.BlockSpec(memory_space=pl.ANY))
        args.append(new_states)
        aliases = {len(args) - 1: 1}
    return pl.pallas_call(
        _ret_sample_kernel,
        grid=(n_steps, R_HEADS),
        in_specs=in_specs,
        out_specs=[pl.BlockSpec((RET_SAMPLE_ROWS, R_VAL_DIM), lambda i, h: (i, h)), st_spec],
        out_shape=[jax.ShapeDtypeStruct((N_SAMPLE, R_V_WIDTH), BF16),
                   jax.ShapeDtypeStruct(state_all.shape, F32)],
        input_output_aliases=aliases,
        compiler_params=_cp(2),
        name="ret_sample",
    )(*args)


XQ_ROWS = 256
XQ_PROMPT_BLKS = N_PROMPT // XQ_ROWS
XQ_BLKS_PER_SEQ = SEQ // XQ_ROWS


def _xattn_heads(q, mk_heads, mv_heads):
    pairs = [(i, h) for i in range(len(mk_heads)) for h in range(X_HEADS)]
    scores = [_dot_nt(q[:, h * X_HEAD_DIM:(h + 1) * X_HEAD_DIM], mk_heads[i](h)) * (X_HEAD_DIM ** -0.5)
              for i, h in pairs]
    probs, invs = [], []
    for s in scores:
        p = jnp.exp(s - jnp.max(s, axis=-1, keepdims=True))
        probs.append(p.astype(BF16))
        invs.append(1.0 / jnp.sum(p, axis=-1, keepdims=True))
    outs = [jnp.dot(p, mv_heads[i](h), preferred_element_type=F32) * inv
            for (i, h), p, inv in zip(pairs, probs, invs)]
    return [outs[i * X_HEADS:(i + 1) * X_HEADS] for i in range(len(mk_heads))]


def _xattn_proj_kernel(final, q_ref, mk_ref, mv_ref, os_ref, w_ref, xp_ref, xs_ref, g_ref, out_a, out_b):
    s_id = pl.program_id(0)

    def project(o, x_ref):
        return x_ref[...] + jnp.dot(o, w_ref[0], preferred_element_type=F32)

    @pl.when(s_id < XQ_PROMPT_BLKS)
    def _():
        def head_of(ref):
            return lambda h: ref[0, :, h * X_HEAD_DIM:(h + 1) * X_HEAD_DIM].astype(BF16)

        outs, = _xattn_heads(q_ref[...], [head_of(mk_ref)], [head_of(mv_ref)])
        x2 = project(jnp.concatenate(outs, axis=1).astype(BF16), xp_ref)
        if final:
            out_a[...] = _rms(x2, g_ref[...])
        else:
            out_a[...] = x2
            out_b[...] = _rms(x2, g_ref[...]).astype(out_b.dtype)

    @pl.when(s_id >= XQ_PROMPT_BLKS)
    def _():
        x2 = project(os_ref[...], xs_ref)
        if final:
            out_b[...] = _rms(x2, g_ref[...])
        else:
            out_a[...] = x2
            out_b[...] = _rms(x2, g_ref[...]).astype(out_b.dtype)


def _xattn_proj(q, mk_all, mv_all, layer, o_sample, w_stack, x_prompt, x_sample, g, final):
    nb = XQ_PROMPT_BLKS
    d = w_stack.shape[2]
    xs_blk = (x_sample.shape[0] - N_SAMPLE) // N_SAMPLE

    def rb(s):
        return jnp.minimum(s, nb - 1)

    mem_spec = pl.BlockSpec((1, MEM_LEN, X_WIDTH), lambda s: (layer, rb(s) // XQ_BLKS_PER_SEQ, 0))
    if final:
        out_specs = [pl.BlockSpec((XQ_ROWS, d), lambda s: (rb(s), 0)),
                     pl.BlockSpec((N_SAMPLE, d), lambda s: (0, 0))]
        out_shape = [jax.ShapeDtypeStruct((N_PROMPT, d), F32), jax.ShapeDtypeStruct((N_SAMPLE, d), F32)]
    else:
        out_specs = [pl.BlockSpec((XQ_ROWS, d), lambda s: (s, 0)),
                     pl.BlockSpec((XQ_ROWS, d), lambda s: (s, 0))]
        out_shape = [jax.ShapeDtypeStruct((N_ROWS, d), F32), jax.ShapeDtypeStruct((N_ROWS, d), BF16)]
    return pl.pallas_call(
        functools.partial(_xattn_proj_kernel, final),
        grid=(nb + 1,),
        in_specs=[pl.BlockSpec((XQ_ROWS, X_WIDTH), lambda s: (rb(s), 0)),
                  mem_spec, mem_spec,
                  pl.BlockSpec((N_SAMPLE, X_WIDTH), lambda s: (0, 0)),
                  pl.BlockSpec((1, X_WIDTH, d), lambda s: (layer, 0, 0)),
                  pl.BlockSpec((XQ_ROWS, d), lambda s: (rb(s), 0)),
                  pl.BlockSpec((N_SAMPLE, d), lambda s: (xs_blk, 0)),
                  pl.BlockSpec((1, d), lambda s: (0, 0))],
        out_specs=out_specs,
        out_shape=out_shape,
        compiler_params=_cp(1),
        name="xattn_proj",
    )(q, mk_all, mv_all, o_sample, w_stack, x_prompt, x_sample, g.reshape(1, d))


XS_SEQS = 4
XS_ROWS = XS_SEQS * DEC_SEQ


def _xattn_sample_kernel(q_ref, ck_ref, cv_ref, o_ref):
    q = q_ref[...]
    ri = lax.broadcasted_iota(jnp.int32, (XS_ROWS, 1), 0)
    def head_of(ref, r):
        return lambda h: ref[0, r, pl.ds(h, MEM_LEN, stride=X_HEADS), :].astype(BF16)

    per_seq = _xattn_heads(q, [head_of(ck_ref, r) for r in range(XS_SEQS)],
                           [head_of(cv_ref, r) for r in range(XS_SEQS)])
    acc = [jnp.zeros((XS_ROWS, X_HEAD_DIM), F32) for _ in range(X_HEADS)]
    for r, outs in enumerate(per_seq):
        mine = _seq_of(ri) == r
        acc = [a + jnp.where(mine, o, 0.0) for a, o in zip(acc, outs)]
    for h, a in enumerate(acc):
        o_ref[:, h * X_HEAD_DIM:(h + 1) * X_HEAD_DIM] = a.astype(o_ref.dtype)


def _xattn_sample(q, ck_all, cv_all, layer):
    n_steps = DEC_BATCH // XS_SEQS
    rb0 = (q.shape[0] - N_SAMPLE) // XS_ROWS
    mem_spec = pl.BlockSpec((1, XS_SEQS, MEM_LEN * X_HEADS, X_HEAD_DIM), lambda i: (layer, i, 0, 0))
    return pl.pallas_call(
        _xattn_sample_kernel,
        grid=(n_steps,),
        in_specs=[pl.BlockSpec((XS_ROWS, X_WIDTH), lambda i: (rb0 + i, 0)), mem_spec, mem_spec],
        out_specs=pl.BlockSpec((XS_ROWS, X_WIDTH), lambda i: (i, 0)),
        out_shape=jax.ShapeDtypeStruct((N_SAMPLE, X_WIDTH), BF16),
        compiler_params=_cp(1),
        name="xattn_sample",
    )(q, ck_all, cv_all)


def kernel(x_prompt, x_sample, cache_win_k, cache_win_v, state_ret, cache_mem_k, cache_mem_v, mem_prompt,
           norm_mix, w_in_even, attn_sinks, w_spatial, b_spatial, norm_v_g, norm_v_b, w_out_even,
           w_in_ret, w_out_ret, norm_cross, norm_mem, w_xq, w_xk, w_xv, w_xo, norm_final):
    w_out_even_b = w_out_even.astype(BF16)
    w_out_ret_b = w_out_ret.astype(BF16)
    w_xo_b = w_xo.astype(BF16)
    w_xq_b = w_xq.astype(BF16)

    slopes = jnp.exp2(-8.0 * (jnp.arange(A_HEADS, dtype=F32) + 1.0) / A_HEADS)
    log_g = jnp.log1p(-jnp.exp2(-5.0 - jnp.arange(R_HEADS, dtype=F32)))

    def mem_rows(cache):
        return cache.reshape(DEPTH, DEC_BATCH, MEM_LEN * X_HEADS, X_HEAD_DIM)

    mk_all, mv_all = _memkv(mem_prompt.reshape(BATCH * MEM_LEN, D_MODEL), norm_mem, w_xk, w_xv)

    xp_src = x_prompt.reshape(N_PROMPT, D_MODEL)
    xs_src = x_sample.reshape(N_SAMPLE, D_MODEL)
    xn = _embed(xp_src, xs_src, norm_mix[0])
    wkp, wvp, rsp, wks, wvs, cvs = [], [], [], [], [], []
    rss = None
    for layer in range(DEPTH):
        i = layer // 2
        xs_blk0 = (xs_src.shape[0] - N_SAMPLE) // N_SAMPLE
        if layer % 2 == 0:
            sl = jnp.stack([slopes, attn_sinks[i].astype(F32)])
            b_s = b_spatial[i].reshape(B_GROUPS, CHUNK, 1)
            lng = norm_v_g[i].reshape(1, B_WIDTH)
            lnb = norm_v_b[i].reshape(1, B_WIDTH)
            z, zs = _in_even(xn, w_in_even, i)
            kv_tail = _kv_tail(xn, w_in_even, i)
            wkp.append(kv_tail[:, :A_KV_WIDTH].reshape(BATCH, WINDOW, A_KV_HEADS, A_HEAD_DIM))
            wvp.append(kv_tail[:, A_KV_WIDTH:].reshape(BATCH, WINDOW, A_KV_HEADS, A_HEAD_DIM))
            ck = cache_win_k[i].reshape(DEC_BATCH, WINDOW, A_KV_WIDTH)
            cv = cache_win_v[i].reshape(DEC_BATCH, WINDOW, A_KV_WIDTH)
            mixed_s, wk_new, wv_new, cv_new = _even_sample(zs, ck, cv, sl, w_spatial[i], b_s, lng, lnb)
            wks.append(wk_new.reshape(DEC_BATCH, WINDOW, A_KV_HEADS, A_HEAD_DIM))
            wvs.append(wv_new.reshape(DEC_BATCH, WINDOW, A_KV_HEADS, A_HEAD_DIM))
            cvs.append(cv_new.reshape(DEC_BATCH, DEC_SEQ, B_WIDTH))
            x_p, q = _even_fused(z, sl, w_spatial[i], b_s, lng, lnb, w_out_even_b, i, xp_src, norm_cross[layer],
                                 w_xq_b, layer)
            x_s, q_s = _mm_res_q(mixed_s, w_out_even_b, i, xs_src, norm_cross[layer], w_xq_b, layer,
                                 N_SAMPLE, 1024, x_blk0=xs_blk0)
        else:
            z = _mm_wcast(xn, w_in_ret, i, BF16, IN_RET_TM, IN_RET_TN, "in_ret")
            o_s, rss = _ret_sample(z, state_ret, i, log_g, rss)
            x_p, q, st_p = _ret_fused(z, log_g, w_out_ret_b, i, xp_src, norm_cross[layer], w_xq_b, layer)
            rsp.append(st_p)
            x_s, q_s = _mm_res_q(o_s, w_out_ret_b, i, xs_src, norm_cross[layer], w_xq_b, layer,
                                 N_SAMPLE, 1024, x_blk0=xs_blk0)
        o_s = _xattn_sample(q_s, mem_rows(cache_mem_k), mem_rows(cache_mem_v), layer)
        if layer + 1 < DEPTH:
            x, xn = _xattn_proj(q, mk_all, mv_all, layer, o_s, w_xo_b, x_p, x_s, norm_mix[layer + 1], False)
            xp_src = xs_src = x
        else:
            y_prompt, y_sample = _xattn_proj(q, mk_all, mv_all, layer, o_s, w_xo_b, x_p, x_s, norm_final, True)

    y_prompt = y_prompt.reshape(BATCH, SEQ, D_MODEL)
    y_sample = y_sample.reshape(DEC_BATCH, DEC_SEQ, D_MODEL)
    mem_shape = (DEPTH, BATCH, MEM_LEN, X_HEADS, X_HEAD_DIM)
    return (y_prompt, y_sample,
            jnp.stack(wkp), jnp.stack(wvp), jnp.stack(rsp),
            mk_all.reshape(mem_shape), mv_all.reshape(mem_shape),
            jnp.stack(wks), jnp.stack(wvs), rss, jnp.stack(cvs))
```

```python
import functools

import jax
import jax.numpy as jnp
from jax import lax
from jax.experimental import pallas as pl
from jax.experimental.pallas import tpu as pltpu

F32 = jnp.float32
BF16 = jnp.bfloat16

D_MODEL = 2048
BATCH = 2
SEQ = 4096
DEPTH = 4
DEC_BATCH = 32
DEC_SEQ = 8
A_HEADS = 16
A_KV_HEADS = 2
A_HEAD_DIM = 64
A_WIDTH = 1024
A_KV_WIDTH = 128
WINDOW = 128
CHUNK = 128
B_GROUPS = 4
B_WIDTH = 1024
B_GROUP_DIM = 256
EVEN_IN = 5376
R_HEADS = 8
R_KEY_DIM = 256
R_VAL_DIM = 512
R_K_WIDTH = 2048
R_V_WIDTH = 4096
R_CHUNK = 128
RET_IN = 12288
MEM_LEN = 256
X_HEADS = 4
X_HEAD_DIM = 128
X_WIDTH = 512
EPS = 1e-6

N_PROMPT = BATCH * SEQ
N_SAMPLE = DEC_BATCH * DEC_SEQ
N_ROWS = N_PROMPT + N_SAMPLE

OFF_Q, OFF_K, OFF_V, OFF_GA, OFF_U, OFF_VB, OFF_GB = 0, 1024, 1152, 1280, 2304, 3328, 4352

MIB = 1024 * 1024
V7X_VMEM_BYTES = 64 * MIB
VMEM_LIMIT = V7X_VMEM_BYTES - 12 * MIB
FUSED_VMEM_LIMIT = V7X_VMEM_BYTES - 8 * MIB
SAMPLE_PROJ_TK = 2048


def _cp(n_axes, vmem_limit=VMEM_LIMIT):
    return pltpu.CompilerParams(dimension_semantics=("arbitrary",) * n_axes,
                                vmem_limit_bytes=vmem_limit)


def _silu(x):
    return x * (1.0 / (1.0 + jnp.exp(-x)))


def _rms(x, g):
    ms = jnp.mean(x * x, axis=-1, keepdims=True)
    return x * lax.rsqrt(ms + EPS) * g


def _standardize(x):
    xc = x - jnp.mean(x, axis=-1, keepdims=True)
    return xc * lax.rsqrt(jnp.mean(xc * xc, axis=-1, keepdims=True) + EPS)


def _dot_nt(a, b):
    return lax.dot_general(a, b, (((1,), (1,)), ((), ())), preferred_element_type=F32)


SIDE_CAST_ROWS = 64


def _side_cast_specs(side_stack, side_layer, n_inner, n_steps):
    _, k, d = side_stack.shape
    n_slabs = k // SIDE_CAST_ROWS
    assert n_slabs <= n_steps

    def slab(j, i):
        return jnp.minimum(j * n_inner + i, n_slabs - 1)

    return (pl.BlockSpec((1, SIDE_CAST_ROWS, d), lambda j, i: (side_layer, slab(j, i), 0)),
            pl.BlockSpec((1, SIDE_CAST_ROWS, d), lambda j, i: (0, slab(j, i), 0)),
            jax.ShapeDtypeStruct((1, k, d), BF16))


def _mm_wcast_kernel(a_ref, w_ref, side_ref, o_ref, side_o_ref, wb_ref):
    @pl.when(pl.program_id(1) == 0)
    def _():
        wb_ref[...] = w_ref[0].astype(BF16)

    o_ref[...] = jnp.dot(a_ref[...], wb_ref[...], preferred_element_type=F32).astype(o_ref.dtype)
    side_o_ref[...] = side_ref[...].astype(BF16)


def _mm_wcast(a, w_stack, layer, out_dtype, tm, tn, side_stack, side_layer, name):
    m, k = a.shape
    n = w_stack.shape[2]
    grid = (n // tn, m // tm)
    side_in, side_out, side_shape = _side_cast_specs(side_stack, side_layer, grid[1], grid[0] * grid[1])
    return pl.pallas_call(
        _mm_wcast_kernel,
        grid=grid,
        in_specs=[pl.BlockSpec((tm, k), lambda j, i: (i, 0)),
                  pl.BlockSpec((1, k, tn), lambda j, i: (layer, 0, j)),
                  side_in],
        out_specs=[pl.BlockSpec((tm, tn), lambda j, i: (i, j)), side_out],
        out_shape=[jax.ShapeDtypeStruct((m, n), out_dtype), side_shape],
        scratch_shapes=[pltpu.VMEM((k, tn), BF16)],
        compiler_params=_cp(2),
        name=name,
    )(a, w_stack, side_stack)


IN_EVEN_TM = 1408
IN_EVEN_TN = 768
IN_RET_TM = 1408
IN_RET_TN = 1024


def _in_even_kernel(a_ref, as_ref, w_ref, side_ref, z_ref, zs_ref, side_o_ref, wb_ref):
    @pl.when(pl.program_id(1) == 0)
    def _():
        wb_ref[...] = w_ref[0].astype(BF16)
        zs_ref[...] = jnp.dot(as_ref[...], wb_ref[...], preferred_element_type=F32)

    z_ref[...] = jnp.dot(a_ref[...], wb_ref[...], preferred_element_type=F32).astype(z_ref.dtype)
    side_o_ref[...] = side_ref[...].astype(BF16)


def _in_even(xn, w_stack, layer, side_stack):
    m, k = xn.shape
    n = w_stack.shape[2]
    tm, tn = IN_EVEN_TM, IN_EVEN_TN
    grid = (n // tn, m // tm)
    side_in, side_out, side_shape = _side_cast_specs(side_stack, layer, grid[1], grid[0] * grid[1])
    return pl.pallas_call(
        _in_even_kernel,
        grid=grid,
        in_specs=[pl.BlockSpec((tm, k), lambda j, i: (i, 0)),
                  pl.BlockSpec((N_SAMPLE, k), lambda j, i: (N_PROMPT // N_SAMPLE, 0)),
                  pl.BlockSpec((1, k, tn), lambda j, i: (layer, 0, j)),
                  side_in],
        out_specs=[pl.BlockSpec((tm, tn), lambda j, i: (i, j)),
                   pl.BlockSpec((N_SAMPLE, tn), lambda j, i: (0, j)),
                   side_out],
        out_shape=[jax.ShapeDtypeStruct((m, n), BF16),
                   jax.ShapeDtypeStruct((N_SAMPLE, n), F32),
                   side_shape],
        scratch_shapes=[pltpu.VMEM((k, tn), BF16)],
        compiler_params=_cp(2),
        name="in_even",
    )(xn, xn, w_stack, side_stack)


def _kv_tail_kernel(a_ref, w_ref, o_ref):
    o_ref[...] = jnp.dot(a_ref[...], w_ref[0].astype(BF16), preferred_element_type=F32)


def _kv_tail(xn, w_stack, layer):
    k = xn.shape[1]
    blks_per_seq = SEQ // WINDOW
    width = 2 * A_KV_WIDTH
    return pl.pallas_call(
        _kv_tail_kernel,
        grid=(BATCH,),
        in_specs=[pl.BlockSpec((WINDOW, k), lambda b: (blks_per_seq * (b + 1) - 1, 0)),
                  pl.BlockSpec((1, k, width), lambda b: (layer, 0, OFF_K // width))],
        out_specs=pl.BlockSpec((WINDOW, width), lambda b: (b, 0)),
        out_shape=jax.ShapeDtypeStruct((BATCH * WINDOW, width), F32),
        compiler_params=_cp(1),
        name="kv_tail",
    )(xn, w_stack)


def _mm_res_q_kernel(nk, a_ref, w_ref, x_ref, g_ref, wq_ref, xo_ref, q_ref):
    k = pl.program_id(1)

    @pl.when(k == 0)
    def _():
        xo_ref[...] = x_ref[...]

    xo_ref[...] += jnp.dot(a_ref[...], w_ref[0], preferred_element_type=F32)

    @pl.when(k == nk - 1)
    def _():
        xn = _rms(xo_ref[...], g_ref[...]).astype(BF16)
        q_ref[...] = jnp.dot(xn, wq_ref[0].astype(BF16), preferred_element_type=F32).astype(q_ref.dtype)


def _mm_res_q(a, w_stack, layer, x, g, wq_stack, wq_layer, tm, tk, x_blk0=0):
    m, kdim = a.shape
    d = w_stack.shape[2]
    nq = wq_stack.shape[2]
    nk = kdim // tk
    return pl.pallas_call(
        functools.partial(_mm_res_q_kernel, nk),
        grid=(m // tm, nk),
        in_specs=[pl.BlockSpec((tm, tk), lambda i, k: (i, k)),
                  pl.BlockSpec((1, tk, d), lambda i, k: (layer, k, 0)),
                  pl.BlockSpec((tm, d), lambda i, k: (x_blk0 + i, 0)),
                  pl.BlockSpec((1, d), lambda i, k: (0, 0)),
                  pl.BlockSpec((1, d, nq), lambda i, k: (wq_layer, 0, 0))],
        out_specs=[pl.BlockSpec((tm, d), lambda i, k: (i, 0)),
                   pl.BlockSpec((tm, nq), lambda i, k: (i, 0))],
        out_shape=[jax.ShapeDtypeStruct((m, d), F32),
                   jax.ShapeDtypeStruct((m, nq), BF16)],
        compiler_params=_cp(2),
        name="mm_res_q",
    )(a, w_stack, x, g.reshape(1, d), wq_stack)


def _embed_kernel(n_prompt_blks, xp_ref, xs_ref, g_ref, n_ref):
    i = pl.program_id(0)

    @pl.when(i < n_prompt_blks)
    def _():
        n_ref[...] = _rms(xp_ref[...], g_ref[...]).astype(n_ref.dtype)

    @pl.when(i >= n_prompt_blks)
    def _():
        n_ref[...] = _rms(xs_ref[...], g_ref[...]).astype(n_ref.dtype)


def _embed(xp, xs, g):
    d = xp.shape[1]
    tm = N_SAMPLE
    nb = N_PROMPT // tm
    return pl.pallas_call(
        functools.partial(_embed_kernel, nb),
        grid=(nb + 1,),
        in_specs=[pl.BlockSpec((tm, d), lambda i: (jnp.minimum(i, nb - 1), 0)),
                  pl.BlockSpec((tm, d), lambda i: (0, 0)),
                  pl.BlockSpec((1, d), lambda i: (0, 0))],
        out_specs=pl.BlockSpec((tm, d), lambda i: (i, 0)),
        out_shape=jax.ShapeDtypeStruct((N_ROWS, d), BF16),
        compiler_params=_cp(1),
        name="embed",
    )(xp, xs, g.reshape(1, d))


def _memkv_kernel(mem_ref, g_ref, wk_ref, wv_ref, mk_ref, mv_ref):
    h = _rms(mem_ref[...], g_ref[0]).astype(BF16)
    mk_ref[0] = jnp.dot(h, wk_ref[0].astype(BF16), preferred_element_type=F32)
    mv_ref[0] = jnp.dot(h, wv_ref[0].astype(BF16), preferred_element_type=F32)


def _memkv(mem, norm_mem, w_xk, w_xv):
    m = mem.shape[0]
    out = jax.ShapeDtypeStruct((DEPTH, m, X_WIDTH), F32)
    return pl.pallas_call(
        _memkv_kernel,
        grid=(DEPTH,),
        in_specs=[pl.BlockSpec((m, D_MODEL), lambda l: (0, 0)),
                  pl.BlockSpec((1, 1, D_MODEL), lambda l: (l, 0, 0)),
                  pl.BlockSpec((1, D_MODEL, X_WIDTH), lambda l: (l, 0, 0)),
                  pl.BlockSpec((1, D_MODEL, X_WIDTH), lambda l: (l, 0, 0))],
        out_specs=[pl.BlockSpec((1, m, X_WIDTH), lambda l: (l, 0, 0)),
                   pl.BlockSpec((1, m, X_WIDTH), lambda l: (l, 0, 0))],
        out_shape=[out, out],
        compiler_params=_cp(1),
        name="memkv",
    )(mem, norm_mem.reshape(DEPTH, 1, D_MODEL), w_xk, w_xv)


SLABS_PER_KV = A_HEADS // A_KV_HEADS // 2
MASK_DIST = 1e30 * 2.0 ** 8


def _expand_band(band, scale):
    x = band.astype(F32)
    if scale != 1.0:
        x = x * scale
    xs = pltpu.roll(x, 64, 1)
    lo = lax.broadcasted_iota(jnp.int32, x.shape, 1) < 64
    z = jnp.zeros_like(x)
    e0 = jnp.concatenate([jnp.where(lo, x, z), jnp.where(lo, z, xs)], axis=0).astype(BF16)
    e1 = jnp.concatenate([jnp.where(lo, xs, z), jnp.where(lo, z, x)], axis=0).astype(BF16)
    return e0, e1


def _band_bias(t, first_key):
    row = lax.broadcasted_iota(jnp.int32, (t, 2 * WINDOW), 0)
    col = lax.broadcasted_iota(jnp.int32, (t, 2 * WINDOW), 1)
    dist = row + WINDOW - col
    valid = (dist >= 0) & (dist < WINDOW) & (col >= first_key)
    return jnp.where(valid, -dist.astype(F32), -MASK_DIST)


def _swa(q_slab, kband, vband, nbias, sl_ref):
    t = nbias.shape[0]
    rows = SLABS_PER_KV * t
    kk = _expand_band(kband, A_HEAD_DIM ** -0.5)
    vv = _expand_band(vband, 1.0)
    nb = jnp.concatenate([nbias] * SLABS_PER_KV, axis=0)
    blk = lax.shift_right_logical(lax.broadcasted_iota(jnp.int32, (rows, 1), 0), t.bit_length() - 1)
    lo = lax.broadcasted_iota(jnp.int32, (rows, 128), 1) < 64
    def per_row(table_row, j, half):
        slab0 = SLABS_PER_KV * j
        col = jnp.full((rows, 1), sl_ref[table_row, 2 * (slab0 + SLABS_PER_KV - 1) + half], F32)
        for i in reversed(range(SLABS_PER_KV - 1)):
            col = jnp.where(blk == i, sl_ref[table_row, 2 * (slab0 + i) + half], col)
        return col

    scores = [_dot_nt(jnp.concatenate([q_slab(SLABS_PER_KV * j + i) for i in range(SLABS_PER_KV)], axis=0), kk[j])
              for j in range(A_KV_HEADS)]
    pps, scales = [], []
    for j, s in enumerate(scores):
        probs, invs = [], []
        for half in range(2):
            slope = per_row(0, j, half)
            sink = per_row(1, j, half)
            sh = s[:, half * 256:(half + 1) * 256] + slope * nb
            m = jnp.maximum(jnp.max(sh, axis=-1, keepdims=True), sink)
            pe = jnp.exp(sh - m)
            den = jnp.sum(pe, axis=-1, keepdims=True) + jnp.exp(sink - m)
            probs.append(pe)
            invs.append(1.0 / den)
        pps.append(jnp.concatenate(probs, axis=1).astype(BF16))
        scales.append(jnp.where(lo, invs[0], invs[1]))
    outs = []
    for j in range(A_KV_HEADS):
        o = jnp.dot(pps[j], vv[j], preferred_element_type=F32) * scales[j]
        outs += [o[i * t:(i + 1) * t] for i in range(SLABS_PER_KV)]
    return outs


def _spatial_gate_group(ws_ref, bs_ref, g, vn_g, rows):
    r = lax.broadcasted_iota(jnp.int32, (CHUNK, CHUNK), 0)
    c = lax.broadcasted_iota(jnp.int32, (CHUNK, CHUNK), 1)
    w = jnp.where(r >= c, ws_ref[g], 0.0).astype(BF16)
    return jnp.dot(w[:rows], vn_g, preferred_element_type=F32) + bs_ref[g][:rows]


EVEN_STEP_ROWS = 512
EVEN_SUBBLKS = EVEN_STEP_ROWS // WINDOW
EVEN_STEPS = N_PROMPT // EVEN_STEP_ROWS


def _even_fused_kernel(sl_ref, z_ref, kvprev_ref, ws_ref, bs_ref, lng_ref, lnb_ref,
                       w_ref, x_ref, g_ref, wq_ref, xo_ref, q_ref, mixed_ref):
    s_id = pl.program_id(0)
    slot = s_id % 2

    @pl.when(s_id == 0)
    def _():
        mixed_ref[1] = jnp.zeros(mixed_ref.shape[1:], mixed_ref.dtype)

    kc = D_MODEL // EVEN_SUBBLKS
    xo_ref[...] = x_ref[...]
    blk0 = jnp.minimum(s_id, EVEN_STEPS - 1) * EVEN_SUBBLKS
    for b in range(EVEN_SUBBLKS):
        xo_ref[...] += jnp.dot(mixed_ref[1 - slot, :, b * kc:(b + 1) * kc], w_ref[0, b * kc:(b + 1) * kc, :],
                               preferred_element_type=F32)
        rows = slice(b * WINDOW, (b + 1) * WINDOW)
        c = (blk0 + b) % (SEQ // WINDOW)
        kv_cur = z_ref[rows, OFF_K:OFF_K + 2 * A_KV_WIDTH]
        kv_prev = kvprev_ref[...] if b == 0 else z_ref[(b - 1) * WINDOW:b * WINDOW, OFF_K:OFF_K + 2 * A_KV_WIDTH]
        band = jnp.concatenate([kv_prev, kv_cur], axis=0)
        nbias = _band_bias(WINDOW, jnp.maximum(WINDOW - c * WINDOW, 0))

        def q_slab(p, rows=rows):
            return z_ref[rows, OFF_Q + p * 128:OFF_Q + (p + 1) * 128]

        for p, att in enumerate(_swa(q_slab, band[:, :A_KV_WIDTH], band[:, A_KV_WIDTH:], nbias, sl_ref)):
            ga = z_ref[rows, OFF_GA + p * 128:OFF_GA + (p + 1) * 128].astype(F32)
            mixed_ref[slot, rows, p * 128:(p + 1) * 128] = (_silu(ga) * att).astype(mixed_ref.dtype)

        vn = _standardize(z_ref[rows, OFF_VB:OFF_VB + B_WIDTH].astype(F32)) * lng_ref[...] + lnb_ref[...]
        for g in range(B_GROUPS):
            lo, hi = g * B_GROUP_DIM, (g + 1) * B_GROUP_DIM
            sg = _spatial_gate_group(ws_ref, bs_ref, g, vn[:, lo:hi].astype(BF16), CHUNK)
            u = z_ref[rows, OFF_U + lo:OFF_U + hi].astype(F32)
            gb = z_ref[rows, OFF_GB + lo:OFF_GB + hi].astype(F32)
            mixed_ref[slot, rows, A_WIDTH + lo:A_WIDTH + hi] = (_silu(gb) * (u * sg)).astype(mixed_ref.dtype)

    q_ref[...] = jnp.dot(_rms(xo_ref[...], g_ref[...]).astype(BF16), wq_ref[0],
                         preferred_element_type=F32).astype(q_ref.dtype)


def _even_fused(z, sl, w_s, b_s, lng, lnb, w_out_b, layer_i, x, g, wq_b, layer):
    ns = EVEN_STEPS
    r = EVEN_STEP_ROWS
    kv_blk = OFF_K // (2 * A_KV_WIDTH)

    def cur(s):
        return jnp.minimum(s, ns - 1)

    def prev(s):
        return jnp.maximum(s - 1, 0)

    once = pl.Buffered(1)
    return pl.pallas_call(
        _even_fused_kernel,
        grid=(ns + 1,),
        in_specs=[pl.BlockSpec(memory_space=pltpu.SMEM),
                  pl.BlockSpec((r, EVEN_IN), lambda s: (cur(s), 0)),
                  pl.BlockSpec((WINDOW, 2 * A_KV_WIDTH),
                               lambda s: (jnp.maximum(cur(s) * EVEN_SUBBLKS - 1, 0), kv_blk)),
                  pl.BlockSpec((B_GROUPS, CHUNK, CHUNK), lambda s: (0, 0, 0)),
                  pl.BlockSpec((B_GROUPS, CHUNK, 1), lambda s: (0, 0, 0)),
                  pl.BlockSpec((1, B_WIDTH), lambda s: (0, 0)),
                  pl.BlockSpec((1, B_WIDTH), lambda s: (0, 0)),
                  pl.BlockSpec((1, D_MODEL, D_MODEL), lambda s: (layer_i, 0, 0), pipeline_mode=once),
                  pl.BlockSpec((r, D_MODEL), lambda s: (prev(s), 0)),
                  pl.BlockSpec((1, D_MODEL), lambda s: (0, 0)),
                  pl.BlockSpec((1, D_MODEL, X_WIDTH), lambda s: (layer, 0, 0), pipeline_mode=once)],
        out_specs=[pl.BlockSpec((r, D_MODEL), lambda s: (prev(s), 0)),
                   pl.BlockSpec((r, X_WIDTH), lambda s: (prev(s), 0))],
        out_shape=[jax.ShapeDtypeStruct((N_PROMPT, D_MODEL), F32),
                   jax.ShapeDtypeStruct((N_PROMPT, X_WIDTH), BF16)],
        scratch_shapes=[pltpu.VMEM((2, r, D_MODEL), BF16)],
        compiler_params=_cp(1, FUSED_VMEM_LIMIT),
        name="even_fused",
    )(sl, z, z, w_s, b_s, lng, lnb, w_out_b, x, g.reshape(1, D_MODEL), wq_b)


SEQ_PER_STEP = 2
ROWS_PER_STEP = SEQ_PER_STEP * DEC_SEQ
assert DEC_SEQ & (DEC_SEQ - 1) == 0


def _seq_of(row_idx):
    return lax.shift_right_logical(row_idx, DEC_SEQ.bit_length() - 1)


def _even_sample_kernel(sl_ref, z_ref, ck_ref, cv_ref, ws_ref, bs_ref, lng_ref, lnb_ref,
                        o_ref, wk_ref, wv_ref, cvo_ref):
    nbias = _band_bias(ROWS_PER_STEP, 0)
    pad_rows = jnp.zeros((ROWS_PER_STEP - DEC_SEQ, EVEN_IN), F32)
    attn = [[None] * SEQ_PER_STEP for _ in range(A_HEADS // 2)]
    gate = [[None] * SEQ_PER_STEP for _ in range(B_GROUPS)]
    for r in range(SEQ_PER_STEP):
        r0, r1 = r * DEC_SEQ, (r + 1) * DEC_SEQ
        zr = z_ref[r0:r1, :]
        k_new = zr[:, OFF_K:OFF_K + A_KV_WIDTH]
        v_new = zr[:, OFF_V:OFF_V + A_KV_WIDTH]
        ck = ck_ref[r]
        cv = cv_ref[r]
        wk_ref[r, 0:WINDOW - DEC_SEQ, :] = ck[DEC_SEQ:, :]
        wk_ref[r, WINDOW - DEC_SEQ:, :] = k_new
        wv_ref[r, 0:WINDOW - DEC_SEQ, :] = cv[DEC_SEQ:, :]
        wv_ref[r, WINDOW - DEC_SEQ:, :] = v_new
        tail = jnp.zeros((WINDOW - DEC_SEQ, A_KV_WIDTH), F32)
        kband = jnp.concatenate([ck, k_new, tail], axis=0)
        vband = jnp.concatenate([cv, v_new, tail], axis=0)
        zq = jnp.concatenate([zr, pad_rows], axis=0)

        def q_slab(p, zq=zq):
            return zq[:, OFF_Q + p * 128:OFF_Q + (p + 1) * 128].astype(BF16)

        for p, a in enumerate(_swa(q_slab, kband, vband, nbias, sl_ref)):
            ga = zr[:, OFF_GA + p * 128:OFF_GA + (p + 1) * 128]
            attn[p][r] = _silu(ga) * a[:DEC_SEQ]

        vn = _standardize(zr[:, OFF_VB:OFF_VB + B_WIDTH]) * lng_ref[...] + lnb_ref[...]
        cvo_ref[r0:r1, :] = vn
        vn_pad = jnp.concatenate([vn, jnp.zeros((CHUNK - DEC_SEQ, B_WIDTH), F32)], axis=0).astype(BF16)
        for g in range(B_GROUPS):
            lo, hi = g * B_GROUP_DIM, (g + 1) * B_GROUP_DIM
            mixed = _spatial_gate_group(ws_ref, bs_ref, g, vn_pad[:, lo:hi], ROWS_PER_STEP)[:DEC_SEQ]
            u = zr[:, OFF_U + lo:OFF_U + hi]
            gb = zr[:, OFF_GB + lo:OFF_GB + hi]
            gate[g][r] = _silu(gb) * (u * mixed)

    for p in range(A_HEADS // 2):
        o_ref[:, p * 128:(p + 1) * 128] = jnp.concatenate(attn[p], axis=0).astype(o_ref.dtype)
    for g in range(B_GROUPS):
        lo, hi = A_WIDTH + g * B_GROUP_DIM, A_WIDTH + (g + 1) * B_GROUP_DIM
        o_ref[:, lo:hi] = jnp.concatenate(gate[g], axis=0).astype(o_ref.dtype)


def _even_sample(zs, ck, cv, sl, w_s, b_s, lng, lnb):
    n_steps = DEC_BATCH // SEQ_PER_STEP
    win = jax.ShapeDtypeStruct((DEC_BATCH, WINDOW, A_KV_WIDTH), F32)
    return pl.pallas_call(
        _even_sample_kernel,
        grid=(n_steps,),
        in_specs=[pl.BlockSpec(memory_space=pltpu.SMEM),
                  pl.BlockSpec((ROWS_PER_STEP, EVEN_IN), lambda i: (i, 0)),
                  pl.BlockSpec((SEQ_PER_STEP, WINDOW, A_KV_WIDTH), lambda i: (i, 0, 0)),
                  pl.BlockSpec((SEQ_PER_STEP, WINDOW, A_KV_WIDTH), lambda i: (i, 0, 0)),
                  pl.BlockSpec((B_GROUPS, CHUNK, CHUNK), lambda i: (0, 0, 0)),
                  pl.BlockSpec((B_GROUPS, CHUNK, 1), lambda i: (0, 0, 0)),
                  pl.BlockSpec((1, B_WIDTH), lambda i: (0, 0)),
                  pl.BlockSpec((1, B_WIDTH), lambda i: (0, 0))],
        out_specs=[pl.BlockSpec((ROWS_PER_STEP, D_MODEL), lambda i: (i, 0)),
                   pl.BlockSpec((SEQ_PER_STEP, WINDOW, A_KV_WIDTH), lambda i: (i, 0, 0)),
                   pl.BlockSpec((SEQ_PER_STEP, WINDOW, A_KV_WIDTH), lambda i: (i, 0, 0)),
                   pl.BlockSpec((ROWS_PER_STEP, B_WIDTH), lambda i: (i, 0))],
        out_shape=[jax.ShapeDtypeStruct((N_SAMPLE, D_MODEL), BF16), win, win,
                   jax.ShapeDtypeStruct((N_SAMPLE, B_WIDTH), F32)],
        compiler_params=_cp(1),
        name="even_sample",
    )(sl, zs, ck, cv, w_s, b_s, lng, lnb)


RET_ROWS = 256
RET_BLKS_PER_SEQ = SEQ // RET_ROWS
RET_PROMPT_BLKS = N_PROMPT // RET_ROWS


def _gated_groupnorm(o, g):
    return (_silu(g.astype(F32)) * _standardize(o)).astype(BF16)


RET_LOOKAHEAD = 1


def _ret_fused_kernel(lg_ref, z_ref, w_ref, x_ref, g_ref, wq_ref, xo_ref, q_ref, so_ref, dm_ref, dc_ref):
    s_id = pl.program_id(0)
    t = RET_ROWS
    k_scale = R_KEY_DIM ** -0.5
    c = s_id % RET_BLKS_PER_SEQ

    @pl.when(s_id == 0)
    def _():
        row = lax.broadcasted_iota(jnp.int32, (t, t), 0)
        col = lax.broadcasted_iota(jnp.int32, (t, t), 1)
        rel = (row - col).astype(F32)
        ti = lax.broadcasted_iota(jnp.int32, (t, 1), 0).astype(F32)
        for h in range(R_HEADS):
            lg = lg_ref[h]
            dm_ref[h] = jnp.where(rel >= 0, jnp.exp(jnp.maximum(rel, 0.0) * lg), 0.0) * k_scale
            dc_ref[h, 0] = jnp.exp((ti + 1.0) * lg)
            dc_ref[h, 1] = jnp.exp((t - 1.0 - ti) * lg) * k_scale

    @pl.when(c == 0)
    def _():
        so_ref[...] = jnp.zeros_like(so_ref)

    def head_matmuls(h):
        lg = lg_ref[h]
        q_decay = dc_ref[h, 0]
        k_decay = dc_ref[h, 1]
        chunk_decay = jnp.exp(jnp.full((1, R_VAL_DIM), float(t), F32) * lg)
        q = z_ref[:, h * R_KEY_DIM:(h + 1) * R_KEY_DIM]
        k = z_ref[:, R_K_WIDTH + h * R_KEY_DIM:R_K_WIDTH + (h + 1) * R_KEY_DIM]
        v = z_ref[:, 2 * R_K_WIDTH + h * R_VAL_DIM:2 * R_K_WIDTH + (h + 1) * R_VAL_DIM]
        st = so_ref[0, h]
        scores = _dot_nt(q, k) * dm_ref[h]
        inner = jnp.dot(scores.astype(BF16), v, preferred_element_type=F32)
        cross = jnp.dot(q, st.astype(BF16), preferred_element_type=F32) * q_decay
        kd_t = (k.astype(F32) * k_decay).T.astype(BF16)
        so_ref[0, h] = chunk_decay * st + jnp.dot(kd_t, v, preferred_element_type=F32)
        return inner + cross

    x1 = x_ref[...]
    ahead = [head_matmuls(h) for h in range(RET_LOOKAHEAD)]
    for h in range(R_HEADS):
        if h + RET_LOOKAHEAD < R_HEADS:
            ahead.append(head_matmuls(h + RET_LOOKAHEAD))
        raw = ahead.pop(0)
        g = z_ref[:, 2 * R_K_WIDTH + R_V_WIDTH + h * R_VAL_DIM:2 * R_K_WIDTH + R_V_WIDTH + (h + 1) * R_VAL_DIM]
        x1 = x1 + jnp.dot(_gated_groupnorm(raw, g), w_ref[0, h * R_VAL_DIM:(h + 1) * R_VAL_DIM, :],
                          preferred_element_type=F32)
    xo_ref[...] = x1
    q_ref[...] = jnp.dot(_rms(x1, g_ref[...]).astype(BF16), wq_ref[0],
                         preferred_element_type=F32).astype(q_ref.dtype)


def _ret_fused(z, log_g, w_out_b, layer_i, x, g, wq_b, layer):
    nb = RET_PROMPT_BLKS
    r = RET_ROWS
    once = pl.Buffered(1)
    return pl.pallas_call(
        _ret_fused_kernel,
        grid=(nb,),
        in_specs=[pl.BlockSpec(memory_space=pltpu.SMEM),
                  pl.BlockSpec((r, RET_IN), lambda s: (s, 0)),
                  pl.BlockSpec((1, R_V_WIDTH, D_MODEL), lambda s: (layer_i, 0, 0), pipeline_mode=once),
                  pl.BlockSpec((r, D_MODEL), lambda s: (s, 0)),
                  pl.BlockSpec((1, D_MODEL), lambda s: (0, 0)),
                  pl.BlockSpec((1, D_MODEL, X_WIDTH), lambda s: (layer, 0, 0), pipeline_mode=once)],
        out_specs=[pl.BlockSpec((r, D_MODEL), lambda s: (s, 0)),
                   pl.BlockSpec((r, X_WIDTH), lambda s: (s, 0)),
                   pl.BlockSpec((1, R_HEADS, R_KEY_DIM, R_VAL_DIM),
                                lambda s: (s // RET_BLKS_PER_SEQ, 0, 0, 0))],
        out_shape=[jax.ShapeDtypeStruct((N_PROMPT, D_MODEL), F32),
                   jax.ShapeDtypeStruct((N_PROMPT, X_WIDTH), BF16),
                   jax.ShapeDtypeStruct((BATCH, R_HEADS, R_KEY_DIM, R_VAL_DIM), F32)],
        scratch_shapes=[pltpu.VMEM((R_HEADS, r, r), F32),
                        pltpu.VMEM((R_HEADS, 2, r, 1), F32)],
        compiler_params=_cp(1, FUSED_VMEM_LIMIT),
        name="ret_fused",
    )(log_g, z, w_out_b, x, g.reshape(1, D_MODEL), wq_b)


RET_SAMPLE_SEQS = 8
RET_SAMPLE_ROWS = RET_SAMPLE_SEQS * DEC_SEQ


def _ret_sample_kernel(lg_ref, q_ref, k_ref, v_ref, g_ref, s_ref, *rest):
    o_ref, so_ref = rest[-2:]
    h = pl.program_id(1)
    lg = lg_ref[h]
    n = RET_SAMPLE_ROWS
    t = DEC_SEQ
    row = lax.broadcasted_iota(jnp.int32, (n, n), 0)
    col = lax.broadcasted_iota(jnp.int32, (n, n), 1)
    rel = (row - col).astype(F32)
    same = _seq_of(row) == _seq_of(col)
    k_scale = R_KEY_DIM ** -0.5
    dmat = jnp.where(same & (rel >= 0), jnp.exp(jnp.maximum(rel, 0.0) * lg), 0.0) * k_scale
    ri = lax.broadcasted_iota(jnp.int32, (n, 1), 0)
    ti = (ri & (t - 1)).astype(F32)
    q_decay = jnp.exp((ti + 1.0) * lg)
    k_decay = jnp.exp((t - 1.0 - ti) * lg) * k_scale
    step_decay = jnp.exp(jnp.full((1, R_VAL_DIM), float(t), F32) * lg)

    q = q_ref[...]
    k = k_ref[...]
    v = v_ref[...]
    scores = _dot_nt(q, k) * dmat
    o = jnp.dot(scores.astype(BF16), v, preferred_element_type=F32)
    kd = k.astype(F32) * k_decay
    k_pad = jnp.zeros((R_CHUNK - n, R_KEY_DIM), F32)
    v_pad = jnp.concatenate([v.astype(F32), jnp.zeros((R_CHUNK - n, R_VAL_DIM), F32)], axis=0).astype(BF16)
    for r in range(RET_SAMPLE_SEQS):
        mine = _seq_of(ri) == r
        st = s_ref[0, r, 0]
        cross = jnp.dot(q, st.astype(BF16), preferred_element_type=F32) * q_decay
        o = o + jnp.where(mine, cross, 0.0)
        kd_t = jnp.concatenate([jnp.where(mine, kd, 0.0), k_pad], axis=0).T.astype(BF16)
        so_ref[0, r, 0] = step_decay * st + jnp.dot(kd_t, v_pad, preferred_element_type=F32)
    o_ref[...] = _gated_groupnorm(o, g_ref[...])


def _ret_sample(z, state_all, idx, log_g, new_states):
    n_steps = DEC_BATCH // RET_SAMPLE_SEQS
    rb0 = N_PROMPT // RET_SAMPLE_ROWS
    kcol = R_K_WIDTH // R_KEY_DIM
    vcol = 2 * R_K_WIDTH // R_VAL_DIM
    gcol = vcol + R_HEADS
    st_spec = pl.BlockSpec((1, RET_SAMPLE_SEQS, 1, R_KEY_DIM, R_VAL_DIM), lambda i, h: (idx, i, h, 0, 0))
    in_specs = [pl.BlockSpec(memory_space=pltpu.SMEM),
                pl.BlockSpec((RET_SAMPLE_ROWS, R_KEY_DIM), lambda i, h: (rb0 + i, h)),
                pl.BlockSpec((RET_SAMPLE_ROWS, R_KEY_DIM), lambda i, h: (rb0 + i, kcol + h)),
                pl.BlockSpec((RET_SAMPLE_ROWS, R_VAL_DIM), lambda i, h: (rb0 + i, vcol + h)),
                pl.BlockSpec((RET_SAMPLE_ROWS, R_VAL_DIM), lambda i, h: (rb0 + i, gcol + h)),
                st_spec]
    args = [log_g, z, z, z, z, state_all]
    aliases = {}
    if new_states is not None:
        in_specs.append(pl.BlockSpec(memory_space=pl.ANY))
        args.append(new_states)
        aliases = {len(args) - 1: 1}
    return pl.pallas_call(
        _ret_sample_kernel,
        grid=(n_steps, R_HEADS),
        in_specs=in_specs,
        out_specs=[pl.BlockSpec((RET_SAMPLE_ROWS, R_VAL_DIM), lambda i, h: (i, h)), st_spec],
        out_shape=[jax.ShapeDtypeStruct((N_SAMPLE, R_V_WIDTH), BF16),
                   jax.ShapeDtypeStruct(state_all.shape, F32)],
        input_output_aliases=aliases,
        compiler_params=_cp(2),
        name="ret_sample",
    )(*args)


XQ_ROWS = 256
XQ_PROMPT_BLKS = N_PROMPT // XQ_ROWS
XQ_BLKS_PER_SEQ = SEQ // XQ_ROWS


def _xattn_heads(q, mk_heads, mv_heads):
    pairs = [(i, h) for i in range(len(mk_heads)) for h in range(X_HEADS)]
    scores = [_dot_nt(q[:, h * X_HEAD_DIM:(h + 1) * X_HEAD_DIM], mk_heads[i](h)) * (X_HEAD_DIM ** -0.5)
              for i, h in pairs]
    probs, invs = [], []
    for s in scores:
        p = jnp.exp(s - jnp.max(s, axis=-1, keepdims=True))
        probs.append(p.astype(BF16))
        invs.append(1.0 / jnp.sum(p, axis=-1, keepdims=True))
    outs = [jnp.dot(p, mv_heads[i](h), preferred_element_type=F32) * inv
            for (i, h), p, inv in zip(pairs, probs, invs)]
    return [outs[i * X_HEADS:(i + 1) * X_HEADS] for i in range(len(mk_heads))]


def _xattn_proj_kernel(final, q_ref, mk_ref, mv_ref, os_ref, w_ref, xp_ref, xs_ref, g_ref, out_a, out_b):
    s_id = pl.program_id(0)

    def project(o, x_ref):
        return x_ref[...] + jnp.dot(o, w_ref[0], preferred_element_type=F32)

    @pl.when(s_id < XQ_PROMPT_BLKS)
    def _():
        def head_of(ref):
            return lambda h: ref[0, :, h * X_HEAD_DIM:(h + 1) * X_HEAD_DIM].astype(BF16)

        outs, = _xattn_heads(q_ref[...], [head_of(mk_ref)], [head_of(mv_ref)])
        x2 = project(jnp.concatenate(outs, axis=1).astype(BF16), xp_ref)
        if final:
            out_a[...] = _rms(x2, g_ref[...])
        else:
            out_a[...] = x2
            out_b[...] = _rms(x2, g_ref[...]).astype(out_b.dtype)

    @pl.when(s_id >= XQ_PROMPT_BLKS)
    def _():
        x2 = project(os_ref[...], xs_ref)
        if final:
            out_b[...] = _rms(x2, g_ref[...])
        else:
            out_a[...] = x2
            out_b[...] = _rms(x2, g_ref[...]).astype(out_b.dtype)


def _xattn_proj(q, mk_all, mv_all, layer, o_sample, w_stack, x_prompt, x_sample, g, final):
    nb = XQ_PROMPT_BLKS
    d = w_stack.shape[2]
    xs_blk = (x_sample.shape[0] - N_SAMPLE) // N_SAMPLE

    def rb(s):
        return jnp.minimum(s, nb - 1)

    mem_spec = pl.BlockSpec((1, MEM_LEN, X_WIDTH), lambda s: (layer, rb(s) // XQ_BLKS_PER_SEQ, 0))
    if final:
        out_specs = [pl.BlockSpec((XQ_ROWS, d), lambda s: (rb(s), 0)),
                     pl.BlockSpec((N_SAMPLE, d), lambda s: (0, 0))]
        out_shape = [jax.ShapeDtypeStruct((N_PROMPT, d), F32), jax.ShapeDtypeStruct((N_SAMPLE, d), F32)]
    else:
        out_specs = [pl.BlockSpec((XQ_ROWS, d), lambda s: (s, 0)),
                     pl.BlockSpec((XQ_ROWS, d), lambda s: (s, 0))]
        out_shape = [jax.ShapeDtypeStruct((N_ROWS, d), F32), jax.ShapeDtypeStruct((N_ROWS, d), BF16)]
    return pl.pallas_call(
        functools.partial(_xattn_proj_kernel, final),
        grid=(nb + 1,),
        in_specs=[pl.BlockSpec((XQ_ROWS, X_WIDTH), lambda s: (rb(s), 0)),
                  mem_spec, mem_spec,
                  pl.BlockSpec((N_SAMPLE, X_WIDTH), lambda s: (0, 0)),
                  pl.BlockSpec((1, X_WIDTH, d), lambda s: (layer, 0, 0)),
                  pl.BlockSpec((XQ_ROWS, d), lambda s: (rb(s), 0)),
                  pl.BlockSpec((N_SAMPLE, d), lambda s: (xs_blk, 0)),
                  pl.BlockSpec((1, d), lambda s: (0, 0))],
        out_specs=out_specs,
        out_shape=out_shape,
        compiler_params=_cp(1),
        name="xattn_proj",
    )(q, mk_all, mv_all, o_sample, w_stack, x_prompt, x_sample, g.reshape(1, d))


XS_SEQS = 4
XS_ROWS = XS_SEQS * DEC_SEQ


def _xattn_sample_kernel(q_ref, ck_ref, cv_ref, o_ref):
    q = q_ref[...]
    ri = lax.broadcasted_iota(jnp.int32, (XS_ROWS, 1), 0)
    def head_of(ref, r):
        return lambda h: ref[0, r, pl.ds(h, MEM_LEN, stride=X_HEADS), :].astype(BF16)

    per_seq = _xattn_heads(q, [head_of(ck_ref, r) for r in range(XS_SEQS)],
                           [head_of(cv_ref, r) for r in range(XS_SEQS)])
    acc = [jnp.zeros((XS_ROWS, X_HEAD_DIM), F32) for _ in range(X_HEADS)]
    for r, outs in enumerate(per_seq):
        mine = _seq_of(ri) == r
        acc = [a + jnp.where(mine, o, 0.0) for a, o in zip(acc, outs)]
    for h, a in enumerate(acc):
        o_ref[:, h * X_HEAD_DIM:(h + 1) * X_HEAD_DIM] = a.astype(o_ref.dtype)


def _xattn_sample(q, ck_all, cv_all, layer):
    n_steps = DEC_BATCH // XS_SEQS
    rb0 = (q.shape[0] - N_SAMPLE) // XS_ROWS
    mem_spec = pl.BlockSpec((1, XS_SEQS, MEM_LEN * X_HEADS, X_HEAD_DIM), lambda i: (layer, i, 0, 0))
    return pl.pallas_call(
        _xattn_sample_kernel,
        grid=(n_steps,),
        in_specs=[pl.BlockSpec((XS_ROWS, X_WIDTH), lambda i: (rb0 + i, 0)), mem_spec, mem_spec],
        out_specs=pl.BlockSpec((XS_ROWS, X_WIDTH), lambda i: (i, 0)),
        out_shape=jax.ShapeDtypeStruct((N_SAMPLE, X_WIDTH), BF16),
        compiler_params=_cp(1),
        name="xattn_sample",
    )(q, ck_all, cv_all)


def kernel(x_prompt, x_sample, cache_win_k, cache_win_v, state_ret, cache_mem_k, cache_mem_v, mem_prompt,
           norm_mix, w_in_even, attn_sinks, w_spatial, b_spatial, norm_v_g, norm_v_b, w_out_even,
           w_in_ret, w_out_ret, norm_cross, norm_mem, w_xq, w_xk, w_xv, w_xo, norm_final):
    w_xo_b = w_xo.astype(BF16)
    w_xq_b = w_xq.astype(BF16)

    slopes = jnp.exp2(-8.0 * (jnp.arange(A_HEADS, dtype=F32) + 1.0) / A_HEADS)
    log_g = jnp.log1p(-jnp.exp2(-5.0 - jnp.arange(R_HEADS, dtype=F32)))

    def mem_rows(cache):
        return cache.reshape(DEPTH, DEC_BATCH, MEM_LEN * X_HEADS, X_HEAD_DIM)

    mk_all, mv_all = _memkv(mem_prompt.reshape(BATCH * MEM_LEN, D_MODEL), norm_mem, w_xk, w_xv)

    xp_src = x_prompt.reshape(N_PROMPT, D_MODEL)
    xs_src = x_sample.reshape(N_SAMPLE, D_MODEL)
    xn = _embed(xp_src, xs_src, norm_mix[0])
    wkp, wvp, rsp, wks, wvs, cvs = [], [], [], [], [], []
    rss = None
    for layer in range(DEPTH):
        i = layer // 2
        xs_blk0 = (xs_src.shape[0] - N_SAMPLE) // N_SAMPLE
        if layer % 2 == 0:
            sl = jnp.stack([slopes, attn_sinks[i].astype(F32)])
            b_s = b_spatial[i].reshape(B_GROUPS, CHUNK, 1)
            lng = norm_v_g[i].reshape(1, B_WIDTH)
            lnb = norm_v_b[i].reshape(1, B_WIDTH)
            z, zs, w_out_b = _in_even(xn, w_in_even, i, w_out_even)
            kv_tail = _kv_tail(xn, w_in_even, i)
            wkp.append(kv_tail[:, :A_KV_WIDTH].reshape(BATCH, WINDOW, A_KV_HEADS, A_HEAD_DIM))
            wvp.append(kv_tail[:, A_KV_WIDTH:].reshape(BATCH, WINDOW, A_KV_HEADS, A_HEAD_DIM))
            ck = cache_win_k[i].reshape(DEC_BATCH, WINDOW, A_KV_WIDTH)
            cv = cache_win_v[i].reshape(DEC_BATCH, WINDOW, A_KV_WIDTH)
            mixed_s, wk_new, wv_new, cv_new = _even_sample(zs, ck, cv, sl, w_spatial[i], b_s, lng, lnb)
            wks.append(wk_new.reshape(DEC_BATCH, WINDOW, A_KV_HEADS, A_HEAD_DIM))
            wvs.append(wv_new.reshape(DEC_BATCH, WINDOW, A_KV_HEADS, A_HEAD_DIM))
            cvs.append(cv_new.reshape(DEC_BATCH, DEC_SEQ, B_WIDTH))
            x_p, q = _even_fused(z, sl, w_spatial[i], b_s, lng, lnb, w_out_b, 0, xp_src, norm_cross[layer],
                                 w_xq_b, layer)
            x_s, q_s = _mm_res_q(mixed_s, w_out_b, 0, xs_src, norm_cross[layer], w_xq_b, layer,
                                 N_SAMPLE, SAMPLE_PROJ_TK, x_blk0=xs_blk0)
        else:
            z, w_out_b = _mm_wcast(xn, w_in_ret, i, BF16, IN_RET_TM, IN_RET_TN, w_out_ret, i, "in_ret")
            o_s, rss = _ret_sample(z, state_ret, i, log_g, rss)
            x_p, q, st_p = _ret_fused(z, log_g, w_out_b, 0, xp_src, norm_cross[layer], w_xq_b, layer)
            rsp.append(st_p)
            x_s, q_s = _mm_res_q(o_s, w_out_b, 0, xs_src, norm_cross[layer], w_xq_b, layer,
                                 N_SAMPLE, SAMPLE_PROJ_TK, x_blk0=xs_blk0)
        o_s = _xattn_sample(q_s, mem_rows(cache_mem_k), mem_rows(cache_mem_v), layer)
        if layer + 1 < DEPTH:
            x, xn = _xattn_proj(q, mk_all, mv_all, layer, o_s, w_xo_b, x_p, x_s, norm_mix[layer + 1], False)
            xp_src = xs_src = x
        else:
            y_prompt, y_sample = _xattn_proj(q, mk_all, mv_all, layer, o_s, w_xo_b, x_p, x_s, norm_final, True)

    y_prompt = y_prompt.reshape(BATCH, SEQ, D_MODEL)
    y_sample = y_sample.reshape(DEC_BATCH, DEC_SEQ, D_MODEL)
    mem_shape = (DEPTH, BATCH, MEM_LEN, X_HEADS, X_HEAD_DIM)
    return (y_prompt, y_sample,
            jnp.stack(wkp), jnp.stack(wvp), jnp.stack(rsp),
            mk_all.reshape(mem_shape), mv_all.reshape(mem_shape),
            jnp.stack(wks), jnp.stack(wvs), rss, jnp.stack(cvs))
```

```python
import functools

import jax
import jax.numpy as jnp
from jax import lax
from jax.experimental import pallas as pl
from jax.experimental.pallas import tpu as pltpu

F32 = jnp.float32
BF16 = jnp.bfloat16

D_MODEL = 2048
BATCH = 2
SEQ = 4096
DEPTH = 4
DEC_BATCH = 32
DEC_SEQ = 8
A_HEADS = 16
A_KV_HEADS = 2
A_HEAD_DIM = 64
A_WIDTH = 1024
A_KV_WIDTH = 128
WINDOW = 128
CHUNK = 128
B_GROUPS = 4
B_WIDTH = 1024
B_GROUP_DIM = 256
EVEN_IN = 5376
R_HEADS = 8
R_KEY_DIM = 256
R_VAL_DIM = 512
R_K_WIDTH = 2048
R_V_WIDTH = 4096
R_CHUNK = 128
RET_IN = 12288
MEM_LEN = 256
X_HEADS = 4
X_HEAD_DIM = 128
X_WIDTH = 512
EPS = 1e-6

N_PROMPT = BATCH * SEQ
N_SAMPLE = DEC_BATCH * DEC_SEQ
N_ROWS = N_PROMPT + N_SAMPLE

OFF_Q, OFF_K, OFF_V, OFF_GA, OFF_U, OFF_VB, OFF_GB = 0, 1024, 1152, 1280, 2304, 3328, 4352

MIB = 1024 * 1024
V7X_VMEM_BYTES = 64 * MIB
VMEM_LIMIT = V7X_VMEM_BYTES - 12 * MIB
FUSED_VMEM_LIMIT = V7X_VMEM_BYTES - 8 * MIB
SAMPLE_PROJ_TK = 2048


def _cp(n_axes, vmem_limit=VMEM_LIMIT):
    return pltpu.CompilerParams(dimension_semantics=("arbitrary",) * n_axes,
                                vmem_limit_bytes=vmem_limit)


def _silu(x):
    return x * (1.0 / (1.0 + jnp.exp(-x)))


def _rms(x, g):
    ms = jnp.mean(x * x, axis=-1, keepdims=True)
    return x * lax.rsqrt(ms + EPS) * g


def _standardize(x):
    xc = x - jnp.mean(x, axis=-1, keepdims=True)
    return xc * lax.rsqrt(jnp.mean(xc * xc, axis=-1, keepdims=True) + EPS)


def _dot_nt(a, b):
    return lax.dot_general(a, b, (((1,), (1,)), ((), ())), preferred_element_type=F32)


SIDE_CAST_ROWS = 64


def _side_cast_specs(side_stack, side_layer, n_inner, n_steps):
    _, k, d = side_stack.shape
    n_slabs = k // SIDE_CAST_ROWS
    assert n_slabs <= n_steps

    def slab(j, i):
        return jnp.minimum(j * n_inner + i, n_slabs - 1)

    return (pl.BlockSpec((1, SIDE_CAST_ROWS, d), lambda j, i: (side_layer, slab(j, i), 0)),
            pl.BlockSpec((1, SIDE_CAST_ROWS, d), lambda j, i: (0, slab(j, i), 0)),
            jax.ShapeDtypeStruct((1, k, d), BF16))


def _mm_wcast_kernel(a_ref, w_ref, side_ref, o_ref, side_o_ref, wb_ref):
    @pl.when(pl.program_id(1) == 0)
    def _():
        wb_ref[...] = w_ref[0].astype(BF16)

    o_ref[...] = jnp.dot(a_ref[...], wb_ref[...], preferred_element_type=F32).astype(o_ref.dtype)
    side_o_ref[...] = side_ref[...].astype(BF16)


def _mm_wcast(a, w_stack, layer, out_dtype, tm, tn, side_stack, side_layer, name):
    m, k = a.shape
    n = w_stack.shape[2]
    grid = (n // tn, m // tm)
    side_in, side_out, side_shape = _side_cast_specs(side_stack, side_layer, grid[1], grid[0] * grid[1])
    return pl.pallas_call(
        _mm_wcast_kernel,
        grid=grid,
        in_specs=[pl.BlockSpec((tm, k), lambda j, i: (i, 0)),
                  pl.BlockSpec((1, k, tn), lambda j, i: (layer, 0, j)),
                  side_in],
        out_specs=[pl.BlockSpec((tm, tn), lambda j, i: (i, j)), side_out],
        out_shape=[jax.ShapeDtypeStruct((m, n), out_dtype), side_shape],
        scratch_shapes=[pltpu.VMEM((k, tn), BF16)],
        compiler_params=_cp(2),
        name=name,
    )(a, w_stack, side_stack)


IN_EVEN_TM = 1408
IN_EVEN_TN = 768
IN_RET_TM = 1408
IN_RET_TN = 1024


def _in_even_kernel(a_ref, as_ref, w_ref, side_ref, z_ref, zs_ref, side_o_ref, wb_ref):
    @pl.when(pl.program_id(1) == 0)
    def _():
        wb_ref[...] = w_ref[0].astype(BF16)
        zs_ref[...] = jnp.dot(as_ref[...], wb_ref[...], preferred_element_type=F32)

    z_ref[...] = jnp.dot(a_ref[...], wb_ref[...], preferred_element_type=F32).astype(z_ref.dtype)
    side_o_ref[...] = side_ref[...].astype(BF16)


def _in_even(xn, w_stack, layer, side_stack):
    m, k = xn.shape
    n = w_stack.shape[2]
    tm, tn = IN_EVEN_TM, IN_EVEN_TN
    grid = (n // tn, m // tm)
    side_in, side_out, side_shape = _side_cast_specs(side_stack, layer, grid[1], grid[0] * grid[1])
    return pl.pallas_call(
        _in_even_kernel,
        grid=grid,
        in_specs=[pl.BlockSpec((tm, k), lambda j, i: (i, 0)),
                  pl.BlockSpec((N_SAMPLE, k), lambda j, i: (N_PROMPT // N_SAMPLE, 0)),
                  pl.BlockSpec((1, k, tn), lambda j, i: (layer, 0, j)),
                  side_in],
        out_specs=[pl.BlockSpec((tm, tn), lambda j, i: (i, j)),
                   pl.BlockSpec((N_SAMPLE, tn), lambda j, i: (0, j)),
                   side_out],
        out_shape=[jax.ShapeDtypeStruct((m, n), BF16),
                   jax.ShapeDtypeStruct((N_SAMPLE, n), F32),
                   side_shape],
        scratch_shapes=[pltpu.VMEM((k, tn), BF16)],
        compiler_params=_cp(2),
        name="in_even",
    )(xn, xn, w_stack, side_stack)


def _kv_tail_kernel(a_ref, w_ref, o_ref):
    o_ref[...] = jnp.dot(a_ref[...], w_ref[0].astype(BF16), preferred_element_type=F32)


def _kv_tail(xn, w_stack, layer):
    k = xn.shape[1]
    blks_per_seq = SEQ // WINDOW
    width = 2 * A_KV_WIDTH
    return pl.pallas_call(
        _kv_tail_kernel,
        grid=(BATCH,),
        in_specs=[pl.BlockSpec((WINDOW, k), lambda b: (blks_per_seq * (b + 1) - 1, 0)),
                  pl.BlockSpec((1, k, width), lambda b: (layer, 0, OFF_K // width))],
        out_specs=pl.BlockSpec((WINDOW, width), lambda b: (b, 0)),
        out_shape=jax.ShapeDtypeStruct((BATCH * WINDOW, width), F32),
        compiler_params=_cp(1),
        name="kv_tail",
    )(xn, w_stack)


def _mm_res_q_kernel(nk, a_ref, w_ref, x_ref, g_ref, wq_ref, xo_ref, q_ref):
    k = pl.program_id(1)

    @pl.when(k == 0)
    def _():
        xo_ref[...] = x_ref[...]

    xo_ref[...] += jnp.dot(a_ref[...], w_ref[0], preferred_element_type=F32)

    @pl.when(k == nk - 1)
    def _():
        xn = _rms(xo_ref[...], g_ref[...]).astype(BF16)
        q_ref[...] = jnp.dot(xn, wq_ref[0].astype(BF16), preferred_element_type=F32).astype(q_ref.dtype)


def _mm_res_q(a, w_stack, layer, x, g, wq_stack, wq_layer, tm, tk, x_blk0=0):
    m, kdim = a.shape
    d = w_stack.shape[2]
    nq = wq_stack.shape[2]
    nk = kdim // tk
    return pl.pallas_call(
        functools.partial(_mm_res_q_kernel, nk),
        grid=(m // tm, nk),
        in_specs=[pl.BlockSpec((tm, tk), lambda i, k: (i, k)),
                  pl.BlockSpec((1, tk, d), lambda i, k: (layer, k, 0)),
                  pl.BlockSpec((tm, d), lambda i, k: (x_blk0 + i, 0)),
                  pl.BlockSpec((1, d), lambda i, k: (0, 0)),
                  pl.BlockSpec((1, d, nq), lambda i, k: (wq_layer, 0, 0))],
        out_specs=[pl.BlockSpec((tm, d), lambda i, k: (i, 0)),
                   pl.BlockSpec((tm, nq), lambda i, k: (i, 0))],
        out_shape=[jax.ShapeDtypeStruct((m, d), F32),
                   jax.ShapeDtypeStruct((m, nq), BF16)],
        compiler_params=_cp(2),
        name="mm_res_q",
    )(a, w_stack, x, g.reshape(1, d), wq_stack)


def _embed_kernel(n_prompt_blks, xp_ref, xs_ref, g_ref, n_ref):
    i = pl.program_id(0)

    @pl.when(i < n_prompt_blks)
    def _():
        n_ref[...] = _rms(xp_ref[...], g_ref[...]).astype(n_ref.dtype)

    @pl.when(i >= n_prompt_blks)
    def _():
        n_ref[...] = _rms(xs_ref[...], g_ref[...]).astype(n_ref.dtype)


def _embed(xp, xs, g):
    d = xp.shape[1]
    tm = N_SAMPLE
    nb = N_PROMPT // tm
    return pl.pallas_call(
        functools.partial(_embed_kernel, nb),
        grid=(nb + 1,),
        in_specs=[pl.BlockSpec((tm, d), lambda i: (jnp.minimum(i, nb - 1), 0)),
                  pl.BlockSpec((tm, d), lambda i: (0, 0)),
                  pl.BlockSpec((1, d), lambda i: (0, 0))],
        out_specs=pl.BlockSpec((tm, d), lambda i: (i, 0)),
        out_shape=jax.ShapeDtypeStruct((N_ROWS, d), BF16),
        compiler_params=_cp(1),
        name="embed",
    )(xp, xs, g.reshape(1, d))


def _memkv_kernel(mem_ref, g_ref, wk_ref, wv_ref, wq_ref, wo_ref,
                  mk_ref, mv_ref, mkb_ref, mvb_ref, wqb_ref, wob_ref):
    h = _rms(mem_ref[...], g_ref[0]).astype(BF16)
    mk = jnp.dot(h, wk_ref[0].astype(BF16), preferred_element_type=F32)
    mv = jnp.dot(h, wv_ref[0].astype(BF16), preferred_element_type=F32)
    mk_ref[0] = mk
    mv_ref[0] = mv
    mkb_ref[0] = mk.astype(BF16)
    mvb_ref[0] = mv.astype(BF16)
    wqb_ref[...] = wq_ref[...].astype(BF16)
    wob_ref[...] = wo_ref[...].astype(BF16)


def _memkv(mem, norm_mem, w_xk, w_xv, w_xq, w_xo):
    m = mem.shape[0]
    kv_spec = pl.BlockSpec((1, m, X_WIDTH), lambda l: (l, 0, 0))
    w_in_spec = pl.BlockSpec((1, D_MODEL, X_WIDTH), lambda l: (l, 0, 0))
    w_out_spec = pl.BlockSpec((1, X_WIDTH, D_MODEL), lambda l: (l, 0, 0))
    return pl.pallas_call(
        _memkv_kernel,
        grid=(DEPTH,),
        in_specs=[pl.BlockSpec((m, D_MODEL), lambda l: (0, 0)),
                  pl.BlockSpec((1, 1, D_MODEL), lambda l: (l, 0, 0)),
                  w_in_spec, w_in_spec, w_in_spec, w_out_spec],
        out_specs=[kv_spec, kv_spec, kv_spec, kv_spec, w_in_spec, w_out_spec],
        out_shape=[jax.ShapeDtypeStruct((DEPTH, m, X_WIDTH), F32),
                   jax.ShapeDtypeStruct((DEPTH, m, X_WIDTH), F32),
                   jax.ShapeDtypeStruct((DEPTH, m, X_WIDTH), BF16),
                   jax.ShapeDtypeStruct((DEPTH, m, X_WIDTH), BF16),
                   jax.ShapeDtypeStruct(w_xq.shape, BF16),
                   jax.ShapeDtypeStruct(w_xo.shape, BF16)],
        compiler_params=_cp(1),
        name="memkv",
    )(mem, norm_mem.reshape(DEPTH, 1, D_MODEL), w_xk, w_xv, w_xq, w_xo)


SLABS_PER_KV = A_HEADS // A_KV_HEADS // 2
MASK_DIST = 1e30 * 2.0 ** 8


def _expand_band(band, scale):
    x = band.astype(F32)
    if scale != 1.0:
        x = x * scale
    xs = pltpu.roll(x, 64, 1)
    lo = lax.broadcasted_iota(jnp.int32, x.shape, 1) < 64
    z = jnp.zeros_like(x)
    e0 = jnp.concatenate([jnp.where(lo, x, z), jnp.where(lo, z, xs)], axis=0).astype(BF16)
    e1 = jnp.concatenate([jnp.where(lo, xs, z), jnp.where(lo, z, x)], axis=0).astype(BF16)
    return e0, e1


def _band_bias(t, first_key):
    row = lax.broadcasted_iota(jnp.int32, (t, 2 * WINDOW), 0)
    col = lax.broadcasted_iota(jnp.int32, (t, 2 * WINDOW), 1)
    dist = row + WINDOW - col
    valid = (dist >= 0) & (dist < WINDOW) & (col >= first_key)
    return jnp.where(valid, -dist.astype(F32), -MASK_DIST)


def _swa(q_slab, kband, vband, nbias, sl_ref):
    t = nbias.shape[0]
    rows = SLABS_PER_KV * t
    kk = _expand_band(kband, A_HEAD_DIM ** -0.5)
    vv = _expand_band(vband, 1.0)
    nb = jnp.concatenate([nbias] * SLABS_PER_KV, axis=0)
    blk = lax.shift_right_logical(lax.broadcasted_iota(jnp.int32, (rows, 1), 0), t.bit_length() - 1)
    lo = lax.broadcasted_iota(jnp.int32, (rows, 128), 1) < 64
    def per_row(table_row, j, half):
        slab0 = SLABS_PER_KV * j
        col = jnp.full((rows, 1), sl_ref[table_row, 2 * (slab0 + SLABS_PER_KV - 1) + half], F32)
        for i in reversed(range(SLABS_PER_KV - 1)):
            col = jnp.where(blk == i, sl_ref[table_row, 2 * (slab0 + i) + half], col)
        return col

    scores = [_dot_nt(jnp.concatenate([q_slab(SLABS_PER_KV * j + i) for i in range(SLABS_PER_KV)], axis=0), kk[j])
              for j in range(A_KV_HEADS)]
    pps, scales = [], []
    for j, s in enumerate(scores):
        probs, invs = [], []
        for half in range(2):
            slope = per_row(0, j, half)
            sink = per_row(1, j, half)
            sh = s[:, half * 256:(half + 1) * 256] + slope * nb
            m = jnp.maximum(jnp.max(sh, axis=-1, keepdims=True), sink)
            pe = jnp.exp(sh - m)
            den = jnp.sum(pe, axis=-1, keepdims=True) + jnp.exp(sink - m)
            probs.append(pe)
            invs.append(1.0 / den)
        pps.append(jnp.concatenate(probs, axis=1).astype(BF16))
        scales.append(jnp.where(lo, invs[0], invs[1]))
    outs = []
    for j in range(A_KV_HEADS):
        o = jnp.dot(pps[j], vv[j], preferred_element_type=F32) * scales[j]
        outs += [o[i * t:(i + 1) * t] for i in range(SLABS_PER_KV)]
    return outs


def _spatial_gate_group(ws_ref, bs_ref, g, vn_g, rows):
    r = lax.broadcasted_iota(jnp.int32, (CHUNK, CHUNK), 0)
    c = lax.broadcasted_iota(jnp.int32, (CHUNK, CHUNK), 1)
    w = jnp.where(r >= c, ws_ref[g], 0.0).astype(BF16)
    return jnp.dot(w[:rows], vn_g, preferred_element_type=F32) + bs_ref[g][:rows]


EVEN_STEP_ROWS = 512
EVEN_SUBBLKS = EVEN_STEP_ROWS // WINDOW
EVEN_STEPS = N_PROMPT // EVEN_STEP_ROWS


def _even_fused_kernel(sl_ref, z_ref, kvprev_ref, ws_ref, bs_ref, lng_ref, lnb_ref,
                       w_ref, x_ref, g_ref, wq_ref, xo_ref, q_ref, mixed_ref):
    s_id = pl.program_id(0)
    slot = s_id % 2

    @pl.when(s_id == 0)
    def _():
        mixed_ref[1] = jnp.zeros(mixed_ref.shape[1:], mixed_ref.dtype)

    kc = D_MODEL // EVEN_SUBBLKS
    xo_ref[...] = x_ref[...]
    blk0 = jnp.minimum(s_id, EVEN_STEPS - 1) * EVEN_SUBBLKS
    for b in range(EVEN_SUBBLKS):
        xo_ref[...] += jnp.dot(mixed_ref[1 - slot, :, b * kc:(b + 1) * kc], w_ref[0, b * kc:(b + 1) * kc, :],
                               preferred_element_type=F32)
        rows = slice(b * WINDOW, (b + 1) * WINDOW)
        c = (blk0 + b) % (SEQ // WINDOW)
        kv_cur = z_ref[rows, OFF_K:OFF_K + 2 * A_KV_WIDTH]
        kv_prev = kvprev_ref[...] if b == 0 else z_ref[(b - 1) * WINDOW:b * WINDOW, OFF_K:OFF_K + 2 * A_KV_WIDTH]
        band = jnp.concatenate([kv_prev, kv_cur], axis=0)
        nbias = _band_bias(WINDOW, jnp.maximum(WINDOW - c * WINDOW, 0))

        def q_slab(p, rows=rows):
            return z_ref[rows, OFF_Q + p * 128:OFF_Q + (p + 1) * 128]

        for p, att in enumerate(_swa(q_slab, band[:, :A_KV_WIDTH], band[:, A_KV_WIDTH:], nbias, sl_ref)):
            ga = z_ref[rows, OFF_GA + p * 128:OFF_GA + (p + 1) * 128].astype(F32)
            mixed_ref[slot, rows, p * 128:(p + 1) * 128] = (_silu(ga) * att).astype(mixed_ref.dtype)

        vn = _standardize(z_ref[rows, OFF_VB:OFF_VB + B_WIDTH].astype(F32)) * lng_ref[...] + lnb_ref[...]
        for g in range(B_GROUPS):
            lo, hi = g * B_GROUP_DIM, (g + 1) * B_GROUP_DIM
            sg = _spatial_gate_group(ws_ref, bs_ref, g, vn[:, lo:hi].astype(BF16), CHUNK)
            u = z_ref[rows, OFF_U + lo:OFF_U + hi].astype(F32)
            gb = z_ref[rows, OFF_GB + lo:OFF_GB + hi].astype(F32)
            mixed_ref[slot, rows, A_WIDTH + lo:A_WIDTH + hi] = (_silu(gb) * (u * sg)).astype(mixed_ref.dtype)

    q_ref[...] = jnp.dot(_rms(xo_ref[...], g_ref[...]).astype(BF16), wq_ref[0],
                         preferred_element_type=F32).astype(q_ref.dtype)


def _even_fused(z, sl, w_s, b_s, lng, lnb, w_out_b, layer_i, x, g, wq_b, layer):
    ns = EVEN_STEPS
    r = EVEN_STEP_ROWS
    kv_blk = OFF_K // (2 * A_KV_WIDTH)

    def cur(s):
        return jnp.minimum(s, ns - 1)

    def prev(s):
        return jnp.maximum(s - 1, 0)

    once = pl.Buffered(1)
    return pl.pallas_call(
        _even_fused_kernel,
        grid=(ns + 1,),
        in_specs=[pl.BlockSpec(memory_space=pltpu.SMEM),
                  pl.BlockSpec((r, EVEN_IN), lambda s: (cur(s), 0)),
                  pl.BlockSpec((WINDOW, 2 * A_KV_WIDTH),
                               lambda s: (jnp.maximum(cur(s) * EVEN_SUBBLKS - 1, 0), kv_blk)),
                  pl.BlockSpec((B_GROUPS, CHUNK, CHUNK), lambda s: (0, 0, 0)),
                  pl.BlockSpec((B_GROUPS, CHUNK, 1), lambda s: (0, 0, 0)),
                  pl.BlockSpec((1, B_WIDTH), lambda s: (0, 0)),
                  pl.BlockSpec((1, B_WIDTH), lambda s: (0, 0)),
                  pl.BlockSpec((1, D_MODEL, D_MODEL), lambda s: (layer_i, 0, 0), pipeline_mode=once),
                  pl.BlockSpec((r, D_MODEL), lambda s: (prev(s), 0)),
                  pl.BlockSpec((1, D_MODEL), lambda s: (0, 0)),
                  pl.BlockSpec((1, D_MODEL, X_WIDTH), lambda s: (layer, 0, 0), pipeline_mode=once)],
        out_specs=[pl.BlockSpec((r, D_MODEL), lambda s: (prev(s), 0)),
                   pl.BlockSpec((r, X_WIDTH), lambda s: (prev(s), 0))],
        out_shape=[jax.ShapeDtypeStruct((N_PROMPT, D_MODEL), F32),
                   jax.ShapeDtypeStruct((N_PROMPT, X_WIDTH), BF16)],
        scratch_shapes=[pltpu.VMEM((2, r, D_MODEL), BF16)],
        compiler_params=_cp(1, FUSED_VMEM_LIMIT),
        name="even_fused",
    )(sl, z, z, w_s, b_s, lng, lnb, w_out_b, x, g.reshape(1, D_MODEL), wq_b)


SEQ_PER_STEP = 4
ROWS_PER_STEP = SEQ_PER_STEP * DEC_SEQ
SAMPLE_Q_ROWS = 16
assert DEC_SEQ & (DEC_SEQ - 1) == 0


def _seq_of(row_idx):
    return lax.shift_right_logical(row_idx, DEC_SEQ.bit_length() - 1)


def _even_sample_kernel(sl_ref, z_ref, ck_ref, cv_ref, ws_ref, bs_ref, lng_ref, lnb_ref,
                        o_ref, wk_ref, wv_ref, cvo_ref):
    nbias = _band_bias(SAMPLE_Q_ROWS, 0)
    pad_rows = jnp.zeros((SAMPLE_Q_ROWS - DEC_SEQ, EVEN_IN), F32)
    attn = [[None] * SEQ_PER_STEP for _ in range(A_HEADS // 2)]
    gate = [[None] * SEQ_PER_STEP for _ in range(B_GROUPS)]
    for r in range(SEQ_PER_STEP):
        r0, r1 = r * DEC_SEQ, (r + 1) * DEC_SEQ
        zr = z_ref[r0:r1, :]
        k_new = zr[:, OFF_K:OFF_K + A_KV_WIDTH]
        v_new = zr[:, OFF_V:OFF_V + A_KV_WIDTH]
        ck = ck_ref[r]
        cv = cv_ref[r]
        wk_ref[r, 0:WINDOW - DEC_SEQ, :] = ck[DEC_SEQ:, :]
        wk_ref[r, WINDOW - DEC_SEQ:, :] = k_new
        wv_ref[r, 0:WINDOW - DEC_SEQ, :] = cv[DEC_SEQ:, :]
        wv_ref[r, WINDOW - DEC_SEQ:, :] = v_new
        tail = jnp.zeros((WINDOW - DEC_SEQ, A_KV_WIDTH), F32)
        kband = jnp.concatenate([ck, k_new, tail], axis=0)
        vband = jnp.concatenate([cv, v_new, tail], axis=0)
        zq = jnp.concatenate([zr, pad_rows], axis=0)

        def q_slab(p, zq=zq):
            return zq[:, OFF_Q + p * 128:OFF_Q + (p + 1) * 128].astype(BF16)

        for p, a in enumerate(_swa(q_slab, kband, vband, nbias, sl_ref)):
            ga = zr[:, OFF_GA + p * 128:OFF_GA + (p + 1) * 128]
            attn[p][r] = _silu(ga) * a[:DEC_SEQ]

        vn = _standardize(zr[:, OFF_VB:OFF_VB + B_WIDTH]) * lng_ref[...] + lnb_ref[...]
        cvo_ref[r0:r1, :] = vn
        vn_pad = jnp.concatenate([vn, jnp.zeros((CHUNK - DEC_SEQ, B_WIDTH), F32)], axis=0).astype(BF16)
        for g in range(B_GROUPS):
            lo, hi = g * B_GROUP_DIM, (g + 1) * B_GROUP_DIM
            mixed = _spatial_gate_group(ws_ref, bs_ref, g, vn_pad[:, lo:hi], SAMPLE_Q_ROWS)[:DEC_SEQ]
            u = zr[:, OFF_U + lo:OFF_U + hi]
            gb = zr[:, OFF_GB + lo:OFF_GB + hi]
            gate[g][r] = _silu(gb) * (u * mixed)

    for p in range(A_HEADS // 2):
        o_ref[:, p * 128:(p + 1) * 128] = jnp.concatenate(attn[p], axis=0).astype(o_ref.dtype)
    for g in range(B_GROUPS):
        lo, hi = A_WIDTH + g * B_GROUP_DIM, A_WIDTH + (g + 1) * B_GROUP_DIM
        o_ref[:, lo:hi] = jnp.concatenate(gate[g], axis=0).astype(o_ref.dtype)


def _even_sample(zs, ck, cv, sl, w_s, b_s, lng, lnb):
    n_steps = DEC_BATCH // SEQ_PER_STEP
    win = jax.ShapeDtypeStruct((DEC_BATCH, WINDOW, A_KV_WIDTH), F32)
    return pl.pallas_call(
        _even_sample_kernel,
        grid=(n_steps,),
        in_specs=[pl.BlockSpec(memory_space=pltpu.SMEM),
                  pl.BlockSpec((ROWS_PER_STEP, EVEN_IN), lambda i: (i, 0)),
                  pl.BlockSpec((SEQ_PER_STEP, WINDOW, A_KV_WIDTH), lambda i: (i, 0, 0)),
                  pl.BlockSpec((SEQ_PER_STEP, WINDOW, A_KV_WIDTH), lambda i: (i, 0, 0)),
                  pl.BlockSpec((B_GROUPS, CHUNK, CHUNK), lambda i: (0, 0, 0)),
                  pl.BlockSpec((B_GROUPS, CHUNK, 1), lambda i: (0, 0, 0)),
                  pl.BlockSpec((1, B_WIDTH), lambda i: (0, 0)),
                  pl.BlockSpec((1, B_WIDTH), lambda i: (0, 0))],
        out_specs=[pl.BlockSpec((ROWS_PER_STEP, D_MODEL), lambda i: (i, 0)),
                   pl.BlockSpec((SEQ_PER_STEP, WINDOW, A_KV_WIDTH), lambda i: (i, 0, 0)),
                   pl.BlockSpec((SEQ_PER_STEP, WINDOW, A_KV_WIDTH), lambda i: (i, 0, 0)),
                   pl.BlockSpec((ROWS_PER_STEP, B_WIDTH), lambda i: (i, 0))],
        out_shape=[jax.ShapeDtypeStruct((N_SAMPLE, D_MODEL), BF16), win, win,
                   jax.ShapeDtypeStruct((N_SAMPLE, B_WIDTH), F32)],
        compiler_params=_cp(1),
        name="even_sample",
    )(sl, zs, ck, cv, w_s, b_s, lng, lnb)


RET_ROWS = 256
RET_BLKS_PER_SEQ = SEQ // RET_ROWS
RET_PROMPT_BLKS = N_PROMPT // RET_ROWS


def _gated_groupnorm(o, g):
    return (_silu(g.astype(F32)) * _standardize(o)).astype(BF16)


RET_LOOKAHEAD = 1


def _ret_fused_kernel(lg_ref, z_ref, w_ref, x_ref, g_ref, wq_ref, xo_ref, q_ref, so_ref, dm_ref, dc_ref):
    s_id = pl.program_id(0)
    t = RET_ROWS
    k_scale = R_KEY_DIM ** -0.5
    c = s_id % RET_BLKS_PER_SEQ

    @pl.when(s_id == 0)
    def _():
        row = lax.broadcasted_iota(jnp.int32, (t, t), 0)
        col = lax.broadcasted_iota(jnp.int32, (t, t), 1)
        rel = (row - col).astype(F32)
        ti = lax.broadcasted_iota(jnp.int32, (t, 1), 0).astype(F32)
        for h in range(R_HEADS):
            lg = lg_ref[h]
            dm_ref[h] = jnp.where(rel >= 0, jnp.exp(jnp.maximum(rel, 0.0) * lg), 0.0) * k_scale
            dc_ref[h, 0] = jnp.exp((ti + 1.0) * lg)
            dc_ref[h, 1] = jnp.exp((t - 1.0 - ti) * lg) * k_scale

    @pl.when(c == 0)
    def _():
        so_ref[...] = jnp.zeros_like(so_ref)

    def head_matmuls(h):
        lg = lg_ref[h]
        q_decay = dc_ref[h, 0]
        k_decay = dc_ref[h, 1]
        chunk_decay = jnp.exp(jnp.full((1, R_VAL_DIM), float(t), F32) * lg)
        q = z_ref[:, h * R_KEY_DIM:(h + 1) * R_KEY_DIM]
        k = z_ref[:, R_K_WIDTH + h * R_KEY_DIM:R_K_WIDTH + (h + 1) * R_KEY_DIM]
        v = z_ref[:, 2 * R_K_WIDTH + h * R_VAL_DIM:2 * R_K_WIDTH + (h + 1) * R_VAL_DIM]
        st = so_ref[0, h]
        scores = _dot_nt(q, k) * dm_ref[h]
        inner = jnp.dot(scores.astype(BF16), v, preferred_element_type=F32)
        cross = jnp.dot(q, st.astype(BF16), preferred_element_type=F32) * q_decay
        kd_t = (k.astype(F32) * k_decay).T.astype(BF16)
        so_ref[0, h] = chunk_decay * st + jnp.dot(kd_t, v, preferred_element_type=F32)
        return inner + cross

    x1 = x_ref[...]
    ahead = [head_matmuls(h) for h in range(RET_LOOKAHEAD)]
    for h in range(R_HEADS):
        if h + RET_LOOKAHEAD < R_HEADS:
            ahead.append(head_matmuls(h + RET_LOOKAHEAD))
        raw = ahead.pop(0)
        g = z_ref[:, 2 * R_K_WIDTH + R_V_WIDTH + h * R_VAL_DIM:2 * R_K_WIDTH + R_V_WIDTH + (h + 1) * R_VAL_DIM]
        x1 = x1 + jnp.dot(_gated_groupnorm(raw, g), w_ref[0, h * R_VAL_DIM:(h + 1) * R_VAL_DIM, :],
                          preferred_element_type=F32)
    xo_ref[...] = x1
    q_ref[...] = jnp.dot(_rms(x1, g_ref[...]).astype(BF16), wq_ref[0],
                         preferred_element_type=F32).astype(q_ref.dtype)


def _ret_fused(z, log_g, w_out_b, layer_i, x, g, wq_b, layer):
    nb = RET_PROMPT_BLKS
    r = RET_ROWS
    once = pl.Buffered(1)
    return pl.pallas_call(
        _ret_fused_kernel,
        grid=(nb,),
        in_specs=[pl.BlockSpec(memory_space=pltpu.SMEM),
                  pl.BlockSpec((r, RET_IN), lambda s: (s, 0)),
                  pl.BlockSpec((1, R_V_WIDTH, D_MODEL), lambda s: (layer_i, 0, 0), pipeline_mode=once),
                  pl.BlockSpec((r, D_MODEL), lambda s: (s, 0)),
                  pl.BlockSpec((1, D_MODEL), lambda s: (0, 0)),
                  pl.BlockSpec((1, D_MODEL, X_WIDTH), lambda s: (layer, 0, 0), pipeline_mode=once)],
        out_specs=[pl.BlockSpec((r, D_MODEL), lambda s: (s, 0)),
                   pl.BlockSpec((r, X_WIDTH), lambda s: (s, 0)),
                   pl.BlockSpec((1, R_HEADS, R_KEY_DIM, R_VAL_DIM),
                                lambda s: (s // RET_BLKS_PER_SEQ, 0, 0, 0))],
        out_shape=[jax.ShapeDtypeStruct((N_PROMPT, D_MODEL), F32),
                   jax.ShapeDtypeStruct((N_PROMPT, X_WIDTH), BF16),
                   jax.ShapeDtypeStruct((BATCH, R_HEADS, R_KEY_DIM, R_VAL_DIM), F32)],
        scratch_shapes=[pltpu.VMEM((R_HEADS, r, r), F32),
                        pltpu.VMEM((R_HEADS, 2, r, 1), F32)],
        compiler_params=_cp(1, FUSED_VMEM_LIMIT),
        name="ret_fused",
    )(log_g, z, w_out_b, x, g.reshape(1, D_MODEL), wq_b)


RET_SAMPLE_SEQS = 8
RET_SAMPLE_ROWS = RET_SAMPLE_SEQS * DEC_SEQ


def _ret_sample_kernel(lg_ref, q_ref, k_ref, v_ref, g_ref, s_ref, *rest):
    o_ref, so_ref = rest[-2:]
    h = pl.program_id(1)
    lg = lg_ref[h]
    n = RET_SAMPLE_ROWS
    t = DEC_SEQ
    row = lax.broadcasted_iota(jnp.int32, (n, n), 0)
    col = lax.broadcasted_iota(jnp.int32, (n, n), 1)
    rel = (row - col).astype(F32)
    same = _seq_of(row) == _seq_of(col)
    k_scale = R_KEY_DIM ** -0.5
    dmat = jnp.where(same & (rel >= 0), jnp.exp(jnp.maximum(rel, 0.0) * lg), 0.0) * k_scale
    ri = lax.broadcasted_iota(jnp.int32, (n, 1), 0)
    ti = (ri & (t - 1)).astype(F32)
    q_decay = jnp.exp((ti + 1.0) * lg)
    k_decay = jnp.exp((t - 1.0 - ti) * lg) * k_scale
    step_decay = jnp.exp(jnp.full((1, R_VAL_DIM), float(t), F32) * lg)

    q = q_ref[...]
    k = k_ref[...]
    v = v_ref[...]
    scores = _dot_nt(q, k) * dmat
    o = jnp.dot(scores.astype(BF16), v, preferred_element_type=F32)
    kd = k.astype(F32) * k_decay
    k_pad = jnp.zeros((R_CHUNK - n, R_KEY_DIM), F32)
    v_pad = jnp.concatenate([v.astype(F32), jnp.zeros((R_CHUNK - n, R_VAL_DIM), F32)], axis=0).astype(BF16)
    for r in range(RET_SAMPLE_SEQS):
        mine = _seq_of(ri) == r
        st = s_ref[0, r, 0]
        cross = jnp.dot(q, st.astype(BF16), preferred_element_type=F32) * q_decay
        o = o + jnp.where(mine, cross, 0.0)
        kd_t = jnp.concatenate([jnp.where(mine, kd, 0.0), k_pad], axis=0).T.astype(BF16)
        so_ref[0, r, 0] = step_decay * st + jnp.dot(kd_t, v_pad, preferred_element_type=F32)
    o_ref[...] = _gated_groupnorm(o, g_ref[...])


def _ret_sample(z, state_all, idx, log_g, new_states):
    n_steps = DEC_BATCH // RET_SAMPLE_SEQS
    rb0 = N_PROMPT // RET_SAMPLE_ROWS
    kcol = R_K_WIDTH // R_KEY_DIM
    vcol = 2 * R_K_WIDTH // R_VAL_DIM
    gcol = vcol + R_HEADS
    st_spec = pl.BlockSpec((1, RET_SAMPLE_SEQS, 1, R_KEY_DIM, R_VAL_DIM), lambda i, h: (idx, i, h, 0, 0))
    in_specs = [pl.BlockSpec(memory_space=pltpu.SMEM),
                pl.BlockSpec((RET_SAMPLE_ROWS, R_KEY_DIM), lambda i, h: (rb0 + i, h)),
                pl.BlockSpec((RET_SAMPLE_ROWS, R_KEY_DIM), lambda i, h: (rb0 + i, kcol + h)),
                pl.BlockSpec((RET_SAMPLE_ROWS, R_VAL_DIM), lambda i, h: (rb0 + i, vcol + h)),
                pl.BlockSpec((RET_SAMPLE_ROWS, R_VAL_DIM), lambda i, h: (rb0 + i, gcol + h)),
                st_spec]
    args = [log_g, z, z, z, z, state_all]
    aliases = {}
    if new_states is not None:
        in_specs.append(pl.BlockSpec(memory_space=pl.ANY))
        args.append(new_states)
        aliases = {len(args) - 1: 1}
    return pl.pallas_call(
        _ret_sample_kernel,
        grid=(n_steps, R_HEADS),
        in_specs=in_specs,
        out_specs=[pl.BlockSpec((RET_SAMPLE_ROWS, R_VAL_DIM), lambda i, h: (i, h)), st_spec],
        out_shape=[jax.ShapeDtypeStruct((N_SAMPLE, R_V_WIDTH), BF16),
                   jax.ShapeDtypeStruct(state_all.shape, F32)],
        input_output_aliases=aliases,
        compiler_params=_cp(2),
        name="ret_sample",
    )(*args)


XQ_ROWS = 256
XQ_PROMPT_BLKS = N_PROMPT // XQ_ROWS
XQ_BLKS_PER_SEQ = SEQ // XQ_ROWS


def _xattn_heads(q, mk_heads, mv_heads):
    pairs = [(i, h) for i in range(len(mk_heads)) for h in range(X_HEADS)]
    scores = [_dot_nt(q[:, h * X_HEAD_DIM:(h + 1) * X_HEAD_DIM], mk_heads[i](h)) * (X_HEAD_DIM ** -0.5)
              for i, h in pairs]
    probs, invs = [], []
    for s in scores:
        p = jnp.exp(s - jnp.max(s, axis=-1, keepdims=True))
        probs.append(p.astype(BF16))
        invs.append(1.0 / jnp.sum(p, axis=-1, keepdims=True))
    outs = [jnp.dot(p, mv_heads[i](h), preferred_element_type=F32) * inv
            for (i, h), p, inv in zip(pairs, probs, invs)]
    return [outs[i * X_HEADS:(i + 1) * X_HEADS] for i in range(len(mk_heads))]


def _xattn_proj_kernel(final, q_ref, mk_ref, mv_ref, os_ref, w_ref, xp_ref, xs_ref, g_ref, out_a, out_b):
    s_id = pl.program_id(0)

    def project(o, x_ref):
        return x_ref[...] + jnp.dot(o, w_ref[0], preferred_element_type=F32)

    @pl.when(s_id < XQ_PROMPT_BLKS)
    def _():
        def head_of(ref):
            return lambda h: ref[0, :, h * X_HEAD_DIM:(h + 1) * X_HEAD_DIM].astype(BF16)

        outs, = _xattn_heads(q_ref[...], [head_of(mk_ref)], [head_of(mv_ref)])
        x2 = project(jnp.concatenate(outs, axis=1).astype(BF16), xp_ref)
        if final:
            out_a[...] = _rms(x2, g_ref[...])
        else:
            out_a[...] = x2
            out_b[...] = _rms(x2, g_ref[...]).astype(out_b.dtype)

    @pl.when(s_id >= XQ_PROMPT_BLKS)
    def _():
        x2 = project(os_ref[...], xs_ref)
        if final:
            out_b[...] = _rms(x2, g_ref[...])
        else:
            out_a[...] = x2
            out_b[...] = _rms(x2, g_ref[...]).astype(out_b.dtype)


def _xattn_proj(q, mk_all, mv_all, layer, o_sample, w_stack, x_prompt, x_sample, g, final):
    nb = XQ_PROMPT_BLKS
    d = w_stack.shape[2]
    xs_blk = (x_sample.shape[0] - N_SAMPLE) // N_SAMPLE

    def rb(s):
        return jnp.minimum(s, nb - 1)

    mem_spec = pl.BlockSpec((1, MEM_LEN, X_WIDTH), lambda s: (layer, rb(s) // XQ_BLKS_PER_SEQ, 0))
    if final:
        out_specs = [pl.BlockSpec((XQ_ROWS, d), lambda s: (rb(s), 0)),
                     pl.BlockSpec((N_SAMPLE, d), lambda s: (0, 0))]
        out_shape = [jax.ShapeDtypeStruct((N_PROMPT, d), F32), jax.ShapeDtypeStruct((N_SAMPLE, d), F32)]
    else:
        out_specs = [pl.BlockSpec((XQ_ROWS, d), lambda s: (s, 0)),
                     pl.BlockSpec((XQ_ROWS, d), lambda s: (s, 0))]
        out_shape = [jax.ShapeDtypeStruct((N_ROWS, d), F32), jax.ShapeDtypeStruct((N_ROWS, d), BF16)]
    return pl.pallas_call(
        functools.partial(_xattn_proj_kernel, final),
        grid=(nb + 1,),
        in_specs=[pl.BlockSpec((XQ_ROWS, X_WIDTH), lambda s: (rb(s), 0)),
                  mem_spec, mem_spec,
                  pl.BlockSpec((N_SAMPLE, X_WIDTH), lambda s: (0, 0)),
                  pl.BlockSpec((1, X_WIDTH, d), lambda s: (layer, 0, 0)),
                  pl.BlockSpec((XQ_ROWS, d), lambda s: (rb(s), 0)),
                  pl.BlockSpec((N_SAMPLE, d), lambda s: (xs_blk, 0)),
                  pl.BlockSpec((1, d), lambda s: (0, 0))],
        out_specs=out_specs,
        out_shape=out_shape,
        compiler_params=_cp(1),
        name="xattn_proj",
    )(q, mk_all, mv_all, o_sample, w_stack, x_prompt, x_sample, g.reshape(1, d))


XS_SEQS = 4
XS_ROWS = XS_SEQS * DEC_SEQ


def _xattn_sample_kernel(q_ref, ck_ref, cv_ref, o_ref):
    q = q_ref[...]
    ri = lax.broadcasted_iota(jnp.int32, (XS_ROWS, 1), 0)
    def head_of(ref, r):
        return lambda h: ref[0, r, pl.ds(h, MEM_LEN, stride=X_HEADS), :].astype(BF16)

    per_seq = _xattn_heads(q, [head_of(ck_ref, r) for r in range(XS_SEQS)],
                           [head_of(cv_ref, r) for r in range(XS_SEQS)])
    acc = [jnp.zeros((XS_ROWS, X_HEAD_DIM), F32) for _ in range(X_HEADS)]
    for r, outs in enumerate(per_seq):
        mine = _seq_of(ri) == r
        acc = [a + jnp.where(mine, o, 0.0) for a, o in zip(acc, outs)]
    for h, a in enumerate(acc):
        o_ref[:, h * X_HEAD_DIM:(h + 1) * X_HEAD_DIM] = a.astype(o_ref.dtype)


def _xattn_sample(q, ck_all, cv_all, layer):
    n_steps = DEC_BATCH // XS_SEQS
    rb0 = (q.shape[0] - N_SAMPLE) // XS_ROWS
    mem_spec = pl.BlockSpec((1, XS_SEQS, MEM_LEN * X_HEADS, X_HEAD_DIM), lambda i: (layer, i, 0, 0))
    return pl.pallas_call(
        _xattn_sample_kernel,
        grid=(n_steps,),
        in_specs=[pl.BlockSpec((XS_ROWS, X_WIDTH), lambda i: (rb0 + i, 0)), mem_spec, mem_spec],
        out_specs=pl.BlockSpec((XS_ROWS, X_WIDTH), lambda i: (i, 0)),
        out_shape=jax.ShapeDtypeStruct((N_SAMPLE, X_WIDTH), BF16),
        compiler_params=_cp(1),
        name="xattn_sample",
    )(q, ck_all, cv_all)


def kernel(x_prompt, x_sample, cache_win_k, cache_win_v, state_ret, cache_mem_k, cache_mem_v, mem_prompt,
           norm_mix, w_in_even, attn_sinks, w_spatial, b_spatial, norm_v_g, norm_v_b, w_out_even,
           w_in_ret, w_out_ret, norm_cross, norm_mem, w_xq, w_xk, w_xv, w_xo, norm_final):
    slopes = jnp.exp2(-8.0 * (jnp.arange(A_HEADS, dtype=F32) + 1.0) / A_HEADS)
    log_g = jnp.log1p(-jnp.exp2(-5.0 - jnp.arange(R_HEADS, dtype=F32)))

    def mem_rows(cache):
        return cache.reshape(DEPTH, DEC_BATCH, MEM_LEN * X_HEADS, X_HEAD_DIM)

    mk_all, mv_all, mk_b, mv_b, w_xq_b, w_xo_b = _memkv(mem_prompt.reshape(BATCH * MEM_LEN, D_MODEL), norm_mem,
                                                        w_xk, w_xv, w_xq, w_xo)

    xp_src = x_prompt.reshape(N_PROMPT, D_MODEL)
    xs_src = x_sample.reshape(N_SAMPLE, D_MODEL)
    xn = _embed(xp_src, xs_src, norm_mix[0])
    wkp, wvp, rsp, wks, wvs, cvs = [], [], [], [], [], []
    rss = None
    for layer in range(DEPTH):
        i = layer // 2
        xs_blk0 = (xs_src.shape[0] - N_SAMPLE) // N_SAMPLE
        if layer % 2 == 0:
            sl = jnp.stack([slopes, attn_sinks[i].astype(F32)])
            b_s = b_spatial[i].reshape(B_GROUPS, CHUNK, 1)
            lng = norm_v_g[i].reshape(1, B_WIDTH)
            lnb = norm_v_b[i].reshape(1, B_WIDTH)
            z, zs, w_out_b = _in_even(xn, w_in_even, i, w_out_even)
            kv_tail = _kv_tail(xn, w_in_even, i)
            wkp.append(kv_tail[:, :A_KV_WIDTH].reshape(BATCH, WINDOW, A_KV_HEADS, A_HEAD_DIM))
            wvp.append(kv_tail[:, A_KV_WIDTH:].reshape(BATCH, WINDOW, A_KV_HEADS, A_HEAD_DIM))
            ck = cache_win_k[i].reshape(DEC_BATCH, WINDOW, A_KV_WIDTH)
            cv = cache_win_v[i].reshape(DEC_BATCH, WINDOW, A_KV_WIDTH)
            mixed_s, wk_new, wv_new, cv_new = _even_sample(zs, ck, cv, sl, w_spatial[i], b_s, lng, lnb)
            wks.append(wk_new.reshape(DEC_BATCH, WINDOW, A_KV_HEADS, A_HEAD_DIM))
            wvs.append(wv_new.reshape(DEC_BATCH, WINDOW, A_KV_HEADS, A_HEAD_DIM))
            cvs.append(cv_new.reshape(DEC_BATCH, DEC_SEQ, B_WIDTH))
            x_p, q = _even_fused(z, sl, w_spatial[i], b_s, lng, lnb, w_out_b, 0, xp_src, norm_cross[layer],
                                 w_xq_b, layer)
            x_s, q_s = _mm_res_q(mixed_s, w_out_b, 0, xs_src, norm_cross[layer], w_xq_b, layer,
                                 N_SAMPLE, SAMPLE_PROJ_TK, x_blk0=xs_blk0)
        else:
            z, w_out_b = _mm_wcast(xn, w_in_ret, i, BF16, IN_RET_TM, IN_RET_TN, w_out_ret, i, "in_ret")
            o_s, rss = _ret_sample(z, state_ret, i, log_g, rss)
            x_p, q, st_p = _ret_fused(z, log_g, w_out_b, 0, xp_src, norm_cross[layer], w_xq_b, layer)
            rsp.append(st_p)
            x_s, q_s = _mm_res_q(o_s, w_out_b, 0, xs_src, norm_cross[layer], w_xq_b, layer,
                                 N_SAMPLE, SAMPLE_PROJ_TK, x_blk0=xs_blk0)
        o_s = _xattn_sample(q_s, mem_rows(cache_mem_k), mem_rows(cache_mem_v), layer)
        if layer + 1 < DEPTH:
            x, xn = _xattn_proj(q, mk_b, mv_b, layer, o_s, w_xo_b, x_p, x_s, norm_mix[layer + 1], False)
            xp_src = xs_src = x
        else:
            y_prompt, y_sample = _xattn_proj(q, mk_b, mv_b, layer, o_s, w_xo_b, x_p, x_s, norm_final, True)

    y_prompt = y_prompt.reshape(BATCH, SEQ, D_MODEL)
    y_sample = y_sample.reshape(DEC_BATCH, DEC_SEQ, D_MODEL)
    mem_shape = (DEPTH, BATCH, MEM_LEN, X_HEADS, X_HEAD_DIM)
    return (y_prompt, y_sample,
            jnp.stack(wkp), jnp.stack(wvp), jnp.stack(rsp),
            mk_all.reshape(mem_shape), mv_all.reshape(mem_shape),
            jnp.stack(wks), jnp.stack(wvs), rss, jnp.stack(cvs))
```

```python
import functools

import jax
import jax.numpy as jnp
from jax import lax
from jax.experimental import pallas as pl
from jax.experimental.pallas import tpu as pltpu

F32 = jnp.float32
BF16 = jnp.bfloat16

D_MODEL = 2048
BATCH = 2
SEQ = 4096
DEPTH = 4
DEC_BATCH = 32
DEC_SEQ = 8
A_HEADS = 16
A_KV_HEADS = 2
A_HEAD_DIM = 64
A_WIDTH = 1024
A_KV_WIDTH = 128
WINDOW = 128
CHUNK = 128
B_GROUPS = 4
B_WIDTH = 1024
B_GROUP_DIM = 256
EVEN_IN = 5376
R_HEADS = 8
R_KEY_DIM = 256
R_VAL_DIM = 512
R_K_WIDTH = 2048
R_V_WIDTH = 4096
R_CHUNK = 128
RET_IN = 12288
MEM_LEN = 256
X_HEADS = 4
X_HEAD_DIM = 128
X_WIDTH = 512
EPS = 1e-6

N_PROMPT = BATCH * SEQ
N_SAMPLE = DEC_BATCH * DEC_SEQ
N_ROWS = N_PROMPT + N_SAMPLE

OFF_Q, OFF_K, OFF_V, OFF_GA, OFF_U, OFF_VB, OFF_GB = 0, 1024, 1152, 1280, 2304, 3328, 4352

MIB = 1024 * 1024
V7X_VMEM_BYTES = 64 * MIB
VMEM_LIMIT = V7X_VMEM_BYTES - 12 * MIB
FUSED_VMEM_LIMIT = V7X_VMEM_BYTES - 8 * MIB
SAMPLE_PROJ_TK = 2048


def _cp(n_axes, vmem_limit=VMEM_LIMIT):
    return pltpu.CompilerParams(dimension_semantics=("arbitrary",) * n_axes,
                                vmem_limit_bytes=vmem_limit)


def _silu(x):
    return x * (1.0 / (1.0 + jnp.exp(-x)))


def _rms(x, g):
    ms = jnp.mean(x * x, axis=-1, keepdims=True)
    return x * lax.rsqrt(ms + EPS) * g


def _standardize(x):
    xc = x - jnp.mean(x, axis=-1, keepdims=True)
    return xc * lax.rsqrt(jnp.mean(xc * xc, axis=-1, keepdims=True) + EPS)


def _dot_nt(a, b):
    return lax.dot_general(a, b, (((1,), (1,)), ((), ())), preferred_element_type=F32)


SIDE_CAST_ROWS = 64


def _side_cast_specs(side_stack, side_layer, n_inner, n_steps):
    _, k, d = side_stack.shape
    n_slabs = k // SIDE_CAST_ROWS
    assert n_slabs <= n_steps

    def slab(j, i):
        return jnp.minimum(j * n_inner + i, n_slabs - 1)

    return (pl.BlockSpec((1, SIDE_CAST_ROWS, d), lambda j, i: (side_layer, slab(j, i), 0)),
            pl.BlockSpec((1, SIDE_CAST_ROWS, d), lambda j, i: (0, slab(j, i), 0)),
            jax.ShapeDtypeStruct((1, k, d), BF16))


def _mm_wcast_kernel(a_ref, w_ref, side_ref, o_ref, side_o_ref, wb_ref):
    @pl.when(pl.program_id(1) == 0)
    def _():
        wb_ref[...] = w_ref[0].astype(BF16)

    o_ref[...] = jnp.dot(a_ref[...], wb_ref[...], preferred_element_type=F32).astype(o_ref.dtype)
    side_o_ref[...] = side_ref[...].astype(BF16)


def _mm_wcast(a, w_stack, layer, out_dtype, tm, tn, side_stack, side_layer, name):
    m, k = a.shape
    n = w_stack.shape[2]
    grid = (n // tn, m // tm)
    side_in, side_out, side_shape = _side_cast_specs(side_stack, side_layer, grid[1], grid[0] * grid[1])
    return pl.pallas_call(
        _mm_wcast_kernel,
        grid=grid,
        in_specs=[pl.BlockSpec((tm, k), lambda j, i: (i, 0)),
                  pl.BlockSpec((1, k, tn), lambda j, i: (layer, 0, j)),
                  side_in],
        out_specs=[pl.BlockSpec((tm, tn), lambda j, i: (i, j)), side_out],
        out_shape=[jax.ShapeDtypeStruct((m, n), out_dtype), side_shape],
        scratch_shapes=[pltpu.VMEM((k, tn), BF16)],
        compiler_params=_cp(2),
        name=name,
    )(a, w_stack, side_stack)


IN_EVEN_TM = 1408
IN_EVEN_TN = 768
IN_RET_TM = 1408
IN_RET_TN = 1024


def _in_even_kernel(a_ref, as_ref, w_ref, side_ref, z_ref, zs_ref, side_o_ref, wb_ref):
    @pl.when(pl.program_id(1) == 0)
    def _():
        wb_ref[...] = w_ref[0].astype(BF16)
        zs_ref[...] = jnp.dot(as_ref[...], wb_ref[...], preferred_element_type=F32)

    z_ref[...] = jnp.dot(a_ref[...], wb_ref[...], preferred_element_type=F32).astype(z_ref.dtype)
    side_o_ref[...] = side_ref[...].astype(BF16)


def _in_even(xn, w_stack, layer, side_stack):
    m, k = xn.shape
    n = w_stack.shape[2]
    tm, tn = IN_EVEN_TM, IN_EVEN_TN
    grid = (n // tn, m // tm)
    side_in, side_out, side_shape = _side_cast_specs(side_stack, layer, grid[1], grid[0] * grid[1])
    return pl.pallas_call(
        _in_even_kernel,
        grid=grid,
        in_specs=[pl.BlockSpec((tm, k), lambda j, i: (i, 0)),
                  pl.BlockSpec((N_SAMPLE, k), lambda j, i: (N_PROMPT // N_SAMPLE, 0)),
                  pl.BlockSpec((1, k, tn), lambda j, i: (layer, 0, j)),
                  side_in],
        out_specs=[pl.BlockSpec((tm, tn), lambda j, i: (i, j)),
                   pl.BlockSpec((N_SAMPLE, tn), lambda j, i: (0, j)),
                   side_out],
        out_shape=[jax.ShapeDtypeStruct((m, n), BF16),
                   jax.ShapeDtypeStruct((N_SAMPLE, n), F32),
                   side_shape],
        scratch_shapes=[pltpu.VMEM((k, tn), BF16)],
        compiler_params=_cp(2),
        name="in_even",
    )(xn, xn, w_stack, side_stack)


def _kv_tail_kernel(a_ref, w_ref, o_ref):
    o_ref[...] = jnp.dot(a_ref[...], w_ref[0].astype(BF16), preferred_element_type=F32)


def _kv_tail(xn, w_stack, layer):
    k = xn.shape[1]
    blks_per_seq = SEQ // WINDOW
    width = 2 * A_KV_WIDTH
    return pl.pallas_call(
        _kv_tail_kernel,
        grid=(BATCH,),
        in_specs=[pl.BlockSpec((WINDOW, k), lambda b: (blks_per_seq * (b + 1) - 1, 0)),
                  pl.BlockSpec((1, k, width), lambda b: (layer, 0, OFF_K // width))],
        out_specs=pl.BlockSpec((WINDOW, width), lambda b: (b, 0)),
        out_shape=jax.ShapeDtypeStruct((BATCH * WINDOW, width), F32),
        compiler_params=_cp(1),
        name="kv_tail",
    )(xn, w_stack)


def _mm_res_q_kernel(nk, a_ref, w_ref, x_ref, g_ref, wq_ref, xo_ref, q_ref):
    k = pl.program_id(1)

    @pl.when(k == 0)
    def _():
        xo_ref[...] = x_ref[...]

    xo_ref[...] += jnp.dot(a_ref[...], w_ref[0], preferred_element_type=F32)

    @pl.when(k == nk - 1)
    def _():
        xn = _rms(xo_ref[...], g_ref[...]).astype(BF16)
        q_ref[...] = jnp.dot(xn, wq_ref[0].astype(BF16), preferred_element_type=F32).astype(q_ref.dtype)


def _mm_res_q(a, w_stack, layer, x, g, wq_stack, wq_layer, tm, tk, x_blk0=0):
    m, kdim = a.shape
    d = w_stack.shape[2]
    nq = wq_stack.shape[2]
    nk = kdim // tk
    return pl.pallas_call(
        functools.partial(_mm_res_q_kernel, nk),
        grid=(m // tm, nk),
        in_specs=[pl.BlockSpec((tm, tk), lambda i, k: (i, k)),
                  pl.BlockSpec((1, tk, d), lambda i, k: (layer, k, 0)),
                  pl.BlockSpec((tm, d), lambda i, k: (x_blk0 + i, 0)),
                  pl.BlockSpec((1, d), lambda i, k: (0, 0)),
                  pl.BlockSpec((1, d, nq), lambda i, k: (wq_layer, 0, 0))],
        out_specs=[pl.BlockSpec((tm, d), lambda i, k: (i, 0)),
                   pl.BlockSpec((tm, nq), lambda i, k: (i, 0))],
        out_shape=[jax.ShapeDtypeStruct((m, d), F32),
                   jax.ShapeDtypeStruct((m, nq), BF16)],
        compiler_params=_cp(2),
        name="mm_res_q",
    )(a, w_stack, x, g.reshape(1, d), wq_stack)


def _embed_kernel(n_prompt_blks, xp_ref, xs_ref, g_ref, n_ref):
    i = pl.program_id(0)

    @pl.when(i < n_prompt_blks)
    def _():
        n_ref[...] = _rms(xp_ref[...], g_ref[...]).astype(n_ref.dtype)

    @pl.when(i >= n_prompt_blks)
    def _():
        n_ref[...] = _rms(xs_ref[...], g_ref[...]).astype(n_ref.dtype)


def _embed(xp, xs, g):
    d = xp.shape[1]
    tm = N_SAMPLE
    nb = N_PROMPT // tm
    return pl.pallas_call(
        functools.partial(_embed_kernel, nb),
        grid=(nb + 1,),
        in_specs=[pl.BlockSpec((tm, d), lambda i: (jnp.minimum(i, nb - 1), 0)),
                  pl.BlockSpec((tm, d), lambda i: (0, 0)),
                  pl.BlockSpec((1, d), lambda i: (0, 0))],
        out_specs=pl.BlockSpec((tm, d), lambda i: (i, 0)),
        out_shape=jax.ShapeDtypeStruct((N_ROWS, d), BF16),
        compiler_params=_cp(1),
        name="embed",
    )(xp, xs, g.reshape(1, d))


def _memkv_kernel(mem_ref, g_ref, wk_ref, wv_ref, wq_ref, wo_ref,
                  mk_ref, mv_ref, mkb_ref, mvb_ref, wqb_ref, wob_ref):
    h = _rms(mem_ref[...], g_ref[0]).astype(BF16)
    mk = jnp.dot(h, wk_ref[0].astype(BF16), preferred_element_type=F32)
    mv = jnp.dot(h, wv_ref[0].astype(BF16), preferred_element_type=F32)
    mk_ref[0] = mk
    mv_ref[0] = mv
    mkb_ref[0] = mk.astype(BF16)
    mvb_ref[0] = mv.astype(BF16)
    wqb_ref[...] = wq_ref[...].astype(BF16)
    wob_ref[...] = wo_ref[...].astype(BF16)


def _memkv(mem, norm_mem, w_xk, w_xv, w_xq, w_xo):
    m = mem.shape[0]
    kv_spec = pl.BlockSpec((1, m, X_WIDTH), lambda l: (l, 0, 0))
    w_in_spec = pl.BlockSpec((1, D_MODEL, X_WIDTH), lambda l: (l, 0, 0))
    w_out_spec = pl.BlockSpec((1, X_WIDTH, D_MODEL), lambda l: (l, 0, 0))
    return pl.pallas_call(
        _memkv_kernel,
        grid=(DEPTH,),
        in_specs=[pl.BlockSpec((m, D_MODEL), lambda l: (0, 0)),
                  pl.BlockSpec((1, 1, D_MODEL), lambda l: (l, 0, 0)),
                  w_in_spec, w_in_spec, w_in_spec, w_out_spec],
        out_specs=[kv_spec, kv_spec, kv_spec, kv_spec, w_in_spec, w_out_spec],
        out_shape=[jax.ShapeDtypeStruct((DEPTH, m, X_WIDTH), F32),
                   jax.ShapeDtypeStruct((DEPTH, m, X_WIDTH), F32),
                   jax.ShapeDtypeStruct((DEPTH, m, X_WIDTH), BF16),
                   jax.ShapeDtypeStruct((DEPTH, m, X_WIDTH), BF16),
                   jax.ShapeDtypeStruct(w_xq.shape, BF16),
                   jax.ShapeDtypeStruct(w_xo.shape, BF16)],
        compiler_params=_cp(1),
        name="memkv",
    )(mem, norm_mem.reshape(DEPTH, 1, D_MODEL), w_xk, w_xv, w_xq, w_xo)


SLABS_PER_KV = A_HEADS // A_KV_HEADS // 2
MASK_DIST = 1e30 * 2.0 ** 8


def _expand_band(band, scale):
    x = band.astype(F32)
    if scale != 1.0:
        x = x * scale
    xs = pltpu.roll(x, 64, 1)
    lo = lax.broadcasted_iota(jnp.int32, x.shape, 1) < 64
    z = jnp.zeros_like(x)
    e0 = jnp.concatenate([jnp.where(lo, x, z), jnp.where(lo, z, xs)], axis=0).astype(BF16)
    e1 = jnp.concatenate([jnp.where(lo, xs, z), jnp.where(lo, z, x)], axis=0).astype(BF16)
    return e0, e1


def _band_bias(t, first_key):
    row = lax.broadcasted_iota(jnp.int32, (t, 2 * WINDOW), 0)
    col = lax.broadcasted_iota(jnp.int32, (t, 2 * WINDOW), 1)
    dist = row + WINDOW - col
    valid = (dist >= 0) & (dist < WINDOW) & (col >= first_key)
    return jnp.where(valid, -dist.astype(F32), -MASK_DIST)


def _swa(q_slab, kband, vband, nbias, sl_ref):
    t = nbias.shape[0]
    rows = SLABS_PER_KV * t
    kk = _expand_band(kband, A_HEAD_DIM ** -0.5)
    vv = _expand_band(vband, 1.0)
    nb = jnp.concatenate([nbias] * SLABS_PER_KV, axis=0)
    blk = lax.shift_right_logical(lax.broadcasted_iota(jnp.int32, (rows, 1), 0), t.bit_length() - 1)
    lo = lax.broadcasted_iota(jnp.int32, (rows, 128), 1) < 64
    def per_row(table_row, j, half):
        slab0 = SLABS_PER_KV * j
        col = jnp.full((rows, 1), sl_ref[table_row, 2 * (slab0 + SLABS_PER_KV - 1) + half], F32)
        for i in reversed(range(SLABS_PER_KV - 1)):
            col = jnp.where(blk == i, sl_ref[table_row, 2 * (slab0 + i) + half], col)
        return col

    scores = [_dot_nt(jnp.concatenate([q_slab(SLABS_PER_KV * j + i) for i in range(SLABS_PER_KV)], axis=0), kk[j])
              for j in range(A_KV_HEADS)]
    pps, scales = [], []
    for j, s in enumerate(scores):
        probs, invs = [], []
        for half in range(2):
            slope = per_row(0, j, half)
            sink = per_row(1, j, half)
            sh = s[:, half * 256:(half + 1) * 256] + slope * nb
            m = jnp.maximum(jnp.max(sh, axis=-1, keepdims=True), sink)
            pe = jnp.exp(sh - m)
            den = jnp.sum(pe, axis=-1, keepdims=True) + jnp.exp(sink - m)
            probs.append(pe)
            invs.append(1.0 / den)
        pps.append(jnp.concatenate(probs, axis=1).astype(BF16))
        scales.append(jnp.where(lo, invs[0], invs[1]))
    outs = []
    for j in range(A_KV_HEADS):
        o = jnp.dot(pps[j], vv[j], preferred_element_type=F32) * scales[j]
        outs += [o[i * t:(i + 1) * t] for i in range(SLABS_PER_KV)]
    return outs


def _spatial_gate_group(ws_ref, bs_ref, g, vn_g, rows):
    r = lax.broadcasted_iota(jnp.int32, (CHUNK, CHUNK), 0)
    c = lax.broadcasted_iota(jnp.int32, (CHUNK, CHUNK), 1)
    w = jnp.where(r >= c, ws_ref[g], 0.0).astype(BF16)
    return jnp.dot(w[:rows], vn_g, preferred_element_type=F32) + bs_ref[g][:rows]


EVEN_STEP_ROWS = 512
EVEN_SUBBLKS = EVEN_STEP_ROWS // WINDOW
EVEN_STEPS = N_PROMPT // EVEN_STEP_ROWS


def _even_fused_kernel(sl_ref, z_ref, kvprev_ref, ws_ref, bs_ref, lng_ref, lnb_ref,
                       w_ref, x_ref, g_ref, wq_ref, xo_ref, q_ref, mixed_ref):
    s_id = pl.program_id(0)
    slot = s_id % 2

    @pl.when(s_id == 0)
    def _():
        mixed_ref[1] = jnp.zeros(mixed_ref.shape[1:], mixed_ref.dtype)

    kc = D_MODEL // EVEN_SUBBLKS
    xo_ref[...] = x_ref[...]
    blk0 = jnp.minimum(s_id, EVEN_STEPS - 1) * EVEN_SUBBLKS
    for b in range(EVEN_SUBBLKS):
        xo_ref[...] += jnp.dot(mixed_ref[1 - slot, :, b * kc:(b + 1) * kc], w_ref[0, b * kc:(b + 1) * kc, :],
                               preferred_element_type=F32)
        rows = slice(b * WINDOW, (b + 1) * WINDOW)
        c = (blk0 + b) % (SEQ // WINDOW)
        kv_cur = z_ref[rows, OFF_K:OFF_K + 2 * A_KV_WIDTH]
        kv_prev = kvprev_ref[...] if b == 0 else z_ref[(b - 1) * WINDOW:b * WINDOW, OFF_K:OFF_K + 2 * A_KV_WIDTH]
        band = jnp.concatenate([kv_prev, kv_cur], axis=0)
        nbias = _band_bias(WINDOW, jnp.maximum(WINDOW - c * WINDOW, 0))

        def q_slab(p, rows=rows):
            return z_ref[rows, OFF_Q + p * 128:OFF_Q + (p + 1) * 128]

        for p, att in enumerate(_swa(q_slab, band[:, :A_KV_WIDTH], band[:, A_KV_WIDTH:], nbias, sl_ref)):
            ga = z_ref[rows, OFF_GA + p * 128:OFF_GA + (p + 1) * 128].astype(F32)
            mixed_ref[slot, rows, p * 128:(p + 1) * 128] = (_silu(ga) * att).astype(mixed_ref.dtype)

        vn = _standardize(z_ref[rows, OFF_VB:OFF_VB + B_WIDTH].astype(F32)) * lng_ref[...] + lnb_ref[...]
        for g in range(B_GROUPS):
            lo, hi = g * B_GROUP_DIM, (g + 1) * B_GROUP_DIM
            sg = _spatial_gate_group(ws_ref, bs_ref, g, vn[:, lo:hi].astype(BF16), CHUNK)
            u = z_ref[rows, OFF_U + lo:OFF_U + hi].astype(F32)
            gb = z_ref[rows, OFF_GB + lo:OFF_GB + hi].astype(F32)
            mixed_ref[slot, rows, A_WIDTH + lo:A_WIDTH + hi] = (_silu(gb) * (u * sg)).astype(mixed_ref.dtype)

    q_ref[...] = jnp.dot(_rms(xo_ref[...], g_ref[...]).astype(BF16), wq_ref[0],
                         preferred_element_type=F32).astype(q_ref.dtype)


def _even_fused(z, sl, w_s, b_s, lng, lnb, w_out_b, layer_i, x, g, wq_b, layer):
    ns = EVEN_STEPS
    r = EVEN_STEP_ROWS
    kv_blk = OFF_K // (2 * A_KV_WIDTH)

    def cur(s):
        return jnp.minimum(s, ns - 1)

    def prev(s):
        return jnp.maximum(s - 1, 0)

    once = pl.Buffered(1)
    return pl.pallas_call(
        _even_fused_kernel,
        grid=(ns + 1,),
        in_specs=[pl.BlockSpec(memory_space=pltpu.SMEM),
                  pl.BlockSpec((r, EVEN_IN), lambda s: (cur(s), 0)),
                  pl.BlockSpec((WINDOW, 2 * A_KV_WIDTH),
                               lambda s: (jnp.maximum(cur(s) * EVEN_SUBBLKS - 1, 0), kv_blk)),
                  pl.BlockSpec((B_GROUPS, CHUNK, CHUNK), lambda s: (0, 0, 0)),
                  pl.BlockSpec((B_GROUPS, CHUNK, 1), lambda s: (0, 0, 0)),
                  pl.BlockSpec((1, B_WIDTH), lambda s: (0, 0)),
                  pl.BlockSpec((1, B_WIDTH), lambda s: (0, 0)),
                  pl.BlockSpec((1, D_MODEL, D_MODEL), lambda s: (layer_i, 0, 0), pipeline_mode=once),
                  pl.BlockSpec((r, D_MODEL), lambda s: (prev(s), 0)),
                  pl.BlockSpec((1, D_MODEL), lambda s: (0, 0)),
                  pl.BlockSpec((1, D_MODEL, X_WIDTH), lambda s: (layer, 0, 0), pipeline_mode=once)],
        out_specs=[pl.BlockSpec((r, D_MODEL), lambda s: (prev(s), 0)),
                   pl.BlockSpec((r, X_WIDTH), lambda s: (prev(s), 0))],
        out_shape=[jax.ShapeDtypeStruct((N_PROMPT, D_MODEL), F32),
                   jax.ShapeDtypeStruct((N_PROMPT, X_WIDTH), BF16)],
        scratch_shapes=[pltpu.VMEM((2, r, D_MODEL), BF16)],
        compiler_params=_cp(1, FUSED_VMEM_LIMIT),
        name="even_fused",
    )(sl, z, z, w_s, b_s, lng, lnb, w_out_b, x, g.reshape(1, D_MODEL), wq_b)


SEQ_PER_STEP = 4
ROWS_PER_STEP = SEQ_PER_STEP * DEC_SEQ
SAMPLE_Q_ROWS = 16
assert DEC_SEQ & (DEC_SEQ - 1) == 0


def _seq_of(row_idx):
    return lax.shift_right_logical(row_idx, DEC_SEQ.bit_length() - 1)


def _even_sample_kernel(sl_ref, z_ref, ck_ref, cv_ref, ws_ref, bs_ref, lng_ref, lnb_ref,
                        o_ref, wk_ref, wv_ref, cvo_ref):
    nbias = _band_bias(SAMPLE_Q_ROWS, 0)
    pad_rows = jnp.zeros((SAMPLE_Q_ROWS - DEC_SEQ, EVEN_IN), F32)
    attn = [[None] * SEQ_PER_STEP for _ in range(A_HEADS // 2)]
    gate = [[None] * SEQ_PER_STEP for _ in range(B_GROUPS)]
    for r in range(SEQ_PER_STEP):
        r0, r1 = r * DEC_SEQ, (r + 1) * DEC_SEQ
        zr = z_ref[r0:r1, :]
        k_new = zr[:, OFF_K:OFF_K + A_KV_WIDTH]
        v_new = zr[:, OFF_V:OFF_V + A_KV_WIDTH]
        ck = ck_ref[r]
        cv = cv_ref[r]
        wk_ref[r, 0:WINDOW - DEC_SEQ, :] = ck[DEC_SEQ:, :]
        wk_ref[r, WINDOW - DEC_SEQ:, :] = k_new
        wv_ref[r, 0:WINDOW - DEC_SEQ, :] = cv[DEC_SEQ:, :]
        wv_ref[r, WINDOW - DEC_SEQ:, :] = v_new
        tail = jnp.zeros((WINDOW - DEC_SEQ, A_KV_WIDTH), F32)
        kband = jnp.concatenate([ck, k_new, tail], axis=0)
        vband = jnp.concatenate([cv, v_new, tail], axis=0)
        zq = jnp.concatenate([zr, pad_rows], axis=0)

        def q_slab(p, zq=zq):
            return zq[:, OFF_Q + p * 128:OFF_Q + (p + 1) * 128].astype(BF16)

        for p, a in enumerate(_swa(q_slab, kband, vband, nbias, sl_ref)):
            ga = zr[:, OFF_GA + p * 128:OFF_GA + (p + 1) * 128]
            attn[p][r] = _silu(ga) * a[:DEC_SEQ]

        vn = _standardize(zr[:, OFF_VB:OFF_VB + B_WIDTH]) * lng_ref[...] + lnb_ref[...]
        cvo_ref[r0:r1, :] = vn
        vn_pad = jnp.concatenate([vn, jnp.zeros((CHUNK - DEC_SEQ, B_WIDTH), F32)], axis=0).astype(BF16)
        for g in range(B_GROUPS):
            lo, hi = g * B_GROUP_DIM, (g + 1) * B_GROUP_DIM
            mixed = _spatial_gate_group(ws_ref, bs_ref, g, vn_pad[:, lo:hi], SAMPLE_Q_ROWS)[:DEC_SEQ]
            u = zr[:, OFF_U + lo:OFF_U + hi]
            gb = zr[:, OFF_GB + lo:OFF_GB + hi]
            gate[g][r] = _silu(gb) * (u * mixed)

    for p in range(A_HEADS // 2):
        o_ref[:, p * 128:(p + 1) * 128] = jnp.concatenate(attn[p], axis=0).astype(o_ref.dtype)
    for g in range(B_GROUPS):
        lo, hi = A_WIDTH + g * B_GROUP_DIM, A_WIDTH + (g + 1) * B_GROUP_DIM
        o_ref[:, lo:hi] = jnp.concatenate(gate[g], axis=0).astype(o_ref.dtype)


def _even_sample(zs, ck, cv, sl, w_s, b_s, lng, lnb):
    n_steps = DEC_BATCH // SEQ_PER_STEP
    win = jax.ShapeDtypeStruct((DEC_BATCH, WINDOW, A_KV_WIDTH), F32)
    return pl.pallas_call(
        _even_sample_kernel,
        grid=(n_steps,),
        in_specs=[pl.BlockSpec(memory_space=pltpu.SMEM),
                  pl.BlockSpec((ROWS_PER_STEP, EVEN_IN), lambda i: (i, 0)),
                  pl.BlockSpec((SEQ_PER_STEP, WINDOW, A_KV_WIDTH), lambda i: (i, 0, 0)),
                  pl.BlockSpec((SEQ_PER_STEP, WINDOW, A_KV_WIDTH), lambda i: (i, 0, 0)),
                  pl.BlockSpec((B_GROUPS, CHUNK, CHUNK), lambda i: (0, 0, 0)),
                  pl.BlockSpec((B_GROUPS, CHUNK, 1), lambda i: (0, 0, 0)),
                  pl.BlockSpec((1, B_WIDTH), lambda i: (0, 0)),
                  pl.BlockSpec((1, B_WIDTH), lambda i: (0, 0))],
        out_specs=[pl.BlockSpec((ROWS_PER_STEP, D_MODEL), lambda i: (i, 0)),
                   pl.BlockSpec((SEQ_PER_STEP, WINDOW, A_KV_WIDTH), lambda i: (i, 0, 0)),
                   pl.BlockSpec((SEQ_PER_STEP, WINDOW, A_KV_WIDTH), lambda i: (i, 0, 0)),
                   pl.BlockSpec((ROWS_PER_STEP, B_WIDTH), lambda i: (i, 0))],
        out_shape=[jax.ShapeDtypeStruct((N_SAMPLE, D_MODEL), BF16), win, win,
                   jax.ShapeDtypeStruct((N_SAMPLE, B_WIDTH), F32)],
        compiler_params=_cp(1),
        name="even_sample",
    )(sl, zs, ck, cv, w_s, b_s, lng, lnb)


RET_ROWS = 256
RET_BLKS_PER_SEQ = SEQ // RET_ROWS
RET_PROMPT_BLKS = N_PROMPT // RET_ROWS


def _gated_groupnorm(o, g):
    return (_silu(g.astype(F32)) * _standardize(o)).astype(BF16)


RET_LOOKAHEAD = 1


def _ret_fused_kernel(lg_ref, z_ref, w_ref, x_ref, g_ref, wq_ref, xo_ref, q_ref, so_ref, dm_ref, dc_ref):
    s_id = pl.program_id(0)
    t = RET_ROWS
    k_scale = R_KEY_DIM ** -0.5
    c = s_id % RET_BLKS_PER_SEQ

    @pl.when(s_id == 0)
    def _():
        row = lax.broadcasted_iota(jnp.int32, (t, t), 0)
        col = lax.broadcasted_iota(jnp.int32, (t, t), 1)
        rel = (row - col).astype(F32)
        ti = lax.broadcasted_iota(jnp.int32, (t, 1), 0).astype(F32)
        for h in range(R_HEADS):
            lg = lg_ref[h]
            dm_ref[h] = jnp.where(rel >= 0, jnp.exp(jnp.maximum(rel, 0.0) * lg), 0.0) * k_scale
            dc_ref[h, 0] = jnp.exp((ti + 1.0) * lg)
            dc_ref[h, 1] = jnp.exp((t - 1.0 - ti) * lg) * k_scale

    @pl.when(c == 0)
    def _():
        so_ref[...] = jnp.zeros_like(so_ref)

    def head_matmuls(h):
        lg = lg_ref[h]
        q_decay = dc_ref[h, 0]
        k_decay = dc_ref[h, 1]
        chunk_decay = jnp.exp(jnp.full((1, R_VAL_DIM), float(t), F32) * lg)
        q = z_ref[:, h * R_KEY_DIM:(h + 1) * R_KEY_DIM]
        k = z_ref[:, R_K_WIDTH + h * R_KEY_DIM:R_K_WIDTH + (h + 1) * R_KEY_DIM]
        v = z_ref[:, 2 * R_K_WIDTH + h * R_VAL_DIM:2 * R_K_WIDTH + (h + 1) * R_VAL_DIM]
        st = so_ref[0, h]
        scores = _dot_nt(q, k) * dm_ref[h]
        inner = jnp.dot(scores.astype(BF16), v, preferred_element_type=F32)
        cross = jnp.dot(q, st.astype(BF16), preferred_element_type=F32) * q_decay
        kd_t = (k.astype(F32) * k_decay).T.astype(BF16)
        so_ref[0, h] = chunk_decay * st + jnp.dot(kd_t, v, preferred_element_type=F32)
        return inner + cross

    x1 = x_ref[...]
    ahead = [head_matmuls(h) for h in range(RET_LOOKAHEAD)]
    for h in range(R_HEADS):
        if h + RET_LOOKAHEAD < R_HEADS:
            ahead.append(head_matmuls(h + RET_LOOKAHEAD))
        raw = ahead.pop(0)
        g = z_ref[:, 2 * R_K_WIDTH + R_V_WIDTH + h * R_VAL_DIM:2 * R_K_WIDTH + R_V_WIDTH + (h + 1) * R_VAL_DIM]
        x1 = x1 + jnp.dot(_gated_groupnorm(raw, g), w_ref[0, h * R_VAL_DIM:(h + 1) * R_VAL_DIM, :],
                          preferred_element_type=F32)
    xo_ref[...] = x1
    q_ref[...] = jnp.dot(_rms(x1, g_ref[...]).astype(BF16), wq_ref[0],
                         preferred_element_type=F32).astype(q_ref.dtype)


def _ret_fused(z, log_g, w_out_b, layer_i, x, g, wq_b, layer):
    nb = RET_PROMPT_BLKS
    r = RET_ROWS
    once = pl.Buffered(1)
    return pl.pallas_call(
        _ret_fused_kernel,
        grid=(nb,),
        in_specs=[pl.BlockSpec(memory_space=pltpu.SMEM),
                  pl.BlockSpec((r, RET_IN), lambda s: (s, 0)),
                  pl.BlockSpec((1, R_V_WIDTH, D_MODEL), lambda s: (layer_i, 0, 0), pipeline_mode=once),
                  pl.BlockSpec((r, D_MODEL), lambda s: (s, 0)),
                  pl.BlockSpec((1, D_MODEL), lambda s: (0, 0)),
                  pl.BlockSpec((1, D_MODEL, X_WIDTH), lambda s: (layer, 0, 0), pipeline_mode=once)],
        out_specs=[pl.BlockSpec((r, D_MODEL), lambda s: (s, 0)),
                   pl.BlockSpec((r, X_WIDTH), lambda s: (s, 0)),
                   pl.BlockSpec((1, R_HEADS, R_KEY_DIM, R_VAL_DIM),
                                lambda s: (s // RET_BLKS_PER_SEQ, 0, 0, 0))],
        out_shape=[jax.ShapeDtypeStruct((N_PROMPT, D_MODEL), F32),
                   jax.ShapeDtypeStruct((N_PROMPT, X_WIDTH), BF16),
                   jax.ShapeDtypeStruct((BATCH, R_HEADS, R_KEY_DIM, R_VAL_DIM), F32)],
        scratch_shapes=[pltpu.VMEM((R_HEADS, r, r), F32),
                        pltpu.VMEM((R_HEADS, 2, r, 1), F32)],
        compiler_params=_cp(1, FUSED_VMEM_LIMIT),
        name="ret_fused",
    )(log_g, z, w_out_b, x, g.reshape(1, D_MODEL), wq_b)


RET_SAMPLE_SEQS = 8
RET_SAMPLE_ROWS = RET_SAMPLE_SEQS * DEC_SEQ


def _ret_sample_kernel(lg_ref, q_ref, k_ref, v_ref, g_ref, s_ref, *rest):
    o_ref, so_ref = rest[-2:]
    h = pl.program_id(1)
    lg = lg_ref[h]
    n = RET_SAMPLE_ROWS
    t = DEC_SEQ
    row = lax.broadcasted_iota(jnp.int32, (n, n), 0)
    col = lax.broadcasted_iota(jnp.int32, (n, n), 1)
    rel = (row - col).astype(F32)
    same = _seq_of(row) == _seq_of(col)
    k_scale = R_KEY_DIM ** -0.5
    dmat = jnp.where(same & (rel >= 0), jnp.exp(jnp.maximum(rel, 0.0) * lg), 0.0) * k_scale
    ri = lax.broadcasted_iota(jnp.int32, (n, 1), 0)
    ti = (ri & (t - 1)).astype(F32)
    q_decay = jnp.exp((ti + 1.0) * lg)
    k_decay = jnp.exp((t - 1.0 - ti) * lg) * k_scale
    step_decay = jnp.exp(jnp.full((1, R_VAL_DIM), float(t), F32) * lg)

    q = q_ref[...]
    k = k_ref[...]
    v = v_ref[...]
    scores = _dot_nt(q, k) * dmat
    o = jnp.dot(scores.astype(BF16), v, preferred_element_type=F32)
    kd = k.astype(F32) * k_decay
    k_pad = jnp.zeros((R_CHUNK - n, R_KEY_DIM), F32)
    v_pad = jnp.concatenate([v.astype(F32), jnp.zeros((R_CHUNK - n, R_VAL_DIM), F32)], axis=0).astype(BF16)
    for r in range(RET_SAMPLE_SEQS):
        mine = _seq_of(ri) == r
        st = s_ref[0, r, 0]
        cross = jnp.dot(q, st.astype(BF16), preferred_element_type=F32) * q_decay
        o = o + jnp.where(mine, cross, 0.0)
        kd_t = jnp.concatenate([jnp.where(mine, kd, 0.0), k_pad], axis=0).T.astype(BF16)
        so_ref[0, r, 0] = step_decay * st + jnp.dot(kd_t, v_pad, preferred_element_type=F32)
    o_ref[...] = _gated_groupnorm(o, g_ref[...])


def _ret_sample(z, state_all, idx, log_g, new_states):
    n_steps = DEC_BATCH // RET_SAMPLE_SEQS
    rb0 = N_PROMPT // RET_SAMPLE_ROWS
    kcol = R_K_WIDTH // R_KEY_DIM
    vcol = 2 * R_K_WIDTH // R_VAL_DIM
    gcol = vcol + R_HEADS
    st_spec = pl.BlockSpec((1, RET_SAMPLE_SEQS, 1, R_KEY_DIM, R_VAL_DIM), lambda i, h: (idx, i, h, 0, 0))
    in_specs = [pl.BlockSpec(memory_space=pltpu.SMEM),
                pl.BlockSpec((RET_SAMPLE_ROWS, R_KEY_DIM), lambda i, h: (rb0 + i, h)),
                pl.BlockSpec((RET_SAMPLE_ROWS, R_KEY_DIM), lambda i, h: (rb0 + i, kcol + h)),
                pl.BlockSpec((RET_SAMPLE_ROWS, R_VAL_DIM), lambda i, h: (rb0 + i, vcol + h)),
                pl.BlockSpec((RET_SAMPLE_ROWS, R_VAL_DIM), lambda i, h: (rb0 + i, gcol + h)),
                st_spec]
    args = [log_g, z, z, z, z, state_all]
    aliases = {}
    if new_states is not None:
        in_specs.append(pl.BlockSpec(memory_space=pl.ANY))
        args.append(new_states)
        aliases = {len(args) - 1: 1}
    return pl.pallas_call(
        _ret_sample_kernel,
        grid=(n_steps, R_HEADS),
        in_specs=in_specs,
        out_specs=[pl.BlockSpec((RET_SAMPLE_ROWS, R_VAL_DIM), lambda i, h: (i, h)), st_spec],
        out_shape=[jax.ShapeDtypeStruct((N_SAMPLE, R_V_WIDTH), BF16),
                   jax.ShapeDtypeStruct(state_all.shape, F32)],
        input_output_aliases=aliases,
        compiler_params=_cp(2),
        name="ret_sample",
    )(*args)


XQ_ROWS = 512
XQ_PROMPT_BLKS = N_PROMPT // XQ_ROWS
XQ_BLKS_PER_SEQ = SEQ // XQ_ROWS


def _xattn_heads(q, mk_heads, mv_heads):
    pairs = [(i, h) for i in range(len(mk_heads)) for h in range(X_HEADS)]
    scores = [_dot_nt(q[:, h * X_HEAD_DIM:(h + 1) * X_HEAD_DIM], mk_heads[i](h)) * (X_HEAD_DIM ** -0.5)
              for i, h in pairs]
    probs, invs = [], []
    for s in scores:
        p = jnp.exp(s - jnp.max(s, axis=-1, keepdims=True))
        probs.append(p.astype(BF16))
        invs.append(1.0 / jnp.sum(p, axis=-1, keepdims=True))
    outs = [jnp.dot(p, mv_heads[i](h), preferred_element_type=F32) * inv
            for (i, h), p, inv in zip(pairs, probs, invs)]
    return [outs[i * X_HEADS:(i + 1) * X_HEADS] for i in range(len(mk_heads))]


def _xattn_proj_kernel(final, q_ref, mk_ref, mv_ref, os_ref, w_ref, xp_ref, xs_ref, g_ref, out_a, out_b):
    s_id = pl.program_id(0)

    def project(o, x_ref):
        return x_ref[...] + jnp.dot(o, w_ref[0], preferred_element_type=F32)

    @pl.when(s_id < XQ_PROMPT_BLKS)
    def _():
        def head_of(ref):
            return lambda h: ref[0, :, h * X_HEAD_DIM:(h + 1) * X_HEAD_DIM].astype(BF16)

        outs, = _xattn_heads(q_ref[...], [head_of(mk_ref)], [head_of(mv_ref)])
        x2 = project(jnp.concatenate(outs, axis=1).astype(BF16), xp_ref)
        if final:
            out_a[...] = _rms(x2, g_ref[...])
        else:
            out_a[...] = x2
            out_b[...] = _rms(x2, g_ref[...]).astype(out_b.dtype)

    @pl.when(s_id >= XQ_PROMPT_BLKS)
    def _():
        x2 = project(os_ref[...], xs_ref)
        if final:
            out_b[...] = _rms(x2, g_ref[...])
        else:
            out_a[0:N_SAMPLE, :] = x2
            out_b[0:N_SAMPLE, :] = _rms(x2, g_ref[...]).astype(out_b.dtype)


def _xattn_proj(q, mk_all, mv_all, layer, o_sample, w_stack, x_prompt, x_sample, g, final):
    nb = XQ_PROMPT_BLKS
    d = w_stack.shape[2]
    xs_blk = (x_sample.shape[0] - N_SAMPLE) // N_SAMPLE

    def rb(s):
        return jnp.minimum(s, nb - 1)

    mem_spec = pl.BlockSpec((1, MEM_LEN, X_WIDTH), lambda s: (layer, rb(s) // XQ_BLKS_PER_SEQ, 0))
    if final:
        out_specs = [pl.BlockSpec((XQ_ROWS, d), lambda s: (rb(s), 0)),
                     pl.BlockSpec((N_SAMPLE, d), lambda s: (0, 0))]
        out_shape = [jax.ShapeDtypeStruct((N_PROMPT, d), F32), jax.ShapeDtypeStruct((N_SAMPLE, d), F32)]
    else:
        out_specs = [pl.BlockSpec((XQ_ROWS, d), lambda s: (s, 0)),
                     pl.BlockSpec((XQ_ROWS, d), lambda s: (s, 0))]
        out_shape = [jax.ShapeDtypeStruct((N_ROWS, d), F32), jax.ShapeDtypeStruct((N_ROWS, d), BF16)]
    return pl.pallas_call(
        functools.partial(_xattn_proj_kernel, final),
        grid=(nb + 1,),
        in_specs=[pl.BlockSpec((XQ_ROWS, X_WIDTH), lambda s: (rb(s), 0)),
                  mem_spec, mem_spec,
                  pl.BlockSpec((N_SAMPLE, X_WIDTH), lambda s: (0, 0)),
                  pl.BlockSpec((1, X_WIDTH, d), lambda s: (layer, 0, 0)),
                  pl.BlockSpec((XQ_ROWS, d), lambda s: (rb(s), 0)),
                  pl.BlockSpec((N_SAMPLE, d), lambda s: (xs_blk, 0)),
                  pl.BlockSpec((1, d), lambda s: (0, 0))],
        out_specs=out_specs,
        out_shape=out_shape,
        compiler_params=_cp(1),
        name="xattn_proj",
    )(q, mk_all, mv_all, o_sample, w_stack, x_prompt, x_sample, g.reshape(1, d))


XS_SEQS = 4
XS_ROWS = XS_SEQS * DEC_SEQ


def _xattn_sample_kernel(q_ref, ck_ref, cv_ref, o_ref):
    q = q_ref[...]
    ri = lax.broadcasted_iota(jnp.int32, (XS_ROWS, 1), 0)
    def head_of(ref, r):
        return lambda h: ref[0, r, pl.ds(h, MEM_LEN, stride=X_HEADS), :].astype(BF16)

    per_seq = _xattn_heads(q, [head_of(ck_ref, r) for r in range(XS_SEQS)],
                           [head_of(cv_ref, r) for r in range(XS_SEQS)])
    acc = [jnp.zeros((XS_ROWS, X_HEAD_DIM), F32) for _ in range(X_HEADS)]
    for r, outs in enumerate(per_seq):
        mine = _seq_of(ri) == r
        acc = [a + jnp.where(mine, o, 0.0) for a, o in zip(acc, outs)]
    for h, a in enumerate(acc):
        o_ref[:, h * X_HEAD_DIM:(h + 1) * X_HEAD_DIM] = a.astype(o_ref.dtype)


def _xattn_sample(q, ck_all, cv_all, layer):
    n_steps = DEC_BATCH // XS_SEQS
    rb0 = (q.shape[0] - N_SAMPLE) // XS_ROWS
    mem_spec = pl.BlockSpec((1, XS_SEQS, MEM_LEN * X_HEADS, X_HEAD_DIM), lambda i: (layer, i, 0, 0))
    return pl.pallas_call(
        _xattn_sample_kernel,
        grid=(n_steps,),
        in_specs=[pl.BlockSpec((XS_ROWS, X_WIDTH), lambda i: (rb0 + i, 0)), mem_spec, mem_spec],
        out_specs=pl.BlockSpec((XS_ROWS, X_WIDTH), lambda i: (i, 0)),
        out_shape=jax.ShapeDtypeStruct((N_SAMPLE, X_WIDTH), BF16),
        compiler_params=_cp(1),
        name="xattn_sample",
    )(q, ck_all, cv_all)


def kernel(x_prompt, x_sample, cache_win_k, cache_win_v, state_ret, cache_mem_k, cache_mem_v, mem_prompt,
           norm_mix, w_in_even, attn_sinks, w_spatial, b_spatial, norm_v_g, norm_v_b, w_out_even,
           w_in_ret, w_out_ret, norm_cross, norm_mem, w_xq, w_xk, w_xv, w_xo, norm_final):
    slopes = jnp.exp2(-8.0 * (jnp.arange(A_HEADS, dtype=F32) + 1.0) / A_HEADS)
    log_g = jnp.log1p(-jnp.exp2(-5.0 - jnp.arange(R_HEADS, dtype=F32)))

    def mem_rows(cache):
        return cache.reshape(DEPTH, DEC_BATCH, MEM_LEN * X_HEADS, X_HEAD_DIM)

    mk_all, mv_all, mk_b, mv_b, w_xq_b, w_xo_b = _memkv(mem_prompt.reshape(BATCH * MEM_LEN, D_MODEL), norm_mem,
                                                        w_xk, w_xv, w_xq, w_xo)

    xp_src = x_prompt.reshape(N_PROMPT, D_MODEL)
    xs_src = x_sample.reshape(N_SAMPLE, D_MODEL)
    xn = _embed(xp_src, xs_src, norm_mix[0])
    wkp, wvp, rsp, wks, wvs, cvs = [], [], [], [], [], []
    rss = None
    for layer in range(DEPTH):
        i = layer // 2
        xs_blk0 = (xs_src.shape[0] - N_SAMPLE) // N_SAMPLE
        if layer % 2 == 0:
            sl = jnp.stack([slopes, attn_sinks[i].astype(F32)])
            b_s = b_spatial[i].reshape(B_GROUPS, CHUNK, 1)
            lng = norm_v_g[i].reshape(1, B_WIDTH)
            lnb = norm_v_b[i].reshape(1, B_WIDTH)
            z, zs, w_out_b = _in_even(xn, w_in_even, i, w_out_even)
            kv_tail = _kv_tail(xn, w_in_even, i)
            wkp.append(kv_tail[:, :A_KV_WIDTH].reshape(BATCH, WINDOW, A_KV_HEADS, A_HEAD_DIM))
            wvp.append(kv_tail[:, A_KV_WIDTH:].reshape(BATCH, WINDOW, A_KV_HEADS, A_HEAD_DIM))
            ck = cache_win_k[i].reshape(DEC_BATCH, WINDOW, A_KV_WIDTH)
            cv = cache_win_v[i].reshape(DEC_BATCH, WINDOW, A_KV_WIDTH)
            mixed_s, wk_new, wv_new, cv_new = _even_sample(zs, ck, cv, sl, w_spatial[i], b_s, lng, lnb)
            wks.append(wk_new.reshape(DEC_BATCH, WINDOW, A_KV_HEADS, A_HEAD_DIM))
            wvs.append(wv_new.reshape(DEC_BATCH, WINDOW, A_KV_HEADS, A_HEAD_DIM))
            cvs.append(cv_new.reshape(DEC_BATCH, DEC_SEQ, B_WIDTH))
            x_p, q = _even_fused(z, sl, w_spatial[i], b_s, lng, lnb, w_out_b, 0, xp_src, norm_cross[layer],
                                 w_xq_b, layer)
            x_s, q_s = _mm_res_q(mixed_s, w_out_b, 0, xs_src, norm_cross[layer], w_xq_b, layer,
                                 N_SAMPLE, SAMPLE_PROJ_TK, x_blk0=xs_blk0)
        else:
            z, w_out_b = _mm_wcast(xn, w_in_ret, i, BF16, IN_RET_TM, IN_RET_TN, w_out_ret, i, "in_ret")
            o_s, rss = _ret_sample(z, state_ret, i, log_g, rss)
            x_p, q, st_p = _ret_fused(z, log_g, w_out_b, 0, xp_src, norm_cross[layer], w_xq_b, layer)
            rsp.append(st_p)
            x_s, q_s = _mm_res_q(o_s, w_out_b, 0, xs_src, norm_cross[layer], w_xq_b, layer,
                                 N_SAMPLE, SAMPLE_PROJ_TK, x_blk0=xs_blk0)
        o_s = _xattn_sample(q_s, mem_rows(cache_mem_k), mem_rows(cache_mem_v), layer)
        if layer + 1 < DEPTH:
            x, xn = _xattn_proj(q, mk_b, mv_b, layer, o_s, w_xo_b, x_p, x_s, norm_mix[layer + 1], False)
            xp_src = xs_src = x
        else:
            y_prompt, y_sample = _xattn_proj(q, mk_b, mv_b, layer, o_s, w_xo_b, x_p, x_s, norm_final, True)

    y_prompt = y_prompt.reshape(BATCH, SEQ, D_MODEL)
    y_sample = y_sample.reshape(DEC_BATCH, DEC_SEQ, D_MODEL)
    mem_shape = (DEPTH, BATCH, MEM_LEN, X_HEADS, X_HEAD_DIM)
    return (y_prompt, y_sample,
            jnp.stack(wkp), jnp.stack(wvp), jnp.stack(rsp),
            mk_all.reshape(mem_shape), mv_all.reshape(mem_shape),
            jnp.stack(wks), jnp.stack(wvs), rss, jnp.stack(cvs))
```

```python
import functools

import jax
import jax.numpy as jnp
from jax import lax
from jax.experimental import pallas as pl
from jax.experimental.pallas import tpu as pltpu

F32 = jnp.float32
BF16 = jnp.bfloat16

D_MODEL = 2048
BATCH = 2
SEQ = 4096
DEPTH = 4
DEC_BATCH = 32
DEC_SEQ = 8
A_HEADS = 16
A_KV_HEADS = 2
A_HEAD_DIM = 64
A_WIDTH = 1024
A_KV_WIDTH = 128
WINDOW = 128
CHUNK = 128
B_GROUPS = 4
B_WIDTH = 1024
B_GROUP_DIM = 256
EVEN_IN = 5376
R_HEADS = 8
R_KEY_DIM = 256
R_VAL_DIM = 512
R_K_WIDTH = 2048
R_V_WIDTH = 4096
R_CHUNK = 128
RET_IN = 12288
MEM_LEN = 256
X_HEADS = 4
X_HEAD_DIM = 128
X_WIDTH = 512
EPS = 1e-6

N_PROMPT = BATCH * SEQ
N_SAMPLE = DEC_BATCH * DEC_SEQ
N_ROWS = N_PROMPT + N_SAMPLE

OFF_Q, OFF_K, OFF_V, OFF_GA, OFF_U, OFF_VB, OFF_GB = 0, 1024, 1152, 1280, 2304, 3328, 4352

MIB = 1024 * 1024
V7X_VMEM_BYTES = 64 * MIB
VMEM_LIMIT = V7X_VMEM_BYTES - 12 * MIB
FUSED_VMEM_LIMIT = V7X_VMEM_BYTES - 8 * MIB
SAMPLE_PROJ_TK = 2048


def _cp(n_axes, vmem_limit=VMEM_LIMIT):
    return pltpu.CompilerParams(dimension_semantics=("arbitrary",) * n_axes,
                                vmem_limit_bytes=vmem_limit)


def _silu(x):
    return x * (1.0 / (1.0 + jnp.exp(-x)))


def _rms(x, g):
    ms = jnp.mean(x * x, axis=-1, keepdims=True)
    return x * lax.rsqrt(ms + EPS) * g


def _standardize(x):
    xc = x - jnp.mean(x, axis=-1, keepdims=True)
    return xc * lax.rsqrt(jnp.mean(xc * xc, axis=-1, keepdims=True) + EPS)


def _dot_nt(a, b):
    return lax.dot_general(a, b, (((1,), (1,)), ((), ())), preferred_element_type=F32)


SIDE_CAST_ROWS = 64


def _side_cast_specs(side_stack, side_layer, n_inner, n_steps):
    _, k, d = side_stack.shape
    n_slabs = k // SIDE_CAST_ROWS
    assert n_slabs <= n_steps

    def slab(j, i):
        return jnp.minimum(j * n_inner + i, n_slabs - 1)

    return (pl.BlockSpec((1, SIDE_CAST_ROWS, d), lambda j, i: (side_layer, slab(j, i), 0)),
            pl.BlockSpec((1, SIDE_CAST_ROWS, d), lambda j, i: (0, slab(j, i), 0)),
            jax.ShapeDtypeStruct((1, k, d), BF16))


def _mm_wcast_kernel(a_ref, w_ref, side_ref, o_ref, side_o_ref, wb_ref):
    @pl.when(pl.program_id(1) == 0)
    def _():
        wb_ref[...] = w_ref[0].astype(BF16)

    o_ref[...] = jnp.dot(a_ref[...], wb_ref[...], preferred_element_type=F32).astype(o_ref.dtype)
    side_o_ref[...] = side_ref[...].astype(BF16)


def _mm_wcast(a, w_stack, layer, out_dtype, tm, tn, side_stack, side_layer, name):
    m, k = a.shape
    n = w_stack.shape[2]
    grid = (n // tn, m // tm)
    side_in, side_out, side_shape = _side_cast_specs(side_stack, side_layer, grid[1], grid[0] * grid[1])
    return pl.pallas_call(
        _mm_wcast_kernel,
        grid=grid,
        in_specs=[pl.BlockSpec((tm, k), lambda j, i: (i, 0)),
                  pl.BlockSpec((1, k, tn), lambda j, i: (layer, 0, j)),
                  side_in],
        out_specs=[pl.BlockSpec((tm, tn), lambda j, i: (i, j)), side_out],
        out_shape=[jax.ShapeDtypeStruct((m, n), out_dtype), side_shape],
        scratch_shapes=[pltpu.VMEM((k, tn), BF16)],
        compiler_params=_cp(2),
        name=name,
    )(a, w_stack, side_stack)


IN_EVEN_TM = 1408
IN_EVEN_TN = 768
IN_RET_TM = 1408
IN_RET_TN = 1024


def _in_even_kernel(a_ref, as_ref, w_ref, side_ref, z_ref, zs_ref, side_o_ref, wb_ref):
    @pl.when(pl.program_id(1) == 0)
    def _():
        wb_ref[...] = w_ref[0].astype(BF16)
        zs_ref[...] = jnp.dot(as_ref[...], wb_ref[...], preferred_element_type=F32)

    z_ref[...] = jnp.dot(a_ref[...], wb_ref[...], preferred_element_type=F32).astype(z_ref.dtype)
    side_o_ref[...] = side_ref[...].astype(BF16)


def _in_even(xn, w_stack, layer, side_stack):
    m, k = xn.shape
    n = w_stack.shape[2]
    tm, tn = IN_EVEN_TM, IN_EVEN_TN
    grid = (n // tn, m // tm)
    side_in, side_out, side_shape = _side_cast_specs(side_stack, layer, grid[1], grid[0] * grid[1])
    return pl.pallas_call(
        _in_even_kernel,
        grid=grid,
        in_specs=[pl.BlockSpec((tm, k), lambda j, i: (i, 0)),
                  pl.BlockSpec((N_SAMPLE, k), lambda j, i: (N_PROMPT // N_SAMPLE, 0)),
                  pl.BlockSpec((1, k, tn), lambda j, i: (layer, 0, j)),
                  side_in],
        out_specs=[pl.BlockSpec((tm, tn), lambda j, i: (i, j)),
                   pl.BlockSpec((N_SAMPLE, tn), lambda j, i: (0, j)),
                   side_out],
        out_shape=[jax.ShapeDtypeStruct((m, n), BF16),
                   jax.ShapeDtypeStruct((N_SAMPLE, n), F32),
                   side_shape],
        scratch_shapes=[pltpu.VMEM((k, tn), BF16)],
        compiler_params=_cp(2),
        name="in_even",
    )(xn, xn, w_stack, side_stack)


def _kv_tail_kernel(a_ref, w_ref, o_ref):
    o_ref[...] = jnp.dot(a_ref[...], w_ref[0].astype(BF16), preferred_element_type=F32)


def _kv_tail(xn, w_stack, layer):
    k = xn.shape[1]
    blks_per_seq = SEQ // WINDOW
    width = 2 * A_KV_WIDTH
    return pl.pallas_call(
        _kv_tail_kernel,
        grid=(BATCH,),
        in_specs=[pl.BlockSpec((WINDOW, k), lambda b: (blks_per_seq * (b + 1) - 1, 0)),
                  pl.BlockSpec((1, k, width), lambda b: (layer, 0, OFF_K // width))],
        out_specs=pl.BlockSpec((WINDOW, width), lambda b: (b, 0)),
        out_shape=jax.ShapeDtypeStruct((BATCH * WINDOW, width), F32),
        compiler_params=_cp(1),
        name="kv_tail",
    )(xn, w_stack)


def _mm_res_q_kernel(nk, a_ref, w_ref, x_ref, g_ref, wq_ref, xo_ref, q_ref):
    k = pl.program_id(1)

    @pl.when(k == 0)
    def _():
        xo_ref[...] = x_ref[...]

    xo_ref[...] += jnp.dot(a_ref[...], w_ref[0], preferred_element_type=F32)

    @pl.when(k == nk - 1)
    def _():
        xn = _rms(xo_ref[...], g_ref[...]).astype(BF16)
        q_ref[...] = jnp.dot(xn, wq_ref[0].astype(BF16), preferred_element_type=F32).astype(q_ref.dtype)


def _mm_res_q(a, w_stack, layer, x, g, wq_stack, wq_layer, tm, tk, x_blk0=0):
    m, kdim = a.shape
    d = w_stack.shape[2]
    nq = wq_stack.shape[2]
    nk = kdim // tk
    return pl.pallas_call(
        functools.partial(_mm_res_q_kernel, nk),
        grid=(m // tm, nk),
        in_specs=[pl.BlockSpec((tm, tk), lambda i, k: (i, k)),
                  pl.BlockSpec((1, tk, d), lambda i, k: (layer, k, 0)),
                  pl.BlockSpec((tm, d), lambda i, k: (x_blk0 + i, 0)),
                  pl.BlockSpec((1, d), lambda i, k: (0, 0)),
                  pl.BlockSpec((1, d, nq), lambda i, k: (wq_layer, 0, 0))],
        out_specs=[pl.BlockSpec((tm, d), lambda i, k: (i, 0)),
                   pl.BlockSpec((tm, nq), lambda i, k: (i, 0))],
        out_shape=[jax.ShapeDtypeStruct((m, d), F32),
                   jax.ShapeDtypeStruct((m, nq), BF16)],
        compiler_params=_cp(2),
        name="mm_res_q",
    )(a, w_stack, x, g.reshape(1, d), wq_stack)


def _embed_kernel(n_prompt_blks, xp_ref, xs_ref, g_ref, n_ref):
    i = pl.program_id(0)

    @pl.when(i < n_prompt_blks)
    def _():
        n_ref[...] = _rms(xp_ref[...], g_ref[...]).astype(n_ref.dtype)

    @pl.when(i >= n_prompt_blks)
    def _():
        n_ref[...] = _rms(xs_ref[...], g_ref[...]).astype(n_ref.dtype)


def _embed(xp, xs, g):
    d = xp.shape[1]
    tm = N_SAMPLE
    nb = N_PROMPT // tm
    return pl.pallas_call(
        functools.partial(_embed_kernel, nb),
        grid=(nb + 1,),
        in_specs=[pl.BlockSpec((tm, d), lambda i: (jnp.minimum(i, nb - 1), 0)),
                  pl.BlockSpec((tm, d), lambda i: (0, 0)),
                  pl.BlockSpec((1, d), lambda i: (0, 0))],
        out_specs=pl.BlockSpec((tm, d), lambda i: (i, 0)),
        out_shape=jax.ShapeDtypeStruct((N_ROWS, d), BF16),
        compiler_params=_cp(1),
        name="embed",
    )(xp, xs, g.reshape(1, d))


def _memkv_kernel(mem_ref, g_ref, wk_ref, wv_ref, wq_ref, wo_ref,
                  mk_ref, mv_ref, mkb_ref, mvb_ref, wqb_ref, wob_ref):
    h = _rms(mem_ref[...], g_ref[0]).astype(BF16)
    mk = jnp.dot(h, wk_ref[0].astype(BF16), preferred_element_type=F32)
    mv = jnp.dot(h, wv_ref[0].astype(BF16), preferred_element_type=F32)
    mk_ref[0] = mk
    mv_ref[0] = mv
    mkb_ref[0] = mk.astype(BF16)
    mvb_ref[0] = mv.astype(BF16)
    wqb_ref[...] = wq_ref[...].astype(BF16)
    wob_ref[...] = wo_ref[...].astype(BF16)


def _memkv(mem, norm_mem, w_xk, w_xv, w_xq, w_xo):
    m = mem.shape[0]
    kv_spec = pl.BlockSpec((1, m, X_WIDTH), lambda l: (l, 0, 0))
    w_in_spec = pl.BlockSpec((1, D_MODEL, X_WIDTH), lambda l: (l, 0, 0))
    w_out_spec = pl.BlockSpec((1, X_WIDTH, D_MODEL), lambda l: (l, 0, 0))
    return pl.pallas_call(
        _memkv_kernel,
        grid=(DEPTH,),
        in_specs=[pl.BlockSpec((m, D_MODEL), lambda l: (0, 0)),
                  pl.BlockSpec((1, 1, D_MODEL), lambda l: (l, 0, 0)),
                  w_in_spec, w_in_spec, w_in_spec, w_out_spec],
        out_specs=[kv_spec, kv_spec, kv_spec, kv_spec, w_in_spec, w_out_spec],
        out_shape=[jax.ShapeDtypeStruct((DEPTH, m, X_WIDTH), F32),
                   jax.ShapeDtypeStruct((DEPTH, m, X_WIDTH), F32),
                   jax.ShapeDtypeStruct((DEPTH, m, X_WIDTH), BF16),
                   jax.ShapeDtypeStruct((DEPTH, m, X_WIDTH), BF16),
                   jax.ShapeDtypeStruct(w_xq.shape, BF16),
                   jax.ShapeDtypeStruct(w_xo.shape, BF16)],
        compiler_params=_cp(1),
        name="memkv",
    )(mem, norm_mem.reshape(DEPTH, 1, D_MODEL), w_xk, w_xv, w_xq, w_xo)


SLABS_PER_KV = A_HEADS // A_KV_HEADS // 2
MASK_DIST = 1e30 * 2.0 ** 8


def _expand_band(band, scale):
    x = band.astype(F32)
    if scale != 1.0:
        x = x * scale
    xs = pltpu.roll(x, 64, 1)
    lo = lax.broadcasted_iota(jnp.int32, x.shape, 1) < 64
    z = jnp.zeros_like(x)
    e0 = jnp.concatenate([jnp.where(lo, x, z), jnp.where(lo, z, xs)], axis=0).astype(BF16)
    e1 = jnp.concatenate([jnp.where(lo, xs, z), jnp.where(lo, z, x)], axis=0).astype(BF16)
    return e0, e1


def _band_bias(t, first_key):
    row = lax.broadcasted_iota(jnp.int32, (t, 2 * WINDOW), 0)
    col = lax.broadcasted_iota(jnp.int32, (t, 2 * WINDOW), 1)
    dist = row + WINDOW - col
    valid = (dist >= 0) & (dist < WINDOW) & (col >= first_key)
    return jnp.where(valid, -dist.astype(F32), -MASK_DIST)


def _swa(q_slab, kband, vband, nbias, sl_ref):
    t = nbias.shape[0]
    rows = SLABS_PER_KV * t
    kk = _expand_band(kband, A_HEAD_DIM ** -0.5)
    vv = _expand_band(vband, 1.0)
    nb = jnp.concatenate([nbias] * SLABS_PER_KV, axis=0)
    blk = lax.shift_right_logical(lax.broadcasted_iota(jnp.int32, (rows, 1), 0), t.bit_length() - 1)
    lo = lax.broadcasted_iota(jnp.int32, (rows, 128), 1) < 64
    def per_row(table_row, j, half):
        slab0 = SLABS_PER_KV * j
        col = jnp.full((rows, 1), sl_ref[table_row, 2 * (slab0 + SLABS_PER_KV - 1) + half], F32)
        for i in reversed(range(SLABS_PER_KV - 1)):
            col = jnp.where(blk == i, sl_ref[table_row, 2 * (slab0 + i) + half], col)
        return col

    scores = [_dot_nt(jnp.concatenate([q_slab(SLABS_PER_KV * j + i) for i in range(SLABS_PER_KV)], axis=0), kk[j])
              for j in range(A_KV_HEADS)]
    pps, scales = [], []
    for j, s in enumerate(scores):
        probs, invs = [], []
        for half in range(2):
            slope = per_row(0, j, half)
            sink = per_row(1, j, half)
            sh = s[:, half * 256:(half + 1) * 256] + slope * nb
            m = jnp.maximum(jnp.max(sh, axis=-1, keepdims=True), sink)
            pe = jnp.exp(sh - m)
            den = jnp.sum(pe, axis=-1, keepdims=True) + jnp.exp(sink - m)
            probs.append(pe)
            invs.append(1.0 / den)
        pps.append(jnp.concatenate(probs, axis=1).astype(BF16))
        scales.append(jnp.where(lo, invs[0], invs[1]))
    outs = []
    for j in range(A_KV_HEADS):
        o = jnp.dot(pps[j], vv[j], preferred_element_type=F32) * scales[j]
        outs += [o[i * t:(i + 1) * t] for i in range(SLABS_PER_KV)]
    return outs


def _spatial_gate_group(ws_ref, bs_ref, g, vn_g, rows):
    r = lax.broadcasted_iota(jnp.int32, (CHUNK, CHUNK), 0)
    c = lax.broadcasted_iota(jnp.int32, (CHUNK, CHUNK), 1)
    w = jnp.where(r >= c, ws_ref[g], 0.0).astype(BF16)
    return jnp.dot(w[:rows], vn_g, preferred_element_type=F32) + bs_ref[g][:rows]


EVEN_STEP_ROWS = 512
EVEN_SUBBLKS = EVEN_STEP_ROWS // WINDOW
EVEN_STEPS = N_PROMPT // EVEN_STEP_ROWS


def _even_fused_kernel(sl_ref, z_ref, kvprev_ref, ws_ref, bs_ref, lng_ref, lnb_ref,
                       w_ref, x_ref, g_ref, wq_ref, xo_ref, q_ref, mixed_ref):
    s_id = pl.program_id(0)
    slot = s_id % 2

    @pl.when(s_id == 0)
    def _():
        mixed_ref[1] = jnp.zeros(mixed_ref.shape[1:], mixed_ref.dtype)

    kc = D_MODEL // EVEN_SUBBLKS
    xo_ref[...] = x_ref[...]
    blk0 = jnp.minimum(s_id, EVEN_STEPS - 1) * EVEN_SUBBLKS
    for b in range(EVEN_SUBBLKS):
        xo_ref[...] += jnp.dot(mixed_ref[1 - slot, :, b * kc:(b + 1) * kc], w_ref[0, b * kc:(b + 1) * kc, :],
                               preferred_element_type=F32)
        rows = slice(b * WINDOW, (b + 1) * WINDOW)
        c = (blk0 + b) % (SEQ // WINDOW)
        kv_cur = z_ref[rows, OFF_K:OFF_K + 2 * A_KV_WIDTH]
        kv_prev = kvprev_ref[...] if b == 0 else z_ref[(b - 1) * WINDOW:b * WINDOW, OFF_K:OFF_K + 2 * A_KV_WIDTH]
        band = jnp.concatenate([kv_prev, kv_cur], axis=0)
        nbias = _band_bias(WINDOW, jnp.maximum(WINDOW - c * WINDOW, 0))

        def q_slab(p, rows=rows):
            return z_ref[rows, OFF_Q + p * 128:OFF_Q + (p + 1) * 128]

        for p, att in enumerate(_swa(q_slab, band[:, :A_KV_WIDTH], band[:, A_KV_WIDTH:], nbias, sl_ref)):
            ga = z_ref[rows, OFF_GA + p * 128:OFF_GA + (p + 1) * 128].astype(F32)
            mixed_ref[slot, rows, p * 128:(p + 1) * 128] = (_silu(ga) * att).astype(mixed_ref.dtype)

        vn = _standardize(z_ref[rows, OFF_VB:OFF_VB + B_WIDTH].astype(F32)) * lng_ref[...] + lnb_ref[...]
        for g in range(B_GROUPS):
            lo, hi = g * B_GROUP_DIM, (g + 1) * B_GROUP_DIM
            sg = _spatial_gate_group(ws_ref, bs_ref, g, vn[:, lo:hi].astype(BF16), CHUNK)
            u = z_ref[rows, OFF_U + lo:OFF_U + hi].astype(F32)
            gb = z_ref[rows, OFF_GB + lo:OFF_GB + hi].astype(F32)
            mixed_ref[slot, rows, A_WIDTH + lo:A_WIDTH + hi] = (_silu(gb) * (u * sg)).astype(mixed_ref.dtype)

    q_ref[...] = jnp.dot(_rms(xo_ref[...], g_ref[...]).astype(BF16), wq_ref[0],
                         preferred_element_type=F32).astype(q_ref.dtype)


def _even_fused(z, sl, w_s, b_s, lng, lnb, w_out_b, layer_i, x, g, wq_b, layer):
    ns = EVEN_STEPS
    r = EVEN_STEP_ROWS
    kv_blk = OFF_K // (2 * A_KV_WIDTH)

    def cur(s):
        return jnp.minimum(s, ns - 1)

    def prev(s):
        return jnp.maximum(s - 1, 0)

    once = pl.Buffered(1)
    return pl.pallas_call(
        _even_fused_kernel,
        grid=(ns + 1,),
        in_specs=[pl.BlockSpec(memory_space=pltpu.SMEM),
                  pl.BlockSpec((r, EVEN_IN), lambda s: (cur(s), 0)),
                  pl.BlockSpec((WINDOW, 2 * A_KV_WIDTH),
                               lambda s: (jnp.maximum(cur(s) * EVEN_SUBBLKS - 1, 0), kv_blk)),
                  pl.BlockSpec((B_GROUPS, CHUNK, CHUNK), lambda s: (0, 0, 0)),
                  pl.BlockSpec((B_GROUPS, CHUNK, 1), lambda s: (0, 0, 0)),
                  pl.BlockSpec((1, B_WIDTH), lambda s: (0, 0)),
                  pl.BlockSpec((1, B_WIDTH), lambda s: (0, 0)),
                  pl.BlockSpec((1, D_MODEL, D_MODEL), lambda s: (layer_i, 0, 0), pipeline_mode=once),
                  pl.BlockSpec((r, D_MODEL), lambda s: (prev(s), 0)),
                  pl.BlockSpec((1, D_MODEL), lambda s: (0, 0)),
                  pl.BlockSpec((1, D_MODEL, X_WIDTH), lambda s: (layer, 0, 0), pipeline_mode=once)],
        out_specs=[pl.BlockSpec((r, D_MODEL), lambda s: (prev(s), 0)),
                   pl.BlockSpec((r, X_WIDTH), lambda s: (prev(s), 0))],
        out_shape=[jax.ShapeDtypeStruct((N_PROMPT, D_MODEL), F32),
                   jax.ShapeDtypeStruct((N_PROMPT, X_WIDTH), BF16)],
        scratch_shapes=[pltpu.VMEM((2, r, D_MODEL), BF16)],
        compiler_params=_cp(1, FUSED_VMEM_LIMIT),
        name="even_fused",
    )(sl, z, z, w_s, b_s, lng, lnb, w_out_b, x, g.reshape(1, D_MODEL), wq_b)


SEQ_PER_STEP = 4
ROWS_PER_STEP = SEQ_PER_STEP * DEC_SEQ
SAMPLE_Q_ROWS = 16
assert DEC_SEQ & (DEC_SEQ - 1) == 0


def _seq_of(row_idx):
    return lax.shift_right_logical(row_idx, DEC_SEQ.bit_length() - 1)


def _even_sample_kernel(sl_ref, z_ref, ck_ref, cv_ref, ws_ref, bs_ref, lng_ref, lnb_ref,
                        o_ref, wk_ref, wv_ref, cvo_ref):
    nbias = _band_bias(SAMPLE_Q_ROWS, 0)
    pad_rows = jnp.zeros((SAMPLE_Q_ROWS - DEC_SEQ, EVEN_IN), F32)
    attn = [[None] * SEQ_PER_STEP for _ in range(A_HEADS // 2)]
    gate = [[None] * SEQ_PER_STEP for _ in range(B_GROUPS)]
    for r in range(SEQ_PER_STEP):
        r0, r1 = r * DEC_SEQ, (r + 1) * DEC_SEQ
        zr = z_ref[r0:r1, :]
        k_new = zr[:, OFF_K:OFF_K + A_KV_WIDTH]
        v_new = zr[:, OFF_V:OFF_V + A_KV_WIDTH]
        ck = ck_ref[r]
        cv = cv_ref[r]
        wk_ref[r, 0:WINDOW - DEC_SEQ, :] = ck[DEC_SEQ:, :]
        wk_ref[r, WINDOW - DEC_SEQ:, :] = k_new
        wv_ref[r, 0:WINDOW - DEC_SEQ, :] = cv[DEC_SEQ:, :]
        wv_ref[r, WINDOW - DEC_SEQ:, :] = v_new
        tail = jnp.zeros((WINDOW - DEC_SEQ, A_KV_WIDTH), F32)
        kband = jnp.concatenate([ck, k_new, tail], axis=0)
        vband = jnp.concatenate([cv, v_new, tail], axis=0)
        zq = jnp.concatenate([zr, pad_rows], axis=0)

        def q_slab(p, zq=zq):
            return zq[:, OFF_Q + p * 128:OFF_Q + (p + 1) * 128].astype(BF16)

        for p, a in enumerate(_swa(q_slab, kband, vband, nbias, sl_ref)):
            ga = zr[:, OFF_GA + p * 128:OFF_GA + (p + 1) * 128]
            attn[p][r] = _silu(ga) * a[:DEC_SEQ]

        vn = _standardize(zr[:, OFF_VB:OFF_VB + B_WIDTH]) * lng_ref[...] + lnb_ref[...]
        cvo_ref[r0:r1, :] = vn
        vn_pad = jnp.concatenate([vn, jnp.zeros((CHUNK - DEC_SEQ, B_WIDTH), F32)], axis=0).astype(BF16)
        for g in range(B_GROUPS):
            lo, hi = g * B_GROUP_DIM, (g + 1) * B_GROUP_DIM
            mixed = _spatial_gate_group(ws_ref, bs_ref, g, vn_pad[:, lo:hi], SAMPLE_Q_ROWS)[:DEC_SEQ]
            u = zr[:, OFF_U + lo:OFF_U + hi]
            gb = zr[:, OFF_GB + lo:OFF_GB + hi]
            gate[g][r] = _silu(gb) * (u * mixed)

    for p in range(A_HEADS // 2):
        o_ref[:, p * 128:(p + 1) * 128] = jnp.concatenate(attn[p], axis=0).astype(o_ref.dtype)
    for g in range(B_GROUPS):
        lo, hi = A_WIDTH + g * B_GROUP_DIM, A_WIDTH + (g + 1) * B_GROUP_DIM
        o_ref[:, lo:hi] = jnp.concatenate(gate[g], axis=0).astype(o_ref.dtype)


def _even_sample(zs, ck, cv, sl, w_s, b_s, lng, lnb):
    n_steps = DEC_BATCH // SEQ_PER_STEP
    win = jax.ShapeDtypeStruct((DEC_BATCH, WINDOW, A_KV_WIDTH), F32)
    return pl.pallas_call(
        _even_sample_kernel,
        grid=(n_steps,),
        in_specs=[pl.BlockSpec(memory_space=pltpu.SMEM),
                  pl.BlockSpec((ROWS_PER_STEP, EVEN_IN), lambda i: (i, 0)),
                  pl.BlockSpec((SEQ_PER_STEP, WINDOW, A_KV_WIDTH), lambda i: (i, 0, 0)),
                  pl.BlockSpec((SEQ_PER_STEP, WINDOW, A_KV_WIDTH), lambda i: (i, 0, 0)),
                  pl.BlockSpec((B_GROUPS, CHUNK, CHUNK), lambda i: (0, 0, 0)),
                  pl.BlockSpec((B_GROUPS, CHUNK, 1), lambda i: (0, 0, 0)),
                  pl.BlockSpec((1, B_WIDTH), lambda i: (0, 0)),
                  pl.BlockSpec((1, B_WIDTH), lambda i: (0, 0))],
        out_specs=[pl.BlockSpec((ROWS_PER_STEP, D_MODEL), lambda i: (i, 0)),
                   pl.BlockSpec((SEQ_PER_STEP, WINDOW, A_KV_WIDTH), lambda i: (i, 0, 0)),
                   pl.BlockSpec((SEQ_PER_STEP, WINDOW, A_KV_WIDTH), lambda i: (i, 0, 0)),
                   pl.BlockSpec((ROWS_PER_STEP, B_WIDTH), lambda i: (i, 0))],
        out_shape=[jax.ShapeDtypeStruct((N_SAMPLE, D_MODEL), BF16), win, win,
                   jax.ShapeDtypeStruct((N_SAMPLE, B_WIDTH), F32)],
        compiler_params=_cp(1),
        name="even_sample",
    )(sl, zs, ck, cv, w_s, b_s, lng, lnb)


RET_ROWS = 256
RET_BLKS_PER_SEQ = SEQ // RET_ROWS
RET_PROMPT_BLKS = N_PROMPT // RET_ROWS


def _gated_groupnorm(o, g):
    return (_silu(g.astype(F32)) * _standardize(o)).astype(BF16)


RET_LOOKAHEAD = 1


def _ret_fused_kernel(lg_ref, z_ref, w_ref, x_ref, g_ref, wq_ref, xo_ref, q_ref, so_ref, dm_ref, dc_ref):
    s_id = pl.program_id(0)
    t = RET_ROWS
    k_scale = R_KEY_DIM ** -0.5
    c = s_id % RET_BLKS_PER_SEQ

    @pl.when(s_id == 0)
    def _():
        row = lax.broadcasted_iota(jnp.int32, (t, t), 0)
        col = lax.broadcasted_iota(jnp.int32, (t, t), 1)
        rel = (row - col).astype(F32)
        ti = lax.broadcasted_iota(jnp.int32, (t, 1), 0).astype(F32)
        for h in range(R_HEADS):
            lg = lg_ref[h]
            dm_ref[h] = jnp.where(rel >= 0, jnp.exp(jnp.maximum(rel, 0.0) * lg), 0.0) * k_scale
            dc_ref[h, 0] = jnp.exp((ti + 1.0) * lg)
            dc_ref[h, 1] = jnp.exp((t - 1.0 - ti) * lg) * k_scale

    @pl.when(c == 0)
    def _():
        so_ref[...] = jnp.zeros_like(so_ref)

    def head_matmuls(h):
        lg = lg_ref[h]
        q_decay = dc_ref[h, 0]
        k_decay = dc_ref[h, 1]
        chunk_decay = jnp.exp(jnp.full((1, R_VAL_DIM), float(t), F32) * lg)
        q = z_ref[:, h * R_KEY_DIM:(h + 1) * R_KEY_DIM]
        k = z_ref[:, R_K_WIDTH + h * R_KEY_DIM:R_K_WIDTH + (h + 1) * R_KEY_DIM]
        v = z_ref[:, 2 * R_K_WIDTH + h * R_VAL_DIM:2 * R_K_WIDTH + (h + 1) * R_VAL_DIM]
        st = so_ref[0, h]
        scores = _dot_nt(q, k) * dm_ref[h]
        inner = jnp.dot(scores.astype(BF16), v, preferred_element_type=F32)
        cross = jnp.dot(q, st.astype(BF16), preferred_element_type=F32) * q_decay
        kd_t = (k.astype(F32) * k_decay).T.astype(BF16)
        so_ref[0, h] = chunk_decay * st + jnp.dot(kd_t, v, preferred_element_type=F32)
        return inner + cross

    x1 = x_ref[...]
    ahead = [head_matmuls(h) for h in range(RET_LOOKAHEAD)]
    for h in range(R_HEADS):
        if h + RET_LOOKAHEAD < R_HEADS:
            ahead.append(head_matmuls(h + RET_LOOKAHEAD))
        raw = ahead.pop(0)
        g = z_ref[:, 2 * R_K_WIDTH + R_V_WIDTH + h * R_VAL_DIM:2 * R_K_WIDTH + R_V_WIDTH + (h + 1) * R_VAL_DIM]
        x1 = x1 + jnp.dot(_gated_groupnorm(raw, g), w_ref[0, h * R_VAL_DIM:(h + 1) * R_VAL_DIM, :],
                          preferred_element_type=F32)
    xo_ref[...] = x1
    q_ref[...] = jnp.dot(_rms(x1, g_ref[...]).astype(BF16), wq_ref[0],
                         preferred_element_type=F32).astype(q_ref.dtype)


def _ret_fused(z, log_g, w_out_b, layer_i, x, g, wq_b, layer):
    nb = RET_PROMPT_BLKS
    r = RET_ROWS
    once = pl.Buffered(1)
    return pl.pallas_call(
        _ret_fused_kernel,
        grid=(nb,),
        in_specs=[pl.BlockSpec(memory_space=pltpu.SMEM),
                  pl.BlockSpec((r, RET_IN), lambda s: (s, 0)),
                  pl.BlockSpec((1, R_V_WIDTH, D_MODEL), lambda s: (layer_i, 0, 0), pipeline_mode=once),
                  pl.BlockSpec((r, D_MODEL), lambda s: (s, 0)),
                  pl.BlockSpec((1, D_MODEL), lambda s: (0, 0)),
                  pl.BlockSpec((1, D_MODEL, X_WIDTH), lambda s: (layer, 0, 0), pipeline_mode=once)],
        out_specs=[pl.BlockSpec((r, D_MODEL), lambda s: (s, 0)),
                   pl.BlockSpec((r, X_WIDTH), lambda s: (s, 0)),
                   pl.BlockSpec((1, R_HEADS, R_KEY_DIM, R_VAL_DIM),
                                lambda s: (s // RET_BLKS_PER_SEQ, 0, 0, 0))],
        out_shape=[jax.ShapeDtypeStruct((N_PROMPT, D_MODEL), F32),
                   jax.ShapeDtypeStruct((N_PROMPT, X_WIDTH), BF16),
                   jax.ShapeDtypeStruct((BATCH, R_HEADS, R_KEY_DIM, R_VAL_DIM), F32)],
        scratch_shapes=[pltpu.VMEM((R_HEADS, r, r), F32),
                        pltpu.VMEM((R_HEADS, 2, r, 1), F32)],
        compiler_params=_cp(1, FUSED_VMEM_LIMIT),
        name="ret_fused",
    )(log_g, z, w_out_b, x, g.reshape(1, D_MODEL), wq_b)


RET_SAMPLE_SEQS = 8
RET_SAMPLE_ROWS = RET_SAMPLE_SEQS * DEC_SEQ


def _ret_sample_kernel(lg_ref, q_ref, k_ref, v_ref, g_ref, s_ref, *rest):
    o_ref, so_ref = rest[-2:]
    h = pl.program_id(1)
    lg = lg_ref[h]
    n = RET_SAMPLE_ROWS
    t = DEC_SEQ
    row = lax.broadcasted_iota(jnp.int32, (n, n), 0)
    col = lax.broadcasted_iota(jnp.int32, (n, n), 1)
    rel = (row - col).astype(F32)
    same = _seq_of(row) == _seq_of(col)
    k_scale = R_KEY_DIM ** -0.5
    dmat = jnp.where(same & (rel >= 0), jnp.exp(jnp.maximum(rel, 0.0) * lg), 0.0) * k_scale
    ri = lax.broadcasted_iota(jnp.int32, (n, 1), 0)
    ti = (ri & (t - 1)).astype(F32)
    q_decay = jnp.exp((ti + 1.0) * lg)
    k_decay = jnp.exp((t - 1.0 - ti) * lg) * k_scale
    step_decay = jnp.exp(jnp.full((1, R_VAL_DIM), float(t), F32) * lg)

    q = q_ref[...]
    k = k_ref[...]
    v = v_ref[...]
    scores = _dot_nt(q, k) * dmat
    o = jnp.dot(scores.astype(BF16), v, preferred_element_type=F32)
    kd = k.astype(F32) * k_decay
    k_pad = jnp.zeros((R_CHUNK - n, R_KEY_DIM), F32)
    v_pad = jnp.concatenate([v.astype(F32), jnp.zeros((R_CHUNK - n, R_VAL_DIM), F32)], axis=0).astype(BF16)
    for r in range(RET_SAMPLE_SEQS):
        mine = _seq_of(ri) == r
        st = s_ref[0, r, 0]
        cross = jnp.dot(q, st.astype(BF16), preferred_element_type=F32) * q_decay
        o = o + jnp.where(mine, cross, 0.0)
        kd_t = jnp.concatenate([jnp.where(mine, kd, 0.0), k_pad], axis=0).T.astype(BF16)
        so_ref[0, r, 0] = step_decay * st + jnp.dot(kd_t, v_pad, preferred_element_type=F32)
    o_ref[...] = _gated_groupnorm(o, g_ref[...])


def _ret_sample(z, state_all, idx, log_g, new_states):
    n_steps = DEC_BATCH // RET_SAMPLE_SEQS
    rb0 = N_PROMPT // RET_SAMPLE_ROWS
    kcol = R_K_WIDTH // R_KEY_DIM
    vcol = 2 * R_K_WIDTH // R_VAL_DIM
    gcol = vcol + R_HEADS
    st_spec = pl.BlockSpec((1, RET_SAMPLE_SEQS, 1, R_KEY_DIM, R_VAL_DIM), lambda i, h: (idx, i, h, 0, 0))
    in_specs = [pl.BlockSpec(memory_space=pltpu.SMEM),
                pl.BlockSpec((RET_SAMPLE_ROWS, R_KEY_DIM), lambda i, h: (rb0 + i, h)),
                pl.BlockSpec((RET_SAMPLE_ROWS, R_KEY_DIM), lambda i, h: (rb0 + i, kcol + h)),
                pl.BlockSpec((RET_SAMPLE_ROWS, R_VAL_DIM), lambda i, h: (rb0 + i, vcol + h)),
                pl.BlockSpec((RET_SAMPLE_ROWS, R_VAL_DIM), lambda i, h: (rb0 + i, gcol + h)),
                st_spec]
    args = [log_g, z, z, z, z, state_all]
    aliases = {}
    if new_states is not None:
        in_specs.append(pl.BlockSpec(memory_space=pl.ANY))
        args.append(new_states)
        aliases = {len(args) - 1: 1}
    return pl.pallas_call(
        _ret_sample_kernel,
        grid=(n_steps, R_HEADS),
        in_specs=in_specs,
        out_specs=[pl.BlockSpec((RET_SAMPLE_ROWS, R_VAL_DIM), lambda i, h: (i, h)), st_spec],
        out_shape=[jax.ShapeDtypeStruct((N_SAMPLE, R_V_WIDTH), BF16),
                   jax.ShapeDtypeStruct(state_all.shape, F32)],
        input_output_aliases=aliases,
        compiler_params=_cp(2),
        name="ret_sample",
    )(*args)


XQ_ROWS = 1024
XQ_PROMPT_BLKS = N_PROMPT // XQ_ROWS
XQ_BLKS_PER_SEQ = SEQ // XQ_ROWS


def _xattn_heads(q, mk_heads, mv_heads):
    pairs = [(i, h) for i in range(len(mk_heads)) for h in range(X_HEADS)]
    scores = [_dot_nt(q[:, h * X_HEAD_DIM:(h + 1) * X_HEAD_DIM], mk_heads[i](h)) * (X_HEAD_DIM ** -0.5)
              for i, h in pairs]
    probs, invs = [], []
    for s in scores:
        p = jnp.exp(s - jnp.max(s, axis=-1, keepdims=True))
        probs.append(p.astype(BF16))
        invs.append(1.0 / jnp.sum(p, axis=-1, keepdims=True))
    outs = [jnp.dot(p, mv_heads[i](h), preferred_element_type=F32) * inv
            for (i, h), p, inv in zip(pairs, probs, invs)]
    return [outs[i * X_HEADS:(i + 1) * X_HEADS] for i in range(len(mk_heads))]


def _xattn_proj_kernel(final, q_ref, mk_ref, mv_ref, os_ref, w_ref, xp_ref, xs_ref, g_ref, out_a, out_b):
    s_id = pl.program_id(0)

    def project(o, x_ref):
        return x_ref[...] + jnp.dot(o, w_ref[0], preferred_element_type=F32)

    @pl.when(s_id < XQ_PROMPT_BLKS)
    def _():
        def head_of(ref):
            return lambda h: ref[0, :, h * X_HEAD_DIM:(h + 1) * X_HEAD_DIM].astype(BF16)

        outs, = _xattn_heads(q_ref[...], [head_of(mk_ref)], [head_of(mv_ref)])
        x2 = project(jnp.concatenate(outs, axis=1).astype(BF16), xp_ref)
        if final:
            out_a[...] = _rms(x2, g_ref[...])
        else:
            out_a[...] = x2
            out_b[...] = _rms(x2, g_ref[...]).astype(out_b.dtype)

    @pl.when(s_id >= XQ_PROMPT_BLKS)
    def _():
        x2 = project(os_ref[...], xs_ref)
        if final:
            out_b[...] = _rms(x2, g_ref[...])
        else:
            out_a[0:N_SAMPLE, :] = x2
            out_b[0:N_SAMPLE, :] = _rms(x2, g_ref[...]).astype(out_b.dtype)


def _xattn_proj(q, mk_all, mv_all, layer, o_sample, w_stack, x_prompt, x_sample, g, final):
    nb = XQ_PROMPT_BLKS
    d = w_stack.shape[2]
    xs_blk = (x_sample.shape[0] - N_SAMPLE) // N_SAMPLE

    def rb(s):
        return jnp.minimum(s, nb - 1)

    mem_spec = pl.BlockSpec((1, MEM_LEN, X_WIDTH), lambda s: (layer, rb(s) // XQ_BLKS_PER_SEQ, 0))
    if final:
        out_specs = [pl.BlockSpec((XQ_ROWS, d), lambda s: (rb(s), 0)),
                     pl.BlockSpec((N_SAMPLE, d), lambda s: (0, 0))]
        out_shape = [jax.ShapeDtypeStruct((N_PROMPT, d), F32), jax.ShapeDtypeStruct((N_SAMPLE, d), F32)]
    else:
        out_specs = [pl.BlockSpec((XQ_ROWS, d), lambda s: (s, 0)),
                     pl.BlockSpec((XQ_ROWS, d), lambda s: (s, 0))]
        out_shape = [jax.ShapeDtypeStruct((N_ROWS, d), F32), jax.ShapeDtypeStruct((N_ROWS, d), BF16)]
    return pl.pallas_call(
        functools.partial(_xattn_proj_kernel, final),
        grid=(nb + 1,),
        in_specs=[pl.BlockSpec((XQ_ROWS, X_WIDTH), lambda s: (rb(s), 0)),
                  mem_spec, mem_spec,
                  pl.BlockSpec((N_SAMPLE, X_WIDTH), lambda s: (0, 0)),
                  pl.BlockSpec((1, X_WIDTH, d), lambda s: (layer, 0, 0)),
                  pl.BlockSpec((XQ_ROWS, d), lambda s: (rb(s), 0)),
                  pl.BlockSpec((N_SAMPLE, d), lambda s: (xs_blk, 0)),
                  pl.BlockSpec((1, d), lambda s: (0, 0))],
        out_specs=out_specs,
        out_shape=out_shape,
        compiler_params=_cp(1),
        name="xattn_proj",
    )(q, mk_all, mv_all, o_sample, w_stack, x_prompt, x_sample, g.reshape(1, d))


XS_SEQS = 4
XS_ROWS = XS_SEQS * DEC_SEQ


def _xattn_sample_kernel(q_ref, ck_ref, cv_ref, o_ref):
    q = q_ref[...]
    ri = lax.broadcasted_iota(jnp.int32, (XS_ROWS, 1), 0)
    def head_of(ref, r):
        return lambda h: ref[0, r, pl.ds(h, MEM_LEN, stride=X_HEADS), :].astype(BF16)

    per_seq = _xattn_heads(q, [head_of(ck_ref, r) for r in range(XS_SEQS)],
                           [head_of(cv_ref, r) for r in range(XS_SEQS)])
    acc = [jnp.zeros((XS_ROWS, X_HEAD_DIM), F32) for _ in range(X_HEADS)]
    for r, outs in enumerate(per_seq):
        mine = _seq_of(ri) == r
        acc = [a + jnp.where(mine, o, 0.0) for a, o in zip(acc, outs)]
    for h, a in enumerate(acc):
        o_ref[:, h * X_HEAD_DIM:(h + 1) * X_HEAD_DIM] = a.astype(o_ref.dtype)


def _xattn_sample(q, ck_all, cv_all, layer):
    n_steps = DEC_BATCH // XS_SEQS
    rb0 = (q.shape[0] - N_SAMPLE) // XS_ROWS
    mem_spec = pl.BlockSpec((1, XS_SEQS, MEM_LEN * X_HEADS, X_HEAD_DIM), lambda i: (layer, i, 0, 0))
    return pl.pallas_call(
        _xattn_sample_kernel,
        grid=(n_steps,),
        in_specs=[pl.BlockSpec((XS_ROWS, X_WIDTH), lambda i: (rb0 + i, 0)), mem_spec, mem_spec],
        out_specs=pl.BlockSpec((XS_ROWS, X_WIDTH), lambda i: (i, 0)),
        out_shape=jax.ShapeDtypeStruct((N_SAMPLE, X_WIDTH), BF16),
        compiler_params=_cp(1),
        name="xattn_sample",
    )(q, ck_all, cv_all)


def kernel(x_prompt, x_sample, cache_win_k, cache_win_v, state_ret, cache_mem_k, cache_mem_v, mem_prompt,
           norm_mix, w_in_even, attn_sinks, w_spatial, b_spatial, norm_v_g, norm_v_b, w_out_even,
           w_in_ret, w_out_ret, norm_cross, norm_mem, w_xq, w_xk, w_xv, w_xo, norm_final):
    slopes = jnp.exp2(-8.0 * (jnp.arange(A_HEADS, dtype=F32) + 1.0) / A_HEADS)
    log_g = jnp.log1p(-jnp.exp2(-5.0 - jnp.arange(R_HEADS, dtype=F32)))

    def mem_rows(cache):
        return cache.reshape(DEPTH, DEC_BATCH, MEM_LEN * X_HEADS, X_HEAD_DIM)

    mk_all, mv_all, mk_b, mv_b, w_xq_b, w_xo_b = _memkv(mem_prompt.reshape(BATCH * MEM_LEN, D_MODEL), norm_mem,
                                                        w_xk, w_xv, w_xq, w_xo)

    xp_src = x_prompt.reshape(N_PROMPT, D_MODEL)
    xs_src = x_sample.reshape(N_SAMPLE, D_MODEL)
    xn = _embed(xp_src, xs_src, norm_mix[0])
    wkp, wvp, rsp, wks, wvs, cvs = [], [], [], [], [], []
    rss = None
    for layer in range(DEPTH):
        i = layer // 2
        xs_blk0 = (xs_src.shape[0] - N_SAMPLE) // N_SAMPLE
        if layer % 2 == 0:
            sl = jnp.stack([slopes, attn_sinks[i].astype(F32)])
            b_s = b_spatial[i].reshape(B_GROUPS, CHUNK, 1)
            lng = norm_v_g[i].reshape(1, B_WIDTH)
            lnb = norm_v_b[i].reshape(1, B_WIDTH)
            z, zs, w_out_b = _in_even(xn, w_in_even, i, w_out_even)
            kv_tail = _kv_tail(xn, w_in_even, i)
            wkp.append(kv_tail[:, :A_KV_WIDTH].reshape(BATCH, WINDOW, A_KV_HEADS, A_HEAD_DIM))
            wvp.append(kv_tail[:, A_KV_WIDTH:].reshape(BATCH, WINDOW, A_KV_HEADS, A_HEAD_DIM))
            ck = cache_win_k[i].reshape(DEC_BATCH, WINDOW, A_KV_WIDTH)
            cv = cache_win_v[i].reshape(DEC_BATCH, WINDOW, A_KV_WIDTH)
            mixed_s, wk_new, wv_new, cv_new = _even_sample(zs, ck, cv, sl, w_spatial[i], b_s, lng, lnb)
            wks.append(wk_new.reshape(DEC_BATCH, WINDOW, A_KV_HEADS, A_HEAD_DIM))
            wvs.append(wv_new.reshape(DEC_BATCH, WINDOW, A_KV_HEADS, A_HEAD_DIM))
            cvs.append(cv_new.reshape(DEC_BATCH, DEC_SEQ, B_WIDTH))
            x_p, q = _even_fused(z, sl, w_spatial[i], b_s, lng, lnb, w_out_b, 0, xp_src, norm_cross[layer],
                                 w_xq_b, layer)
            x_s, q_s = _mm_res_q(mixed_s, w_out_b, 0, xs_src, norm_cross[layer], w_xq_b, layer,
                                 N_SAMPLE, SAMPLE_PROJ_TK, x_blk0=xs_blk0)
        else:
            z, w_out_b = _mm_wcast(xn, w_in_ret, i, BF16, IN_RET_TM, IN_RET_TN, w_out_ret, i, "in_ret")
            o_s, rss = _ret_sample(z, state_ret, i, log_g, rss)
            x_p, q, st_p = _ret_fused(z, log_g, w_out_b, 0, xp_src, norm_cross[layer], w_xq_b, layer)
            rsp.append(st_p)
            x_s, q_s = _mm_res_q(o_s, w_out_b, 0, xs_src, norm_cross[layer], w_xq_b, layer,
                                 N_SAMPLE, SAMPLE_PROJ_TK, x_blk0=xs_blk0)
        o_s = _xattn_sample(q_s, mem_rows(cache_mem_k), mem_rows(cache_mem_v), layer)
        if layer + 1 < DEPTH:
            x, xn = _xattn_proj(q, mk_b, mv_b, layer, o_s, w_xo_b, x_p, x_s, norm_mix[layer + 1], False)
            xp_src = xs_src = x
        else:
            y_prompt, y_sample = _xattn_proj(q, mk_b, mv_b, layer, o_s, w_xo_b, x_p, x_s, norm_final, True)

    y_prompt = y_prompt.reshape(BATCH, SEQ, D_MODEL)
    y_sample = y_sample.reshape(DEC_BATCH, DEC_SEQ, D_MODEL)
    mem_shape = (DEPTH, BATCH, MEM_LEN, X_HEADS, X_HEAD_DIM)
    return (y_prompt, y_sample,
            jnp.stack(wkp), jnp.stack(wvp), jnp.stack(rsp),
            mk_all.reshape(mem_shape), mv_all.reshape(mem_shape),
            jnp.stack(wks), jnp.stack(wvs), rss, jnp.stack(cvs))
```

```python
import functools

import jax
import jax.numpy as jnp
from jax import lax
from jax.experimental import pallas as pl
from jax.experimental.pallas import tpu as pltpu

F32 = jnp.float32
BF16 = jnp.bfloat16

D_MODEL = 2048
BATCH = 2
SEQ = 4096
DEPTH = 4
DEC_BATCH = 32
DEC_SEQ = 8
A_HEADS = 16
A_KV_HEADS = 2
A_HEAD_DIM = 64
A_WIDTH = 1024
A_KV_WIDTH = 128
WINDOW = 128
CHUNK = 128
B_GROUPS = 4
B_WIDTH = 1024
B_GROUP_DIM = 256
EVEN_IN = 5376
R_HEADS = 8
R_KEY_DIM = 256
R_VAL_DIM = 512
R_K_WIDTH = 2048
R_V_WIDTH = 4096
R_CHUNK = 128
RET_IN = 12288
MEM_LEN = 256
X_HEADS = 4
X_HEAD_DIM = 128
X_WIDTH = 512
EPS = 1e-6

N_PROMPT = BATCH * SEQ
N_SAMPLE = DEC_BATCH * DEC_SEQ
N_ROWS = N_PROMPT + N_SAMPLE

OFF_Q, OFF_K, OFF_V, OFF_GA, OFF_U, OFF_VB, OFF_GB = 0, 1024, 1152, 1280, 2304, 3328, 4352

MIB = 1024 * 1024
V7X_VMEM_BYTES = 64 * MIB
VMEM_LIMIT = V7X_VMEM_BYTES - 12 * MIB
FUSED_VMEM_LIMIT = V7X_VMEM_BYTES - 8 * MIB
SAMPLE_PROJ_TK = 2048


def _cp(n_axes, vmem_limit=VMEM_LIMIT):
    return pltpu.CompilerParams(dimension_semantics=("arbitrary",) * n_axes,
                                vmem_limit_bytes=vmem_limit)


def _silu(x):
    return x * (1.0 / (1.0 + jnp.exp(-x)))


def _rms(x, g):
    ms = jnp.mean(x * x, axis=-1, keepdims=True)
    return x * lax.rsqrt(ms + EPS) * g


def _standardize(x):
    xc = x - jnp.mean(x, axis=-1, keepdims=True)
    return xc * lax.rsqrt(jnp.mean(xc * xc, axis=-1, keepdims=True) + EPS)


def _dot_nt(a, b):
    return lax.dot_general(a, b, (((1,), (1,)), ((), ())), preferred_element_type=F32)


SIDE_CAST_ROWS = 64


def _side_cast_specs(side_stack, side_layer, n_inner, n_steps):
    _, k, d = side_stack.shape
    n_slabs = k // SIDE_CAST_ROWS
    assert n_slabs <= n_steps

    def slab(j, i):
        return jnp.minimum(j * n_inner + i, n_slabs - 1)

    return (pl.BlockSpec((1, SIDE_CAST_ROWS, d), lambda j, i: (side_layer, slab(j, i), 0)),
            pl.BlockSpec((1, SIDE_CAST_ROWS, d), lambda j, i: (0, slab(j, i), 0)),
            jax.ShapeDtypeStruct((1, k, d), BF16))


def _mm_wcast_kernel(a_ref, w_ref, side_ref, o_ref, side_o_ref, wb_ref):
    @pl.when(pl.program_id(1) == 0)
    def _():
        wb_ref[...] = w_ref[0].astype(BF16)

    o_ref[...] = jnp.dot(a_ref[...], wb_ref[...], preferred_element_type=F32).astype(o_ref.dtype)
    side_o_ref[...] = side_ref[...].astype(BF16)


def _mm_wcast(a, w_stack, layer, out_dtype, tm, tn, side_stack, side_layer, name):
    m, k = a.shape
    n = w_stack.shape[2]
    grid = (n // tn, m // tm)
    side_in, side_out, side_shape = _side_cast_specs(side_stack, side_layer, grid[1], grid[0] * grid[1])
    return pl.pallas_call(
        _mm_wcast_kernel,
        grid=grid,
        in_specs=[pl.BlockSpec((tm, k), lambda j, i: (i, 0)),
                  pl.BlockSpec((1, k, tn), lambda j, i: (layer, 0, j)),
                  side_in],
        out_specs=[pl.BlockSpec((tm, tn), lambda j, i: (i, j)), side_out],
        out_shape=[jax.ShapeDtypeStruct((m, n), out_dtype), side_shape],
        scratch_shapes=[pltpu.VMEM((k, tn), BF16)],
        compiler_params=_cp(2),
        name=name,
    )(a, w_stack, side_stack)


IN_EVEN_TM = 1408
IN_EVEN_TN = 768
IN_RET_TM = 1408
IN_RET_TN = 1024


def _in_even_kernel(a_ref, as_ref, w_ref, side_ref, z_ref, zs_ref, side_o_ref, wb_ref):
    @pl.when(pl.program_id(1) == 0)
    def _():
        wb_ref[...] = w_ref[0].astype(BF16)
        zs_ref[...] = jnp.dot(as_ref[...], wb_ref[...], preferred_element_type=F32)

    z_ref[...] = jnp.dot(a_ref[...], wb_ref[...], preferred_element_type=F32).astype(z_ref.dtype)
    side_o_ref[...] = side_ref[...].astype(BF16)


def _in_even(xn, w_stack, layer, side_stack):
    m, k = xn.shape
    n = w_stack.shape[2]
    tm, tn = IN_EVEN_TM, IN_EVEN_TN
    grid = (n // tn, m // tm)
    side_in, side_out, side_shape = _side_cast_specs(side_stack, layer, grid[1], grid[0] * grid[1])
    return pl.pallas_call(
        _in_even_kernel,
        grid=grid,
        in_specs=[pl.BlockSpec((tm, k), lambda j, i: (i, 0)),
                  pl.BlockSpec((N_SAMPLE, k), lambda j, i: (N_PROMPT // N_SAMPLE, 0)),
                  pl.BlockSpec((1, k, tn), lambda j, i: (layer, 0, j)),
                  side_in],
        out_specs=[pl.BlockSpec((tm, tn), lambda j, i: (i, j)),
                   pl.BlockSpec((N_SAMPLE, tn), lambda j, i: (0, j)),
                   side_out],
        out_shape=[jax.ShapeDtypeStruct((m, n), BF16),
                   jax.ShapeDtypeStruct((N_SAMPLE, n), F32),
                   side_shape],
        scratch_shapes=[pltpu.VMEM((k, tn), BF16)],
        compiler_params=_cp(2),
        name="in_even",
    )(xn, xn, w_stack, side_stack)


def _kv_tail_kernel(a_ref, w_ref, o_ref):
    o_ref[...] = jnp.dot(a_ref[...], w_ref[0].astype(BF16), preferred_element_type=F32)


def _kv_tail(xn, w_stack, layer):
    k = xn.shape[1]
    blks_per_seq = SEQ // WINDOW
    width = 2 * A_KV_WIDTH
    return pl.pallas_call(
        _kv_tail_kernel,
        grid=(BATCH,),
        in_specs=[pl.BlockSpec((WINDOW, k), lambda b: (blks_per_seq * (b + 1) - 1, 0)),
                  pl.BlockSpec((1, k, width), lambda b: (layer, 0, OFF_K // width))],
        out_specs=pl.BlockSpec((WINDOW, width), lambda b: (b, 0)),
        out_shape=jax.ShapeDtypeStruct((BATCH * WINDOW, width), F32),
        compiler_params=_cp(1),
        name="kv_tail",
    )(xn, w_stack)


def _mm_res_q_kernel(nk, a_ref, w_ref, x_ref, g_ref, wq_ref, xo_ref, q_ref):
    k = pl.program_id(1)

    @pl.when(k == 0)
    def _():
        xo_ref[...] = x_ref[...]

    xo_ref[...] += jnp.dot(a_ref[...], w_ref[0], preferred_element_type=F32)

    @pl.when(k == nk - 1)
    def _():
        xn = _rms(xo_ref[...], g_ref[...]).astype(BF16)
        q_ref[...] = jnp.dot(xn, wq_ref[0].astype(BF16), preferred_element_type=F32).astype(q_ref.dtype)


def _mm_res_q(a, w_stack, layer, x, g, wq_stack, wq_layer, tm, tk, x_blk0=0):
    m, kdim = a.shape
    d = w_stack.shape[2]
    nq = wq_stack.shape[2]
    nk = kdim // tk
    return pl.pallas_call(
        functools.partial(_mm_res_q_kernel, nk),
        grid=(m // tm, nk),
        in_specs=[pl.BlockSpec((tm, tk), lambda i, k: (i, k)),
                  pl.BlockSpec((1, tk, d), lambda i, k: (layer, k, 0)),
                  pl.BlockSpec((tm, d), lambda i, k: (x_blk0 + i, 0)),
                  pl.BlockSpec((1, d), lambda i, k: (0, 0)),
                  pl.BlockSpec((1, d, nq), lambda i, k: (wq_layer, 0, 0))],
        out_specs=[pl.BlockSpec((tm, d), lambda i, k: (i, 0)),
                   pl.BlockSpec((tm, nq), lambda i, k: (i, 0))],
        out_shape=[jax.ShapeDtypeStruct((m, d), F32),
                   jax.ShapeDtypeStruct((m, nq), BF16)],
        compiler_params=_cp(2),
        name="mm_res_q",
    )(a, w_stack, x, g.reshape(1, d), wq_stack)


def _embed_kernel(n_prompt_blks, xp_ref, xs_ref, g_ref, n_ref):
    i = pl.program_id(0)

    @pl.when(i < n_prompt_blks)
    def _():
        n_ref[...] = _rms(xp_ref[...], g_ref[...]).astype(n_ref.dtype)

    @pl.when(i >= n_prompt_blks)
    def _():
        n_ref[...] = _rms(xs_ref[...], g_ref[...]).astype(n_ref.dtype)


def _embed(xp, xs, g):
    d = xp.shape[1]
    tm = N_SAMPLE
    nb = N_PROMPT // tm
    return pl.pallas_call(
        functools.partial(_embed_kernel, nb),
        grid=(nb + 1,),
        in_specs=[pl.BlockSpec((tm, d), lambda i: (jnp.minimum(i, nb - 1), 0)),
                  pl.BlockSpec((tm, d), lambda i: (0, 0)),
                  pl.BlockSpec((1, d), lambda i: (0, 0))],
        out_specs=pl.BlockSpec((tm, d), lambda i: (i, 0)),
        out_shape=jax.ShapeDtypeStruct((N_ROWS, d), BF16),
        compiler_params=_cp(1),
        name="embed",
    )(xp, xs, g.reshape(1, d))


def _memkv_kernel(mem_ref, g_ref, wk_ref, wv_ref, wq_ref, wo_ref,
                  mk_ref, mv_ref, mkb_ref, mvb_ref, wqb_ref, wob_ref):
    h = _rms(mem_ref[...], g_ref[0]).astype(BF16)
    mk = jnp.dot(h, wk_ref[0].astype(BF16), preferred_element_type=F32)
    mv = jnp.dot(h, wv_ref[0].astype(BF16), preferred_element_type=F32)
    mk_ref[0] = mk
    mv_ref[0] = mv
    mkb_ref[0] = mk.astype(BF16)
    mvb_ref[0] = mv.astype(BF16)
    wqb_ref[...] = wq_ref[...].astype(BF16)
    wob_ref[...] = wo_ref[...].astype(BF16)


def _memkv(mem, norm_mem, w_xk, w_xv, w_xq, w_xo):
    m = mem.shape[0]
    kv_spec = pl.BlockSpec((1, m, X_WIDTH), lambda l: (l, 0, 0))
    w_in_spec = pl.BlockSpec((1, D_MODEL, X_WIDTH), lambda l: (l, 0, 0))
    w_out_spec = pl.BlockSpec((1, X_WIDTH, D_MODEL), lambda l: (l, 0, 0))
    return pl.pallas_call(
        _memkv_kernel,
        grid=(DEPTH,),
        in_specs=[pl.BlockSpec((m, D_MODEL), lambda l: (0, 0)),
                  pl.BlockSpec((1, 1, D_MODEL), lambda l: (l, 0, 0)),
                  w_in_spec, w_in_spec, w_in_spec, w_out_spec],
        out_specs=[kv_spec, kv_spec, kv_spec, kv_spec, w_in_spec, w_out_spec],
        out_shape=[jax.ShapeDtypeStruct((DEPTH, m, X_WIDTH), F32),
                   jax.ShapeDtypeStruct((DEPTH, m, X_WIDTH), F32),
                   jax.ShapeDtypeStruct((DEPTH, m, X_WIDTH), BF16),
                   jax.ShapeDtypeStruct((DEPTH, m, X_WIDTH), BF16),
                   jax.ShapeDtypeStruct(w_xq.shape, BF16),
                   jax.ShapeDtypeStruct(w_xo.shape, BF16)],
        compiler_params=_cp(1),
        name="memkv",
    )(mem, norm_mem.reshape(DEPTH, 1, D_MODEL), w_xk, w_xv, w_xq, w_xo)


SLABS_PER_KV = A_HEADS // A_KV_HEADS // 2
MASK_DIST = 1e30 * 2.0 ** 8


def _expand_band(band, scale):
    x = band.astype(F32)
    if scale != 1.0:
        x = x * scale
    xs = pltpu.roll(x, 64, 1)
    lo = lax.broadcasted_iota(jnp.int32, x.shape, 1) < 64
    z = jnp.zeros_like(x)
    e0 = jnp.concatenate([jnp.where(lo, x, z), jnp.where(lo, z, xs)], axis=0).astype(BF16)
    e1 = jnp.concatenate([jnp.where(lo, xs, z), jnp.where(lo, z, x)], axis=0).astype(BF16)
    return e0, e1


def _band_bias(t, first_key):
    row = lax.broadcasted_iota(jnp.int32, (t, 2 * WINDOW), 0)
    col = lax.broadcasted_iota(jnp.int32, (t, 2 * WINDOW), 1)
    dist = row + WINDOW - col
    valid = (dist >= 0) & (dist < WINDOW) & (col >= first_key)
    return jnp.where(valid, -dist.astype(F32), -MASK_DIST)


def _swa(q_slab, kband, vband, nbias, sl_ref):
    t = nbias.shape[0]
    rows = SLABS_PER_KV * t
    kk = _expand_band(kband, A_HEAD_DIM ** -0.5)
    vv = _expand_band(vband, 1.0)
    nb = jnp.concatenate([nbias] * SLABS_PER_KV, axis=0)
    blk = lax.shift_right_logical(lax.broadcasted_iota(jnp.int32, (rows, 1), 0), t.bit_length() - 1)
    lo = lax.broadcasted_iota(jnp.int32, (rows, 128), 1) < 64
    def per_row(table_row, j, half):
        slab0 = SLABS_PER_KV * j
        col = jnp.full((rows, 1), sl_ref[table_row, 2 * (slab0 + SLABS_PER_KV - 1) + half], F32)
        for i in reversed(range(SLABS_PER_KV - 1)):
            col = jnp.where(blk == i, sl_ref[table_row, 2 * (slab0 + i) + half], col)
        return col

    scores = [_dot_nt(jnp.concatenate([q_slab(SLABS_PER_KV * j + i) for i in range(SLABS_PER_KV)], axis=0), kk[j])
              for j in range(A_KV_HEADS)]
    pps, scales = [], []
    for j, s in enumerate(scores):
        probs, invs = [], []
        for half in range(2):
            slope = per_row(0, j, half)
            sink = per_row(1, j, half)
            sh = s[:, half * 256:(half + 1) * 256] + slope * nb
            m = jnp.maximum(jnp.max(sh, axis=-1, keepdims=True), sink)
            pe = jnp.exp(sh - m)
            den = jnp.sum(pe, axis=-1, keepdims=True) + jnp.exp(sink - m)
            probs.append(pe)
            invs.append(1.0 / den)
        pps.append(jnp.concatenate(probs, axis=1).astype(BF16))
        scales.append(jnp.where(lo, invs[0], invs[1]))
    outs = []
    for j in range(A_KV_HEADS):
        o = jnp.dot(pps[j], vv[j], preferred_element_type=F32) * scales[j]
        outs += [o[i * t:(i + 1) * t] for i in range(SLABS_PER_KV)]
    return outs


def _spatial_gate_group(ws_ref, bs_ref, g, vn_g, rows):
    r = lax.broadcasted_iota(jnp.int32, (CHUNK, CHUNK), 0)
    c = lax.broadcasted_iota(jnp.int32, (CHUNK, CHUNK), 1)
    w = jnp.where(r >= c, ws_ref[g], 0.0).astype(BF16)
    return jnp.dot(w[:rows], vn_g, preferred_element_type=F32) + bs_ref[g][:rows]


EVEN_STEP_ROWS = 512
EVEN_SUBBLKS = EVEN_STEP_ROWS // WINDOW
EVEN_STEPS = N_PROMPT // EVEN_STEP_ROWS


def _even_fused_kernel(sl_ref, z_ref, kvprev_ref, ws_ref, bs_ref, lng_ref, lnb_ref,
                       w_ref, x_ref, g_ref, wq_ref, xo_ref, q_ref, mixed_ref):
    s_id = pl.program_id(0)
    slot = s_id % 2

    @pl.when(s_id == 0)
    def _():
        mixed_ref[1] = jnp.zeros(mixed_ref.shape[1:], mixed_ref.dtype)

    kc = D_MODEL // EVEN_SUBBLKS
    xo_ref[...] = x_ref[...]
    blk0 = jnp.minimum(s_id, EVEN_STEPS - 1) * EVEN_SUBBLKS
    for b in range(EVEN_SUBBLKS):
        xo_ref[...] += jnp.dot(mixed_ref[1 - slot, :, b * kc:(b + 1) * kc], w_ref[0, b * kc:(b + 1) * kc, :],
                               preferred_element_type=F32)
        rows = slice(b * WINDOW, (b + 1) * WINDOW)
        c = (blk0 + b) % (SEQ // WINDOW)
        kv_cur = z_ref[rows, OFF_K:OFF_K + 2 * A_KV_WIDTH]
        kv_prev = kvprev_ref[...] if b == 0 else z_ref[(b - 1) * WINDOW:b * WINDOW, OFF_K:OFF_K + 2 * A_KV_WIDTH]
        band = jnp.concatenate([kv_prev, kv_cur], axis=0)
        nbias = _band_bias(WINDOW, jnp.maximum(WINDOW - c * WINDOW, 0))

        def q_slab(p, rows=rows):
            return z_ref[rows, OFF_Q + p * 128:OFF_Q + (p + 1) * 128]

        for p, att in enumerate(_swa(q_slab, band[:, :A_KV_WIDTH], band[:, A_KV_WIDTH:], nbias, sl_ref)):
            ga = z_ref[rows, OFF_GA + p * 128:OFF_GA + (p + 1) * 128].astype(F32)
            mixed_ref[slot, rows, p * 128:(p + 1) * 128] = (_silu(ga) * att).astype(mixed_ref.dtype)

        vn = _standardize(z_ref[rows, OFF_VB:OFF_VB + B_WIDTH].astype(F32)) * lng_ref[...] + lnb_ref[...]
        for g in range(B_GROUPS):
            lo, hi = g * B_GROUP_DIM, (g + 1) * B_GROUP_DIM
            sg = _spatial_gate_group(ws_ref, bs_ref, g, vn[:, lo:hi].astype(BF16), CHUNK)
            u = z_ref[rows, OFF_U + lo:OFF_U + hi].astype(F32)
            gb = z_ref[rows, OFF_GB + lo:OFF_GB + hi].astype(F32)
            mixed_ref[slot, rows, A_WIDTH + lo:A_WIDTH + hi] = (_silu(gb) * (u * sg)).astype(mixed_ref.dtype)

    q_ref[...] = jnp.dot(_rms(xo_ref[...], g_ref[...]).astype(BF16), wq_ref[0],
                         preferred_element_type=F32).astype(q_ref.dtype)


def _even_fused(z, sl, w_s, b_s, lng, lnb, w_out_b, layer_i, x, g, wq_b, layer):
    ns = EVEN_STEPS
    r = EVEN_STEP_ROWS
    kv_blk = OFF_K // (2 * A_KV_WIDTH)

    def cur(s):
        return jnp.minimum(s, ns - 1)

    def prev(s):
        return jnp.maximum(s - 1, 0)

    once = pl.Buffered(1)
    return pl.pallas_call(
        _even_fused_kernel,
        grid=(ns + 1,),
        in_specs=[pl.BlockSpec(memory_space=pltpu.SMEM),
                  pl.BlockSpec((r, EVEN_IN), lambda s: (cur(s), 0)),
                  pl.BlockSpec((WINDOW, 2 * A_KV_WIDTH),
                               lambda s: (jnp.maximum(cur(s) * EVEN_SUBBLKS - 1, 0), kv_blk)),
                  pl.BlockSpec((B_GROUPS, CHUNK, CHUNK), lambda s: (0, 0, 0)),
                  pl.BlockSpec((B_GROUPS, CHUNK, 1), lambda s: (0, 0, 0)),
                  pl.BlockSpec((1, B_WIDTH), lambda s: (0, 0)),
                  pl.BlockSpec((1, B_WIDTH), lambda s: (0, 0)),
                  pl.BlockSpec((1, D_MODEL, D_MODEL), lambda s: (layer_i, 0, 0), pipeline_mode=once),
                  pl.BlockSpec((r, D_MODEL), lambda s: (prev(s), 0)),
                  pl.BlockSpec((1, D_MODEL), lambda s: (0, 0)),
                  pl.BlockSpec((1, D_MODEL, X_WIDTH), lambda s: (layer, 0, 0), pipeline_mode=once)],
        out_specs=[pl.BlockSpec((r, D_MODEL), lambda s: (prev(s), 0)),
                   pl.BlockSpec((r, X_WIDTH), lambda s: (prev(s), 0))],
        out_shape=[jax.ShapeDtypeStruct((N_PROMPT, D_MODEL), F32),
                   jax.ShapeDtypeStruct((N_PROMPT, X_WIDTH), BF16)],
        scratch_shapes=[pltpu.VMEM((2, r, D_MODEL), BF16)],
        compiler_params=_cp(1, FUSED_VMEM_LIMIT),
        name="even_fused",
    )(sl, z, z, w_s, b_s, lng, lnb, w_out_b, x, g.reshape(1, D_MODEL), wq_b)


SEQ_PER_STEP = 4
ROWS_PER_STEP = SEQ_PER_STEP * DEC_SEQ
SAMPLE_Q_ROWS = 16
assert DEC_SEQ & (DEC_SEQ - 1) == 0


def _seq_of(row_idx):
    return lax.shift_right_logical(row_idx, DEC_SEQ.bit_length() - 1)


def _even_sample_kernel(sl_ref, z_ref, ck_ref, cv_ref, ws_ref, bs_ref, lng_ref, lnb_ref,
                        o_ref, wk_ref, wv_ref, cvo_ref):
    nbias = _band_bias(SAMPLE_Q_ROWS, 0)
    pad_rows = jnp.zeros((SAMPLE_Q_ROWS - DEC_SEQ, EVEN_IN), F32)
    attn = [[None] * SEQ_PER_STEP for _ in range(A_HEADS // 2)]
    gate = [[None] * SEQ_PER_STEP for _ in range(B_GROUPS)]
    for r in range(SEQ_PER_STEP):
        r0, r1 = r * DEC_SEQ, (r + 1) * DEC_SEQ
        zr = z_ref[r0:r1, :]
        k_new = zr[:, OFF_K:OFF_K + A_KV_WIDTH]
        v_new = zr[:, OFF_V:OFF_V + A_KV_WIDTH]
        ck = ck_ref[r]
        cv = cv_ref[r]
        wk_ref[r, 0:WINDOW - DEC_SEQ, :] = ck[DEC_SEQ:, :]
        wk_ref[r, WINDOW - DEC_SEQ:, :] = k_new
        wv_ref[r, 0:WINDOW - DEC_SEQ, :] = cv[DEC_SEQ:, :]
        wv_ref[r, WINDOW - DEC_SEQ:, :] = v_new
        tail = jnp.zeros((WINDOW - DEC_SEQ, A_KV_WIDTH), F32)
        kband = jnp.concatenate([ck, k_new, tail], axis=0)
        vband = jnp.concatenate([cv, v_new, tail], axis=0)
        zq = jnp.concatenate([zr, pad_rows], axis=0)

        def q_slab(p, zq=zq):
            return zq[:, OFF_Q + p * 128:OFF_Q + (p + 1) * 128].astype(BF16)

        for p, a in enumerate(_swa(q_slab, kband, vband, nbias, sl_ref)):
            ga = zr[:, OFF_GA + p * 128:OFF_GA + (p + 1) * 128]
            attn[p][r] = _silu(ga) * a[:DEC_SEQ]

        vn = _standardize(zr[:, OFF_VB:OFF_VB + B_WIDTH]) * lng_ref[...] + lnb_ref[...]
        cvo_ref[r0:r1, :] = vn
        vn_pad = jnp.concatenate([vn, jnp.zeros((CHUNK - DEC_SEQ, B_WIDTH), F32)], axis=0).astype(BF16)
        for g in range(B_GROUPS):
            lo, hi = g * B_GROUP_DIM, (g + 1) * B_GROUP_DIM
            mixed = _spatial_gate_group(ws_ref, bs_ref, g, vn_pad[:, lo:hi], SAMPLE_Q_ROWS)[:DEC_SEQ]
            u = zr[:, OFF_U + lo:OFF_U + hi]
            gb = zr[:, OFF_GB + lo:OFF_GB + hi]
            gate[g][r] = _silu(gb) * (u * mixed)

    for p in range(A_HEADS // 2):
        o_ref[:, p * 128:(p + 1) * 128] = jnp.concatenate(attn[p], axis=0).astype(o_ref.dtype)
    for g in range(B_GROUPS):
        lo, hi = A_WIDTH + g * B_GROUP_DIM, A_WIDTH + (g + 1) * B_GROUP_DIM
        o_ref[:, lo:hi] = jnp.concatenate(gate[g], axis=0).astype(o_ref.dtype)


def _even_sample(zs, ck, cv, sl, w_s, b_s, lng, lnb):
    n_steps = DEC_BATCH // SEQ_PER_STEP
    win = jax.ShapeDtypeStruct((DEC_BATCH, WINDOW, A_KV_WIDTH), F32)
    return pl.pallas_call(
        _even_sample_kernel,
        grid=(n_steps,),
        in_specs=[pl.BlockSpec(memory_space=pltpu.SMEM),
                  pl.BlockSpec((ROWS_PER_STEP, EVEN_IN), lambda i: (i, 0)),
                  pl.BlockSpec((SEQ_PER_STEP, WINDOW, A_KV_WIDTH), lambda i: (i, 0, 0)),
                  pl.BlockSpec((SEQ_PER_STEP, WINDOW, A_KV_WIDTH), lambda i: (i, 0, 0)),
                  pl.BlockSpec((B_GROUPS, CHUNK, CHUNK), lambda i: (0, 0, 0)),
                  pl.BlockSpec((B_GROUPS, CHUNK, 1), lambda i: (0, 0, 0)),
                  pl.BlockSpec((1, B_WIDTH), lambda i: (0, 0)),
                  pl.BlockSpec((1, B_WIDTH), lambda i: (0, 0))],
        out_specs=[pl.BlockSpec((ROWS_PER_STEP, D_MODEL), lambda i: (i, 0)),
                   pl.BlockSpec((SEQ_PER_STEP, WINDOW, A_KV_WIDTH), lambda i: (i, 0, 0)),
                   pl.BlockSpec((SEQ_PER_STEP, WINDOW, A_KV_WIDTH), lambda i: (i, 0, 0)),
                   pl.BlockSpec((ROWS_PER_STEP, B_WIDTH), lambda i: (i, 0))],
        out_shape=[jax.ShapeDtypeStruct((N_SAMPLE, D_MODEL), BF16), win, win,
                   jax.ShapeDtypeStruct((N_SAMPLE, B_WIDTH), F32)],
        compiler_params=_cp(1),
        name="even_sample",
    )(sl, zs, ck, cv, w_s, b_s, lng, lnb)


RET_ROWS = 256
RET_BLKS_PER_SEQ = SEQ // RET_ROWS
RET_PROMPT_BLKS = N_PROMPT // RET_ROWS


def _gated_groupnorm(o, g):
    return (_silu(g.astype(F32)) * _standardize(o)).astype(BF16)


RET_LOOKAHEAD = 1


def _ret_fused_kernel(lg_ref, z_ref, w_ref, x_ref, g_ref, wq_ref, xo_ref, q_ref, so_ref, dm_ref, dc_ref):
    s_id = pl.program_id(0)
    t = RET_ROWS
    k_scale = R_KEY_DIM ** -0.5
    c = s_id % RET_BLKS_PER_SEQ

    @pl.when(s_id == 0)
    def _():
        row = lax.broadcasted_iota(jnp.int32, (t, t), 0)
        col = lax.broadcasted_iota(jnp.int32, (t, t), 1)
        rel = (row - col).astype(F32)
        ti = lax.broadcasted_iota(jnp.int32, (t, 1), 0).astype(F32)
        for h in range(R_HEADS):
            lg = lg_ref[h]
            dm_ref[h] = jnp.where(rel >= 0, jnp.exp(jnp.maximum(rel, 0.0) * lg), 0.0) * k_scale
            dc_ref[h, 0] = jnp.exp((ti + 1.0) * lg)
            dc_ref[h, 1] = jnp.exp((t - 1.0 - ti) * lg) * k_scale

    @pl.when(c == 0)
    def _():
        so_ref[...] = jnp.zeros_like(so_ref)

    def head_matmuls(h):
        lg = lg_ref[h]
        q_decay = dc_ref[h, 0]
        k_decay = dc_ref[h, 1]
        chunk_decay = jnp.exp(jnp.full((1, R_VAL_DIM), float(t), F32) * lg)
        q = z_ref[:, h * R_KEY_DIM:(h + 1) * R_KEY_DIM]
        k = z_ref[:, R_K_WIDTH + h * R_KEY_DIM:R_K_WIDTH + (h + 1) * R_KEY_DIM]
        v = z_ref[:, 2 * R_K_WIDTH + h * R_VAL_DIM:2 * R_K_WIDTH + (h + 1) * R_VAL_DIM]
        st = so_ref[0, h]
        scores = _dot_nt(q, k) * dm_ref[h]
        inner = jnp.dot(scores.astype(BF16), v, preferred_element_type=F32)
        cross = jnp.dot(q, st.astype(BF16), preferred_element_type=F32) * q_decay
        kd_t = (k.astype(F32) * k_decay).T.astype(BF16)
        so_ref[0, h] = chunk_decay * st + jnp.dot(kd_t, v, preferred_element_type=F32)
        return inner + cross

    x1 = x_ref[...]
    ahead = [head_matmuls(h) for h in range(RET_LOOKAHEAD)]
    for h in range(R_HEADS):
        if h + RET_LOOKAHEAD < R_HEADS:
            ahead.append(head_matmuls(h + RET_LOOKAHEAD))
        raw = ahead.pop(0)
        g = z_ref[:, 2 * R_K_WIDTH + R_V_WIDTH + h * R_VAL_DIM:2 * R_K_WIDTH + R_V_WIDTH + (h + 1) * R_VAL_DIM]
        x1 = x1 + jnp.dot(_gated_groupnorm(raw, g), w_ref[0, h * R_VAL_DIM:(h + 1) * R_VAL_DIM, :],
                          preferred_element_type=F32)
    xo_ref[...] = x1
    q_ref[...] = jnp.dot(_rms(x1, g_ref[...]).astype(BF16), wq_ref[0],
                         preferred_element_type=F32).astype(q_ref.dtype)


def _ret_fused(z, log_g, w_out_b, layer_i, x, g, wq_b, layer):
    nb = RET_PROMPT_BLKS
    r = RET_ROWS
    once = pl.Buffered(1)
    return pl.pallas_call(
        _ret_fused_kernel,
        grid=(nb,),
        in_specs=[pl.BlockSpec(memory_space=pltpu.SMEM),
                  pl.BlockSpec((r, RET_IN), lambda s: (s, 0)),
                  pl.BlockSpec((1, R_V_WIDTH, D_MODEL), lambda s: (layer_i, 0, 0), pipeline_mode=once),
                  pl.BlockSpec((r, D_MODEL), lambda s: (s, 0)),
                  pl.BlockSpec((1, D_MODEL), lambda s: (0, 0)),
                  pl.BlockSpec((1, D_MODEL, X_WIDTH), lambda s: (layer, 0, 0), pipeline_mode=once)],
        out_specs=[pl.BlockSpec((r, D_MODEL), lambda s: (s, 0)),
                   pl.BlockSpec((r, X_WIDTH), lambda s: (s, 0)),
                   pl.BlockSpec((1, R_HEADS, R_KEY_DIM, R_VAL_DIM),
                                lambda s: (s // RET_BLKS_PER_SEQ, 0, 0, 0))],
        out_shape=[jax.ShapeDtypeStruct((N_PROMPT, D_MODEL), F32),
                   jax.ShapeDtypeStruct((N_PROMPT, X_WIDTH), BF16),
                   jax.ShapeDtypeStruct((BATCH, R_HEADS, R_KEY_DIM, R_VAL_DIM), F32)],
        scratch_shapes=[pltpu.VMEM((R_HEADS, r, r), F32),
                        pltpu.VMEM((R_HEADS, 2, r, 1), F32)],
        compiler_params=_cp(1, FUSED_VMEM_LIMIT),
        name="ret_fused",
    )(log_g, z, w_out_b, x, g.reshape(1, D_MODEL), wq_b)


RET_SAMPLE_SEQS = 16
RET_SAMPLE_ROWS = RET_SAMPLE_SEQS * DEC_SEQ


def _ret_sample_kernel(lg_ref, q_ref, k_ref, v_ref, g_ref, s_ref, *rest):
    o_ref, so_ref = rest[-2:]
    h = pl.program_id(1)
    lg = lg_ref[h]
    n = RET_SAMPLE_ROWS
    t = DEC_SEQ
    row = lax.broadcasted_iota(jnp.int32, (n, n), 0)
    col = lax.broadcasted_iota(jnp.int32, (n, n), 1)
    rel = (row - col).astype(F32)
    same = _seq_of(row) == _seq_of(col)
    k_scale = R_KEY_DIM ** -0.5
    dmat = jnp.where(same & (rel >= 0), jnp.exp(jnp.maximum(rel, 0.0) * lg), 0.0) * k_scale
    ri = lax.broadcasted_iota(jnp.int32, (n, 1), 0)
    ti = (ri & (t - 1)).astype(F32)
    q_decay = jnp.exp((ti + 1.0) * lg)
    k_decay = jnp.exp((t - 1.0 - ti) * lg) * k_scale
    step_decay = jnp.exp(jnp.full((1, R_VAL_DIM), float(t), F32) * lg)

    q = q_ref[...]
    k = k_ref[...]
    v = v_ref[...]
    scores = _dot_nt(q, k) * dmat
    o = jnp.dot(scores.astype(BF16), v, preferred_element_type=F32)
    kd = k.astype(F32) * k_decay
    pad = R_CHUNK - n
    v_pad = v if pad == 0 else jnp.concatenate([v.astype(F32), jnp.zeros((pad, R_VAL_DIM), F32)], axis=0).astype(BF16)
    for r in range(RET_SAMPLE_SEQS):
        mine = _seq_of(ri) == r
        st = s_ref[0, r, 0]
        cross = jnp.dot(q, st.astype(BF16), preferred_element_type=F32) * q_decay
        o = o + jnp.where(mine, cross, 0.0)
        kd_r = jnp.where(mine, kd, 0.0)
        if pad:
            kd_r = jnp.concatenate([kd_r, jnp.zeros((pad, R_KEY_DIM), F32)], axis=0)
        kd_t = kd_r.T.astype(BF16)
        so_ref[0, r, 0] = step_decay * st + jnp.dot(kd_t, v_pad, preferred_element_type=F32)
    o_ref[...] = _gated_groupnorm(o, g_ref[...])


def _ret_sample(z, state_all, idx, log_g, new_states):
    n_steps = DEC_BATCH // RET_SAMPLE_SEQS
    rb0 = N_PROMPT // RET_SAMPLE_ROWS
    kcol = R_K_WIDTH // R_KEY_DIM
    vcol = 2 * R_K_WIDTH // R_VAL_DIM
    gcol = vcol + R_HEADS
    st_spec = pl.BlockSpec((1, RET_SAMPLE_SEQS, 1, R_KEY_DIM, R_VAL_DIM), lambda i, h: (idx, i, h, 0, 0))
    in_specs = [pl.BlockSpec(memory_space=pltpu.SMEM),
                pl.BlockSpec((RET_SAMPLE_ROWS, R_KEY_DIM), lambda i, h: (rb0 + i, h)),
                pl.BlockSpec((RET_SAMPLE_ROWS, R_KEY_DIM), lambda i, h: (rb0 + i, kcol + h)),
                pl.BlockSpec((RET_SAMPLE_ROWS, R_VAL_DIM), lambda i, h: (rb0 + i, vcol + h)),
                pl.BlockSpec((RET_SAMPLE_ROWS, R_VAL_DIM), lambda i, h: (rb0 + i, gcol + h)),
                st_spec]
    args = [log_g, z, z, z, z, state_all]
    aliases = {}
    if new_states is not None:
        in_specs.append(pl.BlockSpec(memory_space=pl.ANY))
        args.append(new_states)
        aliases = {len(args) - 1: 1}
    return pl.pallas_call(
        _ret_sample_kernel,
        grid=(n_steps, R_HEADS),
        in_specs=in_specs,
        out_specs=[pl.BlockSpec((RET_SAMPLE_ROWS, R_VAL_DIM), lambda i, h: (i, h)), st_spec],
        out_shape=[jax.ShapeDtypeStruct((N_SAMPLE, R_V_WIDTH), BF16),
                   jax.ShapeDtypeStruct(state_all.shape, F32)],
        input_output_aliases=aliases,
        compiler_params=_cp(2),
        name="ret_sample",
    )(*args)


XQ_ROWS = 512
XQ_PROMPT_BLKS = N_PROMPT // XQ_ROWS
XQ_BLKS_PER_SEQ = SEQ // XQ_ROWS


def _xattn_heads(q, mk_heads, mv_heads):
    pairs = [(i, h) for i in range(len(mk_heads)) for h in range(X_HEADS)]
    scores = [_dot_nt(q[:, h * X_HEAD_DIM:(h + 1) * X_HEAD_DIM], mk_heads[i](h)) * (X_HEAD_DIM ** -0.5)
              for i, h in pairs]
    probs, invs = [], []
    for s in scores:
        p = jnp.exp(s - jnp.max(s, axis=-1, keepdims=True))
        probs.append(p.astype(BF16))
        invs.append(1.0 / jnp.sum(p, axis=-1, keepdims=True))
    outs = [jnp.dot(p, mv_heads[i](h), preferred_element_type=F32) * inv
            for (i, h), p, inv in zip(pairs, probs, invs)]
    return [outs[i * X_HEADS:(i + 1) * X_HEADS] for i in range(len(mk_heads))]


def _xattn_proj_kernel(final, q_ref, mk_ref, mv_ref, os_ref, w_ref, xp_ref, xs_ref, g_ref, out_a, out_b):
    s_id = pl.program_id(0)

    def project(o, x_ref):
        return x_ref[...] + jnp.dot(o, w_ref[0], preferred_element_type=F32)

    @pl.when(s_id < XQ_PROMPT_BLKS)
    def _():
        def head_of(ref):
            return lambda h: ref[0, :, h * X_HEAD_DIM:(h + 1) * X_HEAD_DIM].astype(BF16)

        outs, = _xattn_heads(q_ref[...], [head_of(mk_ref)], [head_of(mv_ref)])
        x2 = project(jnp.concatenate(outs, axis=1).astype(BF16), xp_ref)
        if final:
            out_a[...] = _rms(x2, g_ref[...])
        else:
            out_a[...] = x2
            out_b[...] = _rms(x2, g_ref[...]).astype(out_b.dtype)

    @pl.when(s_id >= XQ_PROMPT_BLKS)
    def _():
        x2 = project(os_ref[...], xs_ref)
        if final:
            out_b[...] = _rms(x2, g_ref[...])
        else:
            out_a[0:N_SAMPLE, :] = x2
            out_b[0:N_SAMPLE, :] = _rms(x2, g_ref[...]).astype(out_b.dtype)


def _xattn_proj(q, mk_all, mv_all, layer, o_sample, w_stack, x_prompt, x_sample, g, final):
    nb = XQ_PROMPT_BLKS
    d = w_stack.shape[2]
    xs_blk = (x_sample.shape[0] - N_SAMPLE) // N_SAMPLE

    def rb(s):
        return jnp.minimum(s, nb - 1)

    mem_spec = pl.BlockSpec((1, MEM_LEN, X_WIDTH), lambda s: (layer, rb(s) // XQ_BLKS_PER_SEQ, 0))
    if final:
        out_specs = [pl.BlockSpec((XQ_ROWS, d), lambda s: (rb(s), 0)),
                     pl.BlockSpec((N_SAMPLE, d), lambda s: (0, 0))]
        out_shape = [jax.ShapeDtypeStruct((N_PROMPT, d), F32), jax.ShapeDtypeStruct((N_SAMPLE, d), F32)]
    else:
        out_specs = [pl.BlockSpec((XQ_ROWS, d), lambda s: (s, 0)),
                     pl.BlockSpec((XQ_ROWS, d), lambda s: (s, 0))]
        out_shape = [jax.ShapeDtypeStruct((N_ROWS, d), F32), jax.ShapeDtypeStruct((N_ROWS, d), BF16)]
    return pl.pallas_call(
        functools.partial(_xattn_proj_kernel, final),
        grid=(nb + 1,),
        in_specs=[pl.BlockSpec((XQ_ROWS, X_WIDTH), lambda s: (rb(s), 0)),
                  mem_spec, mem_spec,
                  pl.BlockSpec((N_SAMPLE, X_WIDTH), lambda s: (0, 0)),
                  pl.BlockSpec((1, X_WIDTH, d), lambda s: (layer, 0, 0)),
                  pl.BlockSpec((XQ_ROWS, d), lambda s: (rb(s), 0)),
                  pl.BlockSpec((N_SAMPLE, d), lambda s: (xs_blk, 0)),
                  pl.BlockSpec((1, d), lambda s: (0, 0))],
        out_specs=out_specs,
        out_shape=out_shape,
        compiler_params=_cp(1),
        name="xattn_proj",
    )(q, mk_all, mv_all, o_sample, w_stack, x_prompt, x_sample, g.reshape(1, d))


XS_SEQS = 4
XS_ROWS = XS_SEQS * DEC_SEQ


def _xattn_sample_kernel(q_ref, ck_ref, cv_ref, o_ref):
    q = q_ref[...]
    ri = lax.broadcasted_iota(jnp.int32, (XS_ROWS, 1), 0)
    def head_of(ref, r):
        return lambda h: ref[0, r, pl.ds(h, MEM_LEN, stride=X_HEADS), :].astype(BF16)

    per_seq = _xattn_heads(q, [head_of(ck_ref, r) for r in range(XS_SEQS)],
                           [head_of(cv_ref, r) for r in range(XS_SEQS)])
    acc = [jnp.zeros((XS_ROWS, X_HEAD_DIM), F32) for _ in range(X_HEADS)]
    for r, outs in enumerate(per_seq):
        mine = _seq_of(ri) == r
        acc = [a + jnp.where(mine, o, 0.0) for a, o in zip(acc, outs)]
    for h, a in enumerate(acc):
        o_ref[:, h * X_HEAD_DIM:(h + 1) * X_HEAD_DIM] = a.astype(o_ref.dtype)


def _xattn_sample(q, ck_all, cv_all, layer):
    n_steps = DEC_BATCH // XS_SEQS
    rb0 = (q.shape[0] - N_SAMPLE) // XS_ROWS
    mem_spec = pl.BlockSpec((1, XS_SEQS, MEM_LEN * X_HEADS, X_HEAD_DIM), lambda i: (layer, i, 0, 0))
    return pl.pallas_call(
        _xattn_sample_kernel,
        grid=(n_steps,),
        in_specs=[pl.BlockSpec((XS_ROWS, X_WIDTH), lambda i: (rb0 + i, 0)), mem_spec, mem_spec],
        out_specs=pl.BlockSpec((XS_ROWS, X_WIDTH), lambda i: (i, 0)),
        out_shape=jax.ShapeDtypeStruct((N_SAMPLE, X_WIDTH), BF16),
        compiler_params=_cp(1),
        name="xattn_sample",
    )(q, ck_all, cv_all)


def kernel(x_prompt, x_sample, cache_win_k, cache_win_v, state_ret, cache_mem_k, cache_mem_v, mem_prompt,
           norm_mix, w_in_even, attn_sinks, w_spatial, b_spatial, norm_v_g, norm_v_b, w_out_even,
           w_in_ret, w_out_ret, norm_cross, norm_mem, w_xq, w_xk, w_xv, w_xo, norm_final):
    slopes = jnp.exp2(-8.0 * (jnp.arange(A_HEADS, dtype=F32) + 1.0) / A_HEADS)
    log_g = jnp.log1p(-jnp.exp2(-5.0 - jnp.arange(R_HEADS, dtype=F32)))

    def mem_rows(cache):
        return cache.reshape(DEPTH, DEC_BATCH, MEM_LEN * X_HEADS, X_HEAD_DIM)

    mk_all, mv_all, mk_b, mv_b, w_xq_b, w_xo_b = _memkv(mem_prompt.reshape(BATCH * MEM_LEN, D_MODEL), norm_mem,
                                                        w_xk, w_xv, w_xq, w_xo)

    xp_src = x_prompt.reshape(N_PROMPT, D_MODEL)
    xs_src = x_sample.reshape(N_SAMPLE, D_MODEL)
    xn = _embed(xp_src, xs_src, norm_mix[0])
    wkp, wvp, rsp, wks, wvs, cvs = [], [], [], [], [], []
    rss = None
    for layer in range(DEPTH):
        i = layer // 2
        xs_blk0 = (xs_src.shape[0] - N_SAMPLE) // N_SAMPLE
        if layer % 2 == 0:
            sl = jnp.stack([slopes, attn_sinks[i].astype(F32)])
            b_s = b_spatial[i].reshape(B_GROUPS, CHUNK, 1)
            lng = norm_v_g[i].reshape(1, B_WIDTH)
            lnb = norm_v_b[i].reshape(1, B_WIDTH)
            z, zs, w_out_b = _in_even(xn, w_in_even, i, w_out_even)
            kv_tail = _kv_tail(xn, w_in_even, i)
            wkp.append(kv_tail[:, :A_KV_WIDTH].reshape(BATCH, WINDOW, A_KV_HEADS, A_HEAD_DIM))
            wvp.append(kv_tail[:, A_KV_WIDTH:].reshape(BATCH, WINDOW, A_KV_HEADS, A_HEAD_DIM))
            ck = cache_win_k[i].reshape(DEC_BATCH, WINDOW, A_KV_WIDTH)
            cv = cache_win_v[i].reshape(DEC_BATCH, WINDOW, A_KV_WIDTH)
            mixed_s, wk_new, wv_new, cv_new = _even_sample(zs, ck, cv, sl, w_spatial[i], b_s, lng, lnb)
            wks.append(wk_new.reshape(DEC_BATCH, WINDOW, A_KV_HEADS, A_HEAD_DIM))
            wvs.append(wv_new.reshape(DEC_BATCH, WINDOW, A_KV_HEADS, A_HEAD_DIM))
            cvs.append(cv_new.reshape(DEC_BATCH, DEC_SEQ, B_WIDTH))
            x_p, q = _even_fused(z, sl, w_spatial[i], b_s, lng, lnb, w_out_b, 0, xp_src, norm_cross[layer],
                                 w_xq_b, layer)
            x_s, q_s = _mm_res_q(mixed_s, w_out_b, 0, xs_src, norm_cross[layer], w_xq_b, layer,
                                 N_SAMPLE, SAMPLE_PROJ_TK, x_blk0=xs_blk0)
        else:
            z, w_out_b = _mm_wcast(xn, w_in_ret, i, BF16, IN_RET_TM, IN_RET_TN, w_out_ret, i, "in_ret")
            o_s, rss = _ret_sample(z, state_ret, i, log_g, rss)
            x_p, q, st_p = _ret_fused(z, log_g, w_out_b, 0, xp_src, norm_cross[layer], w_xq_b, layer)
            rsp.append(st_p)
            x_s, q_s = _mm_res_q(o_s, w_out_b, 0, xs_src, norm_cross[layer], w_xq_b, layer,
                                 N_SAMPLE, SAMPLE_PROJ_TK, x_blk0=xs_blk0)
        o_s = _xattn_sample(q_s, mem_rows(cache_mem_k), mem_rows(cache_mem_v), layer)
        if layer + 1 < DEPTH:
            x, xn = _xattn_proj(q, mk_b, mv_b, layer, o_s, w_xo_b, x_p, x_s, norm_mix[layer + 1], False)
            xp_src = xs_src = x
        else:
            y_prompt, y_sample = _xattn_proj(q, mk_b, mv_b, layer, o_s, w_xo_b, x_p, x_s, norm_final, True)

    y_prompt = y_prompt.reshape(BATCH, SEQ, D_MODEL)
    y_sample = y_sample.reshape(DEC_BATCH, DEC_SEQ, D_MODEL)
    mem_shape = (DEPTH, BATCH, MEM_LEN, X_HEADS, X_HEAD_DIM)
    return (y_prompt, y_sample,
            jnp.stack(wkp), jnp.stack(wvp), jnp.stack(rsp),
            mk_all.reshape(mem_shape), mv_all.reshape(mem_shape),
            jnp.stack(wks), jnp.stack(wvs), rss, jnp.stack(cvs))
```
